```python
import math, functools
import jax, jax.numpy as jnp
from jax import lax
import numpy as np

D_MODEL = 1024
BATCH = 4
SEQ = 4096
DEPTH = 2
DEC_BATCH = 32
DEC_SEQ = 1
PAST_LEN = 16384
PAGE_SIZE = 128

HEAD_DIM = D_MODEL // 16
NSA_HEADS = 8
NSA_KV_HEADS = 2
NSA_GROUP = NSA_HEADS // NSA_KV_HEADS
MLSTM_HEADS = 4
POOL_GROUPS = 4
NSA_W = NSA_HEADS * HEAD_DIM
KV_W = NSA_KV_HEADS * HEAD_DIM
MLSTM_W = MLSTM_HEADS * HEAD_DIM
POOL_W = POOL_GROUPS * HEAD_DIM
MIX_W = NSA_W + MLSTM_W + POOL_W
IN_SPLITS = (NSA_W, 6 * KV_W, 3 * NSA_HEADS, 3 * MLSTM_W, 2 * MLSTM_HEADS, MLSTM_W, POOL_W)
IN_W = NSA_W + 6 * KV_W + 3 * NSA_HEADS + 3 * MLSTM_W + 2 * MLSTM_HEADS + MLSTM_W + POOL_W
CMP_STRIDE = 16
CMP_LEN = 2 * CMP_STRIDE
SLC_BLOCK = 64
SLC_RATIO = SLC_BLOCK // CMP_STRIDE
N_SEL = 16
WINDOW = 512
Q_BLOCK = 128
MLSTM_CHUNK = 64
POOL_WINDOWS = (2, 4, 8, 16)
POOL_STATE = 15
D_FF = 2816
FFN_CONV = 3
RMS_EPS = 1e-6
NEG = -1e30
FORCE = 1e6

kernel_name = 'hybrid_nsa_mlstm_pool_decoder_step'


def rmsnorm(x, g):
    x32 = x.astype(jnp.float32)
    y = x32 * lax.rsqrt(jnp.mean(x32 * x32, axis=-1, keepdims=True) + RMS_EPS) * g
    return y.astype(x.dtype)


def masked_softmax(s, mask):
    s = jnp.where(mask, s, NEG)
    e = jnp.exp(s - s.max(axis=-1, keepdims=True)) * mask
    return e / jnp.maximum(e.sum(axis=-1, keepdims=True), jnp.finfo(jnp.float32).tiny)


def split_cols(p):
    outs, o = [], 0
    for w in IN_SPLITS:
        outs.append(p[..., o:o + w])
        o += w
    return outs


def compress(rows, wpos, w, b):
    B, Tp = rows.shape[:2]
    seg = rows.reshape(B, Tp // CMP_STRIDE, CMP_STRIDE, NSA_KV_HEADS, HEAD_DIM)
    pooled = (jnp.einsum('bnjhd,jd->bnhd', seg[:, :-1], wpos[:CMP_STRIDE])
              + jnp.einsum('bnjhd,jd->bnhd', seg[:, 1:], wpos[CMP_STRIDE:]))
    return jnp.einsum('bnhd,de->bnhe', pooled, w) + b


def nsa_core(q, g, qpos, kc, vc, ks, vs, kw, vw, kpos):
    B, Tq = q.shape[:2]
    f32 = jnp.float32
    scale = HEAD_DIM ** -0.5
    qg = q.reshape(B, Tq, NSA_KV_HEADS, NSA_GROUP, HEAD_DIM)
    nc = kc.shape[1]
    cmp_end = CMP_STRIDE * jnp.arange(nc) + CMP_LEN - 1
    s_c = jnp.einsum('bqhgd,bnhd->bhgqn', qg, kc, preferred_element_type=f32) * scale
    p_c = masked_softmax(s_c, cmp_end[None, :] <= qpos[:, None])
    o_c = jnp.einsum('bhgqn,bnhd->bqhgd', p_c, vc.astype(f32))
    ns = ks.shape[2]
    imp = jnp.pad(p_c.sum(axis=2), ((0, 0), (0, 0), (0, 0), (1, SLC_RATIO * ns - nc)))
    imp = (imp[..., :SLC_RATIO * ns].reshape(B, NSA_KV_HEADS, Tq, ns, SLC_RATIO).sum(-1)
           + imp[..., SLC_RATIO::SLC_RATIO])
    j = jnp.arange(ns)[None, :]
    cur = (qpos // SLC_BLOCK)[:, None]
    valid = j * SLC_BLOCK <= qpos[:, None]
    forced = (j == 0) | (j == cur) | (j == cur - 1)
    score = jnp.where(valid, jnp.where(forced, FORCE, imp), -1.0)
    top, idx = lax.top_k(score, min(N_SEL, ns))
    gather = jax.vmap(jax.vmap(lambda blocks, ix: blocks[ix]))
    gk = gather(ks, idx)
    gv = gather(vs, idx)
    s_s = jnp.einsum('bqhgd,bhqksd->bhgqks', qg, gk, preferred_element_type=f32) * scale
    tok = idx[..., None] * SLC_BLOCK + jnp.arange(SLC_BLOCK)
    m_s = (top >= 0)[..., None] & (tok <= qpos[:, None, None])
    sh = s_s.shape
    p_s = masked_softmax(s_s.reshape(sh[:4] + (-1,)),
                         m_s.reshape(B, NSA_KV_HEADS, 1, Tq, -1)).reshape(sh)
    o_s = jnp.einsum('bhgqks,bhqksd->bqhgd', p_s, gv.astype(f32))
    s_w = jnp.einsum('bqhgd,bkhd->bhgqk', qg, kw, preferred_element_type=f32) * scale
    dlt = qpos[:, None] - kpos[None, :]
    p_w = masked_softmax(s_w, (dlt >= 0) & (dlt < WINDOW) & (kpos >= 0)[None, :])
    o_w = jnp.einsum('bhgqk,bkhd->bqhgd', p_w, vw.astype(f32))
    gt = jax.nn.sigmoid(g.astype(f32)).reshape(B, Tq, NSA_KV_HEADS, NSA_GROUP, 3)
    o = gt[..., 0:1] * o_c + gt[..., 1:2] * o_s + gt[..., 2:3] * o_w
    return o.reshape(B, Tq, NSA_W).astype(q.dtype)


def unpack_nsa(nq, nkv):
    B, T = nq.shape[:2]
    kv = nkv.reshape(B, T, 6, NSA_KV_HEADS, HEAD_DIM)
    return [kv[:, :, i] for i in range(6)]


def nsa_prompt(nq, nkv, ng, cmp_pos, cmp_w, cmp_b):
    B, S = nq.shape[:2]
    k_cmp, v_cmp, k_slc, v_slc, k_win, v_win = unpack_nsa(nq, nkv)
    kc = compress(k_cmp, cmp_pos[0], cmp_w[0], cmp_b[0])
    vc = compress(v_cmp, cmp_pos[1], cmp_w[1], cmp_b[1])
    ns = S // SLC_BLOCK
    ks = k_slc.reshape(B, ns, SLC_BLOCK, NSA_KV_HEADS, HEAD_DIM).transpose(0, 3, 1, 2, 4)
    vs = v_slc.reshape(B, ns, SLC_BLOCK, NSA_KV_HEADS, HEAD_DIM).transpose(0, 3, 1, 2, 4)
    pad = ((0, 0), (WINDOW, 0), (0, 0), (0, 0))
    kw, vw = jnp.pad(k_win, pad), jnp.pad(v_win, pad)
    nqb = S // Q_BLOCK
    qb = nq.reshape(B, nqb, Q_BLOCK, NSA_W).swapaxes(0, 1)
    gb = ng.reshape(B, nqb, Q_BLOCK, 3 * NSA_HEADS).swapaxes(0, 1)

    def one_block(args):
        i, qi, gi = args
        start = i * Q_BLOCK
        qpos = start + jnp.arange(Q_BLOCK)
        kpos = start - WINDOW + jnp.arange(WINDOW + Q_BLOCK)
        kwi = lax.dynamic_slice_in_dim(kw, start, WINDOW + Q_BLOCK, axis=1)
        vwi = lax.dynamic_slice_in_dim(vw, start, WINDOW + Q_BLOCK, axis=1)
        return nsa_core(qi, gi, qpos, kc, vc, ks, vs, kwi, vwi, kpos)

    o = lax.map(one_block, (jnp.arange(nqb), qb, gb)).swapaxes(0, 1).reshape(B, S, NSA_W)
    wb = min(WINDOW, S)
    new_k = jnp.stack([k_cmp, k_slc], axis=2)
    new_v = jnp.stack([v_cmp, v_slc], axis=2)
    return o, (new_k, new_v, k_win[:, S - wb:], v_win[:, S - wb:])


def nsa_sample(nq, nkv, ng, past_k, past_v, win_k, win_v, cmp_pos, cmp_w, cmp_b):
    DB, DS = nq.shape[:2]
    past = past_k.shape[1]
    k_cmp, v_cmp, k_slc, v_slc, k_win, v_win = unpack_nsa(nq, nkv)
    tot = past + DS
    tp = -(-tot // SLC_BLOCK) * SLC_BLOCK
    pad = ((0, 0), (0, tp - tot), (0, 0), (0, 0))
    rows = lambda old, new: jnp.pad(jnp.concatenate([old.astype(new.dtype), new], axis=1), pad)
    kc = compress(rows(past_k[:, :, 0], k_cmp), cmp_pos[0], cmp_w[0], cmp_b[0])
    vc = compress(rows(past_v[:, :, 0], v_cmp), cmp_pos[1], cmp_w[1], cmp_b[1])
    ns = tp // SLC_BLOCK
    ks = rows(past_k[:, :, 1], k_slc).reshape(DB, ns, SLC_BLOCK, NSA_KV_HEADS, HEAD_DIM).transpose(0, 3, 1, 2, 4)
    vs = rows(past_v[:, :, 1], v_slc).reshape(DB, ns, SLC_BLOCK, NSA_KV_HEADS, HEAD_DIM).transpose(0, 3, 1, 2, 4)
    wb = win_k.shape[1]
    kw = jnp.concatenate([win_k.astype(k_win.dtype), k_win], axis=1)
    vw = jnp.concatenate([win_v.astype(v_win.dtype), v_win], axis=1)
    kpos = past - wb + jnp.arange(wb + DS)
    qpos = past + jnp.arange(DS)
    o = nsa_core(nq, ng, qpos, kc, vc, ks, vs, kw, vw, kpos)
    new_k = jnp.stack([k_cmp, k_slc], axis=2)
    new_v = jnp.stack([v_cmp, v_slc], axis=2)
    return o, (new_k, new_v, kw[:, -wb:], vw[:, -wb:])


def mlstm_chunk(carry, xs):
    C, n, m = carry
    q, k, v, ig, lf = xs
    L = q.shape[1]
    b = jnp.cumsum(lf, axis=1)
    a = b + m[:, None, :]
    bt, it, at = b.transpose(0, 2, 1), ig.transpose(0, 2, 1), a.transpose(0, 2, 1)
    causal = jnp.tril(jnp.ones((L, L), dtype=bool))
    logd = jnp.where(causal, bt[..., :, None] - bt[..., None, :] + it[..., None, :], -jnp.inf)
    mt = jnp.maximum(at, logd.max(axis=-1))
    dm = jnp.exp(logd - mt[..., None])
    inter = jnp.exp(at - mt)
    qk = jnp.einsum('blhd,bshd->bhls', q, k) * dm
    num = jnp.einsum('bhls,bshd->blhd', qk, v) + jnp.einsum('bhl,blhd,bhde->blhe', inter, q, C)
    den = qk.sum(axis=-1) + inter * jnp.einsum('blhd,bhd->bhl', q, n)
    h = num / jnp.maximum(jnp.abs(den), jnp.exp(-mt)).transpose(0, 2, 1)[..., None]
    mL = mt[:, :, -1]
    w = jnp.exp(b[:, -1:, :] - b + ig - mL[:, None, :])
    decay = jnp.exp(a[:, -1] - mL)
    C = decay[..., None, None] * C + jnp.einsum('bsh,bshd,bshe->bhde', w, k, v)
    n = decay[..., None] * n + jnp.einsum('bsh,bshd->bhd', w, k)
    return (C, n, mL), h


def mlstm(q, k, v, ig, lf, state):
    B, T = q.shape[:2]
    L = math.gcd(T, MLSTM_CHUNK)
    nc = T // L
    chunk = lambda t: t.reshape((B, nc, L) + t.shape[2:]).swapaxes(0, 1)
    state = tuple(s.astype(jnp.float32) for s in state)
    state, h = lax.scan(mlstm_chunk, state, tuple(chunk(t) for t in (q, k, v, ig, lf)))
    return h.swapaxes(0, 1).reshape(B, T, MLSTM_HEADS, HEAD_DIM), state


def pool_mix(u, prev, pos0, w_grp, scale):
    B, T, _ = u.shape
    zr = jnp.concatenate([prev.astype(u.dtype), u], axis=1)
    z = zr.astype(jnp.float32)
    cs = jnp.concatenate([jnp.zeros((B, 1, POOL_W), jnp.float32), jnp.cumsum(z, axis=1)], axis=1)
    P = POOL_STATE
    pos = pos0 + jnp.arange(T)
    end = cs[:, P + 1:]
    outs = []
    for gi, w in enumerate(POOL_WINDOWS):
        sl = slice(gi * HEAD_DIM, (gi + 1) * HEAD_DIM)
        cnt = jnp.minimum(w, pos + 1).astype(jnp.float32)[None, :, None]
        mean = (end[..., sl] - cs[:, P + 1 - w:P + 1 - w + T, sl]) / cnt
        outs.append((mean - z[:, P:, sl]) @ w_grp[gi].astype(jnp.float32))
    return jnp.concatenate(outs, axis=-1) * scale, zr[:, -P:]


def conv_ffn(h, prev, w_up, conv_w, conv_b, w_down):
    T = h.shape[1]
    up = h @ w_up
    z = jnp.concatenate([prev.astype(up.dtype), up], axis=1)
    y = conv_b
    for j in range(FFN_CONV):
        y = y + conv_w[j] * z[:, j:j + T]
    a, b = jnp.split(y, 2, axis=-1)
    return (jax.nn.silu(a) * b) @ w_down, z[:, -(FFN_CONV - 1):]


def trunk_layer(x, c, lw, nsa_fn, mstate, pool_prev, ffn_prev, pos0):
    (g1, g2, ada_w, ada_b, w_in, gate_b, pool_w, pool_scale, w_out, w_up, conv_w, conv_b, w_down) = lw
    B, T, _ = x.shape
    f32 = jnp.float32
    mod = jax.nn.silu(c) @ ada_w + ada_b
    sh1, sc1, gt1, sh2, sc2, gt2 = [t[:, None, :] for t in jnp.split(mod, 6, axis=-1)]
    h = rmsnorm(x, g1) * (1 + sc1) + sh1
    nq, nkv, ng, mqkv, mif, mo, pu = split_cols(h @ w_in)
    o_nsa, nsa_state = nsa_fn(nq, nkv, ng)
    mq, mk, mv = [t.astype(f32).reshape(B, T, MLSTM_HEADS, HEAD_DIM) for t in jnp.split(mqkv, 3, axis=-1)]
    gates = mif.astype(f32) + gate_b
    ig = gates[..., :MLSTM_HEADS]
    lf = jax.nn.log_sigmoid(gates[..., MLSTM_HEADS:])
    hm, mstate = mlstm(mq, mk * HEAD_DIM ** -0.5, mv, ig, lf, mstate)
    o_m = jax.nn.sigmoid(mo.astype(f32)) * hm.reshape(B, T, MLSTM_W)
    o_p, pool_state = pool_mix(pu, pool_prev, pos0, pool_w, pool_scale)
    mix = jnp.concatenate([o_nsa.astype(x.dtype), o_m.astype(x.dtype), o_p.astype(x.dtype)], axis=-1) @ w_out
    x = x + gt1 * mix
    h2 = rmsnorm(x, g2) * (1 + sc2) + sh2
    f, ffn_state = conv_ffn(h2, ffn_prev, w_up, conv_w, conv_b, w_down)
    x = x + gt2 * f
    return x, nsa_state, mstate, pool_state, ffn_state


def setup_inputs(seed: int = 0) -> dict:
    key = jax.random.key(seed)
    keys = iter(jax.random.split(key, 48))
    nrm = lambda shape, s=1.0: s * jax.random.normal(next(keys), shape, jnp.float32)
    n_pages = PAST_LEN // PAGE_SIZE
    n_phys = (5 * DEC_BATCH * n_pages + 3) // 4
    wb = min(WINDOW, PAST_LEN)
    d = D_MODEL
    inp = {}
    inp['x_prompt'] = nrm((BATCH, SEQ, d))
    inp['x_sample'] = nrm((DEC_BATCH, DEC_SEQ, d))
    inp['cache_k'] = nrm((DEPTH, n_phys, PAGE_SIZE, 2, NSA_KV_HEADS, HEAD_DIM))
    inp['cache_v'] = nrm((DEPTH, n_phys, PAGE_SIZE, 2, NSA_KV_HEADS, HEAD_DIM))
    inp['cache_win_k'] = nrm((DEPTH, DEC_BATCH, wb, NSA_KV_HEADS, HEAD_DIM))
    inp['cache_win_v'] = nrm((DEPTH, DEC_BATCH, wb, NSA_KV_HEADS, HEAD_DIM))
    inp['state_mlstm_C'] = nrm((DEPTH, DEC_BATCH, MLSTM_HEADS, HEAD_DIM, HEAD_DIM), 0.5)
    inp['state_mlstm_n'] = nrm((DEPTH, DEC_BATCH, MLSTM_HEADS, HEAD_DIM), 0.5)
    inp['state_mlstm_m'] = nrm((DEPTH, DEC_BATCH, MLSTM_HEADS))
    inp['state_pool'] = nrm((DEPTH, DEC_BATCH, POOL_STATE, POOL_W))
    inp['state_ffn_conv'] = nrm((DEPTH, DEC_BATCH, FFN_CONV - 1, 2 * D_FF))
    perm = jax.random.permutation(next(keys), n_phys)
    inp['page_table'] = perm[:DEC_BATCH * n_pages].reshape(DEC_BATCH, n_pages).astype(jnp.int32)
    inp['c_prompt'] = nrm((BATCH, d))
    inp['c_sample'] = nrm((DEC_BATCH, d))
    inp['norm1_g'] = 1.0 + nrm((DEPTH, d), 0.02)
    inp['norm2_g'] = 1.0 + nrm((DEPTH, d), 0.02)
    inp['ada_w'] = nrm((DEPTH, d, 6 * d), 0.5 * d ** -0.5)
    inp['ada_b'] = nrm((DEPTH, 6 * d), 0.02)
    inp['w_in'] = nrm((DEPTH, d, IN_W), d ** -0.5)
    inp['nsa_cmp_pos'] = nrm((DEPTH, 2, CMP_LEN, HEAD_DIM), CMP_LEN ** -0.5)
    inp['nsa_cmp_w'] = nrm((DEPTH, 2, HEAD_DIM, HEAD_DIM), HEAD_DIM ** -0.5)
    inp['nsa_cmp_b'] = nrm((DEPTH, 2, HEAD_DIM), 0.02)
    inp['mlstm_gate_b'] = jnp.concatenate([nrm((DEPTH, MLSTM_HEADS), 0.1),
                                           3.0 + nrm((DEPTH, MLSTM_HEADS), 0.5)], axis=-1)
    inp['pool_w'] = nrm((DEPTH, POOL_GROUPS, HEAD_DIM, HEAD_DIM), HEAD_DIM ** -0.5)
    inp['pool_scale'] = 1.0 + nrm((DEPTH, POOL_W), 0.1)
    inp['w_out'] = nrm((DEPTH, MIX_W, d), MIX_W ** -0.5)
    inp['ffn_w_up'] = nrm((DEPTH, d, 2 * D_FF), d ** -0.5)
    inp['ffn_conv_w'] = nrm((DEPTH, FFN_CONV, 2 * D_FF), FFN_CONV ** -0.5)
    inp['ffn_conv_b'] = nrm((DEPTH, 2 * D_FF), 0.02)
    inp['ffn_w_down'] = nrm((DEPTH, D_FF, d), D_FF ** -0.5)
    inp['final_g'] = 1.0 + nrm((d,), 0.02)
    return inp


def reference(x_prompt, x_sample, cache_k, cache_v, cache_win_k, cache_win_v, state_mlstm_C,
              state_mlstm_n, state_mlstm_m, state_pool, state_ffn_conv, page_table, c_prompt, c_sample,
              norm1_g, norm2_g, ada_w, ada_b, w_in, nsa_cmp_pos, nsa_cmp_w, nsa_cmp_b, mlstm_gate_b,
              pool_w, pool_scale, w_out, ffn_w_up, ffn_conv_w, ffn_conv_b, ffn_w_down, final_g):
    B = x_prompt.shape[0]
    DB = x_sample.shape[0]
    past = page_table.shape[1] * cache_k.shape[2]
    f32 = jnp.float32
    xp, xq = x_prompt, x_sample
    acc_p = [[] for _ in range(9)]
    acc_s = [[] for _ in range(9)]
    for l in range(DEPTH):
        lw = (norm1_g[l], norm2_g[l], ada_w[l], ada_b[l], w_in[l], mlstm_gate_b[l], pool_w[l],
              pool_scale[l], w_out[l], ffn_w_up[l], ffn_conv_w[l], ffn_conv_b[l], ffn_w_down[l])
        cp, cw, cb = nsa_cmp_pos[l], nsa_cmp_w[l], nsa_cmp_b[l]
        m0 = (jnp.zeros((B, MLSTM_HEADS, HEAD_DIM, HEAD_DIM), f32),
              jnp.zeros((B, MLSTM_HEADS, HEAD_DIM), f32),
              jnp.zeros((B, MLSTM_HEADS), f32))
        xp, nsa_p, m_p, pool_p, ffn_p = trunk_layer(
            xp, c_prompt, lw, functools.partial(nsa_prompt, cmp_pos=cp, cmp_w=cw, cmp_b=cb), m0,
            jnp.zeros((B, POOL_STATE, POOL_W), xp.dtype),
            jnp.zeros((B, FFN_CONV - 1, 2 * D_FF), xp.dtype), 0)
        past_k = cache_k[l][page_table].reshape((DB, past) + cache_k.shape[3:])
        past_v = cache_v[l][page_table].reshape((DB, past) + cache_v.shape[3:])
        nsa_fn = functools.partial(nsa_sample, past_k=past_k, past_v=past_v, win_k=cache_win_k[l],
                                   win_v=cache_win_v[l], cmp_pos=cp, cmp_w=cw, cmp_b=cb)
        xq, nsa_s, m_s, pool_s, ffn_s = trunk_layer(
            xq, c_sample, lw, nsa_fn, (state_mlstm_C[l], state_mlstm_n[l], state_mlstm_m[l]),
            state_pool[l], state_ffn_conv[l], past)
        for a, v in zip(acc_p, nsa_p + m_p + (pool_p, ffn_p)):
            a.append(v)
        for a, v in zip(acc_s, nsa_s + m_s + (pool_s, ffn_s)):
            a.append(v)
    sp = [jnp.stack(a, axis=0) for a in acc_p]
    ss = [jnp.stack(a, axis=0) for a in acc_s]
    y_prompt = rmsnorm(xp, final_g)
    y_sample = rmsnorm(xq, final_g)
    return (y_prompt, y_sample, sp[0], sp[1], sp[2], sp[3], sp[4], sp[5], sp[6], sp[7], sp[8],
            ss[0], ss[1], ss[2], ss[3], ss[4], ss[5], ss[6], ss[7], ss[8])
```

```python
import functools

import numpy as np
import jax
import jax.numpy as jnp
from jax import lax
from jax.experimental import pallas as pl
from jax.experimental.pallas import tpu as pltpu

F32 = jnp.float32
BF16 = jnp.bfloat16

HD = 64
NSA_HEADS = 8
HKV = 2
GRP = NSA_HEADS // HKV
MH = 4
NSA_W = NSA_HEADS * HD
KV_W = HKV * HD
MLSTM_W = MH * HD
POOL_W = 4 * HD
CMP_STRIDE = 16
CMP_LEN = 32
SLC_BLOCK = 64
SLC_RATIO = SLC_BLOCK // CMP_STRIDE
N_SEL = 16
WINDOW = 512
POOL_WINDOWS = (2, 4, 8, 16)
POOL_STATE = 15
FFN_CONV = 3
RMS_EPS = 1e-6
NEG = -1e30
FORCE = 1e6
Q_SCALE = HD ** -0.5

LANES = 128
SUBLANES = 8
VMEM_LIMIT = 56 * 1024 * 1024

C_Q = 0
C_NK = C_Q + 8 * LANES
C_NV = C_NK + 256
C_KS = C_NV + 256
C_VS = C_KS + 256
C_VW = C_VS + 256
C_WKV = C_VW + 256
C_MQKV = C_WKV + 256
C_MO = C_MQKV + 3 * MLSTM_W
C_PU = C_MO + MLSTM_W
C_SM = C_PU + POOL_W
C_END = C_SM + LANES
SM_IG = 3 * NSA_HEADS
SM_FG = SM_IG + MH


def _dot(a, b):
    return jnp.dot(a, b, preferred_element_type=F32)


def _dot_nt(a, b):
    return lax.dot_general(a, b, (((1,), (1,)), ((), ())), preferred_element_type=F32)


def _dot_tn(a, b):
    return lax.dot_general(a, b, (((0,), (0,)), ((), ())), preferred_element_type=F32)


def _dot_split3(x, m):
    x1 = x.astype(BF16)
    r1 = x - x1.astype(F32)
    x2 = r1.astype(BF16)
    x3 = (r1 - x2.astype(F32)).astype(BF16)
    return _dot(x1, m) + _dot(x2, m) + _dot(x3, m)


def _masked_softmax(s, mask):
    s = jnp.where(mask, s, NEG)
    e = jnp.where(mask, jnp.exp(s - jnp.max(s, axis=-1, keepdims=True)), 0.0)
    return e / jnp.maximum(jnp.sum(e, axis=-1, keepdims=True), jnp.finfo(jnp.float32).tiny)


def _norm_mod(x, g, sc, sh):
    ms = jnp.mean(x * x, axis=-1, keepdims=True)
    return (x * lax.rsqrt(ms + RMS_EPS) * g) * (1.0 + sc) + sh


def _rmsnorm(x, g):
    ms = jnp.mean(x * x, axis=-1, keepdims=True)
    return x * lax.rsqrt(ms + RMS_EPS) * g


def _log_sigmoid(x):
    return jnp.minimum(x, 0.0) - jnp.log1p(jnp.exp(-jnp.abs(x)))


def _cumsum_rows(x):
    n = x.shape[0]
    row = lax.broadcasted_iota(jnp.int32, x.shape, 0)
    sh = 1
    while sh < n:
        x = x + jnp.where(row >= sh, pltpu.roll(x, sh, 0), 0.0)
        sh *= 2
    return x


def _const_spec(shape):
    nd = len(shape)
    return pl.BlockSpec(shape, lambda *_: (0,) * nd, pipeline_mode=pl.Buffered(1))


def _params(*sem):
    return pltpu.CompilerParams(dimension_semantics=sem, vmem_limit_bytes=VMEM_LIMIT)


def _ada_kernel(c_ref, w_ref, b_ref, o_ref):
    c = c_ref[...]
    s = c * jax.nn.sigmoid(c)
    o_ref[0] = _dot(s.astype(BF16), w_ref[0]) + b_ref[0]


def _ada_mod(c_all, ada_w, ada_b):
    depth, d, n = ada_w.shape
    rows = c_all.shape[0]
    tn = 1536
    return pl.pallas_call(
        _ada_kernel,
        grid=(depth, n // tn),
        in_specs=[pl.BlockSpec((rows, d), lambda l, j: (0, 0)),
                  pl.BlockSpec((1, d, tn), lambda l, j: (l, 0, j)),
                  pl.BlockSpec((1, 1, tn), lambda l, j: (l, 0, j))],
        out_specs=pl.BlockSpec((1, rows, tn), lambda l, j: (l, 0, j)),
        out_shape=jax.ShapeDtypeStruct((depth, rows, n), F32),
        compiler_params=_params("arbitrary", "arbitrary"),
        name="ada_mod",
    )(c_all, ada_w.astype(BF16), ada_b.reshape(depth, 1, n))


def _prep_w_in(w):
    d = w.shape[0]
    o = 0
    nq = w[:, o:o + NSA_W]; o += NSA_W
    nkv = w[:, o:o + 6 * KV_W]; o += 6 * KV_W
    ng = w[:, o:o + 3 * NSA_HEADS]; o += 3 * NSA_HEADS
    mqkv = w[:, o:o + 3 * MLSTM_W]; o += 3 * MLSTM_W
    mif = w[:, o:o + 2 * MH]; o += 2 * MH
    mo = w[:, o:o + MLSTM_W]; o += MLSTM_W
    pu = w[:, o:o + POOL_W]
    k_cmp, v_cmp, k_slc, v_slc, k_win, v_win = [nkv[:, KV_W * i:KV_W * (i + 1)] for i in range(6)]
    z = jnp.zeros((d, HD), w.dtype)
    h0 = lambda t: t[:, :HD]
    h1 = lambda t: t[:, HD:]
    qcols = []
    for hd in range(NSA_HEADS):
        qh = nq[:, HD * hd:HD * (hd + 1)]
        qcols += [qh, z] if hd < GRP else [z, qh]
    cols = qcols + [
        k_cmp, k_slc, v_cmp, v_slc,
        h0(k_slc), z, z, h1(k_slc),
        h0(v_slc), h0(v_slc), h1(v_slc), h1(v_slc),
        h0(v_win), h0(v_win), h1(v_win), h1(v_win),
        k_win, v_win, mqkv, mo, pu,
        ng, mif, jnp.zeros((d, LANES - 3 * NSA_HEADS - 2 * MH), w.dtype)]
    out = jnp.concatenate(cols, axis=1).astype(BF16)
    assert out.shape[1] == C_END
    return out


def _pool_mix(pu, sums, cnt, pw_ref, pscale_ref):
    lane = lax.broadcasted_iota(jnp.int32, pu.shape, 1)
    grp = lane // HD
    mean = jnp.where(grp == 0, sums[2] / cnt[2],
                     jnp.where(grp == 1, sums[4] / cnt[4],
                               jnp.where(grp == 2, sums[8] / cnt[8], sums[16] / cnt[16])))
    d = mean - pu
    return _dot(d.astype(BF16), pw_ref[...]) * pscale_ref[...]


def _inproj_prompt_kernel(x_ref, g_ref, sh_ref, sc_ref, w_ref, pw_ref, pscale_ref,
                          q_ref, nk_ref, nv_ref, ks_ref, vs_ref, kw_ref, vw_ref, wkv_ref,
                          mqkv_ref, mo_ref, op_ref, sm_ref, ps_ref, zs_ref, *, tm):
    t = pl.program_id(1)
    h = _norm_mod(x_ref[0], g_ref[...], sc_ref[0], sh_ref[0]).astype(BF16)

    def seg(a, b):
        return _dot(h, w_ref[:, a:b])

    q_ref[0] = (seg(C_Q, C_NK) * Q_SCALE).astype(BF16)
    nk_ref[0] = seg(C_NK, C_NV)
    nv_ref[0] = seg(C_NV, C_KS)
    row = t * tm + lax.broadcasted_iota(jnp.int32, (tm, 2 * LANES), 0)
    lane = lax.broadcasted_iota(jnp.int32, (tm, 2 * LANES), 1)
    is_aux = (lane >= HD) & (lane < 3 * HD)
    aux = jnp.where(lane < LANES, lane - HD, lane - LANES)
    onehot = jnp.where(is_aux & (aux == row // SLC_BLOCK), 1.0, 0.0)
    ks_ref[0] = (seg(C_KS, C_VS) + onehot).astype(BF16)
    vs_ref[0] = seg(C_VS, C_VW).astype(BF16)
    vw_ref[0] = seg(C_VW, C_WKV).astype(BF16)
    wkv = seg(C_WKV, C_MQKV)
    wkv_ref[0] = wkv
    kw_ref[0] = wkv[:, :KV_W].astype(BF16)
    mqkv_ref[0] = seg(C_MQKV, C_MO)
    mo_ref[0] = seg(C_MO, C_PU)
    sm_ref[0] = seg(C_SM, C_END)

    pu = seg(C_PU, C_SM)
    halo = 2 * SUBLANES

    @pl.when(t == 0)
    def _():
        zs_ref[0:halo, :] = jnp.zeros((halo, POOL_W), F32)

    @pl.when(t > 0)
    def _():
        zs_ref[0:halo, :] = zs_ref[tm:tm + halo, :]

    zs_ref[halo:halo + tm, :] = pu
    acc = pu
    sums = {}
    for i in range(1, POOL_STATE + 1):
        acc = acc + zs_ref[pl.ds(halo - i, tm), :]
        if i + 1 in POOL_WINDOWS:
            sums[i + 1] = acc
    pos1 = (t * tm + lax.broadcasted_iota(jnp.int32, (tm, 1), 0) + 1).astype(F32)
    cnt = {w: jnp.minimum(float(w), pos1) for w in POOL_WINDOWS}
    op_ref[0] = _pool_mix(pu, sums, cnt, pw_ref, pscale_ref).astype(BF16)
    ps_ref[0] = zs_ref[tm:tm + halo, :]


def _inproj_prompt(x, g1, sh1, sc1, w_b, pool_bd, pool_scale, tm=256):
    B, T, D = x.shape
    nT = T // tm
    assert T % tm == 0 and tm >= 2 * SUBLANES and WINDOW % tm == 0
    nwin = WINDOW // tm
    row = lambda w: pl.BlockSpec((1, tm, w), lambda b, t: (b, t, 0))
    mod = pl.BlockSpec((1, 1, D), lambda b, t: (b, 0, 0))
    outs = [
        (row(8 * LANES), (B, T, 8 * LANES), BF16),
        (row(256), (B, T, 256), F32),
        (row(256), (B, T, 256), F32),
        (row(256), (B, T, 256), BF16),
        (row(256), (B, T, 256), BF16),
        (row(KV_W), (B, T, KV_W), BF16),
        (row(256), (B, T, 256), BF16),
        (pl.BlockSpec((1, tm, 256), lambda b, t: (b, jnp.maximum(t - (nT - nwin), 0), 0)),
         (B, WINDOW, 256), F32),
        (row(3 * MLSTM_W), (B, T, 3 * MLSTM_W), F32),
        (row(MLSTM_W), (B, T, MLSTM_W), F32),
        (row(POOL_W), (B, T, POOL_W), BF16),
        (row(LANES), (B, T, LANES), F32),
        (pl.BlockSpec((1, 2 * SUBLANES, POOL_W), lambda b, t: (b, 0, 0)), (B, 2 * SUBLANES, POOL_W), F32),
    ]
    return pl.pallas_call(
        functools.partial(_inproj_prompt_kernel, tm=tm),
        grid=(B, nT),
        in_specs=[pl.BlockSpec((1, tm, D), lambda b, t: (b, t, 0)),
                  _const_spec((1, D)), mod, mod,
                  _const_spec(w_b.shape), _const_spec(pool_bd.shape), _const_spec((1, POOL_W))],
        out_specs=[o[0] for o in outs],
        out_shape=[jax.ShapeDtypeStruct(o[1], o[2]) for o in outs],
        scratch_shapes=[pltpu.VMEM((2 * SUBLANES + tm, POOL_W), F32)],
        compiler_params=_params("arbitrary", "arbitrary"),
        name="inproj_prompt",
    )(x, g1, sh1, sc1, w_b, pool_bd, pool_scale)


def _compress_prompt_kernel(k_ref, v_ref, wpk_ref, wpv_ref, wk_ref, wv_ref, bk_ref, bv_ref,
                            kc_ref, vc_ref, *, nseg):
    def pooled(src_ref, wp_ref):
        a = jnp.zeros((nseg, KV_W), F32)
        b = jnp.zeros((nseg, KV_W), F32)
        for j in range(CMP_STRIDE):
            xj = src_ref[0, pl.ds(j, nseg, stride=CMP_STRIDE), :]
            a = a + xj * wp_ref[j:j + 1, :]
            b = b + xj * wp_ref[CMP_STRIDE + j:CMP_STRIDE + j + 1, :]
        return a + pltpu.roll(b, nseg - 1, 0)

    kc_ref[0] = (_dot(pooled(k_ref, wpk_ref).astype(BF16), wk_ref[...]) + bk_ref[...]).astype(BF16)
    vc_ref[0] = (_dot(pooled(v_ref, wpv_ref).astype(BF16), wv_ref[...]) + bv_ref[...]).astype(BF16)


def _compress_prompt(nk, nv, cw):
    B, T, _ = nk.shape
    nseg = T // CMP_STRIDE
    src = pl.BlockSpec((1, T, KV_W), lambda b: (b, 0, 0))
    return pl.pallas_call(
        functools.partial(_compress_prompt_kernel, nseg=nseg),
        grid=(B,),
        in_specs=[src, src, _const_spec((CMP_LEN, KV_W)), _const_spec((CMP_LEN, KV_W)),
                  _const_spec((KV_W, KV_W)), _const_spec((KV_W, 2 * KV_W)),
                  _const_spec((1, KV_W)), _const_spec((1, 2 * KV_W))],
        out_specs=[pl.BlockSpec((1, nseg, KV_W), lambda b: (b, 0, 0)),
                   pl.BlockSpec((1, nseg, 2 * KV_W), lambda b: (b, 0, 0))],
        out_shape=[jax.ShapeDtypeStruct((B, nseg, KV_W), BF16),
                   jax.ShapeDtypeStruct((B, nseg, 2 * KV_W), BF16)],
        compiler_params=_params("arbitrary"),
        name="compress_prompt",
    )(nk, nv, cw["wpk"], cw["wpv"], cw["wk"], cw["wv_dup"], cw["bk"], cw["bv_dup"])


def _prep_compress(cmp_pos, cmp_w, cmp_b):
    tile2 = lambda t: jnp.concatenate([t, t], axis=-1)
    z = jnp.zeros((HD, HD), F32)
    wk, wv = cmp_w[0], cmp_w[1]
    wk_bd = jnp.block([[wk, z], [z, wk]])
    wv_bd = jnp.block([[wv, z], [z, wv]])
    wv_dup = jnp.block([[wv, wv, z, z], [z, z, wv, wv]])
    return dict(
        wpk=tile2(cmp_pos[0]), wpv=tile2(cmp_pos[1]),
        wk=wk_bd.astype(BF16), wv=wv_bd.astype(BF16), wv_dup=wv_dup.astype(BF16),
        bk=tile2(cmp_b[0])[None, :], bv=tile2(cmp_b[1])[None, :],
        bv_dup=jnp.concatenate([cmp_b[1]] * 4)[None, :])


def _importance_matrix(nc_rows, nc_valid, ns, lane_off, width):
    m = np.zeros((nc_rows, width), np.float32)
    for j in range(ns):
        for n in range(SLC_RATIO * j - 1, SLC_RATIO * j + SLC_RATIO):
            if 0 <= n < nc_valid:
                m[n, lane_off + j] = 1.0
    return m


def _nsa_prompt_kernel(q_ref, sm_ref, kc_ref, vc_ref, ks_ref, vs_ref, kw_ref, vw_ref, m2_ref,
                       o_ref, acc_ref, m_ref, l_ref, *, T, tk, qb):
    i = pl.program_id(1)
    ns = T // SLC_BLOCK
    nc = kc_ref.shape[1]
    rows = GRP * qb
    qpos = i * qb + lax.broadcasted_iota(jnp.int32, (qb, 1), 0)
    qpos4 = jnp.concatenate([qpos] * GRP, axis=0)
    sig = jax.nn.sigmoid(sm_ref[0])
    lane = lax.broadcasted_iota(jnp.int32, (qb, LANES), 1)
    wk = WINDOW + qb

    for h in range(HKV):
        qh = jnp.concatenate(
            [q_ref[0, :, LANES * (GRP * h + g):LANES * (GRP * h + g + 1)] for g in range(GRP)], axis=0)

        s = _dot_nt(qh, kc_ref[0])
        n_idx = lax.broadcasted_iota(jnp.int32, (1, nc), 1)
        p_c = _masked_softmax(s, (CMP_STRIDE * n_idx + CMP_LEN - 1) <= qpos4)
        o_c = _dot(p_c.astype(BF16), vc_ref[0, :, LANES * h:LANES * (h + 1)])

        psum = p_c[0:qb] + p_c[qb:2 * qb] + p_c[2 * qb:3 * qb] + p_c[3 * qb:4 * qb]
        imp = _dot_split3(psum, m2_ref[h])
        jb = lane - HD if h == 0 else lane
        is_aux = ((lane >= HD) if h == 0 else (lane < HD)) & (jb < ns)
        cur = qpos // SLC_BLOCK
        valid = jb * SLC_BLOCK <= qpos
        forced = (jb == 0) | (jb == cur) | (jb == cur - 1)
        score = jnp.where(valid, jnp.where(forced, FORCE, imp), -1.0)
        score = jnp.where(is_aux, score, -jnp.inf)
        selneg = jnp.where(is_aux, NEG, 0.0)
        for _ in range(N_SEL):
            mx = jnp.max(score, axis=-1, keepdims=True)
            first = jnp.min(jnp.where(score == mx, lane, 2 * LANES), axis=-1, keepdims=True)
            pick = lane == first
            selneg = jnp.where(pick & (mx >= 0.0), 0.0, selneg)
            score = jnp.where(pick, -jnp.inf, score)
        qaug = qh + jnp.concatenate([selneg.astype(BF16)] * GRP, axis=0)

        m_ref[...] = jnp.full((rows, LANES), -jnp.inf, F32)
        l_ref[...] = jnp.zeros((rows, LANES), F32)
        acc_ref[...] = jnp.zeros((rows, LANES), F32)

        def chunk(c, carry):
            start = pl.multiple_of(c * tk, tk)
            k = ks_ref[0, pl.ds(start, tk), LANES * h:LANES * (h + 1)]
            v = vs_ref[0, pl.ds(start, tk), LANES * h:LANES * (h + 1)]
            sc = _dot_nt(qaug, k)
            tok = start + lax.broadcasted_iota(jnp.int32, (1, tk), 1)
            sc = jnp.where(tok <= qpos4, sc, NEG)
            m_prev = m_ref[...]
            m_new = jnp.maximum(m_prev, jnp.max(sc, axis=-1, keepdims=True))
            alpha = jnp.exp(m_prev - m_new)
            p = jnp.exp(sc - m_new[:, 0:1])
            l_ref[...] = alpha * l_ref[...] + jnp.sum(p, axis=-1, keepdims=True)
            acc_ref[...] = alpha * acc_ref[...] + _dot(p.astype(BF16), v)
            m_ref[...] = m_new
            return carry

        lax.fori_loop(0, (i * qb + qb + tk - 1) // tk, chunk, 0)
        o_s = acc_ref[...] / l_ref[...]

        start = pl.multiple_of(jnp.maximum(i - WINDOW // qb, 0) * qb, qb)
        sw = _dot_nt(qh, kw_ref[0, pl.ds(start, wk), :])
        dlt = qpos4 - (start + lax.broadcasted_iota(jnp.int32, (1, wk), 1))
        p_w = _masked_softmax(sw, (dlt >= 0) & (dlt < WINDOW))
        o_w = _dot(p_w.astype(BF16), vw_ref[0, pl.ds(start, wk), LANES * h:LANES * (h + 1)])

        outs = []
        for g in range(GRP):
            c0 = 3 * (GRP * h + g)
            r = slice(qb * g, qb * (g + 1))
            outs.append(sig[:, c0:c0 + 1] * o_c[r] + sig[:, c0 + 1:c0 + 2] * o_s[r]
                        + sig[:, c0 + 2:c0 + 3] * o_w[r])
        base = GRP * HD * h
        o_ref[0, :, base:base + LANES] = jnp.where(lane < HD, outs[0], outs[1]).astype(BF16)
        o_ref[0, :, base + LANES:base + 2 * LANES] = jnp.where(lane < HD, outs[2], outs[3]).astype(BF16)


def _nsa_prompt(q, sm, kc, vc, ks, vs, kw, vw, qb=128, tk=512):
    B, T, _ = q.shape
    ns = T // SLC_BLOCK
    nc = kc.shape[1]
    assert ns <= HD and T % tk == 0 and T >= WINDOW + qb and WINDOW % qb == 0
    m2 = np.stack([_importance_matrix(nc, nc - 1, ns, HD, LANES),
                   _importance_matrix(nc, nc - 1, ns, 0, LANES)])
    full = lambda w: pl.BlockSpec((1, T, w), lambda b, i: (b, 0, 0))
    return pl.pallas_call(
        functools.partial(_nsa_prompt_kernel, T=T, tk=tk, qb=qb),
        grid=(B, T // qb),
        in_specs=[pl.BlockSpec((1, qb, 8 * LANES), lambda b, i: (b, i, 0)),
                  pl.BlockSpec((1, qb, LANES), lambda b, i: (b, i, 0)),
                  pl.BlockSpec((1, nc, KV_W), lambda b, i: (b, 0, 0)),
                  pl.BlockSpec((1, nc, 2 * KV_W), lambda b, i: (b, 0, 0)),
                  full(256), full(256), full(KV_W), full(256),
                  _const_spec(m2.shape)],
        out_specs=pl.BlockSpec((1, qb, NSA_W), lambda b, i: (b, i, 0)),
        out_shape=jax.ShapeDtypeStruct((B, T, NSA_W), BF16),
        scratch_shapes=[pltpu.VMEM((GRP * qb, LANES), F32)] * 3,
        compiler_params=_params("arbitrary", "arbitrary"),
        name="nsa_prompt",
    )(q, sm, kc, vc, ks, vs, kw, vw, jnp.asarray(m2, BF16))


def _mlstm_gates(g):
    b = pltpu.roll(_cumsum_rows(_log_sigmoid(g)), LANES - MH, 1)
    return b, g - b


def _mlstm_prompt_kernel(mqkv_ref, sm_ref, mo_ref, gb_ref, om_ref, c_out, n_out, m_out,
                         c_s, n_s, m_s, *, B, L):
    t = pl.program_id(0)

    @pl.when(t == 0)
    def _():
        c_s[...] = jnp.zeros(c_s.shape, F32)
        n_s[...] = jnp.zeros(n_s.shape, F32)
        m_s[...] = jnp.zeros(m_s.shape, F32)

    li = lax.broadcasted_iota(jnp.int32, (L, L), 0)
    si = lax.broadcasted_iota(jnp.int32, (L, L), 1)
    for b in range(B):
        b_al, r = _mlstm_gates(sm_ref[b] + gb_ref[...])
        r_t = jnp.concatenate([r, jnp.zeros((LANES - L, LANES), F32)], axis=0).T if L < LANES else r.T
        heads = []
        for hd in range(MH):
            idx = b * MH + hd
            bcol = b_al[:, SM_IG + hd:SM_IG + hd + 1]
            rcol = r[:, SM_IG + hd:SM_IG + hd + 1]
            rrow = r_t[SM_IG + hd:SM_IG + hd + 1, 0:L]
            mprev = m_s[idx][:, 0:1]
            acol = bcol + mprev
            logd = jnp.where(si <= li, bcol + rrow, NEG)
            mt = jnp.maximum(acol, jnp.max(logd, axis=-1, keepdims=True))
            dm = jnp.exp(logd - mt)
            inter = jnp.exp(acol - mt)
            q = mqkv_ref[b, :, HD * hd:HD * (hd + 1)]
            k = mqkv_ref[b, :, MLSTM_W + HD * hd:MLSTM_W + HD * (hd + 1)] * Q_SCALE
            v = mqkv_ref[b, :, 2 * MLSTM_W + HD * hd:2 * MLSTM_W + HD * (hd + 1)]
            qb, kb = q.astype(BF16), k.astype(BF16)
            qk = _dot_nt(qb, kb) * dm
            c_prev = c_s[idx]
            n_prev = n_s[idx]
            num = _dot(qk.astype(BF16), v.astype(BF16)) + inter * _dot(qb, c_prev.astype(BF16))
            den = jnp.sum(qk, axis=-1, keepdims=True) + inter * jnp.sum(q * n_prev, axis=-1, keepdims=True)
            hout = num / jnp.maximum(jnp.abs(den), jnp.exp(-mt))
            heads.append(jax.nn.sigmoid(mo_ref[b, :, HD * hd:HD * (hd + 1)]) * hout)
            m_last = mt[L - 1:L, :]
            b_last = bcol[L - 1:L, :]
            wcol = jnp.exp(b_last + rcol - m_last)
            decay = jnp.exp(b_last + mprev - m_last)
            c_s[idx] = decay * c_prev + _dot_tn(kb, (wcol * v).astype(BF16))
            n_s[idx] = decay * n_prev + jnp.sum(wcol * k, axis=0, keepdims=True)
            m_s[idx] = jnp.broadcast_to(m_last, (1, LANES))
        om_ref[b] = jnp.concatenate(heads, axis=-1).astype(BF16)

    @pl.when(t == pl.num_programs(0) - 1)
    def _():
        c_out[...] = c_s[...]
        n_out[...] = n_s[...]
        m_out[...] = m_s[...]


def _mlstm_prompt(mqkv, sm, mo, gate_b_tile, L=64):
    B, T, _ = mqkv.shape
    assert T % L == 0 and L <= LANES
    blk = lambda w: pl.BlockSpec((B, L, w), lambda t: (0, t, 0))
    st = lambda shape: pl.BlockSpec(shape, lambda t: (0,) * len(shape))
    shapes = [(B * MH, HD, HD), (B * MH, 1, HD), (B * MH, 1, LANES)]
    return pl.pallas_call(
        functools.partial(_mlstm_prompt_kernel, B=B, L=L),
        grid=(T // L,),
        in_specs=[blk(3 * MLSTM_W), blk(LANES), blk(MLSTM_W), _const_spec((1, LANES))],
        out_specs=[blk(MLSTM_W)] + [st(s) for s in shapes],
        out_shape=[jax.ShapeDtypeStruct((B, T, MLSTM_W), BF16)] + [jax.ShapeDtypeStruct(s, F32) for s in shapes],
        scratch_shapes=[pltpu.VMEM(s, F32) for s in shapes],
        compiler_params=_params("arbitrary"),
        name="mlstm_prompt",
    )(mqkv, sm, mo, gate_b_tile)


FF_CHUNK = 256


def _outffn_prompt_kernel(x_ref, on_ref, om_ref, op_ref, gt1_ref, sh2_ref, sc2_ref, gt2_ref,
                          g2_ref, fg_ref, wout_ref, wup_ref, cw_ref, cb_ref, wdn_ref,
                          y_ref, fs_ref, prev_ref, es_ref, *, tm, d_ff, final):
    t = pl.program_id(1)
    mix = (_dot(on_ref[0], wout_ref[0:NSA_W, :])
           + _dot(om_ref[0], wout_ref[NSA_W:NSA_W + MLSTM_W, :])
           + _dot(op_ref[0], wout_ref[NSA_W + MLSTM_W:, :]))
    x1 = x_ref[0] + gt1_ref[0] * mix
    h2 = _norm_mod(x1, g2_ref[...], sc2_ref[0], sh2_ref[0]).astype(BF16)

    @pl.when(t == 0)
    def _():
        prev_ref[...] = jnp.zeros(prev_ref.shape, F32)

    f = jnp.zeros(x1.shape, F32)
    w = FF_CHUNK
    for c in range(d_ff // w):
        for half, off in ((0, c * w), (1, d_ff + c * w)):
            es_ref[0:SUBLANES, half * w:(half + 1) * w] = prev_ref[:, off:off + w]
            es_ref[SUBLANES:SUBLANES + tm, half * w:(half + 1) * w] = _dot(h2, wup_ref[:, off:off + w])
            prev_ref[:, off:off + w] = es_ref[tm:tm + SUBLANES, half * w:(half + 1) * w]
        ys = []
        for half, off in ((0, c * w), (1, d_ff + c * w)):
            y = cb_ref[:, off:off + w]
            for j in range(FFN_CONV):
                y = y + cw_ref[j:j + 1, off:off + w] * es_ref[pl.ds(SUBLANES - (FFN_CONV - 1) + j, tm), half * w:(half + 1) * w]
            ys.append(y)
        act = ys[0] * jax.nn.sigmoid(ys[0]) * ys[1]
        f = f + _dot(act.astype(BF16), wdn_ref[c * w:(c + 1) * w, :])
    x2 = x1 + gt2_ref[0] * f
    y_ref[0] = _rmsnorm(x2, fg_ref[...]) if final else x2
    fs_ref[0] = prev_ref[...]


def _outffn_prompt(x, on, om, op, gt1, sh2, sc2, gt2, g2, fg, w_out, w_up, conv_w, conv_b, w_dn, final, tm=256):
    B, T, D = x.shape
    d_ff = w_dn.shape[0]
    assert T % tm == 0 and d_ff % FF_CHUNK == 0
    row = lambda w: pl.BlockSpec((1, tm, w), lambda b, t: (b, t, 0))
    mod = pl.BlockSpec((1, 1, D), lambda b, t: (b, 0, 0))
    return pl.pallas_call(
        functools.partial(_outffn_prompt_kernel, tm=tm, d_ff=d_ff, final=final),
        grid=(B, T // tm),
        in_specs=[row(D), row(NSA_W), row(MLSTM_W), row(POOL_W), mod, mod, mod, mod,
                  _const_spec((1, D)), _const_spec((1, D)), _const_spec(w_out.shape), _const_spec(w_up.shape),
                  _const_spec(conv_w.shape), _const_spec(conv_b.shape), _const_spec(w_dn.shape)],
        out_specs=[row(D), pl.BlockSpec((1, SUBLANES, 2 * d_ff), lambda b, t: (b, 0, 0))],
        out_shape=[jax.ShapeDtypeStruct((B, T, D), F32), jax.ShapeDtypeStruct((B, SUBLANES, 2 * d_ff), F32)],
        scratch_shapes=[pltpu.VMEM((SUBLANES, 2 * d_ff), F32), pltpu.VMEM((SUBLANES + tm, 2 * FF_CHUNK), F32)],
        compiler_params=_params("arbitrary", "arbitrary"),
        name="outffn_prompt",
    )(x, on, om, op, gt1, sh2, sc2, gt2, g2, fg, w_out, w_up, conv_w, conv_b, w_dn)


def _prep_layer(l, w):
    gate_b = jnp.zeros((1, LANES), F32).at[0, SM_IG:SM_IG + 2 * MH].set(w["mlstm_gate_b"][l])
    pw = w["pool_w"][l]
    z = jnp.zeros((HD, HD), F32)
    pool_bd = jnp.block([[pw[i] if i == j else z for j in range(4)] for i in range(4)]).astype(BF16)
    wn = w["w_out"][l][:NSA_W].reshape(NSA_HEADS, HD, -1)
    zn = jnp.zeros_like(wn[:GRP])
    w_nsa_pad = jnp.concatenate([jnp.concatenate([wn[:GRP], zn], axis=1),
                                 jnp.concatenate([zn, wn[GRP:]], axis=1)], axis=0)
    return dict(
        w_nsa_pad=w_nsa_pad.reshape(NSA_HEADS * LANES, -1).astype(BF16),
        g1=w["norm1_g"][l][None], g2=w["norm2_g"][l][None],
        w_b=_prep_w_in(w["w_in"][l]), pool_bd=pool_bd, pool_scale=w["pool_scale"][l][None],
        cw=_prep_compress(w["nsa_cmp_pos"][l], w["nsa_cmp_w"][l], w["nsa_cmp_b"][l]),
        gate_b=gate_b, w_out=w["w_out"][l].astype(BF16), w_up=w["ffn_w_up"][l].astype(BF16),
        conv_w=w["ffn_conv_w"][l], conv_b=w["ffn_conv_b"][l][None], w_dn=w["ffn_w_down"][l].astype(BF16),
        fg=w["final_g"][None])


def _prompt_layer(x, mod, lw, final):
    B, T, D = x.shape
    sh1, sc1, gt1, sh2, sc2, gt2 = [mod[:, :, D * i:D * (i + 1)] for i in range(6)]
    (q, nk, nv, ks, vs, kw, vw, wkv, mqkv, mo, op, sm, ps) = _inproj_prompt(
        x, lw["g1"], sh1, sc1, lw["w_b"], lw["pool_bd"], lw["pool_scale"])
    kc, vc = _compress_prompt(nk, nv, lw["cw"])
    on = _nsa_prompt(q, sm, kc, vc, ks, vs, kw, vw)
    om, c_st, n_st, m_st = _mlstm_prompt(mqkv, sm, mo, lw["gate_b"])
    y, fs = _outffn_prompt(x, on, om, op, gt1, sh2, sc2, gt2, lw["g2"], lw["fg"], lw["w_out"],
                           lw["w_up"], lw["conv_w"], lw["conv_b"], lw["w_dn"], final)
    wb = wkv.shape[1]
    states = (nk.reshape(B, T, 2, HKV, HD), nv.reshape(B, T, 2, HKV, HD),
              wkv[:, :, :KV_W].reshape(B, wb, HKV, HD), wkv[:, :, KV_W:].reshape(B, wb, HKV, HD),
              c_st.reshape(B, MH, HD, HD), n_st.reshape(B, MH, HD), m_st[:, 0, 0].reshape(B, MH),
              ps[:, 2 * SUBLANES - POOL_STATE:], fs[:, SUBLANES - (FFN_CONV - 1):])
    return y, states


def _inproj_sample_kernel(x_ref, g_ref, sh_ref, sc_ref, w_ref, pw_ref, pscale_ref, prev_ref,
                          q_ref, nk_ref, nv_ref, wkv_ref, mqkv_ref, mo_ref, op_ref, sm_ref, pu_ref, *, pos0):
    h = _norm_mod(x_ref[...], g_ref[...], sc_ref[...], sh_ref[...]).astype(BF16)

    def seg(a, b):
        return _dot(h, w_ref[:, a:b])

    q_ref[...] = seg(C_Q, C_NK) * Q_SCALE
    nk_ref[...] = seg(C_NK, C_NV)
    nv_ref[...] = seg(C_NV, C_KS)
    wkv_ref[...] = seg(C_WKV, C_MQKV)
    mqkv_ref[...] = seg(C_MQKV, C_MO)
    mo_ref[...] = seg(C_MO, C_PU)
    sm_ref[...] = seg(C_SM, C_END)
    pu = seg(C_PU, C_SM)
    pu_ref[...] = pu
    acc = pu
    sums = {}
    for i in range(1, POOL_STATE + 1):
        acc = acc + prev_ref[:, POOL_STATE - i, :]
        if i + 1 in POOL_WINDOWS:
            sums[i + 1] = acc
    cnt = {w: float(min(w, pos0 + 1)) for w in POOL_WINDOWS}
    op_ref[...] = _pool_mix(pu, sums, cnt, pw_ref, pscale_ref)


def _inproj_sample(l, x, g1, sh1, sc1, w_b, pool_bd, pool_scale, state_pool, pos0):
    DB, D = x.shape
    full = lambda shape: pl.BlockSpec(shape, lambda i: (0,) * len(shape))
    widths = [8 * LANES, 256, 256, 256, 3 * MLSTM_W, MLSTM_W, POOL_W, LANES, POOL_W]
    return pl.pallas_call(
        functools.partial(_inproj_sample_kernel, pos0=pos0),
        grid=(1,),
        in_specs=[full((DB, D)), full((1, D)), full((DB, D)), full((DB, D)), full(w_b.shape),
                  full(pool_bd.shape), full((1, POOL_W)),
                  pl.BlockSpec((None, DB, POOL_STATE, POOL_W), lambda i: (l, 0, 0, 0))],
        out_specs=[full((DB, w)) for w in widths],
        out_shape=[jax.ShapeDtypeStruct((DB, w), F32) for w in widths],
        compiler_params=_params("arbitrary"),
        name="inproj_sample",
    )(x, g1, sh1, sc1, w_b, pool_bd, pool_scale, state_pool)


PAGES_PER_STEP = 16


def _compress_sample_kernel(pt_ref, *refs, npg):
    del pt_ref
    k_refs, v_refs = refs[:npg], refs[npg:2 * npg]
    wpk_ref, wpv_ref, ak_ref, bk_ref, av_ref, bv_ref = refs[2 * npg:]
    seg = SUBLANES

    def halves(page_ref, wp_ref):
        x = page_ref[...].reshape(page_ref.shape[0] // CMP_STRIDE, CMP_STRIDE, KV_W)
        a = jnp.sum(x * wp_ref[0:CMP_STRIDE, :][None], axis=1)
        b = jnp.sum(x * wp_ref[CMP_STRIDE:CMP_LEN, :][None], axis=1)
        return a, b

    for i in range(npg):
        n = k_refs[i].shape[0] // CMP_STRIDE
        a, b = halves(k_refs[i], wpk_ref)
        ak_ref[0, i * n:(i + 1) * n, :] = a
        bk_ref[0, i * n:(i + 1) * n, :] = b
        a, b = halves(v_refs[i], wpv_ref)
        av_ref[0, i * n:(i + 1) * n, :] = a
        bv_ref[0, i * n:(i + 1) * n, :] = b


def _compress_sample(l, cache_k, cache_v, page_table, cw):
    DB, n_pages = page_table.shape
    page = cache_k.shape[2]
    npg = min(PAGES_PER_STEP, n_pages)
    assert n_pages % npg == 0 and page % CMP_STRIDE == 0
    spp = page // CMP_STRIDE
    nseg = n_pages * spp

    def page_spec(i):
        return pl.BlockSpec((None, None, page, KV_W), lambda b, c, pt: (l, pt[b, c * npg + i], 0, 0))

    out = pl.BlockSpec((1, npg * spp, KV_W), lambda b, c, pt: (b, c, 0))
    wp = pl.BlockSpec((CMP_LEN, KV_W), lambda b, c, pt: (0, 0))
    grid_spec = pltpu.PrefetchScalarGridSpec(
        num_scalar_prefetch=1, grid=(DB, n_pages // npg),
        in_specs=[page_spec(i) for i in range(npg)] * 2 + [wp, wp],
        out_specs=[out] * 4)
    return pl.pallas_call(
        functools.partial(_compress_sample_kernel, npg=npg),
        grid_spec=grid_spec,
        out_shape=[jax.ShapeDtypeStruct((DB, nseg, KV_W), F32)] * 4,
        compiler_params=_params("arbitrary", "arbitrary"),
        name="compress_sample",
    )(page_table, *([cache_k] * npg), *([cache_v] * npg), cw["wpk"], cw["wpv"])


def _nsa_sample_a_kernel(q_ref, ak_ref, bk_ref, av_ref, bv_ref, nk_ref, nv_ref, wkv_ref, wink_ref, winv_ref,
                         wpk_ref, wpv_ref, wk_ref, wv_ref, bkb_ref, bvb_ref, m_ref,
                         idx_ref, ocw_ref, wko_ref, wvo_ref, *, past, ns):
    ncs = ak_ref.shape[1]
    q8 = q_ref[0].astype(BF16)
    rown = lax.broadcasted_iota(jnp.int32, (ncs, 1), 0)

    def comp(a_ref, b_ref, new_row, wp_ref, w_ref, bias_ref):
        b_new = new_row * wp_ref[CMP_STRIDE:CMP_STRIDE + 1, :]
        pooled = a_ref[0] + jnp.where(rown == ncs - 1, b_new, pltpu.roll(b_ref[0], ncs - 1, 0))
        return (_dot(pooled.astype(BF16), w_ref[...]) + bias_ref[...]).astype(BF16)

    kc = comp(ak_ref, bk_ref, nk_ref[0][:, :KV_W], wpk_ref, wk_ref, bkb_ref)
    vc = comp(av_ref, bv_ref, nv_ref[0][:, :KV_W], wpv_ref, wv_ref, bvb_ref)
    n_idx = lax.broadcasted_iota(jnp.int32, (1, ncs), 1)
    p = _masked_softmax(_dot_nt(q8, kc), (CMP_STRIDE * n_idx + CMP_LEN - 1) <= past)
    o_c = _dot(p.astype(BF16), vc)

    rowi = lax.broadcasted_iota(jnp.int32, p.shape, 0)
    p0 = jnp.sum(p[0:GRP], axis=0, keepdims=True)
    p1 = jnp.sum(p[GRP:2 * GRP], axis=0, keepdims=True)
    imp = _dot_split3(jnp.where(rowi == 0, p0, jnp.where(rowi == 1, p1, 0.0)), m_ref[...])
    lane = lax.broadcasted_iota(jnp.int32, imp.shape, 1)
    cur = past // SLC_BLOCK
    valid = lane * SLC_BLOCK <= past
    forced = (lane == 0) | (lane == cur) | (lane == cur - 1)
    score = jnp.where(valid, jnp.where(forced, FORCE, imp), -1.0)
    score = jnp.where(lane < ns, score, -jnp.inf)
    lane_o = lax.broadcasted_iota(jnp.int32, (SUBLANES, LANES), 1)
    idx = jnp.full((SUBLANES, LANES), -1, jnp.int32)
    for r in range(min(N_SEL, ns)):
        mx = jnp.max(score, axis=-1, keepdims=True)
        first = jnp.min(jnp.where(score == mx, lane, 1 << 20), axis=-1, keepdims=True)
        idx = jnp.where(lane_o == r, jnp.where(mx >= 0.0, first, -1), idx)
        score = jnp.where(lane == first, -jnp.inf, score)
    idx_ref[0] = idx

    wb = wink_ref.shape[0]
    k_new = wkv_ref[0][:, :KV_W]
    v_new = wkv_ref[0][:, KV_W:]
    s_w = _dot_nt(q8, wink_ref[...].astype(BF16))
    dlt = wb - lax.broadcasted_iota(jnp.int32, (1, wb), 1)
    mask = (dlt < WINDOW) & (past - dlt >= 0)
    s_new = jnp.sum(q8.astype(F32) * k_new.astype(BF16).astype(F32), axis=-1, keepdims=True)
    s_w = jnp.where(mask, s_w, NEG)
    mx = jnp.maximum(jnp.max(s_w, axis=-1, keepdims=True), s_new)
    e = jnp.where(mask, jnp.exp(s_w - mx), 0.0)
    e_new = jnp.exp(s_new - mx)
    den = jnp.sum(e, axis=-1, keepdims=True) + e_new
    o_w = (_dot(e.astype(BF16), winv_ref[...].astype(BF16)) + e_new * v_new.astype(BF16).astype(F32)) / den
    ocw_ref[0] = jnp.concatenate([o_c, o_w], axis=-1)
    wko_ref[0, pl.ds(0, wb - 1), :] = wink_ref[pl.ds(1, wb - 1), :]
    wko_ref[0, pl.ds(wb - 1, 1), :] = k_new
    wvo_ref[0, pl.ds(0, wb - 1), :] = winv_ref[pl.ds(1, wb - 1), :]
    wvo_ref[0, pl.ds(wb - 1, 1), :] = v_new


def _nsa_sample_a(l, q3, halves, nk3, nv3, wkv3, win_k, win_v, cw, past):
    DB = q3.shape[0]
    ncs = halves[0].shape[1]
    wb = win_k.shape[2]
    tot = past + 1
    ns = -(-tot // SLC_BLOCK)
    nsl = -(-ns // LANES) * LANES
    m = _importance_matrix(ncs, ncs, ns, 0, nsl)
    per = lambda shape: pl.BlockSpec((1,) + shape, lambda b: (b,) + (0,) * len(shape))
    win = pl.BlockSpec((None, None, wb, KV_W), lambda b: (l, b, 0, 0))
    shapes = [(DB, SUBLANES, LANES), (DB, NSA_HEADS, 2 * KV_W), (DB, wb, KV_W), (DB, wb, KV_W)]
    dts = [jnp.int32, F32, F32, F32]
    return pl.pallas_call(
        functools.partial(_nsa_sample_a_kernel, past=past, ns=ns),
        grid=(DB,),
        in_specs=[per((NSA_HEADS, LANES))] + [per((ncs, KV_W))] * 4 + [per((1, 256))] * 3 + [win, win]
                 + [_const_spec((CMP_LEN, KV_W))] * 2 + [_const_spec((KV_W, KV_W))] * 2
                 + [_const_spec((1, KV_W))] * 2 + [_const_spec(m.shape)],
        out_specs=[per(s[1:]) for s in shapes],
        out_shape=[jax.ShapeDtypeStruct(s, d) for s, d in zip(shapes, dts)],
        compiler_params=_params("arbitrary"),
        name="nsa_sample_a",
    )(q3, *halves, nk3, nv3, wkv3, win_k, win_v, cw["wpk"], cw["wpv"], cw["wk"], cw["wv"],
      cw["bk"], cw["bv"], jnp.asarray(m, BF16))


def _nsa_sample_b_kernel(pt_ref, ix_ref, *refs, past, nbp, nsel):
    del pt_ref
    k_refs, v_refs = refs[:HKV * nsel], refs[HKV * nsel:2 * HKV * nsel]
    q_ref, ocw_ref, nk_ref, nv_ref, g_ref, o_ref = refs[2 * HKV * nsel:]
    b = pl.program_id(0)
    q8 = q_ref[0].astype(BF16)
    k_new = nk_ref[0][:, KV_W:].astype(BF16).astype(F32)
    v_new = nv_ref[0][:, KV_W:].astype(BF16).astype(F32)
    s_new = jnp.sum(q8.astype(F32) * k_new, axis=-1, keepdims=True)
    nk = nsel * SLC_BLOCK
    lane = lax.broadcasted_iota(jnp.int32, (1, nk), 1)
    row = lax.broadcasted_iota(jnp.int32, (NSA_HEADS, KV_W), 0)
    o_s = jnp.zeros((NSA_HEADS, KV_W), F32)
    for h in range(HKV):
        kb = jnp.concatenate([k_refs[h * nsel + r][...] for r in range(nsel)], axis=0).astype(BF16)
        vb = jnp.concatenate([v_refs[h * nsel + r][...] for r in range(nsel)], axis=0).astype(BF16)
        jv = jnp.full((1, nk), -1, jnp.int32)
        n_new = jnp.int32(0)
        for r in range(nsel):
            j = ix_ref[b, h, r]
            jv = jnp.where(lane // SLC_BLOCK == r, j, jv)
            n_new = n_new + (j == nbp).astype(jnp.int32)
        tok = jv * SLC_BLOCK + lane % SLC_BLOCK
        mask = (jv >= 0) & (jv < nbp) & (tok <= past)
        has_new = n_new > 0
        s = jnp.where(mask, _dot_nt(q8, kb), NEG)
        sn = jnp.where(has_new, s_new, NEG)
        mx = jnp.maximum(jnp.max(s, axis=-1, keepdims=True), sn)
        e = jnp.where(mask, jnp.exp(s - mx), 0.0)
        e_new = jnp.where(has_new, jnp.exp(sn - mx), 0.0)
        den = jnp.maximum(jnp.sum(e, axis=-1, keepdims=True) + e_new, jnp.finfo(jnp.float32).tiny)
        o_h = (_dot(e.astype(BF16), vb) + e_new * v_new) / den
        o_s = jnp.where(row // GRP == h, o_h, o_s)
    g = jax.nn.sigmoid(g_ref[0])
    ocw = ocw_ref[0]
    o = g[:, 0:1] * ocw[:, :KV_W] + g[:, 1:2] * o_s + g[:, 2:3] * ocw[:, KV_W:]
    lane_o = lax.broadcasted_iota(jnp.int32, (NSA_HEADS, KV_W), 1)
    o_ref[0] = jnp.where((lane_o // HD) == (row // GRP), o, 0.0)


def _nsa_sample_b(l, cache_k, cache_v, page_table, idx, q3, ocw, nk3, nv3, g3, past):
    DB = q3.shape[0]
    page = cache_k.shape[2]
    bpp = page // SLC_BLOCK
    nbp = past // SLC_BLOCK
    nsel = idx.shape[2]

    def blk_spec(h, r):
        def imap(b, pt, ix):
            j = jnp.clip(ix[b, h, r], 0, nbp - 1)
            return (l, pt[b, j // bpp], j % bpp, 1)
        return pl.BlockSpec((None, None, SLC_BLOCK, KV_W), imap)

    per = lambda shape: pl.BlockSpec((1,) + shape, lambda b, pt, ix: (b,) + (0,) * len(shape))
    kv_specs = [blk_spec(h, r) for h in range(HKV) for r in range(nsel)]
    grid_spec = pltpu.PrefetchScalarGridSpec(
        num_scalar_prefetch=2, grid=(DB,),
        in_specs=kv_specs * 2 + [per((NSA_HEADS, LANES)), per((NSA_HEADS, 2 * KV_W)), per((1, 256)), per((1, 256)),
                                 per((NSA_HEADS, 3))],
        out_specs=per((NSA_HEADS, KV_W)))
    n = HKV * nsel
    return pl.pallas_call(
        functools.partial(_nsa_sample_b_kernel, past=past, nbp=nbp, nsel=nsel),
        grid_spec=grid_spec,
        out_shape=jax.ShapeDtypeStruct((DB, NSA_HEADS, KV_W), F32),
        compiler_params=_params("arbitrary"),
        name="nsa_sample_b",
    )(page_table, idx, *([cache_k] * n), *([cache_v] * n), q3, ocw, nk3, nv3, g3)


SEQ_PER_STEP = 8


def _mlstm_sample_kernel(mqkv_ref, sm_ref, mo_ref, gb_ref, c_ref, n_ref, m_ref,
                         om_ref, c_out, n_out, m_out, *, nb):
    eye = (lax.broadcasted_iota(jnp.int32, (HD, HD), 0) == lax.broadcasted_iota(jnp.int32, (HD, HD), 1))
    lane = lax.broadcasted_iota(jnp.int32, (1, LANES), 1)

    def col(rowv):
        return jnp.sum(jnp.where(eye, jnp.broadcast_to(rowv, (HD, HD)), 0.0), axis=1, keepdims=True)

    for b in range(nb):
        g = sm_ref[b] + gb_ref[...]
        lf = _log_sigmoid(g)
        heads = []
        m_tile = jnp.zeros((1, LANES), F32)
        for hd in range(MH):
            ig = g[:, SM_IG + hd:SM_IG + hd + 1]
            a = lf[:, SM_FG + hd:SM_FG + hd + 1] + m_ref[b][:, hd:hd + 1]
            mt = jnp.maximum(a, ig)
            dm = jnp.exp(ig - mt)
            inter = jnp.exp(a - mt)
            q = mqkv_ref[b][:, HD * hd:HD * (hd + 1)]
            k = mqkv_ref[b][:, MLSTM_W + HD * hd:MLSTM_W + HD * (hd + 1)] * Q_SCALE
            v = mqkv_ref[b][:, 2 * MLSTM_W + HD * hd:2 * MLSTM_W + HD * (hd + 1)]
            c_prev = c_ref[b, hd]
            n_prev = n_ref[b][hd:hd + 1, :]
            qk = jnp.sum(q * k, axis=-1, keepdims=True) * dm
            q_c = jnp.sum(col(q) * c_prev, axis=0, keepdims=True)
            num = qk * v + inter * q_c
            den = qk + inter * jnp.sum(q * n_prev, axis=-1, keepdims=True)
            hout = num / jnp.maximum(jnp.abs(den), jnp.exp(-mt))
            heads.append(jax.nn.sigmoid(mo_ref[b][:, HD * hd:HD * (hd + 1)]) * hout)
            w = jnp.exp(ig - mt)
            decay = jnp.exp(a - mt)
            c_out[b, hd] = decay * c_prev + (w * col(k)) * v
            n_out[b, hd:hd + 1, :] = decay * n_prev + w * k
            m_tile = jnp.where(lane == hd, mt, m_tile)
        om_ref[b] = jnp.concatenate(heads, axis=-1)
        m_out[b] = m_tile


def _mlstm_sample(l, mqkv3, sm3, mo3, gate_b_tile, state_c, state_n, state_m4):
    DB = mqkv3.shape[0]
    nb = min(SEQ_PER_STEP, DB)
    assert DB % nb == 0
    per = lambda w: pl.BlockSpec((nb, 1, w), lambda i: (i, 0, 0))
    return pl.pallas_call(
        functools.partial(_mlstm_sample_kernel, nb=nb),
        grid=(DB // nb,),
        in_specs=[per(3 * MLSTM_W), per(LANES), per(MLSTM_W), _const_spec((1, LANES)),
                  pl.BlockSpec((None, nb, MH, HD, HD), lambda i: (l, i, 0, 0, 0)),
                  pl.BlockSpec((None, nb, MH, HD), lambda i: (l, i, 0, 0)),
                  pl.BlockSpec((None, nb, 1, MH), lambda i: (l, i, 0, 0))],
        out_specs=[per(MLSTM_W), pl.BlockSpec((nb, MH, HD, HD), lambda i: (i, 0, 0, 0)),
                   pl.BlockSpec((nb, MH, HD), lambda i: (i, 0, 0)), per(LANES)],
        out_shape=[jax.ShapeDtypeStruct((DB, 1, MLSTM_W), F32), jax.ShapeDtypeStruct((DB, MH, HD, HD), F32),
                   jax.ShapeDtypeStruct((DB, MH, HD), F32), jax.ShapeDtypeStruct((DB, 1, LANES), F32)],
        compiler_params=_params("arbitrary"),
        name="mlstm_sample",
    )(mqkv3, sm3, mo3, gate_b_tile, state_c, state_n, state_m4)


def _outffn_sample_kernel(x_ref, on_ref, om_ref, op_ref, gt1_ref, sh2_ref, sc2_ref, gt2_ref, g2_ref, fg_ref,
                          wn_ref, wout_ref, wup_ref, cw_ref, cb_ref, wdn_ref, prev_ref, y_ref, up_ref, *, d_ff, final):
    mix = (_dot(on_ref[...].astype(BF16), wn_ref[...])
           + _dot(om_ref[...].astype(BF16), wout_ref[NSA_W:NSA_W + MLSTM_W, :])
           + _dot(op_ref[...].astype(BF16), wout_ref[NSA_W + MLSTM_W:, :]))
    x1 = x_ref[...] + gt1_ref[...] * mix
    h2 = _norm_mod(x1, g2_ref[...], sc2_ref[...], sh2_ref[...]).astype(BF16)
    up = _dot(h2, wup_ref[...])
    up_ref[...] = up
    y = cb_ref[...] + cw_ref[FFN_CONV - 1:FFN_CONV, :] * up
    for j in range(FFN_CONV - 1):
        y = y + cw_ref[j:j + 1, :] * prev_ref[:, j, :]
    a, b = y[:, :d_ff], y[:, d_ff:]
    f = _dot((a * jax.nn.sigmoid(a) * b).astype(BF16), wdn_ref[...])
    x2 = x1 + gt2_ref[...] * f
    y_ref[...] = _rmsnorm(x2, fg_ref[...]) if final else x2


def _outffn_sample(l, x, on, om, op, gt1, sh2, sc2, gt2, g2, fg, w_nsa_pad, w_out, w_up, conv_w, conv_b, w_dn,
                   state_ffn, final):
    DB, D = x.shape
    d_ff = w_dn.shape[0]
    full = lambda shape: pl.BlockSpec(shape, lambda i: (0,) * len(shape))
    args = (x, on, om, op, gt1, sh2, sc2, gt2, g2, fg, w_nsa_pad, w_out, w_up, conv_w, conv_b, w_dn)
    return pl.pallas_call(
        functools.partial(_outffn_sample_kernel, d_ff=d_ff, final=final),
        grid=(1,),
        in_specs=[full(a.shape) for a in args]
                 + [pl.BlockSpec((None, DB, FFN_CONV - 1, 2 * d_ff), lambda i: (l, 0, 0, 0))],
        out_specs=[full((DB, D)), full((DB, 2 * d_ff))],
        out_shape=[jax.ShapeDtypeStruct((DB, D), F32), jax.ShapeDtypeStruct((DB, 2 * d_ff), F32)],
        compiler_params=_params("arbitrary"),
        name="outffn_sample",
    )(*args, state_ffn)


def _sample_layer(l, x, mod, lw, caches, final):
    DB, D = x.shape
    ck, cv, page_table, win_k, win_v, st_c, st_n, st_m, st_pool, st_ffn = caches
    past = page_table.shape[1] * ck.shape[2]
    sh1, sc1, gt1, sh2, sc2, gt2 = [mod[:, D * i:D * (i + 1)] for i in range(6)]
    q, nk, nv, wkv, mqkv, mo, op, sm, pu = _inproj_sample(
        l, x, lw["g1"], sh1, sc1, lw["w_b"], lw["pool_bd"], lw["pool_scale"], st_pool, past)
    r3 = lambda t: t[:, None, :]
    halves = _compress_sample(l, ck, cv, page_table, lw["cw"])
    q3 = q.reshape(DB, NSA_HEADS, LANES)
    idx, ocw, wk_new, wv_new = _nsa_sample_a(l, q3, halves, r3(nk), r3(nv), r3(wkv), win_k, win_v, lw["cw"], past)
    g3 = sm[:, :3 * NSA_HEADS].reshape(DB, NSA_HEADS, 3)
    on = _nsa_sample_b(l, ck, cv, page_table, idx[:, :HKV, :N_SEL], q3, ocw, r3(nk), r3(nv), g3, past)
    om, c_new, n_new, m_new = _mlstm_sample(l, r3(mqkv), r3(sm), r3(mo), lw["gate_b"], st_c, st_n,
                                            st_m.reshape(st_m.shape[0], DB, 1, MH))
    y, up = _outffn_sample(l, x, on.reshape(DB, NSA_HEADS * LANES), om[:, 0], op, gt1, sh2, sc2, gt2,
                           lw["g2"], lw["fg"], lw["w_nsa_pad"], lw["w_out"], lw["w_up"], lw["conv_w"],
                           lw["conv_b"], lw["w_dn"], st_ffn, final)
    wb = wk_new.shape[1]
    states = (nk.reshape(DB, 1, 2, HKV, HD), nv.reshape(DB, 1, 2, HKV, HD),
              wk_new.reshape(DB, wb, HKV, HD), wv_new.reshape(DB, wb, HKV, HD),
              c_new, n_new, m_new[:, 0, :MH],
              jnp.concatenate([st_pool[l][:, 1:], pu[:, None, :]], axis=1),
              jnp.concatenate([st_ffn[l][:, 1:], up[:, None, :]], axis=1))
    return y, states


def kernel(x_prompt, x_sample, cache_k, cache_v, cache_win_k, cache_win_v, state_mlstm_C, state_mlstm_n,
           state_mlstm_m, state_pool, state_ffn_conv, page_table, c_prompt, c_sample, norm1_g, norm2_g, ada_w,
           ada_b, w_in, nsa_cmp_pos, nsa_cmp_w, nsa_cmp_b, mlstm_gate_b, pool_w, pool_scale, w_out, ffn_w_up,
           ffn_conv_w, ffn_conv_b, ffn_w_down, final_g):
    B = x_prompt.shape[0]
    DB, DS, D = x_sample.shape
    assert DS == 1
    depth = w_in.shape[0]
    w = dict(norm1_g=norm1_g, norm2_g=norm2_g, w_in=w_in, nsa_cmp_pos=nsa_cmp_pos, nsa_cmp_w=nsa_cmp_w,
             nsa_cmp_b=nsa_cmp_b, mlstm_gate_b=mlstm_gate_b, pool_w=pool_w, pool_scale=pool_scale, w_out=w_out,
             ffn_w_up=ffn_w_up, ffn_conv_w=ffn_conv_w, ffn_conv_b=ffn_conv_b, ffn_w_down=ffn_w_down,
             final_g=final_g)
    mod = _ada_mod(jnp.concatenate([c_prompt, c_sample], axis=0), ada_w, ada_b)
    n_phys, page = cache_k.shape[1], cache_k.shape[2]
    ck = cache_k.reshape(depth, n_phys, page, 2 * KV_W)
    cv = cache_v.reshape(depth, n_phys, page, 2 * KV_W)
    wb = cache_win_k.shape[2]
    caches = (ck, cv, page_table, cache_win_k.reshape(depth, DB, wb, KV_W), cache_win_v.reshape(depth, DB, wb, KV_W),
              state_mlstm_C, state_mlstm_n, state_mlstm_m, state_pool, state_ffn_conv)
    xp, xs = x_prompt, x_sample[:, 0, :]
    acc_p = [[] for _ in range(9)]
    acc_s = [[] for _ in range(9)]
    for l in range(depth):
        lw = _prep_layer(l, w)
        final = l == depth - 1
        xp, st_p = _prompt_layer(xp, mod[l, :B][:, None, :], lw, final)
        xs, st_s = _sample_layer(l, xs, mod[l, B:], lw, caches, final)
        for a, v in zip(acc_p, st_p):
            a.append(v)
        for a, v in zip(acc_s, st_s):
            a.append(v)
    sp = [jnp.stack(a, axis=0) for a in acc_p]
    ss = [jnp.stack(a, axis=0) for a in acc_s]
    return (xp, xs[:, None, :], *sp, *ss)
```

```python
import functools

import numpy as np
import jax
import jax.numpy as jnp
from jax import lax
from jax.experimental import pallas as pl
from jax.experimental.pallas import tpu as pltpu

F32 = jnp.float32
BF16 = jnp.bfloat16

HD = 64
NSA_HEADS = 8
HKV = 2
GRP = NSA_HEADS // HKV
MH = 4
NSA_W = NSA_HEADS * HD
KV_W = HKV * HD
MLSTM_W = MH * HD
POOL_W = 4 * HD
CMP_STRIDE = 16
CMP_LEN = 32
SLC_BLOCK = 64
SLC_RATIO = SLC_BLOCK // CMP_STRIDE
N_SEL = 16
WINDOW = 512
POOL_WINDOWS = (2, 4, 8, 16)
POOL_STATE = 15
FFN_CONV = 3
RMS_EPS = 1e-6
NEG = -1e30
FORCE = 1e6
Q_SCALE = HD ** -0.5

LANES = 128
SUBLANES = 8
VMEM_LIMIT = 56 * 1024 * 1024

C_NK = 0
C_NV = C_NK + 256
C_KS = C_NV + 256
C_VS = C_KS + 256
C_VW = C_VS + 256
C_WKV = C_VW + 256
C_MQKV = C_WKV + 256
C_MO = C_MQKV + 3 * MLSTM_W
C_PU = C_MO + MLSTM_W
C_SM = C_PU + POOL_W
C_END = C_SM + LANES
SM_IG = 3 * NSA_HEADS
SM_FG = SM_IG + MH


def _dot(a, b):
    return jnp.dot(a, b, preferred_element_type=F32)


def _dot_nt(a, b):
    return lax.dot_general(a, b, (((1,), (1,)), ((), ())), preferred_element_type=F32)


def _dot_tn(a, b):
    return lax.dot_general(a, b, (((0,), (0,)), ((), ())), preferred_element_type=F32)


def _dot_split3(x, m):
    x1 = x.astype(BF16)
    r1 = x - x1.astype(F32)
    x2 = r1.astype(BF16)
    x3 = (r1 - x2.astype(F32)).astype(BF16)
    return _dot(x1, m) + _dot(x2, m) + _dot(x3, m)


def _masked_softmax(s, mask):
    s = jnp.where(mask, s, NEG)
    e = jnp.where(mask, jnp.exp(s - jnp.max(s, axis=-1, keepdims=True)), 0.0)
    return e / jnp.maximum(jnp.sum(e, axis=-1, keepdims=True), jnp.finfo(jnp.float32).tiny)


def _norm_mod(x, g, sc, sh):
    ms = jnp.mean(x * x, axis=-1, keepdims=True)
    return (x * lax.rsqrt(ms + RMS_EPS) * g) * (1.0 + sc) + sh


def _rmsnorm(x, g):
    ms = jnp.mean(x * x, axis=-1, keepdims=True)
    return x * lax.rsqrt(ms + RMS_EPS) * g


def _log_sigmoid(x):
    return jnp.minimum(x, 0.0) - jnp.log1p(jnp.exp(-jnp.abs(x)))


def _cumsum_rows(x):
    n = x.shape[0]
    row = lax.broadcasted_iota(jnp.int32, x.shape, 0)
    sh = 1
    while sh < n:
        x = x + jnp.where(row >= sh, pltpu.roll(x, sh, 0), 0.0)
        sh *= 2
    return x


def _const_spec(shape):
    nd = len(shape)
    return pl.BlockSpec(shape, lambda *_: (0,) * nd, pipeline_mode=pl.Buffered(1))


def _params(*sem):
    return pltpu.CompilerParams(dimension_semantics=sem, vmem_limit_bytes=VMEM_LIMIT)


def _ada_kernel(c_ref, w_ref, b_ref, o_ref):
    c = c_ref[...]
    s = c * jax.nn.sigmoid(c)
    o_ref[0] = _dot(s.astype(BF16), w_ref[0]) + b_ref[0]


def _ada_mod(c_all, ada_w, ada_b):
    depth, d, n = ada_w.shape
    rows = c_all.shape[0]
    tn = 1536
    return pl.pallas_call(
        _ada_kernel,
        grid=(depth, n // tn),
        in_specs=[pl.BlockSpec((rows, d), lambda l, j: (0, 0)),
                  pl.BlockSpec((1, d, tn), lambda l, j: (l, 0, j)),
                  pl.BlockSpec((1, 1, tn), lambda l, j: (l, 0, j))],
        out_specs=pl.BlockSpec((1, rows, tn), lambda l, j: (l, 0, j)),
        out_shape=jax.ShapeDtypeStruct((depth, rows, n), F32),
        compiler_params=_params("arbitrary", "arbitrary"),
        name="ada_mod",
    )(c_all, ada_w.astype(BF16), ada_b.reshape(depth, 1, n))


def _prep_w_in(w):
    d = w.shape[0]
    o = 0
    nq = w[:, o:o + NSA_W]; o += NSA_W
    nkv = w[:, o:o + 6 * KV_W]; o += 6 * KV_W
    ng = w[:, o:o + 3 * NSA_HEADS]; o += 3 * NSA_HEADS
    mqkv = w[:, o:o + 3 * MLSTM_W]; o += 3 * MLSTM_W
    mif = w[:, o:o + 2 * MH]; o += 2 * MH
    mo = w[:, o:o + MLSTM_W]; o += MLSTM_W
    pu = w[:, o:o + POOL_W]
    k_cmp, v_cmp, k_slc, v_slc, k_win, v_win = [nkv[:, KV_W * i:KV_W * (i + 1)] for i in range(6)]
    z = jnp.zeros((d, HD), w.dtype)
    h0 = lambda t: t[:, :HD]
    h1 = lambda t: t[:, HD:]
    cols = [
        k_cmp, k_slc, v_cmp, v_slc,
        h0(k_slc), z, z, h1(k_slc),
        h0(v_slc), z, h1(v_slc), z,
        h0(v_win), z, h1(v_win), z,
        k_win, v_win, mqkv, mo, pu,
        ng, mif, jnp.zeros((d, LANES - 3 * NSA_HEADS - 2 * MH), w.dtype)]
    out = jnp.concatenate(cols, axis=1).astype(BF16)
    assert out.shape[1] == C_END
    return out


def _pool_mix(pu, sums, cnt, pw_ref, pscale_ref):
    lane = lax.broadcasted_iota(jnp.int32, pu.shape, 1)
    grp = lane // HD
    mean = jnp.where(grp == 0, sums[2] / cnt[2],
                     jnp.where(grp == 1, sums[4] / cnt[4],
                               jnp.where(grp == 2, sums[8] / cnt[8], sums[16] / cnt[16])))
    d = mean - pu
    return _dot(d.astype(BF16), pw_ref[...]) * pscale_ref[...]


def _split2(x):
    hi = x.astype(BF16)
    return hi, (x - hi.astype(F32)).astype(BF16)


def _dot_split(a, b):
    return _dot(a[0], b[0]) + _dot(a[1], b[0]) + _dot(a[0], b[1])


def _inproj_prompt_kernel(x_ref, g_ref, sh_ref, sc_ref, w_ref, wqh_ref, wql_ref, pw_ref, pscale_ref,
                          q_ref, nk_ref, nv_ref, ks_ref, vs_ref, kw_ref, vw_ref, wkv_ref,
                          mqkv_ref, mo_ref, op_ref, sm_ref, ps_ref, zs_ref, *, tm):
    t = pl.program_id(1)
    hf = _norm_mod(x_ref[0], g_ref[...], sc_ref[0], sh_ref[0])
    h, h_lo = _split2(hf)

    def seg(a, b):
        return _dot(h, w_ref[:, a:b])

    q_ref[0] = _dot_split((h, h_lo), (wqh_ref[...], wql_ref[...])) * Q_SCALE
    nk_ref[0] = seg(C_NK, C_NV)
    nv_ref[0] = seg(C_NV, C_KS)
    row = t * tm + lax.broadcasted_iota(jnp.int32, (tm, 2 * LANES), 0)
    lane = lax.broadcasted_iota(jnp.int32, (tm, 2 * LANES), 1)
    is_aux = (lane >= HD) & (lane < 3 * HD)
    aux = jnp.where(lane < LANES, lane - HD, lane - LANES)
    onehot = jnp.where(is_aux & (aux == row // SLC_BLOCK), 1.0, 0.0)
    ks_ref[0] = (seg(C_KS, C_VS) + onehot).astype(BF16)
    ones_col = jnp.where(lane % LANES == HD, 1.0, 0.0)
    vs_ref[0] = (seg(C_VS, C_VW) + ones_col).astype(BF16)
    vw_ref[0] = (seg(C_VW, C_WKV) + ones_col).astype(BF16)
    wkv = seg(C_WKV, C_MQKV)
    wkv_ref[0] = wkv
    kw_ref[0] = wkv[:, :KV_W].astype(BF16)
    mqkv_ref[0] = seg(C_MQKV, C_MO)
    mo_ref[0] = seg(C_MO, C_PU)
    sm_ref[0] = seg(C_SM, C_END)

    pu = seg(C_PU, C_SM)
    halo = 2 * SUBLANES

    @pl.when(t == 0)
    def _():
        zs_ref[0:halo, :] = jnp.zeros((halo, POOL_W), F32)

    @pl.when(t > 0)
    def _():
        zs_ref[0:halo, :] = zs_ref[tm:tm + halo, :]

    zs_ref[halo:halo + tm, :] = pu
    acc = pu
    sums = {}
    for i in range(1, POOL_STATE + 1):
        acc = acc + zs_ref[pl.ds(halo - i, tm), :]
        if i + 1 in POOL_WINDOWS:
            sums[i + 1] = acc
    pos1 = (t * tm + lax.broadcasted_iota(jnp.int32, (tm, 1), 0) + 1).astype(F32)
    cnt = {w: jnp.minimum(float(w), pos1) for w in POOL_WINDOWS}
    op_ref[0] = _pool_mix(pu, sums, cnt, pw_ref, pscale_ref).astype(BF16)
    ps_ref[0] = zs_ref[tm:tm + halo, :]


def _inproj_prompt(x, g1, sh1, sc1, w_b, w_q2, pool_bd, pool_scale, tm=256):
    B, T, D = x.shape
    nT = T // tm
    assert T % tm == 0 and tm >= 2 * SUBLANES and WINDOW % tm == 0
    nwin = WINDOW // tm
    row = lambda w: pl.BlockSpec((1, tm, w), lambda b, t: (b, t, 0))
    mod = pl.BlockSpec((1, 1, D), lambda b, t: (b, 0, 0))
    outs = [
        (row(NSA_W), (B, T, NSA_W), F32),
        (row(256), (B, T, 256), F32),
        (row(256), (B, T, 256), F32),
        (row(256), (B, T, 256), BF16),
        (row(256), (B, T, 256), BF16),
        (row(KV_W), (B, T, KV_W), BF16),
        (row(256), (B, T, 256), BF16),
        (pl.BlockSpec((1, tm, 256), lambda b, t: (b, jnp.maximum(t - (nT - nwin), 0), 0)),
         (B, WINDOW, 256), F32),
        (row(3 * MLSTM_W), (B, T, 3 * MLSTM_W), F32),
        (row(MLSTM_W), (B, T, MLSTM_W), F32),
        (row(POOL_W), (B, T, POOL_W), BF16),
        (row(LANES), (B, T, LANES), F32),
        (pl.BlockSpec((1, 2 * SUBLANES, POOL_W), lambda b, t: (b, 0, 0)), (B, 2 * SUBLANES, POOL_W), F32),
    ]
    return pl.pallas_call(
        functools.partial(_inproj_prompt_kernel, tm=tm),
        grid=(B, nT),
        in_specs=[pl.BlockSpec((1, tm, D), lambda b, t: (b, t, 0)),
                  _const_spec((1, D)), mod, mod,
                  _const_spec(w_b.shape), _const_spec(w_q2[0].shape), _const_spec(w_q2[1].shape),
                  _const_spec(pool_bd.shape), _const_spec((1, POOL_W))],
        out_specs=[o[0] for o in outs],
        out_shape=[jax.ShapeDtypeStruct(o[1], o[2]) for o in outs],
        scratch_shapes=[pltpu.VMEM((2 * SUBLANES + tm, POOL_W), F32)],
        compiler_params=_params("arbitrary", "arbitrary"),
        name="inproj_prompt",
    )(x, g1, sh1, sc1, w_b, w_q2[0], w_q2[1], pool_bd, pool_scale)


def _compress_prompt_kernel(k_ref, v_ref, wpk_ref, wpv_ref, wk_ref, wkl_ref, wv_ref, bk_ref, bv_ref,
                            kc_ref, vc_ref, *, nseg):
    def pooled(src_ref, wp_ref):
        a = jnp.zeros((nseg, KV_W), F32)
        b = jnp.zeros((nseg, KV_W), F32)
        for j in range(CMP_STRIDE):
            xj = src_ref[0, pl.ds(j, nseg, stride=CMP_STRIDE), :]
            a = a + xj * wp_ref[j:j + 1, :]
            b = b + xj * wp_ref[CMP_STRIDE + j:CMP_STRIDE + j + 1, :]
        return a + pltpu.roll(b, nseg - 1, 0)

    kc_ref[0] = _dot_split(_split2(pooled(k_ref, wpk_ref)), (wk_ref[...], wkl_ref[...])) + bk_ref[...]
    vc_ref[0] = (_dot(pooled(v_ref, wpv_ref).astype(BF16), wv_ref[...]) + bv_ref[...]).astype(BF16)


def _compress_prompt(nk, nv, cw):
    B, T, _ = nk.shape
    nseg = T // CMP_STRIDE
    src = pl.BlockSpec((1, T, KV_W), lambda b: (b, 0, 0))
    return pl.pallas_call(
        functools.partial(_compress_prompt_kernel, nseg=nseg),
        grid=(B,),
        in_specs=[src, src, _const_spec((CMP_LEN, KV_W)), _const_spec((CMP_LEN, KV_W)),
                  _const_spec((KV_W, KV_W)), _const_spec((KV_W, KV_W)), _const_spec((KV_W, 2 * KV_W)),
                  _const_spec((1, KV_W)), _const_spec((1, 2 * KV_W))],
        out_specs=[pl.BlockSpec((1, nseg, KV_W), lambda b: (b, 0, 0)),
                   pl.BlockSpec((1, nseg, 2 * KV_W), lambda b: (b, 0, 0))],
        out_shape=[jax.ShapeDtypeStruct((B, nseg, KV_W), F32),
                   jax.ShapeDtypeStruct((B, nseg, 2 * KV_W), BF16)],
        compiler_params=_params("arbitrary"),
        name="compress_prompt",
    )(nk, nv, cw["wpk"], cw["wpv"], cw["wk"], cw["wk_lo"], cw["wv_dup"], cw["bk"], cw["bv_dup"])


def _prep_compress(cmp_pos, cmp_w, cmp_b, page):
    tile2 = lambda t: jnp.concatenate([t, t], axis=-1)
    z = jnp.zeros((HD, HD), F32)
    wk, wv = cmp_w[0], cmp_w[1]
    wk_bd = jnp.block([[wk, z], [z, wk]])
    wv_dup = jnp.block([[wv, z, z, z], [z, z, wv, z]])

    def pos_t(p):
        reps = page // CMP_STRIDE
        return jnp.concatenate([jnp.tile(p[:CMP_STRIDE].T, (1, reps)), jnp.tile(p[CMP_STRIDE:].T, (1, reps))], axis=0)

    return dict(
        wpk=tile2(cmp_pos[0]), wpv=tile2(cmp_pos[1]),
        wk=wk_bd.astype(BF16), wk_lo=(wk_bd - wk_bd.astype(BF16).astype(F32)).astype(BF16),
        wv_dup=wv_dup.astype(BF16),
        bk=tile2(cmp_b[0])[None, :],
        bv_dup=jnp.concatenate([cmp_b[1], jnp.zeros((HD,), F32)] * 2)[None, :],
        wpk_t=pos_t(cmp_pos[0]), wpv_t=pos_t(cmp_pos[1]),
        wpk_b0=cmp_pos[0][CMP_STRIDE][:, None], wpv_b0=cmp_pos[1][CMP_STRIDE][:, None],
        wk_t=wk.T.astype(BF16), wv_t=wv.T.astype(BF16),
        bk_col=cmp_b[0][:, None], bv_col=cmp_b[1][:, None])


def _importance_matrix(nc_rows, nc_valid, ns, lane_off, width):
    m = np.zeros((nc_rows, width), np.float32)
    for j in range(ns):
        for n in range(SLC_RATIO * j - 1, SLC_RATIO * j + SLC_RATIO):
            if 0 <= n < nc_valid:
                m[n, lane_off + j] = 1.0
    return m


def _select_blocks_t(st_ref, h, ns):
    nb, nq = st_ref.shape[1], st_ref.shape[2]
    ngrp = nb // SUBLANES
    groups = [st_ref[h, SUBLANES * r:SUBLANES * (r + 1), :] for r in range(ngrp)]
    ranks = [jnp.zeros((SUBLANES, nq), F32) for _ in range(ngrp)]
    jrow = lax.broadcasted_iota(jnp.int32, (SUBLANES, nq), 0)
    for jp in range(min(ns, nb)):
        row = jnp.broadcast_to(st_ref[h, pl.ds(jp, 1), :], (SUBLANES, nq))
        for r in range(ngrp):
            ge = jnp.where(row >= groups[r], 1.0, 0.0)
            gt = jnp.where(row > groups[r], 1.0, 0.0)
            if jp < SUBLANES * r:
                inc = ge
            elif jp >= SUBLANES * (r + 1):
                inc = gt
            else:
                inc = jnp.where(jrow + SUBLANES * r > jp, ge, gt)
            ranks[r] = ranks[r] + inc
    return jnp.concatenate(
        [jnp.where((ranks[r] < N_SEL) & (groups[r] >= 0.0), 0.0, NEG) for r in range(ngrp)], axis=0)


def _nsa_prompt_kernel(q_ref, sm_ref, kc_ref, vc_ref, ks_ref, vs_ref, kw_ref, vw_ref, mt_ref,
                       o_ref, acc_ref, m_ref, qa_ref, st_ref, *, T, tk, qb):
    i = pl.program_id(1)
    ns = T // SLC_BLOCK
    nc = kc_ref.shape[1]
    rows = GRP * qb
    qpos = i * qb + lax.broadcasted_iota(jnp.int32, (qb, 1), 0)
    qpos4 = jnp.concatenate([qpos] * GRP, axis=0)
    qpos_t = i * qb + lax.broadcasted_iota(jnp.int32, (HD, qb), 1)
    jb_t = lax.broadcasted_iota(jnp.int32, (HD, qb), 0)
    sig = jax.nn.sigmoid(sm_ref[0])
    lane = lax.broadcasted_iota(jnp.int32, (qb, LANES), 1)
    wk = WINDOW + qb
    n_idx = lax.broadcasted_iota(jnp.int32, (1, nc), 1)
    cmp_mask = (CMP_STRIDE * n_idx + CMP_LEN - 1) <= qpos4
    wstart = pl.multiple_of(jnp.maximum(i - WINDOW // qb, 0) * qb, qb)
    dlt = qpos - (wstart + lax.broadcasted_iota(jnp.int32, (1, wk), 1))
    wbias = jnp.where((dlt >= 0) & (dlt < WINDOW), 0.0, NEG)
    cur_t = qpos_t // SLC_BLOCK
    valid_t = (jb_t * SLC_BLOCK <= qpos_t) & (jb_t < ns)
    forced_t = (jb_t == 0) | (jb_t == cur_t) | (jb_t == cur_t - 1)
    zpad = jnp.zeros((LANES - HD, qb), F32)
    kc_hi, kc_lo = _split2(kc_ref[0])

    o_cs, o_ws = [], []
    for h in range(HKV):
        parts = []
        for g in range(GRP):
            hd = GRP * h + g
            src = q_ref[0, :, LANES * (hd // 2):LANES * (hd // 2 + 1)]
            if hd % 2 != h:
                src = pltpu.roll(src, HD, 1)
            parts.append(jnp.where((lane < HD) if h == 0 else (lane >= HD), src, 0.0))
        qh, qh_lo = _split2(jnp.concatenate(parts, axis=0))

        p_c = _masked_softmax(_dot_nt(qh, kc_hi) + _dot_nt(qh_lo, kc_hi) + _dot_nt(qh, kc_lo), cmp_mask)
        o_cs.append(_dot(p_c.astype(BF16), vc_ref[0, :, LANES * h:LANES * (h + 1)]))

        psum = p_c[0:qb] + p_c[qb:2 * qb] + p_c[2 * qb:3 * qb] + p_c[3 * qb:4 * qb]
        x1 = psum.astype(BF16)
        r1 = psum - x1.astype(F32)
        x2 = r1.astype(BF16)
        x3 = (r1 - x2.astype(F32)).astype(BF16)
        mt = mt_ref[...]
        imp_t = _dot_nt(mt, x1) + _dot_nt(mt, x2) + _dot_nt(mt, x3)
        score_t = jnp.where(valid_t, jnp.where(forced_t, FORCE, imp_t), -1.0)
        st_ref[h] = jnp.where(jb_t < ns, score_t, -jnp.inf)
        sel_t = _select_blocks_t(st_ref, h, ns)
        selneg = (jnp.concatenate([zpad, sel_t], axis=0) if h == 0
                  else jnp.concatenate([sel_t, zpad], axis=0)).T
        qa_ref[h] = qh + jnp.concatenate([selneg.astype(BF16)] * GRP, axis=0)

        sw = _dot_nt(qh, kw_ref[0, pl.ds(wstart, wk), :])
        sw = jnp.concatenate([sw[qb * g:qb * (g + 1)] + wbias for g in range(GRP)], axis=0)
        e = jnp.exp(sw - jnp.max(sw, axis=-1, keepdims=True))
        o_w = _dot(e.astype(BF16), vw_ref[0, pl.ds(wstart, wk), LANES * h:LANES * (h + 1)])
        o_ws.append(o_w / o_w[:, HD:HD + 1])

    m_ref[...] = jnp.full(m_ref.shape, -jnp.inf, F32)
    acc_ref[...] = jnp.zeros(acc_ref.shape, F32)

    def chunk(c, diagonal):
        start = pl.multiple_of(c * tk, tk)
        for h in range(HKV):
            k = ks_ref[0, pl.ds(start, tk), LANES * h:LANES * (h + 1)]
            v = vs_ref[0, pl.ds(start, tk), LANES * h:LANES * (h + 1)]
            sc = _dot_nt(qa_ref[h], k)
            if diagonal:
                tok = start + lax.broadcasted_iota(jnp.int32, (1, tk), 1)
                sc = jnp.where(tok <= qpos4, sc, NEG)
            m_prev = m_ref[h]
            m_new = jnp.maximum(m_prev, jnp.max(sc, axis=-1, keepdims=True))
            alpha = jnp.exp(m_prev - m_new)
            p = jnp.exp(sc - m_new[:, 0:1])
            acc_ref[h] = alpha * acc_ref[h] + _dot(p.astype(BF16), v)
            m_ref[h] = m_new

    n_full = (i * qb) // tk

    def full_chunk(c, carry):
        chunk(c, False)
        return carry

    lax.fori_loop(0, n_full, full_chunk, 0)
    chunk(n_full, True)

    for h in range(HKV):
        acc = acc_ref[h]
        o_s = acc / acc[:, HD:HD + 1]
        outs = []
        for g in range(GRP):
            c0 = 3 * (GRP * h + g)
            r = slice(qb * g, qb * (g + 1))
            outs.append(sig[:, c0:c0 + 1] * o_cs[h][r] + sig[:, c0 + 1:c0 + 2] * o_s[r]
                        + sig[:, c0 + 2:c0 + 3] * o_ws[h][r])
        base = GRP * HD * h
        for pair in range(GRP // 2):
            hi = pltpu.roll(outs[2 * pair + 1], HD, 1)
            o_ref[0, :, base + LANES * pair:base + LANES * (pair + 1)] = jnp.where(
                lane < HD, outs[2 * pair], hi).astype(BF16)


def _nsa_prompt(q, sm, kc, vc, ks, vs, kw, vw, qb=128, tk=512):
    B, T, _ = q.shape
    ns = T // SLC_BLOCK
    nc = kc.shape[1]
    assert ns <= HD and T % tk == 0 and T >= WINDOW + qb and WINDOW % qb == 0 and tk % qb == 0
    m2 = _importance_matrix(nc, nc - 1, ns, 0, HD).T
    full = lambda w: pl.BlockSpec((1, T, w), lambda b, i: (b, 0, 0))
    return pl.pallas_call(
        functools.partial(_nsa_prompt_kernel, T=T, tk=tk, qb=qb),
        grid=(B, T // qb),
        in_specs=[pl.BlockSpec((1, qb, NSA_W), lambda b, i: (b, i, 0)),
                  pl.BlockSpec((1, qb, LANES), lambda b, i: (b, i, 0)),
                  pl.BlockSpec((1, nc, KV_W), lambda b, i: (b, 0, 0)),
                  pl.BlockSpec((1, nc, 2 * KV_W), lambda b, i: (b, 0, 0)),
                  full(256), full(256), full(KV_W), full(256),
                  _const_spec(m2.shape)],
        out_specs=pl.BlockSpec((1, qb, NSA_W), lambda b, i: (b, i, 0)),
        out_shape=jax.ShapeDtypeStruct((B, T, NSA_W), BF16),
        scratch_shapes=[pltpu.VMEM((HKV, GRP * qb, LANES), F32)] * 2
                       + [pltpu.VMEM((HKV, GRP * qb, LANES), BF16), pltpu.VMEM((HKV, HD, qb), F32)],
        compiler_params=_params("arbitrary", "arbitrary"),
        name="nsa_prompt",
    )(q, sm, kc, vc, ks, vs, kw, vw, jnp.asarray(m2, BF16))


def _mlstm_gates(g):
    b = pltpu.roll(_cumsum_rows(_log_sigmoid(g)), LANES - MH, 1)
    return b, g - b


def _mlstm_prompt_kernel(mqkv_ref, sm_ref, mo_ref, gb_ref, om_ref, c_out, n_out, m_out,
                         c_s, n_s, m_s, *, B, L):
    t = pl.program_id(0)

    @pl.when(t == 0)
    def _():
        c_s[...] = jnp.zeros(c_s.shape, F32)
        n_s[...] = jnp.zeros(n_s.shape, F32)
        m_s[...] = jnp.zeros(m_s.shape, F32)

    li = lax.broadcasted_iota(jnp.int32, (L, L), 0)
    si = lax.broadcasted_iota(jnp.int32, (L, L), 1)
    for b in range(B):
        b_al, r = _mlstm_gates(sm_ref[b] + gb_ref[...])
        r_t = jnp.concatenate([r, jnp.zeros((LANES - L, LANES), F32)], axis=0).T if L < LANES else r.T
        heads = []
        for hd in range(MH):
            idx = b * MH + hd
            bcol = b_al[:, SM_IG + hd:SM_IG + hd + 1]
            rcol = r[:, SM_IG + hd:SM_IG + hd + 1]
            rrow = r_t[SM_IG + hd:SM_IG + hd + 1, 0:L]
            mprev = m_s[idx][:, 0:1]
            acol = bcol + mprev
            logd = jnp.where(si <= li, bcol + rrow, NEG)
            mt = jnp.maximum(acol, jnp.max(logd, axis=-1, keepdims=True))
            dm = jnp.exp(logd - mt)
            inter = jnp.exp(acol - mt)
            q = mqkv_ref[b, :, HD * hd:HD * (hd + 1)]
            k = mqkv_ref[b, :, MLSTM_W + HD * hd:MLSTM_W + HD * (hd + 1)] * Q_SCALE
            v = mqkv_ref[b, :, 2 * MLSTM_W + HD * hd:2 * MLSTM_W + HD * (hd + 1)]
            qb, kb = q.astype(BF16), k.astype(BF16)
            qk = _dot_nt(qb, kb) * dm
            c_prev = c_s[idx]
            n_prev = n_s[idx]
            num = _dot(qk.astype(BF16), v.astype(BF16)) + inter * _dot(qb, c_prev.astype(BF16))
            den = jnp.sum(qk, axis=-1, keepdims=True) + inter * jnp.sum(q * n_prev, axis=-1, keepdims=True)
            hout = num / jnp.maximum(jnp.abs(den), jnp.exp(-mt))
            heads.append(jax.nn.sigmoid(mo_ref[b, :, HD * hd:HD * (hd + 1)]) * hout)
            m_last = mt[L - 1:L, :]
            b_last = bcol[L - 1:L, :]
            wcol = jnp.exp(b_last + rcol - m_last)
            decay = jnp.exp(b_last + mprev - m_last)
            c_s[idx] = decay * c_prev + _dot_tn(kb, (wcol * v).astype(BF16))
            n_s[idx] = decay * n_prev + jnp.sum(wcol * k, axis=0, keepdims=True)
            m_s[idx] = jnp.broadcast_to(m_last, (1, LANES))
        om_ref[b] = jnp.concatenate(heads, axis=-1).astype(BF16)

    @pl.when(t == pl.num_programs(0) - 1)
    def _():
        c_out[...] = c_s[...]
        n_out[...] = n_s[...]
        m_out[...] = m_s[...]


def _mlstm_prompt(mqkv, sm, mo, gate_b_tile, L=128):
    B, T, _ = mqkv.shape
    assert T % L == 0 and L <= LANES
    blk = lambda w: pl.BlockSpec((B, L, w), lambda t: (0, t, 0))
    st = lambda shape: pl.BlockSpec(shape, lambda t: (0,) * len(shape))
    shapes = [(B * MH, HD, HD), (B * MH, 1, HD), (B * MH, 1, LANES)]
    return pl.pallas_call(
        functools.partial(_mlstm_prompt_kernel, B=B, L=L),
        grid=(T // L,),
        in_specs=[blk(3 * MLSTM_W), blk(LANES), blk(MLSTM_W), _const_spec((1, LANES))],
        out_specs=[blk(MLSTM_W)] + [st(s) for s in shapes],
        out_shape=[jax.ShapeDtypeStruct((B, T, MLSTM_W), BF16)] + [jax.ShapeDtypeStruct(s, F32) for s in shapes],
        scratch_shapes=[pltpu.VMEM(s, F32) for s in shapes],
        compiler_params=_params("arbitrary"),
        name="mlstm_prompt",
    )(mqkv, sm, mo, gate_b_tile)


FF_CHUNK = 256


def _outffn_prompt_kernel(x_ref, on_ref, om_ref, op_ref, gt1_ref, sh2_ref, sc2_ref, gt2_ref,
                          g2_ref, fg_ref, wout_ref, wup_ref, cw_ref, cb_ref, wdn_ref,
                          y_ref, fs_ref, prev_ref, es_ref, *, tm, d_ff, final):
    t = pl.program_id(1)
    mix = (_dot(on_ref[0], wout_ref[0:NSA_W, :])
           + _dot(om_ref[0], wout_ref[NSA_W:NSA_W + MLSTM_W, :])
           + _dot(op_ref[0], wout_ref[NSA_W + MLSTM_W:, :]))
    x1 = x_ref[0] + gt1_ref[0] * mix
    h2 = _norm_mod(x1, g2_ref[...], sc2_ref[0], sh2_ref[0]).astype(BF16)

    @pl.when(t == 0)
    def _():
        prev_ref[...] = jnp.zeros(prev_ref.shape, F32)

    f = jnp.zeros(x1.shape, F32)
    w = FF_CHUNK
    for c in range(d_ff // w):
        for half, off in ((0, c * w), (1, d_ff + c * w)):
            es_ref[0:SUBLANES, half * w:(half + 1) * w] = prev_ref[:, off:off + w]
            es_ref[SUBLANES:SUBLANES + tm, half * w:(half + 1) * w] = _dot(h2, wup_ref[:, off:off + w])
            prev_ref[:, off:off + w] = es_ref[tm:tm + SUBLANES, half * w:(half + 1) * w]
        ys = []
        for half, off in ((0, c * w), (1, d_ff + c * w)):
            y = cb_ref[:, off:off + w]
            for j in range(FFN_CONV):
                y = y + cw_ref[j:j + 1, off:off + w] * es_ref[pl.ds(SUBLANES - (FFN_CONV - 1) + j, tm), half * w:(half + 1) * w]
            ys.append(y)
        act = ys[0] * jax.nn.sigmoid(ys[0]) * ys[1]
        f = f + _dot(act.astype(BF16), wdn_ref[c * w:(c + 1) * w, :])
    x2 = x1 + gt2_ref[0] * f
    y_ref[0] = _rmsnorm(x2, fg_ref[...]) if final else x2
    fs_ref[0] = prev_ref[...]


def _outffn_prompt(x, on, om, op, gt1, sh2, sc2, gt2, g2, fg, w_out, w_up, conv_w, conv_b, w_dn, final, tm=256):
    B, T, D = x.shape
    d_ff = w_dn.shape[0]
    assert T % tm == 0 and d_ff % FF_CHUNK == 0
    row = lambda w: pl.BlockSpec((1, tm, w), lambda b, t: (b, t, 0))
    mod = pl.BlockSpec((1, 1, D), lambda b, t: (b, 0, 0))
    return pl.pallas_call(
        functools.partial(_outffn_prompt_kernel, tm=tm, d_ff=d_ff, final=final),
        grid=(B, T // tm),
        in_specs=[row(D), row(NSA_W), row(MLSTM_W), row(POOL_W), mod, mod, mod, mod,
                  _const_spec((1, D)), _const_spec((1, D)), _const_spec(w_out.shape), _const_spec(w_up.shape),
                  _const_spec(conv_w.shape), _const_spec(conv_b.shape), _const_spec(w_dn.shape)],
        out_specs=[row(D), pl.BlockSpec((1, SUBLANES, 2 * d_ff), lambda b, t: (b, 0, 0))],
        out_shape=[jax.ShapeDtypeStruct((B, T, D), F32), jax.ShapeDtypeStruct((B, SUBLANES, 2 * d_ff), F32)],
        scratch_shapes=[pltpu.VMEM((SUBLANES, 2 * d_ff), F32), pltpu.VMEM((SUBLANES + tm, 2 * FF_CHUNK), F32)],
        compiler_params=_params("arbitrary", "arbitrary"),
        name="outffn_prompt",
    )(x, on, om, op, gt1, sh2, sc2, gt2, g2, fg, w_out, w_up, conv_w, conv_b, w_dn)


def _prep_layer(l, w, page):
    gate_b = jnp.zeros((1, LANES), F32).at[0, SM_IG:SM_IG + 2 * MH].set(w["mlstm_gate_b"][l])
    pw = w["pool_w"][l]
    z = jnp.zeros((HD, HD), F32)
    pool_bd = jnp.block([[pw[i] if i == j else z for j in range(4)] for i in range(4)]).astype(BF16)
    wq = w["w_in"][l][:, :NSA_W]
    wq_hi = wq.astype(BF16)
    return dict(
        w_q=wq_hi, w_q2=(wq_hi, (wq - wq_hi.astype(F32)).astype(BF16)),
        g1=w["norm1_g"][l][None], g2=w["norm2_g"][l][None],
        w_b=_prep_w_in(w["w_in"][l]), pool_bd=pool_bd, pool_scale=w["pool_scale"][l][None],
        cw=_prep_compress(w["nsa_cmp_pos"][l], w["nsa_cmp_w"][l], w["nsa_cmp_b"][l], page),
        gate_b=gate_b, w_out=w["w_out"][l].astype(BF16), w_up=w["ffn_w_up"][l].astype(BF16),
        conv_w=w["ffn_conv_w"][l], conv_b=w["ffn_conv_b"][l][None], w_dn=w["ffn_w_down"][l].astype(BF16),
        fg=w["final_g"][None])


def _prompt_layer(x, mod, lw, final):
    B, T, D = x.shape
    sh1, sc1, gt1, sh2, sc2, gt2 = [mod[:, :, D * i:D * (i + 1)] for i in range(6)]
    (q, nk, nv, ks, vs, kw, vw, wkv, mqkv, mo, op, sm, ps) = _inproj_prompt(
        x, lw["g1"], sh1, sc1, lw["w_b"], lw["w_q2"], lw["pool_bd"], lw["pool_scale"])
    kc, vc = _compress_prompt(nk, nv, lw["cw"])
    on = _nsa_prompt(q, sm, kc, vc, ks, vs, kw, vw)
    om, c_st, n_st, m_st = _mlstm_prompt(mqkv, sm, mo, lw["gate_b"])
    y, fs = _outffn_prompt(x, on, om, op, gt1, sh2, sc2, gt2, lw["g2"], lw["fg"], lw["w_out"],
                           lw["w_up"], lw["conv_w"], lw["conv_b"], lw["w_dn"], final)
    wb = wkv.shape[1]
    states = (nk.reshape(B, T, 2, HKV, HD), nv.reshape(B, T, 2, HKV, HD),
              wkv[:, :, :KV_W].reshape(B, wb, HKV, HD), wkv[:, :, KV_W:].reshape(B, wb, HKV, HD),
              c_st.reshape(B, MH, HD, HD), n_st.reshape(B, MH, HD), m_st[:, 0, 0].reshape(B, MH),
              ps[:, 2 * SUBLANES - POOL_STATE:], fs[:, SUBLANES - (FFN_CONV - 1):])
    return y, states


def _inproj_sample_kernel(x_ref, g_ref, sh_ref, sc_ref, w_ref, wq_ref, pw_ref, pscale_ref, prev_ref,
                          q_ref, nk_ref, nv_ref, wkv_ref, mqkv_ref, mo_ref, op_ref, sm_ref, pu_ref, *, pos0):
    h = _norm_mod(x_ref[...], g_ref[...], sc_ref[...], sh_ref[...]).astype(BF16)

    def seg(a, b):
        return _dot(h, w_ref[:, a:b])

    q_ref[...] = _dot(h, wq_ref[...]) * Q_SCALE
    nk_ref[...] = seg(C_NK, C_NV)
    nv_ref[...] = seg(C_NV, C_KS)
    wkv_ref[...] = seg(C_WKV, C_MQKV)
    mqkv_ref[...] = seg(C_MQKV, C_MO)
    mo_ref[...] = seg(C_MO, C_PU)
    sm_ref[...] = seg(C_SM, C_END)
    pu = seg(C_PU, C_SM)
    pu_ref[...] = pu
    acc = pu
    sums = {}
    for i in range(1, POOL_STATE + 1):
        acc = acc + prev_ref[:, POOL_STATE - i, :]
        if i + 1 in POOL_WINDOWS:
            sums[i + 1] = acc
    cnt = {w: float(min(w, pos0 + 1)) for w in POOL_WINDOWS}
    op_ref[...] = _pool_mix(pu, sums, cnt, pw_ref, pscale_ref)


def _inproj_sample(l, x, g1, sh1, sc1, w_b, w_q, pool_bd, pool_scale, state_pool, pos0):
    DB, D = x.shape
    full = lambda shape: pl.BlockSpec(shape, lambda i: (0,) * len(shape))
    widths = [NSA_W, 256, 256, 256, 3 * MLSTM_W, MLSTM_W, POOL_W, LANES, POOL_W]
    return pl.pallas_call(
        functools.partial(_inproj_sample_kernel, pos0=pos0),
        grid=(1,),
        in_specs=[full((DB, D)), full((1, D)), full((DB, D)), full((DB, D)), full(w_b.shape),
                  full(w_q.shape), full(pool_bd.shape), full((1, POOL_W)),
                  pl.BlockSpec((None, DB, POOL_STATE, POOL_W), lambda i: (l, 0, 0, 0))],
        out_specs=[full((DB, w)) for w in widths],
        out_shape=[jax.ShapeDtypeStruct((DB, w), F32) for w in widths],
        compiler_params=_params("arbitrary"),
        name="inproj_sample",
    )(x, g1, sh1, sc1, w_b, w_q, pool_bd, pool_scale, state_pool)


PAGES_PER_STEP = 16


def _compress_sample_kernel(pt_ref, *refs, npg):
    del pt_ref
    k_refs, v_refs = refs[:npg], refs[npg:2 * npg]
    wpk_ref, wpv_ref, ak_ref, bk_ref, av_ref, bv_ref = refs[2 * npg:]
    seg = SUBLANES

    def halves(page_ref, wp_ref):
        x = page_ref[...].reshape(page_ref.shape[0] // CMP_STRIDE, CMP_STRIDE, KV_W)
        a = jnp.sum(x * wp_ref[0:CMP_STRIDE, :][None], axis=1)
        b = jnp.sum(x * wp_ref[CMP_STRIDE:CMP_LEN, :][None], axis=1)
        return a, b

    for i in range(npg):
        n = k_refs[i].shape[0] // CMP_STRIDE
        a, b = halves(k_refs[i], wpk_ref)
        ak_ref[0, i * n:(i + 1) * n, :] = a
        bk_ref[0, i * n:(i + 1) * n, :] = b
        a, b = halves(v_refs[i], wpv_ref)
        av_ref[0, i * n:(i + 1) * n, :] = a
        bv_ref[0, i * n:(i + 1) * n, :] = b


def _compress_sample(l, cache_k, cache_v, page_table, cw):
    DB, n_pages = page_table.shape
    page = cache_k.shape[2]
    npg = min(PAGES_PER_STEP, n_pages)
    assert n_pages % npg == 0 and page % CMP_STRIDE == 0
    spp = page // CMP_STRIDE
    nseg = n_pages * spp

    def page_spec(i):
        return pl.BlockSpec((None, None, page, KV_W), lambda b, c, pt: (l, pt[b, c * npg + i], 0, 0))

    out = pl.BlockSpec((1, npg * spp, KV_W), lambda b, c, pt: (b, c, 0))
    wp = pl.BlockSpec((CMP_LEN, KV_W), lambda b, c, pt: (0, 0))
    grid_spec = pltpu.PrefetchScalarGridSpec(
        num_scalar_prefetch=1, grid=(DB, n_pages // npg),
        in_specs=[page_spec(i) for i in range(npg)] * 2 + [wp, wp],
        out_specs=[out] * 4)
    return pl.pallas_call(
        functools.partial(_compress_sample_kernel, npg=npg),
        grid_spec=grid_spec,
        out_shape=[jax.ShapeDtypeStruct((DB, nseg, KV_W), F32)] * 4,
        compiler_params=_params("arbitrary", "arbitrary"),
        name="compress_sample",
    )(page_table, *([cache_k] * npg), *([cache_v] * npg), cw["wpk"], cw["wpv"])


def _nsa_sample_a_kernel(q_ref, ak_ref, bk_ref, av_ref, bv_ref, nk_ref, nv_ref, wkv_ref, wink_ref, winv_ref,
                         wpk_ref, wpv_ref, wk_ref, wv_ref, bkb_ref, bvb_ref, m_ref,
                         idx_ref, ocw_ref, wko_ref, wvo_ref, *, past, ns):
    ncs = ak_ref.shape[1]
    q8 = q_ref[0].astype(BF16)
    rown = lax.broadcasted_iota(jnp.int32, (ncs, 1), 0)

    def comp(a_ref, b_ref, new_row, wp_ref, w_ref, bias_ref):
        b_new = new_row * wp_ref[CMP_STRIDE:CMP_STRIDE + 1, :]
        pooled = a_ref[0] + jnp.where(rown == ncs - 1, b_new, pltpu.roll(b_ref[0], ncs - 1, 0))
        return (_dot(pooled.astype(BF16), w_ref[...]) + bias_ref[...]).astype(BF16)

    kc = comp(ak_ref, bk_ref, nk_ref[0][:, :KV_W], wpk_ref, wk_ref, bkb_ref)
    vc = comp(av_ref, bv_ref, nv_ref[0][:, :KV_W], wpv_ref, wv_ref, bvb_ref)
    n_idx = lax.broadcasted_iota(jnp.int32, (1, ncs), 1)
    p = _masked_softmax(_dot_nt(q8, kc), (CMP_STRIDE * n_idx + CMP_LEN - 1) <= past)
    o_c = _dot(p.astype(BF16), vc)

    rowi = lax.broadcasted_iota(jnp.int32, p.shape, 0)
    p0 = jnp.sum(p[0:GRP], axis=0, keepdims=True)
    p1 = jnp.sum(p[GRP:2 * GRP], axis=0, keepdims=True)
    imp = _dot_split3(jnp.where(rowi == 0, p0, jnp.where(rowi == 1, p1, 0.0)), m_ref[...])
    lane = lax.broadcasted_iota(jnp.int32, imp.shape, 1)
    cur = past // SLC_BLOCK
    valid = lane * SLC_BLOCK <= past
    forced = (lane == 0) | (lane == cur) | (lane == cur - 1)
    score = jnp.where(valid, jnp.where(forced, FORCE, imp), -1.0)
    score = jnp.where(lane < ns, score, -jnp.inf)
    lane_o = lax.broadcasted_iota(jnp.int32, (SUBLANES, LANES), 1)
    idx = jnp.full((SUBLANES, LANES), -1, jnp.int32)
    for r in range(min(N_SEL, ns)):
        mx = jnp.max(score, axis=-1, keepdims=True)
        first = jnp.min(jnp.where(score == mx, lane, 1 << 20), axis=-1, keepdims=True)
        idx = jnp.where(lane_o == r, jnp.where(mx >= 0.0, first, -1), idx)
        score = jnp.where(lane == first, -jnp.inf, score)
    idx_ref[0] = idx

    wb = wink_ref.shape[0]
    k_new = wkv_ref[0][:, :KV_W]
    v_new = wkv_ref[0][:, KV_W:]
    s_w = _dot_nt(q8, wink_ref[...].astype(BF16))
    dlt = wb - lax.broadcasted_iota(jnp.int32, (1, wb), 1)
    mask = (dlt < WINDOW) & (past - dlt >= 0)
    s_new = jnp.sum(q8.astype(F32) * k_new.astype(BF16).astype(F32), axis=-1, keepdims=True)
    s_w = jnp.where(mask, s_w, NEG)
    mx = jnp.maximum(jnp.max(s_w, axis=-1, keepdims=True), s_new)
    e = jnp.where(mask, jnp.exp(s_w - mx), 0.0)
    e_new = jnp.exp(s_new - mx)
    den = jnp.sum(e, axis=-1, keepdims=True) + e_new
    o_w = (_dot(e.astype(BF16), winv_ref[...].astype(BF16)) + e_new * v_new.astype(BF16).astype(F32)) / den
    ocw_ref[0] = jnp.concatenate([o_c, o_w], axis=-1)
    wko_ref[0, pl.ds(0, wb - 1), :] = wink_ref[pl.ds(1, wb - 1), :]
    wko_ref[0, pl.ds(wb - 1, 1), :] = k_new
    wvo_ref[0, pl.ds(0, wb - 1), :] = winv_ref[pl.ds(1, wb - 1), :]
    wvo_ref[0, pl.ds(wb - 1, 1), :] = v_new


def _nsa_sample_a(l, q3, halves, nk3, nv3, wkv3, win_k, win_v, cw, past):
    DB = q3.shape[0]
    ncs = halves[0].shape[1]
    wb = win_k.shape[2]
    tot = past + 1
    ns = -(-tot // SLC_BLOCK)
    nsl = -(-ns // LANES) * LANES
    m = _importance_matrix(ncs, ncs, ns, 0, nsl)
    per = lambda shape: pl.BlockSpec((1,) + shape, lambda b: (b,) + (0,) * len(shape))
    win = pl.BlockSpec((None, None, wb, KV_W), lambda b: (l, b, 0, 0))
    shapes = [(DB, SUBLANES, LANES), (DB, NSA_HEADS, 2 * KV_W), (DB, wb, KV_W), (DB, wb, KV_W)]
    dts = [jnp.int32, F32, F32, F32]
    return pl.pallas_call(
        functools.partial(_nsa_sample_a_kernel, past=past, ns=ns),
        grid=(DB,),
        in_specs=[per((NSA_HEADS, LANES))] + [per((ncs, KV_W))] * 4 + [per((1, 256))] * 3 + [win, win]
                 + [_const_spec((CMP_LEN, KV_W))] * 2 + [_const_spec((KV_W, KV_W))] * 2
                 + [_const_spec((1, KV_W))] * 2 + [_const_spec(m.shape)],
        out_specs=[per(s[1:]) for s in shapes],
        out_shape=[jax.ShapeDtypeStruct(s, d) for s, d in zip(shapes, dts)],
        compiler_params=_params("arbitrary"),
        name="nsa_sample_a",
    )(q3, *halves, nk3, nv3, wkv3, win_k, win_v, cw["wpk"], cw["wpv"], cw["wk"], cw["wv"],
      cw["bk"], cw["bv"], jnp.asarray(m, BF16))


def _nsa_sample_b_kernel(pt_ref, ix_ref, *refs, past, nbp, nsel):
    del pt_ref
    k_refs, v_refs = refs[:HKV * nsel], refs[HKV * nsel:2 * HKV * nsel]
    q_ref, ocw_ref, nk_ref, nv_ref, g_ref, o_ref = refs[2 * HKV * nsel:]
    b = pl.program_id(0)
    q8 = q_ref[0].astype(BF16)
    k_new = nk_ref[0][:, KV_W:].astype(BF16).astype(F32)
    v_new = nv_ref[0][:, KV_W:].astype(BF16).astype(F32)
    s_new = jnp.sum(q8.astype(F32) * k_new, axis=-1, keepdims=True)
    nk = nsel * SLC_BLOCK
    lane = lax.broadcasted_iota(jnp.int32, (1, nk), 1)
    row = lax.broadcasted_iota(jnp.int32, (NSA_HEADS, KV_W), 0)
    o_s = jnp.zeros((NSA_HEADS, KV_W), F32)
    for h in range(HKV):
        kb = jnp.concatenate([k_refs[h * nsel + r][...] for r in range(nsel)], axis=0).astype(BF16)
        vb = jnp.concatenate([v_refs[h * nsel + r][...] for r in range(nsel)], axis=0).astype(BF16)
        jv = jnp.full((1, nk), -1, jnp.int32)
        n_new = jnp.int32(0)
        for r in range(nsel):
            j = ix_ref[b, h, r]
            jv = jnp.where(lane // SLC_BLOCK == r, j, jv)
            n_new = n_new + (j == nbp).astype(jnp.int32)
        tok = jv * SLC_BLOCK + lane % SLC_BLOCK
        mask = (jv >= 0) & (jv < nbp) & (tok <= past)
        has_new = n_new > 0
        s = jnp.where(mask, _dot_nt(q8, kb), NEG)
        sn = jnp.where(has_new, s_new, NEG)
        mx = jnp.maximum(jnp.max(s, axis=-1, keepdims=True), sn)
        e = jnp.where(mask, jnp.exp(s - mx), 0.0)
        e_new = jnp.where(has_new, jnp.exp(sn - mx), 0.0)
        den = jnp.maximum(jnp.sum(e, axis=-1, keepdims=True) + e_new, jnp.finfo(jnp.float32).tiny)
        o_h = (_dot(e.astype(BF16), vb) + e_new * v_new) / den
        o_s = jnp.where(row // GRP == h, o_h, o_s)
    g = jax.nn.sigmoid(g_ref[0])
    ocw = ocw_ref[0]
    o = g[:, 0:1] * ocw[:, :KV_W] + g[:, 1:2] * o_s + g[:, 2:3] * ocw[:, KV_W:]
    lane_o = lax.broadcasted_iota(jnp.int32, (NSA_HEADS, KV_W), 1)
    o_ref[0] = jnp.where((lane_o // HD) == (row // GRP), o, 0.0)


def _nsa_sample_b(l, cache_k, cache_v, page_table, idx, q3, ocw, nk3, nv3, g3, past):
    DB = q3.shape[0]
    page = cache_k.shape[2]
    bpp = page // SLC_BLOCK
    nbp = past // SLC_BLOCK
    nsel = idx.shape[2]

    def blk_spec(h, r):
        def imap(b, pt, ix):
            j = jnp.clip(ix[b, h, r], 0, nbp - 1)
            return (l, pt[b, j // bpp], j % bpp, 1)
        return pl.BlockSpec((None, None, SLC_BLOCK, KV_W), imap)

    per = lambda shape: pl.BlockSpec((1,) + shape, lambda b, pt, ix: (b,) + (0,) * len(shape))
    kv_specs = [blk_spec(h, r) for h in range(HKV) for r in range(nsel)]
    grid_spec = pltpu.PrefetchScalarGridSpec(
        num_scalar_prefetch=2, grid=(DB,),
        in_specs=kv_specs * 2 + [per((NSA_HEADS, LANES)), per((NSA_HEADS, 2 * KV_W)), per((1, 256)), per((1, 256)),
                                 per((NSA_HEADS, 3))],
        out_specs=per((NSA_HEADS, KV_W)))
    n = HKV * nsel
    return pl.pallas_call(
        functools.partial(_nsa_sample_b_kernel, past=past, nbp=nbp, nsel=nsel),
        grid_spec=grid_spec,
        out_shape=jax.ShapeDtypeStruct((DB, NSA_HEADS, KV_W), F32),
        compiler_params=_params("arbitrary"),
        name="nsa_sample_b",
    )(page_table, idx, *([cache_k] * n), *([cache_v] * n), q3, ocw, nk3, nv3, g3)


def _row_to_col(rowv):
    n = rowv.shape[1]
    eye = lax.broadcasted_iota(jnp.int32, (n, n), 0) == lax.broadcasted_iota(jnp.int32, (n, n), 1)
    return jnp.sum(jnp.where(eye, jnp.broadcast_to(rowv, (n, n)), 0.0), axis=1, keepdims=True)


def _compress_sample_t_kernel(pt_ref, *refs, npg):
    del pt_ref
    k_refs, v_refs = refs[:npg], refs[npg:2 * npg]
    wk_ref, wv_ref, sg_ref, ok_ref, ov_ref = refs[2 * npg:]
    for src, w_ref, o_ref in ((k_refs, wk_ref, ok_ref), (v_refs, wv_ref, ov_ref)):
        wa = w_ref[0:HD, :]
        wb = w_ref[HD:2 * HD, :]
        for h in range(HKV):
            prods = []
            for i in range(npg):
                x = src[i][h]
                prods.append(jnp.concatenate([x * wa, x * wb], axis=0).astype(BF16))
            o_ref[0, h] = _dot(jnp.concatenate(prods, axis=1), sg_ref[...])


def _compress_sample_t(l, cache_kt, cache_vt, page_table, cw):
    DB, n_pages = page_table.shape
    page = cache_kt.shape[-1]
    npg = min(PAGES_PER_STEP, n_pages)
    assert n_pages % npg == 0 and page % CMP_STRIDE == 0
    spp = page // CMP_STRIDE
    nseg = n_pages * spp
    assert (npg * spp) % LANES == 0 or npg * spp == nseg
    sg = np.zeros((npg * page, npg * spp), np.float32)
    sg[np.arange(npg * page), np.arange(npg * page) // CMP_STRIDE] = 1.0

    def page_spec(i):
        return pl.BlockSpec((None, None, None, HKV, HD, page),
                            lambda b, c, pt: (l, pt[b, c * npg + i], 0, 0, 0, 0))

    out = pl.BlockSpec((1, HKV, 2 * HD, npg * spp), lambda b, c, pt: (b, 0, 0, c))
    cst = lambda shape: pl.BlockSpec(shape, lambda b, c, pt: (0,) * len(shape))
    grid_spec = pltpu.PrefetchScalarGridSpec(
        num_scalar_prefetch=1, grid=(DB, n_pages // npg),
        in_specs=[page_spec(i) for i in range(npg)] * 2
                 + [cst((2 * HD, page)), cst((2 * HD, page)), cst(sg.shape)],
        out_specs=[out] * 2)
    return pl.pallas_call(
        functools.partial(_compress_sample_t_kernel, npg=npg),
        grid_spec=grid_spec,
        out_shape=[jax.ShapeDtypeStruct((DB, HKV, 2 * HD, nseg), F32)] * 2,
        compiler_params=_params("arbitrary", "arbitrary"),
        name="compress_sample",
    )(page_table, *([cache_kt] * npg), *([cache_vt] * npg), cw["wpk_t"], cw["wpv_t"], jnp.asarray(sg, BF16))


def _nsa_sample_a_t_kernel(q_ref, hk_ref, hv_ref, nk_ref, nv_ref, wkv_ref, wink_ref, winv_ref,
                           wpk_ref, wpv_ref, wk_ref, wv_ref, bk_ref, bv_ref, m_ref,
                           idx_ref, ocw_ref, wko_ref, wvo_ref, *, past, ns):
    ncs = hk_ref.shape[3]
    wb = wink_ref.shape[3]
    q8 = q_ref[0].astype(BF16)
    row = lax.broadcasted_iota(jnp.int32, (NSA_HEADS, 1), 0)
    lane_c = lax.broadcasted_iota(jnp.int32, (HD, ncs), 1)
    lane_w = lax.broadcasted_iota(jnp.int32, (HD, wb), 1)
    n_idx = lax.broadcasted_iota(jnp.int32, (1, ncs), 1)

    def per_head(fn):
        a, b = fn(0), fn(1)
        return jnp.where(row // GRP == 0, a, b)

    def comp_t(h_ref, new_row, wp_ref, w_ref, bias_ref, h):
        at = h_ref[0, h, 0:HD, :]
        bt = h_ref[0, h, HD:2 * HD, :]
        b_new = _row_to_col(new_row[:, HD * h:HD * (h + 1)]) * wp_ref[...]
        pooled = at + jnp.where(lane_c == ncs - 1, b_new, pltpu.roll(bt, ncs - 1, 1))
        return (_dot(w_ref[...], pooled.astype(BF16)) + bias_ref[...]).astype(BF16)

    kct = [comp_t(hk_ref, nk_ref[0], wpk_ref, wk_ref, bk_ref, h) for h in range(HKV)]
    vct = [comp_t(hv_ref, nv_ref[0], wpv_ref, wv_ref, bv_ref, h) for h in range(HKV)]
    s = per_head(lambda h: _dot(q8, kct[h]))
    p = _masked_softmax(s, (CMP_STRIDE * n_idx + CMP_LEN - 1) <= past)
    pb = p.astype(BF16)
    o_c = per_head(lambda h: _dot_nt(pb, vct[h]))

    rowi = lax.broadcasted_iota(jnp.int32, p.shape, 0)
    p0 = jnp.sum(p[0:GRP], axis=0, keepdims=True)
    p1 = jnp.sum(p[GRP:2 * GRP], axis=0, keepdims=True)
    imp = _dot_split3(jnp.where(rowi == 0, p0, jnp.where(rowi == 1, p1, 0.0)), m_ref[...])
    lane = lax.broadcasted_iota(jnp.int32, imp.shape, 1)
    cur = past // SLC_BLOCK
    valid = lane * SLC_BLOCK <= past
    forced = (lane == 0) | (lane == cur) | (lane == cur - 1)
    score = jnp.where(valid, jnp.where(forced, FORCE, imp), -1.0)
    score = jnp.where(lane < ns, score, -jnp.inf)
    lane_o = lax.broadcasted_iota(jnp.int32, (SUBLANES, LANES), 1)
    idx = jnp.full((SUBLANES, LANES), -1, jnp.int32)
    for r in range(min(N_SEL, ns)):
        mx = jnp.max(score, axis=-1, keepdims=True)
        first = jnp.min(jnp.where(score == mx, lane, 1 << 20), axis=-1, keepdims=True)
        idx = jnp.where(lane_o == r, jnp.where(mx >= 0.0, first, -1), idx)
        score = jnp.where(lane == first, -jnp.inf, score)
    idx_ref[0] = idx

    sel_head = lambda t: jnp.where(row // GRP == 0, t[:, :HD], t[:, HD:])
    k_new = wkv_ref[0][:, :KV_W]
    v_new = wkv_ref[0][:, KV_W:]
    s_w = per_head(lambda h: _dot(q8, wink_ref[0, h].astype(BF16)))
    dlt = wb - lax.broadcasted_iota(jnp.int32, (1, wb), 1)
    mask = (dlt < WINDOW) & (past - dlt >= 0)
    s_new = jnp.sum(q8.astype(F32) * sel_head(k_new).astype(BF16).astype(F32), axis=-1, keepdims=True)
    s_w = jnp.where(mask, s_w, NEG)
    mx = jnp.maximum(jnp.max(s_w, axis=-1, keepdims=True), s_new)
    e = jnp.where(mask, jnp.exp(s_w - mx), 0.0)
    e_new = jnp.exp(s_new - mx)
    den = jnp.sum(e, axis=-1, keepdims=True) + e_new
    eb = e.astype(BF16)
    o_w = per_head(lambda h: _dot_nt(eb, winv_ref[0, h].astype(BF16)))
    o_w = (o_w + e_new * sel_head(v_new).astype(BF16).astype(F32)) / den
    ocw_ref[0] = jnp.concatenate([o_c, o_w], axis=-1)
    for h in range(HKV):
        for src, new, dst in ((wink_ref, k_new, wko_ref), (winv_ref, v_new, wvo_ref)):
            col = _row_to_col(new[:, HD * h:HD * (h + 1)])
            dst[0, h] = jnp.where(lane_w == wb - 1, col, pltpu.roll(src[0, h], wb - 1, 1))


def _nsa_sample_a_t(l, q3, hk, hv, nk3, nv3, wkv3, win_kt, win_vt, cw, past):
    DB = q3.shape[0]
    ncs = hk.shape[3]
    wb = win_kt.shape[-1]
    ns = -(-(past + 1) // SLC_BLOCK)
    nsl = -(-ns // LANES) * LANES
    m = _importance_matrix(ncs, ncs, ns, 0, nsl)
    per = lambda shape: pl.BlockSpec((1,) + shape, lambda b: (b,) + (0,) * len(shape))
    win = pl.BlockSpec((None, 1, HKV, HD, wb), lambda b: (l, b, 0, 0, 0))
    shapes = [(DB, SUBLANES, LANES), (DB, NSA_HEADS, 2 * HD), (DB, HKV, HD, wb), (DB, HKV, HD, wb)]
    dts = [jnp.int32, F32, F32, F32]
    return pl.pallas_call(
        functools.partial(_nsa_sample_a_t_kernel, past=past, ns=ns),
        grid=(DB,),
        in_specs=[per((NSA_HEADS, HD))] + [per((HKV, 2 * HD, ncs))] * 2 + [per((1, 256))] * 3 + [win, win]
                 + [_const_spec((HD, 1))] * 2 + [_const_spec((HD, HD))] * 2
                 + [_const_spec((HD, 1))] * 2 + [_const_spec(m.shape)],
        out_specs=[per(s[1:]) for s in shapes],
        out_shape=[jax.ShapeDtypeStruct(s, d) for s, d in zip(shapes, dts)],
        compiler_params=_params("arbitrary"),
        name="nsa_sample_a",
    )(q3, hk, hv, nk3, nv3, wkv3, win_kt, win_vt, cw["wpk_b0"], cw["wpv_b0"], cw["wk_t"], cw["wv_t"],
      cw["bk_col"], cw["bv_col"], jnp.asarray(m, BF16))


def _nsa_sample_b_t_kernel(pt_ref, ix_ref, *refs, past, nbp, nsel, page):
    del pt_ref
    k_refs, v_refs = refs[:HKV * nsel], refs[HKV * nsel:2 * HKV * nsel]
    q_ref, ocw_ref, nk_ref, nv_ref, g_ref, o_ref = refs[2 * HKV * nsel:]
    b = pl.program_id(0)
    bpp = page // SLC_BLOCK
    q8 = q_ref[0].astype(BF16)
    row = lax.broadcasted_iota(jnp.int32, (NSA_HEADS, 1), 0)
    sel_head = lambda t: jnp.where(row // GRP == 0, t[:, :HD], t[:, HD:])
    k_new = sel_head(nk_ref[0][:, KV_W:]).astype(BF16).astype(F32)
    v_new = sel_head(nv_ref[0][:, KV_W:]).astype(BF16).astype(F32)
    s_new = jnp.sum(q8.astype(F32) * k_new, axis=-1, keepdims=True)
    nk = nsel * page
    lane = lax.broadcasted_iota(jnp.int32, (1, nk), 1)
    o_s = jnp.zeros((NSA_HEADS, HD), F32)
    for h in range(HKV):
        kt = jnp.concatenate([k_refs[h * nsel + r][...] for r in range(nsel)], axis=1).astype(BF16)
        vt = jnp.concatenate([v_refs[h * nsel + r][...] for r in range(nsel)], axis=1).astype(BF16)
        jv = jnp.full((1, nk), -1, jnp.int32)
        n_new = jnp.int32(0)
        for r in range(nsel):
            j = ix_ref[b, h, r]
            jv = jnp.where(lane // page == r, j, jv)
            n_new = n_new + (j == nbp).astype(jnp.int32)
        t_in = lane % page
        tok = (jv // bpp) * page + t_in
        mask = (jv >= 0) & (jv < nbp) & (t_in // SLC_BLOCK == jv % bpp) & (tok <= past)
        has_new = n_new > 0
        s = jnp.where(mask, _dot(q8, kt), NEG)
        sn = jnp.where(has_new, s_new, NEG)
        mx = jnp.maximum(jnp.max(s, axis=-1, keepdims=True), sn)
        e = jnp.where(mask, jnp.exp(s - mx), 0.0)
        e_new = jnp.where(has_new, jnp.exp(sn - mx), 0.0)
        den = jnp.maximum(jnp.sum(e, axis=-1, keepdims=True) + e_new, jnp.finfo(jnp.float32).tiny)
        o_h = (_dot_nt(e.astype(BF16), vt) + e_new * v_new) / den
        o_s = jnp.where(row // GRP == h, o_h, o_s)
    g = jax.nn.sigmoid(g_ref[0])
    ocw = ocw_ref[0]
    o_ref[0] = g[:, 0:1] * ocw[:, :HD] + g[:, 1:2] * o_s + g[:, 2:3] * ocw[:, HD:]


def _nsa_sample_b_t(l, cache_kt, cache_vt, page_table, idx, q3, ocw, nk3, nv3, g3, past):
    DB = q3.shape[0]
    page = cache_kt.shape[-1]
    bpp = page // SLC_BLOCK
    nbp = past // SLC_BLOCK
    nsel = idx.shape[2]

    def tile_spec(h, r):
        def imap(b, pt, ix):
            j = jnp.clip(ix[b, h, r], 0, nbp - 1)
            return (l, pt[b, j // bpp], 1, h, 0, 0)
        return pl.BlockSpec((None, None, None, None, HD, page), imap)

    per = lambda shape: pl.BlockSpec((1,) + shape, lambda b, pt, ix: (b,) + (0,) * len(shape))
    kv_specs = [tile_spec(h, r) for h in range(HKV) for r in range(nsel)]
    grid_spec = pltpu.PrefetchScalarGridSpec(
        num_scalar_prefetch=2, grid=(DB,),
        in_specs=kv_specs * 2 + [per((NSA_HEADS, HD)), per((NSA_HEADS, 2 * HD)), per((1, 256)), per((1, 256)),
                                 per((NSA_HEADS, 3))],
        out_specs=per((NSA_HEADS, HD)))
    n = HKV * nsel
    return pl.pallas_call(
        functools.partial(_nsa_sample_b_t_kernel, past=past, nbp=nbp, nsel=nsel, page=page),
        grid_spec=grid_spec,
        out_shape=jax.ShapeDtypeStruct((DB, NSA_HEADS, HD), F32),
        compiler_params=_params("arbitrary"),
        name="nsa_sample_b",
    )(page_table, idx, *([cache_kt] * n), *([cache_vt] * n), q3, ocw, nk3, nv3, g3)


SEQ_PER_STEP = 8


def _mlstm_sample_kernel(mqkv_ref, sm_ref, mo_ref, gb_ref, c_ref, n_ref, m_ref,
                         om_ref, c_out, n_out, m_out, *, nb):
    eye = (lax.broadcasted_iota(jnp.int32, (HD, HD), 0) == lax.broadcasted_iota(jnp.int32, (HD, HD), 1))
    lane = lax.broadcasted_iota(jnp.int32, (1, LANES), 1)

    def col(rowv):
        return jnp.sum(jnp.where(eye, jnp.broadcast_to(rowv, (HD, HD)), 0.0), axis=1, keepdims=True)

    for b in range(nb):
        g = sm_ref[b] + gb_ref[...]
        lf = _log_sigmoid(g)
        heads = []
        m_tile = jnp.zeros((1, LANES), F32)
        for hd in range(MH):
            ig = g[:, SM_IG + hd:SM_IG + hd + 1]
            a = lf[:, SM_FG + hd:SM_FG + hd + 1] + m_ref[b][:, hd:hd + 1]
            mt = jnp.maximum(a, ig)
            dm = jnp.exp(ig - mt)
            inter = jnp.exp(a - mt)
            q = mqkv_ref[b][:, HD * hd:HD * (hd + 1)]
            k = mqkv_ref[b][:, MLSTM_W + HD * hd:MLSTM_W + HD * (hd + 1)] * Q_SCALE
            v = mqkv_ref[b][:, 2 * MLSTM_W + HD * hd:2 * MLSTM_W + HD * (hd + 1)]
            c_prev = c_ref[b, hd]
            n_prev = n_ref[b][hd:hd + 1, :]
            qk = jnp.sum(q * k, axis=-1, keepdims=True) * dm
            q_c = jnp.sum(col(q) * c_prev, axis=0, keepdims=True)
            num = qk * v + inter * q_c
            den = qk + inter * jnp.sum(q * n_prev, axis=-1, keepdims=True)
            hout = num / jnp.maximum(jnp.abs(den), jnp.exp(-mt))
            heads.append(jax.nn.sigmoid(mo_ref[b][:, HD * hd:HD * (hd + 1)]) * hout)
            w = jnp.exp(ig - mt)
            decay = jnp.exp(a - mt)
            c_out[b, hd] = decay * c_prev + (w * col(k)) * v
            n_out[b, hd:hd + 1, :] = decay * n_prev + w * k
            m_tile = jnp.where(lane == hd, mt, m_tile)
        om_ref[b] = jnp.concatenate(heads, axis=-1)
        m_out[b] = m_tile


def _mlstm_sample(l, mqkv3, sm3, mo3, gate_b_tile, state_c, state_n, state_m4):
    DB = mqkv3.shape[0]
    nb = min(SEQ_PER_STEP, DB)
    assert DB % nb == 0
    per = lambda w: pl.BlockSpec((nb, 1, w), lambda i: (i, 0, 0))
    return pl.pallas_call(
        functools.partial(_mlstm_sample_kernel, nb=nb),
        grid=(DB // nb,),
        in_specs=[per(3 * MLSTM_W), per(LANES), per(MLSTM_W), _const_spec((1, LANES)),
                  pl.BlockSpec((None, nb, MH, HD, HD), lambda i: (l, i, 0, 0, 0)),
                  pl.BlockSpec((None, nb, MH, HD), lambda i: (l, i, 0, 0)),
                  pl.BlockSpec((None, nb, 1, MH), lambda i: (l, i, 0, 0))],
        out_specs=[per(MLSTM_W), pl.BlockSpec((nb, MH, HD, HD), lambda i: (i, 0, 0, 0)),
                   pl.BlockSpec((nb, MH, HD), lambda i: (i, 0, 0)), per(LANES)],
        out_shape=[jax.ShapeDtypeStruct((DB, 1, MLSTM_W), F32), jax.ShapeDtypeStruct((DB, MH, HD, HD), F32),
                   jax.ShapeDtypeStruct((DB, MH, HD), F32), jax.ShapeDtypeStruct((DB, 1, LANES), F32)],
        compiler_params=_params("arbitrary"),
        name="mlstm_sample",
    )(mqkv3, sm3, mo3, gate_b_tile, state_c, state_n, state_m4)


def _outffn_sample_kernel(x_ref, on_ref, om_ref, op_ref, gt1_ref, sh2_ref, sc2_ref, gt2_ref, g2_ref, fg_ref,
                          wout_ref, wup_ref, cw_ref, cb_ref, wdn_ref, prev_ref, y_ref, up_ref, *, d_ff, final):
    mix = (_dot(on_ref[...].astype(BF16), wout_ref[0:NSA_W, :])
           + _dot(om_ref[...].astype(BF16), wout_ref[NSA_W:NSA_W + MLSTM_W, :])
           + _dot(op_ref[...].astype(BF16), wout_ref[NSA_W + MLSTM_W:, :]))
    x1 = x_ref[...] + gt1_ref[...] * mix
    h2 = _norm_mod(x1, g2_ref[...], sc2_ref[...], sh2_ref[...]).astype(BF16)
    up = _dot(h2, wup_ref[...])
    up_ref[...] = up
    y = cb_ref[...] + cw_ref[FFN_CONV - 1:FFN_CONV, :] * up
    for j in range(FFN_CONV - 1):
        y = y + cw_ref[j:j + 1, :] * prev_ref[:, j, :]
    a, b = y[:, :d_ff], y[:, d_ff:]
    f = _dot((a * jax.nn.sigmoid(a) * b).astype(BF16), wdn_ref[...])
    x2 = x1 + gt2_ref[...] * f
    y_ref[...] = _rmsnorm(x2, fg_ref[...]) if final else x2


def _outffn_sample(l, x, on, om, op, gt1, sh2, sc2, gt2, g2, fg, w_out, w_up, conv_w, conv_b, w_dn,
                   state_ffn, final):
    DB, D = x.shape
    d_ff = w_dn.shape[0]
    full = lambda shape: pl.BlockSpec(shape, lambda i: (0,) * len(shape))
    args = (x, on, om, op, gt1, sh2, sc2, gt2, g2, fg, w_out, w_up, conv_w, conv_b, w_dn)
    return pl.pallas_call(
        functools.partial(_outffn_sample_kernel, d_ff=d_ff, final=final),
        grid=(1,),
        in_specs=[full(a.shape) for a in args]
                 + [pl.BlockSpec((None, DB, FFN_CONV - 1, 2 * d_ff), lambda i: (l, 0, 0, 0))],
        out_specs=[full((DB, D)), full((DB, 2 * d_ff))],
        out_shape=[jax.ShapeDtypeStruct((DB, D), F32), jax.ShapeDtypeStruct((DB, 2 * d_ff), F32)],
        compiler_params=_params("arbitrary"),
        name="outffn_sample",
    )(*args, state_ffn)


def _sample_layer(l, x, mod, lw, caches, final):
    DB, D = x.shape
    ckt, cvt, page_table, win_kt, win_vt, st_c, st_n, st_m, st_pool, st_ffn = caches
    past = page_table.shape[1] * ckt.shape[-1]
    sh1, sc1, gt1, sh2, sc2, gt2 = [mod[:, D * i:D * (i + 1)] for i in range(6)]
    q, nk, nv, wkv, mqkv, mo, op, sm, pu = _inproj_sample(
        l, x, lw["g1"], sh1, sc1, lw["w_b"], lw["w_q"], lw["pool_bd"], lw["pool_scale"], st_pool, past)
    r3 = lambda t: t[:, None, :]
    hk, hv = _compress_sample_t(l, ckt, cvt, page_table, lw["cw"])
    q3 = q.reshape(DB, NSA_HEADS, HD)
    idx, ocw, wk_new, wv_new = _nsa_sample_a_t(l, q3, hk, hv, r3(nk), r3(nv), r3(wkv), win_kt, win_vt,
                                               lw["cw"], past)
    g3 = sm[:, :3 * NSA_HEADS].reshape(DB, NSA_HEADS, 3)
    on = _nsa_sample_b_t(l, ckt, cvt, page_table, idx[:, :HKV, :N_SEL], q3, ocw, r3(nk), r3(nv), g3, past)
    om, c_new, n_new, m_new = _mlstm_sample(l, r3(mqkv), r3(sm), r3(mo), lw["gate_b"], st_c, st_n,
                                            st_m.reshape(st_m.shape[0], DB, 1, MH))
    y, up = _outffn_sample(l, x, on.reshape(DB, NSA_W), om[:, 0], op, gt1, sh2, sc2, gt2,
                           lw["g2"], lw["fg"], lw["w_out"], lw["w_up"], lw["conv_w"],
                           lw["conv_b"], lw["w_dn"], st_ffn, final)
    states = (nk.reshape(DB, 1, 2, HKV, HD), nv.reshape(DB, 1, 2, HKV, HD),
              jnp.transpose(wk_new, (0, 3, 1, 2)), jnp.transpose(wv_new, (0, 3, 1, 2)),
              c_new, n_new, m_new[:, 0, :MH],
              jnp.concatenate([st_pool[l][:, 1:], pu[:, None, :]], axis=1),
              jnp.concatenate([st_ffn[l][:, 1:], up[:, None, :]], axis=1))
    return y, states


def kernel(x_prompt, x_sample, cache_k, cache_v, cache_win_k, cache_win_v, state_mlstm_C, state_mlstm_n,
           state_mlstm_m, state_pool, state_ffn_conv, page_table, c_prompt, c_sample, norm1_g, norm2_g, ada_w,
           ada_b, w_in, nsa_cmp_pos, nsa_cmp_w, nsa_cmp_b, mlstm_gate_b, pool_w, pool_scale, w_out, ffn_w_up,
           ffn_conv_w, ffn_conv_b, ffn_w_down, final_g):
    B = x_prompt.shape[0]
    DB, DS, D = x_sample.shape
    assert DS == 1
    depth = w_in.shape[0]
    w = dict(norm1_g=norm1_g, norm2_g=norm2_g, w_in=w_in, nsa_cmp_pos=nsa_cmp_pos, nsa_cmp_w=nsa_cmp_w,
             nsa_cmp_b=nsa_cmp_b, mlstm_gate_b=mlstm_gate_b, pool_w=pool_w, pool_scale=pool_scale, w_out=w_out,
             ffn_w_up=ffn_w_up, ffn_conv_w=ffn_conv_w, ffn_conv_b=ffn_conv_b, ffn_w_down=ffn_w_down,
             final_g=final_g)
    mod = _ada_mod(jnp.concatenate([c_prompt, c_sample], axis=0), ada_w, ada_b)
    page = cache_k.shape[2]
    ckt = jnp.transpose(cache_k, (0, 1, 3, 4, 5, 2))
    cvt = jnp.transpose(cache_v, (0, 1, 3, 4, 5, 2))
    caches = (ckt, cvt, page_table, jnp.transpose(cache_win_k, (0, 1, 3, 4, 2)),
              jnp.transpose(cache_win_v, (0, 1, 3, 4, 2)),
              state_mlstm_C, state_mlstm_n, state_mlstm_m, state_pool, state_ffn_conv)
    xp, xs = x_prompt, x_sample[:, 0, :]
    acc_p = [[] for _ in range(9)]
    acc_s = [[] for _ in range(9)]
    for l in range(depth):
        lw = _prep_layer(l, w, page)
        final = l == depth - 1
        xp, st_p = _prompt_layer(xp, mod[l, :B][:, None, :], lw, final)
        xs, st_s = _sample_layer(l, xs, mod[l, B:], lw, caches, final)
        for a, v in zip(acc_p, st_p):
            a.append(v)
        for a, v in zip(acc_s, st_s):
            a.append(v)
    sp = [jnp.stack(a, axis=0) for a in acc_p]
    ss = [jnp.stack(a, axis=0) for a in acc_s]
    return (xp, xs[:, None, :], *sp, *ss)
```

```python
import functools

import numpy as np
import jax
import jax.numpy as jnp
from jax import lax
from jax.experimental import pallas as pl
from jax.experimental.pallas import tpu as pltpu

F32 = jnp.float32
BF16 = jnp.bfloat16

HD = 64
NSA_HEADS = 8
HKV = 2
GRP = NSA_HEADS // HKV
MH = 4
NSA_W = NSA_HEADS * HD
KV_W = HKV * HD
MLSTM_W = MH * HD
POOL_W = 4 * HD
CMP_STRIDE = 16
CMP_LEN = 32
SLC_BLOCK = 64
SLC_RATIO = SLC_BLOCK // CMP_STRIDE
N_SEL = 16
WINDOW = 512
POOL_WINDOWS = (2, 4, 8, 16)
POOL_STATE = 15
FFN_CONV = 3
RMS_EPS = 1e-6
NEG = -1e30
FORCE = 1e6
Q_SCALE = HD ** -0.5

LANES = 128
SUBLANES = 8
VMEM_LIMIT = 56 * 1024 * 1024

C_NK = 0
C_NV = C_NK + 256
C_KS = C_NV + 256
C_VS = C_KS + 256
C_VW = C_VS + 256
C_WKV = C_VW + 256
C_MQKV = C_WKV + 256
C_MO = C_MQKV + 3 * MLSTM_W
C_PU = C_MO + MLSTM_W
C_SM = C_PU + POOL_W
C_END = C_SM + LANES
SM_IG = 3 * NSA_HEADS
SM_FG = SM_IG + MH


def _dot(a, b):
    return jnp.dot(a, b, preferred_element_type=F32)


def _dot_nt(a, b):
    return lax.dot_general(a, b, (((1,), (1,)), ((), ())), preferred_element_type=F32)


def _dot_tn(a, b):
    return lax.dot_general(a, b, (((0,), (0,)), ((), ())), preferred_element_type=F32)


def _dot_split3(x, m):
    x1 = x.astype(BF16)
    r1 = x - x1.astype(F32)
    x2 = r1.astype(BF16)
    x3 = (r1 - x2.astype(F32)).astype(BF16)
    return _dot(x1, m) + _dot(x2, m) + _dot(x3, m)


def _masked_softmax(s, mask):
    s = jnp.where(mask, s, NEG)
    e = jnp.where(mask, jnp.exp(s - jnp.max(s, axis=-1, keepdims=True)), 0.0)
    return e / jnp.maximum(jnp.sum(e, axis=-1, keepdims=True), jnp.finfo(jnp.float32).tiny)


def _norm_mod(x, g, sc, sh):
    ms = jnp.mean(x * x, axis=-1, keepdims=True)
    return (x * lax.rsqrt(ms + RMS_EPS) * g) * (1.0 + sc) + sh


def _rmsnorm(x, g):
    ms = jnp.mean(x * x, axis=-1, keepdims=True)
    return x * lax.rsqrt(ms + RMS_EPS) * g


def _log_sigmoid(x):
    return jnp.minimum(x, 0.0) - jnp.log1p(jnp.exp(-jnp.abs(x)))


def _cumsum_rows(x):
    n = x.shape[0]
    row = lax.broadcasted_iota(jnp.int32, x.shape, 0)
    sh = 1
    while sh < n:
        x = x + jnp.where(row >= sh, pltpu.roll(x, sh, 0), 0.0)
        sh *= 2
    return x


def _const_spec(shape):
    nd = len(shape)
    return pl.BlockSpec(shape, lambda *_: (0,) * nd, pipeline_mode=pl.Buffered(1))


def _params(*sem):
    return pltpu.CompilerParams(dimension_semantics=sem, vmem_limit_bytes=VMEM_LIMIT)


def _ada_kernel(c_ref, w_ref, b_ref, o_ref):
    c = c_ref[...]
    s = c * jax.nn.sigmoid(c)
    o_ref[0] = _dot(s.astype(BF16), w_ref[0]) + b_ref[0]


def _ada_mod(c_all, ada_w, ada_b):
    depth, d, n = ada_w.shape
    rows = c_all.shape[0]
    tn = 1536
    return pl.pallas_call(
        _ada_kernel,
        grid=(depth, n // tn),
        in_specs=[pl.BlockSpec((rows, d), lambda l, j: (0, 0)),
                  pl.BlockSpec((1, d, tn), lambda l, j: (l, 0, j)),
                  pl.BlockSpec((1, 1, tn), lambda l, j: (l, 0, j))],
        out_specs=pl.BlockSpec((1, rows, tn), lambda l, j: (l, 0, j)),
        out_shape=jax.ShapeDtypeStruct((depth, rows, n), F32),
        compiler_params=_params("arbitrary", "arbitrary"),
        name="ada_mod",
    )(c_all, ada_w.astype(BF16), ada_b.reshape(depth, 1, n))


def _prep_w_in(w):
    d = w.shape[0]
    o = 0
    nq = w[:, o:o + NSA_W]; o += NSA_W
    nkv = w[:, o:o + 6 * KV_W]; o += 6 * KV_W
    ng = w[:, o:o + 3 * NSA_HEADS]; o += 3 * NSA_HEADS
    mqkv = w[:, o:o + 3 * MLSTM_W]; o += 3 * MLSTM_W
    mif = w[:, o:o + 2 * MH]; o += 2 * MH
    mo = w[:, o:o + MLSTM_W]; o += MLSTM_W
    pu = w[:, o:o + POOL_W]
    k_cmp, v_cmp, k_slc, v_slc, k_win, v_win = [nkv[:, KV_W * i:KV_W * (i + 1)] for i in range(6)]
    z = jnp.zeros((d, HD), w.dtype)
    h0 = lambda t: t[:, :HD]
    h1 = lambda t: t[:, HD:]
    cols = [
        k_cmp, k_slc, v_cmp, v_slc,
        h0(k_slc), z, z, h1(k_slc),
        h0(v_slc), z, h1(v_slc), z,
        h0(v_win), z, h1(v_win), z,
        k_win, v_win, mqkv, mo, pu,
        ng, mif, jnp.zeros((d, LANES - 3 * NSA_HEADS - 2 * MH), w.dtype)]
    out = jnp.concatenate(cols, axis=1).astype(BF16)
    assert out.shape[1] == C_END
    return out


def _pool_mix(pu, sums, cnt, pw_ref, pscale_ref):
    lane = lax.broadcasted_iota(jnp.int32, pu.shape, 1)
    grp = lane // HD
    mean = jnp.where(grp == 0, sums[2] / cnt[2],
                     jnp.where(grp == 1, sums[4] / cnt[4],
                               jnp.where(grp == 2, sums[8] / cnt[8], sums[16] / cnt[16])))
    d = mean - pu
    return _dot(d.astype(BF16), pw_ref[...]) * pscale_ref[...]


def _split2(x):
    hi = x.astype(BF16)
    return hi, (x - hi.astype(F32)).astype(BF16)


def _dot_split(a, b):
    return _dot(a[0], b[0]) + _dot(a[1], b[0]) + _dot(a[0], b[1])


def _inproj_prompt_kernel(x_ref, g_ref, sh_ref, sc_ref, w_ref, wqh_ref, wql_ref, pw_ref, pscale_ref,
                          q_ref, nk_ref, nv_ref, ks_ref, vs_ref, kw_ref, vw_ref, wkv_ref,
                          mqkv_ref, mo_ref, op_ref, sm_ref, ps_ref, zs_ref, *, tm):
    t = pl.program_id(1)
    hf = _norm_mod(x_ref[0], g_ref[...], sc_ref[0], sh_ref[0])
    h, h_lo = _split2(hf)

    def seg(a, b):
        return _dot(h, w_ref[:, a:b])

    q_ref[0] = _dot_split((h, h_lo), (wqh_ref[...], wql_ref[...])) * Q_SCALE
    nk_ref[0] = seg(C_NK, C_NV)
    nv_ref[0] = seg(C_NV, C_KS)
    row = t * tm + lax.broadcasted_iota(jnp.int32, (tm, 2 * LANES), 0)
    lane = lax.broadcasted_iota(jnp.int32, (tm, 2 * LANES), 1)
    is_aux = (lane >= HD) & (lane < 3 * HD)
    aux = jnp.where(lane < LANES, lane - HD, lane - LANES)
    onehot = jnp.where(is_aux & (aux == row // SLC_BLOCK), 1.0, 0.0)
    ks_ref[0] = (seg(C_KS, C_VS) + onehot).astype(BF16)
    ones_col = jnp.where(lane % LANES == HD, 1.0, 0.0)
    vs_ref[0] = (seg(C_VS, C_VW) + ones_col).astype(BF16)
    vw_ref[0] = (seg(C_VW, C_WKV) + ones_col).astype(BF16)
    wkv = seg(C_WKV, C_MQKV)
    wkv_ref[0] = wkv
    kw_ref[0] = wkv[:, :KV_W].astype(BF16)
    mqkv_ref[0] = seg(C_MQKV, C_MO)
    mo_ref[0] = seg(C_MO, C_PU)
    sm_ref[0] = seg(C_SM, C_END)

    pu = seg(C_PU, C_SM)
    halo = 2 * SUBLANES

    @pl.when(t == 0)
    def _():
        zs_ref[0:halo, :] = jnp.zeros((halo, POOL_W), F32)

    @pl.when(t > 0)
    def _():
        zs_ref[0:halo, :] = zs_ref[tm:tm + halo, :]

    zs_ref[halo:halo + tm, :] = pu
    acc = pu
    sums = {}
    for i in range(1, POOL_STATE + 1):
        acc = acc + zs_ref[pl.ds(halo - i, tm), :]
        if i + 1 in POOL_WINDOWS:
            sums[i + 1] = acc
    pos1 = (t * tm + lax.broadcasted_iota(jnp.int32, (tm, 1), 0) + 1).astype(F32)
    cnt = {w: jnp.minimum(float(w), pos1) for w in POOL_WINDOWS}
    op_ref[0] = _pool_mix(pu, sums, cnt, pw_ref, pscale_ref).astype(BF16)
    ps_ref[0] = zs_ref[tm:tm + halo, :]


def _inproj_prompt(x, g1, sh1, sc1, w_b, w_q2, pool_bd, pool_scale, tm=512):
    B, T, D = x.shape
    nT = T // tm
    assert T % tm == 0 and tm >= 2 * SUBLANES and WINDOW % tm == 0
    nwin = WINDOW // tm
    row = lambda w: pl.BlockSpec((1, tm, w), lambda b, t: (b, t, 0))
    mod = pl.BlockSpec((1, 1, D), lambda b, t: (b, 0, 0))
    outs = [
        (row(NSA_W), (B, T, NSA_W), F32),
        (row(256), (B, T, 256), F32),
        (row(256), (B, T, 256), F32),
        (row(256), (B, T, 256), BF16),
        (row(256), (B, T, 256), BF16),
        (row(KV_W), (B, T, KV_W), BF16),
        (row(256), (B, T, 256), BF16),
        (pl.BlockSpec((1, tm, 256), lambda b, t: (b, jnp.maximum(t - (nT - nwin), 0), 0)),
         (B, WINDOW, 256), F32),
        (row(3 * MLSTM_W), (B, T, 3 * MLSTM_W), F32),
        (row(MLSTM_W), (B, T, MLSTM_W), F32),
        (row(POOL_W), (B, T, POOL_W), BF16),
        (row(LANES), (B, T, LANES), F32),
        (pl.BlockSpec((1, 2 * SUBLANES, POOL_W), lambda b, t: (b, 0, 0)), (B, 2 * SUBLANES, POOL_W), F32),
    ]
    return pl.pallas_call(
        functools.partial(_inproj_prompt_kernel, tm=tm),
        grid=(B, nT),
        in_specs=[pl.BlockSpec((1, tm, D), lambda b, t: (b, t, 0)),
                  _const_spec((1, D)), mod, mod,
                  _const_spec(w_b.shape), _const_spec(w_q2[0].shape), _const_spec(w_q2[1].shape),
                  _const_spec(pool_bd.shape), _const_spec((1, POOL_W))],
        out_specs=[o[0] for o in outs],
        out_shape=[jax.ShapeDtypeStruct(o[1], o[2]) for o in outs],
        scratch_shapes=[pltpu.VMEM((2 * SUBLANES + tm, POOL_W), F32)],
        compiler_params=_params("arbitrary", "arbitrary"),
        name="inproj_prompt",
    )(x, g1, sh1, sc1, w_b, w_q2[0], w_q2[1], pool_bd, pool_scale)


def _compress_prompt_kernel(k_ref, v_ref, wpk_ref, wpv_ref, wk_ref, wkl_ref, wv_ref, bk_ref, bv_ref,
                            kc_ref, vc_ref, *, nseg):
    def pooled(src_ref, wp_ref):
        a = jnp.zeros((nseg, KV_W), F32)
        b = jnp.zeros((nseg, KV_W), F32)
        for j in range(CMP_STRIDE):
            xj = src_ref[0, pl.ds(j, nseg, stride=CMP_STRIDE), :]
            a = a + xj * wp_ref[j:j + 1, :]
            b = b + xj * wp_ref[CMP_STRIDE + j:CMP_STRIDE + j + 1, :]
        return a + pltpu.roll(b, nseg - 1, 0)

    kc_ref[0] = _dot_split(_split2(pooled(k_ref, wpk_ref)), (wk_ref[...], wkl_ref[...])) + bk_ref[...]
    vc_ref[0] = (_dot(pooled(v_ref, wpv_ref).astype(BF16), wv_ref[...]) + bv_ref[...]).astype(BF16)


def _compress_prompt(nk, nv, cw):
    B, T, _ = nk.shape
    nseg = T // CMP_STRIDE
    src = pl.BlockSpec((1, T, KV_W), lambda b: (b, 0, 0))
    return pl.pallas_call(
        functools.partial(_compress_prompt_kernel, nseg=nseg),
        grid=(B,),
        in_specs=[src, src, _const_spec((CMP_LEN, KV_W)), _const_spec((CMP_LEN, KV_W)),
                  _const_spec((KV_W, KV_W)), _const_spec((KV_W, KV_W)), _const_spec((KV_W, 2 * KV_W)),
                  _const_spec((1, KV_W)), _const_spec((1, 2 * KV_W))],
        out_specs=[pl.BlockSpec((1, nseg, KV_W), lambda b: (b, 0, 0)),
                   pl.BlockSpec((1, nseg, 2 * KV_W), lambda b: (b, 0, 0))],
        out_shape=[jax.ShapeDtypeStruct((B, nseg, KV_W), F32),
                   jax.ShapeDtypeStruct((B, nseg, 2 * KV_W), BF16)],
        compiler_params=_params("arbitrary"),
        name="compress_prompt",
    )(nk, nv, cw["wpk"], cw["wpv"], cw["wk"], cw["wk_lo"], cw["wv_dup"], cw["bk"], cw["bv_dup"])


def _prep_compress(cmp_pos, cmp_w, cmp_b, page):
    tile2 = lambda t: jnp.concatenate([t, t], axis=-1)
    z = jnp.zeros((HD, HD), F32)
    wk, wv = cmp_w[0], cmp_w[1]
    wk_bd = jnp.block([[wk, z], [z, wk]])
    wv_dup = jnp.block([[wv, z, z, z], [z, z, wv, z]])

    def pos_t(p):
        reps = page // CMP_STRIDE
        return jnp.concatenate([jnp.tile(p[:CMP_STRIDE].T, (1, reps)), jnp.tile(p[CMP_STRIDE:].T, (1, reps))], axis=0)

    return dict(
        wpk=tile2(cmp_pos[0]), wpv=tile2(cmp_pos[1]),
        wk=wk_bd.astype(BF16), wk_lo=(wk_bd - wk_bd.astype(BF16).astype(F32)).astype(BF16),
        wv_dup=wv_dup.astype(BF16),
        bk=tile2(cmp_b[0])[None, :],
        bv_dup=jnp.concatenate([cmp_b[1], jnp.zeros((HD,), F32)] * 2)[None, :],
        wpk_t=pos_t(cmp_pos[0]), wpv_t=pos_t(cmp_pos[1]),
        wpk_b0=cmp_pos[0][CMP_STRIDE][:, None], wpv_b0=cmp_pos[1][CMP_STRIDE][:, None],
        wk_t=wk.T.astype(BF16), wv_t=wv.T.astype(BF16),
        bk_col=cmp_b[0][:, None], bv_col=cmp_b[1][:, None])


def _importance_matrix(nc_rows, nc_valid, ns, lane_off, width):
    m = np.zeros((nc_rows, width), np.float32)
    for j in range(ns):
        for n in range(SLC_RATIO * j - 1, SLC_RATIO * j + SLC_RATIO):
            if 0 <= n < nc_valid:
                m[n, lane_off + j] = 1.0
    return m


def _select_blocks_t(st_ref, h, ns):
    nb, nq = st_ref.shape[1], st_ref.shape[2]
    ngrp = nb // SUBLANES
    groups = [st_ref[h, SUBLANES * r:SUBLANES * (r + 1), :] for r in range(ngrp)]
    ranks = [jnp.zeros((SUBLANES, nq), F32) for _ in range(ngrp)]
    jrow = lax.broadcasted_iota(jnp.int32, (SUBLANES, nq), 0)
    for jp in range(min(ns, nb)):
        row = jnp.broadcast_to(st_ref[h, pl.ds(jp, 1), :], (SUBLANES, nq))
        for r in range(ngrp):
            ge = jnp.where(row >= groups[r], 1.0, 0.0)
            gt = jnp.where(row > groups[r], 1.0, 0.0)
            if jp < SUBLANES * r:
                inc = ge
            elif jp >= SUBLANES * (r + 1):
                inc = gt
            else:
                inc = jnp.where(jrow + SUBLANES * r > jp, ge, gt)
            ranks[r] = ranks[r] + inc
    return jnp.concatenate(
        [jnp.where((ranks[r] < N_SEL) & (groups[r] >= 0.0), 0.0, NEG) for r in range(ngrp)], axis=0)


def _nsa_prompt_kernel(q_ref, sm_ref, kc_ref, vc_ref, ks_ref, vs_ref, kw_ref, vw_ref, mt_ref,
                       o_ref, acc_ref, m_ref, qa_ref, st_ref, *, T, tk, qb):
    i = pl.program_id(1)
    ns = T // SLC_BLOCK
    nc = kc_ref.shape[1]
    rows = GRP * qb
    qpos = i * qb + lax.broadcasted_iota(jnp.int32, (qb, 1), 0)
    qpos4 = jnp.concatenate([qpos] * GRP, axis=0)
    qpos_t = i * qb + lax.broadcasted_iota(jnp.int32, (HD, qb), 1)
    jb_t = lax.broadcasted_iota(jnp.int32, (HD, qb), 0)
    sig = jax.nn.sigmoid(sm_ref[0])
    lane = lax.broadcasted_iota(jnp.int32, (qb, LANES), 1)
    wk = WINDOW + qb
    n_idx = lax.broadcasted_iota(jnp.int32, (1, nc), 1)
    cmp_mask = (CMP_STRIDE * n_idx + CMP_LEN - 1) <= qpos4
    wstart = pl.multiple_of(jnp.maximum(i - WINDOW // qb, 0) * qb, qb)
    dlt = qpos - (wstart + lax.broadcasted_iota(jnp.int32, (1, wk), 1))
    wbias = jnp.where((dlt >= 0) & (dlt < WINDOW), 0.0, NEG)
    cur_t = qpos_t // SLC_BLOCK
    valid_t = (jb_t * SLC_BLOCK <= qpos_t) & (jb_t < ns)
    forced_t = (jb_t == 0) | (jb_t == cur_t) | (jb_t == cur_t - 1)
    zpad = jnp.zeros((LANES - HD, qb), F32)
    kc_hi, kc_lo = _split2(kc_ref[0])

    o_cs, o_ws = [], []
    for h in range(HKV):
        parts = []
        for g in range(GRP):
            hd = GRP * h + g
            src = q_ref[0, :, LANES * (hd // 2):LANES * (hd // 2 + 1)]
            if hd % 2 != h:
                src = pltpu.roll(src, HD, 1)
            parts.append(jnp.where((lane < HD) if h == 0 else (lane >= HD), src, 0.0))
        qh, qh_lo = _split2(jnp.concatenate(parts, axis=0))

        p_c = _masked_softmax(_dot_nt(qh, kc_hi) + _dot_nt(qh_lo, kc_hi) + _dot_nt(qh, kc_lo), cmp_mask)
        o_cs.append(_dot(p_c.astype(BF16), vc_ref[0, :, LANES * h:LANES * (h + 1)]))

        psum = p_c[0:qb] + p_c[qb:2 * qb] + p_c[2 * qb:3 * qb] + p_c[3 * qb:4 * qb]
        x1 = psum.astype(BF16)
        r1 = psum - x1.astype(F32)
        x2 = r1.astype(BF16)
        x3 = (r1 - x2.astype(F32)).astype(BF16)
        mt = mt_ref[...]
        imp_t = _dot_nt(mt, x1) + _dot_nt(mt, x2) + _dot_nt(mt, x3)
        score_t = jnp.where(valid_t, jnp.where(forced_t, FORCE, imp_t), -1.0)
        st_ref[h] = jnp.where(jb_t < ns, score_t, -jnp.inf)
        sel_t = _select_blocks_t(st_ref, h, ns)
        selneg = (jnp.concatenate([zpad, sel_t], axis=0) if h == 0
                  else jnp.concatenate([sel_t, zpad], axis=0)).T
        qa_ref[h] = qh + jnp.concatenate([selneg.astype(BF16)] * GRP, axis=0)

        sw = _dot_nt(qh, kw_ref[0, pl.ds(wstart, wk), :])
        sw = jnp.concatenate([sw[qb * g:qb * (g + 1)] + wbias for g in range(GRP)], axis=0)
        e = jnp.exp(sw - jnp.max(sw, axis=-1, keepdims=True))
        o_w = _dot(e.astype(BF16), vw_ref[0, pl.ds(wstart, wk), LANES * h:LANES * (h + 1)])
        o_ws.append(o_w / o_w[:, HD:HD + 1])

    m_ref[...] = jnp.full(m_ref.shape, -jnp.inf, F32)
    acc_ref[...] = jnp.zeros(acc_ref.shape, F32)

    def chunk(c, diagonal):
        start = pl.multiple_of(c * tk, tk)
        for h in range(HKV):
            k = ks_ref[0, pl.ds(start, tk), LANES * h:LANES * (h + 1)]
            v = vs_ref[0, pl.ds(start, tk), LANES * h:LANES * (h + 1)]
            sc = _dot_nt(qa_ref[h], k)
            if diagonal:
                tok = start + lax.broadcasted_iota(jnp.int32, (1, tk), 1)
                sc = jnp.where(tok <= qpos4, sc, NEG)
            m_prev = m_ref[h]
            m_new = jnp.maximum(m_prev, jnp.max(sc, axis=-1, keepdims=True))
            alpha = jnp.exp(m_prev - m_new)
            p = jnp.exp(sc - m_new[:, 0:1])
            acc_ref[h] = alpha * acc_ref[h] + _dot(p.astype(BF16), v)
            m_ref[h] = m_new

    n_full = (i * qb) // tk

    def full_chunk(c, carry):
        chunk(c, False)
        return carry

    lax.fori_loop(0, n_full, full_chunk, 0)
    chunk(n_full, True)

    for h in range(HKV):
        acc = acc_ref[h]
        o_s = acc / acc[:, HD:HD + 1]
        outs = []
        for g in range(GRP):
            c0 = 3 * (GRP * h + g)
            r = slice(qb * g, qb * (g + 1))
            outs.append(sig[:, c0:c0 + 1] * o_cs[h][r] + sig[:, c0 + 1:c0 + 2] * o_s[r]
                        + sig[:, c0 + 2:c0 + 3] * o_ws[h][r])
        base = GRP * HD * h
        for pair in range(GRP // 2):
            hi = pltpu.roll(outs[2 * pair + 1], HD, 1)
            o_ref[0, :, base + LANES * pair:base + LANES * (pair + 1)] = jnp.where(
                lane < HD, outs[2 * pair], hi).astype(BF16)


def _nsa_prompt(q, sm, kc, vc, ks, vs, kw, vw, qb=256, tk=512):
    B, T, _ = q.shape
    ns = T // SLC_BLOCK
    nc = kc.shape[1]
    assert ns <= HD and T % tk == 0 and T >= WINDOW + qb and WINDOW % qb == 0 and tk % qb == 0
    m2 = _importance_matrix(nc, nc - 1, ns, 0, HD).T
    full = lambda w: pl.BlockSpec((1, T, w), lambda b, i: (b, 0, 0))
    return pl.pallas_call(
        functools.partial(_nsa_prompt_kernel, T=T, tk=tk, qb=qb),
        grid=(B, T // qb),
        in_specs=[pl.BlockSpec((1, qb, NSA_W), lambda b, i: (b, i, 0)),
                  pl.BlockSpec((1, qb, LANES), lambda b, i: (b, i, 0)),
                  pl.BlockSpec((1, nc, KV_W), lambda b, i: (b, 0, 0)),
                  pl.BlockSpec((1, nc, 2 * KV_W), lambda b, i: (b, 0, 0)),
                  full(256), full(256), full(KV_W), full(256),
                  _const_spec(m2.shape)],
        out_specs=pl.BlockSpec((1, qb, NSA_W), lambda b, i: (b, i, 0)),
        out_shape=jax.ShapeDtypeStruct((B, T, NSA_W), BF16),
        scratch_shapes=[pltpu.VMEM((HKV, GRP * qb, LANES), F32)] * 2
                       + [pltpu.VMEM((HKV, GRP * qb, LANES), BF16), pltpu.VMEM((HKV, HD, qb), F32)],
        compiler_params=_params("arbitrary", "arbitrary"),
        name="nsa_prompt",
    )(q, sm, kc, vc, ks, vs, kw, vw, jnp.asarray(m2, BF16))


def _mlstm_gates(g):
    b = pltpu.roll(_cumsum_rows(_log_sigmoid(g)), LANES - MH, 1)
    return b, g - b


def _mlstm_prompt_kernel(mqkv_ref, sm_ref, mo_ref, gb_ref, om_ref, c_out, n_out, m_out,
                         c_s, n_s, m_s, *, B, L):
    t = pl.program_id(0)

    @pl.when(t == 0)
    def _():
        c_s[...] = jnp.zeros(c_s.shape, F32)
        n_s[...] = jnp.zeros(n_s.shape, F32)
        m_s[...] = jnp.zeros(m_s.shape, F32)

    li = lax.broadcasted_iota(jnp.int32, (L, L), 0)
    si = lax.broadcasted_iota(jnp.int32, (L, L), 1)
    for b in range(B):
        b_al, r = _mlstm_gates(sm_ref[b] + gb_ref[...])
        r_t = jnp.concatenate([r, jnp.zeros((LANES - L, LANES), F32)], axis=0).T if L < LANES else r.T
        heads = []
        for hd in range(MH):
            idx = b * MH + hd
            bcol = b_al[:, SM_IG + hd:SM_IG + hd + 1]
            rcol = r[:, SM_IG + hd:SM_IG + hd + 1]
            rrow = r_t[SM_IG + hd:SM_IG + hd + 1, 0:L]
            mprev = m_s[idx][:, 0:1]
            acol = bcol + mprev
            logd = jnp.where(si <= li, bcol + rrow, NEG)
            mt = jnp.maximum(acol, jnp.max(logd, axis=-1, keepdims=True))
            dm = jnp.exp(logd - mt)
            inter = jnp.exp(acol - mt)
            q = mqkv_ref[b, :, HD * hd:HD * (hd + 1)]
            k = mqkv_ref[b, :, MLSTM_W + HD * hd:MLSTM_W + HD * (hd + 1)] * Q_SCALE
            v = mqkv_ref[b, :, 2 * MLSTM_W + HD * hd:2 * MLSTM_W + HD * (hd + 1)]
            qb, kb = q.astype(BF16), k.astype(BF16)
            qk = _dot_nt(qb, kb) * dm
            c_prev = c_s[idx]
            n_prev = n_s[idx]
            num = _dot(qk.astype(BF16), v.astype(BF16)) + inter * _dot(qb, c_prev.astype(BF16))
            den = jnp.sum(qk, axis=-1, keepdims=True) + inter * jnp.sum(q * n_prev, axis=-1, keepdims=True)
            hout = num / jnp.maximum(jnp.abs(den), jnp.exp(-mt))
            heads.append(jax.nn.sigmoid(mo_ref[b, :, HD * hd:HD * (hd + 1)]) * hout)
            m_last = mt[L - 1:L, :]
            b_last = bcol[L - 1:L, :]
            wcol = jnp.exp(b_last + rcol - m_last)
            decay = jnp.exp(b_last + mprev - m_last)
            c_s[idx] = decay * c_prev + _dot_tn(kb, (wcol * v).astype(BF16))
            n_s[idx] = decay * n_prev + jnp.sum(wcol * k, axis=0, keepdims=True)
            m_s[idx] = jnp.broadcast_to(m_last, (1, LANES))
        om_ref[b] = jnp.concatenate(heads, axis=-1).astype(BF16)

    @pl.when(t == pl.num_programs(0) - 1)
    def _():
        c_out[...] = c_s[...]
        n_out[...] = n_s[...]
        m_out[...] = m_s[...]


def _mlstm_prompt(mqkv, sm, mo, gate_b_tile, L=128):
    B, T, _ = mqkv.shape
    assert T % L == 0 and L <= LANES
    blk = lambda w: pl.BlockSpec((B, L, w), lambda t: (0, t, 0))
    st = lambda shape: pl.BlockSpec(shape, lambda t: (0,) * len(shape))
    shapes = [(B * MH, HD, HD), (B * MH, 1, HD), (B * MH, 1, LANES)]
    return pl.pallas_call(
        functools.partial(_mlstm_prompt_kernel, B=B, L=L),
        grid=(T // L,),
        in_specs=[blk(3 * MLSTM_W), blk(LANES), blk(MLSTM_W), _const_spec((1, LANES))],
        out_specs=[blk(MLSTM_W)] + [st(s) for s in shapes],
        out_shape=[jax.ShapeDtypeStruct((B, T, MLSTM_W), BF16)] + [jax.ShapeDtypeStruct(s, F32) for s in shapes],
        scratch_shapes=[pltpu.VMEM(s, F32) for s in shapes],
        compiler_params=_params("arbitrary"),
        name="mlstm_prompt",
    )(mqkv, sm, mo, gate_b_tile)


FF_CHUNK = 256


def _outffn_prompt_kernel(x_ref, on_ref, om_ref, op_ref, gt1_ref, sh2_ref, sc2_ref, gt2_ref,
                          g2_ref, fg_ref, wout_ref, wup_ref, cw_ref, cb_ref, wdn_ref,
                          y_ref, fs_ref, prev_ref, es_ref, *, tm, d_ff, final):
    t = pl.program_id(1)
    mix = (_dot(on_ref[0], wout_ref[0:NSA_W, :])
           + _dot(om_ref[0], wout_ref[NSA_W:NSA_W + MLSTM_W, :])
           + _dot(op_ref[0], wout_ref[NSA_W + MLSTM_W:, :]))
    x1 = x_ref[0] + gt1_ref[0] * mix
    h2 = _norm_mod(x1, g2_ref[...], sc2_ref[0], sh2_ref[0]).astype(BF16)

    @pl.when(t == 0)
    def _():
        prev_ref[...] = jnp.zeros(prev_ref.shape, F32)

    f = jnp.zeros(x1.shape, F32)
    w = FF_CHUNK
    for c in range(d_ff // w):
        for half, off in ((0, c * w), (1, d_ff + c * w)):
            es_ref[0:SUBLANES, half * w:(half + 1) * w] = prev_ref[:, off:off + w]
            es_ref[SUBLANES:SUBLANES + tm, half * w:(half + 1) * w] = _dot(h2, wup_ref[:, off:off + w])
            prev_ref[:, off:off + w] = es_ref[tm:tm + SUBLANES, half * w:(half + 1) * w]
        ys = []
        for half, off in ((0, c * w), (1, d_ff + c * w)):
            y = cb_ref[:, off:off + w]
            for j in range(FFN_CONV):
                y = y + cw_ref[j:j + 1, off:off + w] * es_ref[pl.ds(SUBLANES - (FFN_CONV - 1) + j, tm), half * w:(half + 1) * w]
            ys.append(y)
        act = ys[0] * jax.nn.sigmoid(ys[0]) * ys[1]
        f = f + _dot(act.astype(BF16), wdn_ref[c * w:(c + 1) * w, :])
    x2 = x1 + gt2_ref[0] * f
    y_ref[0] = _rmsnorm(x2, fg_ref[...]) if final else x2
    fs_ref[0] = prev_ref[...]


def _outffn_prompt(x, on, om, op, gt1, sh2, sc2, gt2, g2, fg, w_out, w_up, conv_w, conv_b, w_dn, final, tm=512):
    B, T, D = x.shape
    d_ff = w_dn.shape[0]
    assert T % tm == 0 and d_ff % FF_CHUNK == 0
    row = lambda w: pl.BlockSpec((1, tm, w), lambda b, t: (b, t, 0))
    mod = pl.BlockSpec((1, 1, D), lambda b, t: (b, 0, 0))
    return pl.pallas_call(
        functools.partial(_outffn_prompt_kernel, tm=tm, d_ff=d_ff, final=final),
        grid=(B, T // tm),
        in_specs=[row(D), row(NSA_W), row(MLSTM_W), row(POOL_W), mod, mod, mod, mod,
                  _const_spec((1, D)), _const_spec((1, D)), _const_spec(w_out.shape), _const_spec(w_up.shape),
                  _const_spec(conv_w.shape), _const_spec(conv_b.shape), _const_spec(w_dn.shape)],
        out_specs=[row(D), pl.BlockSpec((1, SUBLANES, 2 * d_ff), lambda b, t: (b, 0, 0))],
        out_shape=[jax.ShapeDtypeStruct((B, T, D), F32), jax.ShapeDtypeStruct((B, SUBLANES, 2 * d_ff), F32)],
        scratch_shapes=[pltpu.VMEM((SUBLANES, 2 * d_ff), F32), pltpu.VMEM((SUBLANES + tm, 2 * FF_CHUNK), F32)],
        compiler_params=_params("arbitrary", "arbitrary"),
        name="outffn_prompt",
    )(x, on, om, op, gt1, sh2, sc2, gt2, g2, fg, w_out, w_up, conv_w, conv_b, w_dn)


def _prep_layer(l, w, page):
    gate_b = jnp.zeros((1, LANES), F32).at[0, SM_IG:SM_IG + 2 * MH].set(w["mlstm_gate_b"][l])
    pw = w["pool_w"][l]
    z = jnp.zeros((HD, HD), F32)
    pool_bd = jnp.block([[pw[i] if i == j else z for j in range(4)] for i in range(4)]).astype(BF16)
    wq = w["w_in"][l][:, :NSA_W]
    wq_hi = wq.astype(BF16)
    return dict(
        w_q=wq_hi, w_q2=(wq_hi, (wq - wq_hi.astype(F32)).astype(BF16)),
        g1=w["norm1_g"][l][None], g2=w["norm2_g"][l][None],
        w_b=_prep_w_in(w["w_in"][l]), pool_bd=pool_bd, pool_scale=w["pool_scale"][l][None],
        cw=_prep_compress(w["nsa_cmp_pos"][l], w["nsa_cmp_w"][l], w["nsa_cmp_b"][l], page),
        gate_b=gate_b, w_out=w["w_out"][l].astype(BF16), w_up=w["ffn_w_up"][l].astype(BF16),
        conv_w=w["ffn_conv_w"][l], conv_b=w["ffn_conv_b"][l][None], w_dn=w["ffn_w_down"][l].astype(BF16),
        fg=w["final_g"][None])


def _prompt_layer(x, mod, lw, final):
    B, T, D = x.shape
    sh1, sc1, gt1, sh2, sc2, gt2 = [mod[:, :, D * i:D * (i + 1)] for i in range(6)]
    (q, nk, nv, ks, vs, kw, vw, wkv, mqkv, mo, op, sm, ps) = _inproj_prompt(
        x, lw["g1"], sh1, sc1, lw["w_b"], lw["w_q2"], lw["pool_bd"], lw["pool_scale"])
    kc, vc = _compress_prompt(nk, nv, lw["cw"])
    on = _nsa_prompt(q, sm, kc, vc, ks, vs, kw, vw)
    om, c_st, n_st, m_st = _mlstm_prompt(mqkv, sm, mo, lw["gate_b"])
    y, fs = _outffn_prompt(x, on, om, op, gt1, sh2, sc2, gt2, lw["g2"], lw["fg"], lw["w_out"],
                           lw["w_up"], lw["conv_w"], lw["conv_b"], lw["w_dn"], final)
    wb = wkv.shape[1]
    states = (nk.reshape(B, T, 2, HKV, HD), nv.reshape(B, T, 2, HKV, HD),
              wkv[:, :, :KV_W].reshape(B, wb, HKV, HD), wkv[:, :, KV_W:].reshape(B, wb, HKV, HD),
              c_st.reshape(B, MH, HD, HD), n_st.reshape(B, MH, HD), m_st[:, 0, 0].reshape(B, MH),
              ps[:, 2 * SUBLANES - POOL_STATE:], fs[:, SUBLANES - (FFN_CONV - 1):])
    return y, states


def _inproj_sample_kernel(x_ref, g_ref, sh_ref, sc_ref, w_ref, wq_ref, pw_ref, pscale_ref, prev_ref,
                          q_ref, nk_ref, nv_ref, wkv_ref, mqkv_ref, mo_ref, op_ref, sm_ref, pu_ref, *, pos0):
    h = _norm_mod(x_ref[...], g_ref[...], sc_ref[...], sh_ref[...]).astype(BF16)

    def seg(a, b):
        return _dot(h, w_ref[:, a:b])

    q_ref[...] = _dot(h, wq_ref[...]) * Q_SCALE
    nk_ref[...] = seg(C_NK, C_NV)
    nv_ref[...] = seg(C_NV, C_KS)
    wkv_ref[...] = seg(C_WKV, C_MQKV)
    mqkv_ref[...] = seg(C_MQKV, C_MO)
    mo_ref[...] = seg(C_MO, C_PU)
    sm_ref[...] = seg(C_SM, C_END)
    pu = seg(C_PU, C_SM)
    pu_ref[...] = pu
    acc = pu
    sums = {}
    for i in range(1, POOL_STATE + 1):
        acc = acc + prev_ref[:, POOL_STATE - i, :]
        if i + 1 in POOL_WINDOWS:
            sums[i + 1] = acc
    cnt = {w: float(min(w, pos0 + 1)) for w in POOL_WINDOWS}
    op_ref[...] = _pool_mix(pu, sums, cnt, pw_ref, pscale_ref)


def _inproj_sample(l, x, g1, sh1, sc1, w_b, w_q, pool_bd, pool_scale, state_pool, pos0):
    DB, D = x.shape
    full = lambda shape: pl.BlockSpec(shape, lambda i: (0,) * len(shape))
    widths = [NSA_W, 256, 256, 256, 3 * MLSTM_W, MLSTM_W, POOL_W, LANES, POOL_W]
    return pl.pallas_call(
        functools.partial(_inproj_sample_kernel, pos0=pos0),
        grid=(1,),
        in_specs=[full((DB, D)), full((1, D)), full((DB, D)), full((DB, D)), full(w_b.shape),
                  full(w_q.shape), full(pool_bd.shape), full((1, POOL_W)),
                  pl.BlockSpec((None, DB, POOL_STATE, POOL_W), lambda i: (l, 0, 0, 0))],
        out_specs=[full((DB, w)) for w in widths],
        out_shape=[jax.ShapeDtypeStruct((DB, w), F32) for w in widths],
        compiler_params=_params("arbitrary"),
        name="inproj_sample",
    )(x, g1, sh1, sc1, w_b, w_q, pool_bd, pool_scale, state_pool)


PAGES_PER_STEP = 16


def _compress_sample_kernel(pt_ref, *refs, npg):
    del pt_ref
    k_refs, v_refs = refs[:npg], refs[npg:2 * npg]
    wpk_ref, wpv_ref, ak_ref, bk_ref, av_ref, bv_ref = refs[2 * npg:]
    seg = SUBLANES

    def halves(page_ref, wp_ref):
        x = page_ref[...].reshape(page_ref.shape[0] // CMP_STRIDE, CMP_STRIDE, KV_W)
        a = jnp.sum(x * wp_ref[0:CMP_STRIDE, :][None], axis=1)
        b = jnp.sum(x * wp_ref[CMP_STRIDE:CMP_LEN, :][None], axis=1)
        return a, b

    for i in range(npg):
        n = k_refs[i].shape[0] // CMP_STRIDE
        a, b = halves(k_refs[i], wpk_ref)
        ak_ref[0, i * n:(i + 1) * n, :] = a
        bk_ref[0, i * n:(i + 1) * n, :] = b
        a, b = halves(v_refs[i], wpv_ref)
        av_ref[0, i * n:(i + 1) * n, :] = a
        bv_ref[0, i * n:(i + 1) * n, :] = b


def _compress_sample(l, cache_k, cache_v, page_table, cw):
    DB, n_pages = page_table.shape
    page = cache_k.shape[2]
    npg = min(PAGES_PER_STEP, n_pages)
    assert n_pages % npg == 0 and page % CMP_STRIDE == 0
    spp = page // CMP_STRIDE
    nseg = n_pages * spp

    def page_spec(i):
        return pl.BlockSpec((None, None, page, KV_W), lambda b, c, pt: (l, pt[b, c * npg + i], 0, 0))

    out = pl.BlockSpec((1, npg * spp, KV_W), lambda b, c, pt: (b, c, 0))
    wp = pl.BlockSpec((CMP_LEN, KV_W), lambda b, c, pt: (0, 0))
    grid_spec = pltpu.PrefetchScalarGridSpec(
        num_scalar_prefetch=1, grid=(DB, n_pages // npg),
        in_specs=[page_spec(i) for i in range(npg)] * 2 + [wp, wp],
        out_specs=[out] * 4)
    return pl.pallas_call(
        functools.partial(_compress_sample_kernel, npg=npg),
        grid_spec=grid_spec,
        out_shape=[jax.ShapeDtypeStruct((DB, nseg, KV_W), F32)] * 4,
        compiler_params=_params("arbitrary", "arbitrary"),
        name="compress_sample",
    )(page_table, *([cache_k] * npg), *([cache_v] * npg), cw["wpk"], cw["wpv"])


def _nsa_sample_a_kernel(q_ref, ak_ref, bk_ref, av_ref, bv_ref, nk_ref, nv_ref, wkv_ref, wink_ref, winv_ref,
                         wpk_ref, wpv_ref, wk_ref, wv_ref, bkb_ref, bvb_ref, m_ref,
                         idx_ref, ocw_ref, wko_ref, wvo_ref, *, past, ns):
    ncs = ak_ref.shape[1]
    q8 = q_ref[0].astype(BF16)
    rown = lax.broadcasted_iota(jnp.int32, (ncs, 1), 0)

    def comp(a_ref, b_ref, new_row, wp_ref, w_ref, bias_ref):
        b_new = new_row * wp_ref[CMP_STRIDE:CMP_STRIDE + 1, :]
        pooled = a_ref[0] + jnp.where(rown == ncs - 1, b_new, pltpu.roll(b_ref[0], ncs - 1, 0))
        return (_dot(pooled.astype(BF16), w_ref[...]) + bias_ref[...]).astype(BF16)

    kc = comp(ak_ref, bk_ref, nk_ref[0][:, :KV_W], wpk_ref, wk_ref, bkb_ref)
    vc = comp(av_ref, bv_ref, nv_ref[0][:, :KV_W], wpv_ref, wv_ref, bvb_ref)
    n_idx = lax.broadcasted_iota(jnp.int32, (1, ncs), 1)
    p = _masked_softmax(_dot_nt(q8, kc), (CMP_STRIDE * n_idx + CMP_LEN - 1) <= past)
    o_c = _dot(p.astype(BF16), vc)

    rowi = lax.broadcasted_iota(jnp.int32, p.shape, 0)
    p0 = jnp.sum(p[0:GRP], axis=0, keepdims=True)
    p1 = jnp.sum(p[GRP:2 * GRP], axis=0, keepdims=True)
    imp = _dot_split3(jnp.where(rowi == 0, p0, jnp.where(rowi == 1, p1, 0.0)), m_ref[...])
    lane = lax.broadcasted_iota(jnp.int32, imp.shape, 1)
    cur = past // SLC_BLOCK
    valid = lane * SLC_BLOCK <= past
    forced = (lane == 0) | (lane == cur) | (lane == cur - 1)
    score = jnp.where(valid, jnp.where(forced, FORCE, imp), -1.0)
    score = jnp.where(lane < ns, score, -jnp.inf)
    lane_o = lax.broadcasted_iota(jnp.int32, (SUBLANES, LANES), 1)
    idx = jnp.full((SUBLANES, LANES), -1, jnp.int32)
    for r in range(min(N_SEL, ns)):
        mx = jnp.max(score, axis=-1, keepdims=True)
        first = jnp.min(jnp.where(score == mx, lane, 1 << 20), axis=-1, keepdims=True)
        idx = jnp.where(lane_o == r, jnp.where(mx >= 0.0, first, -1), idx)
        score = jnp.where(lane == first, -jnp.inf, score)
    idx_ref[0] = idx

    wb = wink_ref.shape[0]
    k_new = wkv_ref[0][:, :KV_W]
    v_new = wkv_ref[0][:, KV_W:]
    s_w = _dot_nt(q8, wink_ref[...].astype(BF16))
    dlt = wb - lax.broadcasted_iota(jnp.int32, (1, wb), 1)
    mask = (dlt < WINDOW) & (past - dlt >= 0)
    s_new = jnp.sum(q8.astype(F32) * k_new.astype(BF16).astype(F32), axis=-1, keepdims=True)
    s_w = jnp.where(mask, s_w, NEG)
    mx = jnp.maximum(jnp.max(s_w, axis=-1, keepdims=True), s_new)
    e = jnp.where(mask, jnp.exp(s_w - mx), 0.0)
    e_new = jnp.exp(s_new - mx)
    den = jnp.sum(e, axis=-1, keepdims=True) + e_new
    o_w = (_dot(e.astype(BF16), winv_ref[...].astype(BF16)) + e_new * v_new.astype(BF16).astype(F32)) / den
    ocw_ref[0] = jnp.concatenate([o_c, o_w], axis=-1)
    wko_ref[0, pl.ds(0, wb - 1), :] = wink_ref[pl.ds(1, wb - 1), :]
    wko_ref[0, pl.ds(wb - 1, 1), :] = k_new
    wvo_ref[0, pl.ds(0, wb - 1), :] = winv_ref[pl.ds(1, wb - 1), :]
    wvo_ref[0, pl.ds(wb - 1, 1), :] = v_new


def _nsa_sample_a(l, q3, halves, nk3, nv3, wkv3, win_k, win_v, cw, past):
    DB = q3.shape[0]
    ncs = halves[0].shape[1]
    wb = win_k.shape[2]
    tot = past + 1
    ns = -(-tot // SLC_BLOCK)
    nsl = -(-ns // LANES) * LANES
    m = _importance_matrix(ncs, ncs, ns, 0, nsl)
    per = lambda shape: pl.BlockSpec((1,) + shape, lambda b: (b,) + (0,) * len(shape))
    win = pl.BlockSpec((None, None, wb, KV_W), lambda b: (l, b, 0, 0))
    shapes = [(DB, SUBLANES, LANES), (DB, NSA_HEADS, 2 * KV_W), (DB, wb, KV_W), (DB, wb, KV_W)]
    dts = [jnp.int32, F32, F32, F32]
    return pl.pallas_call(
        functools.partial(_nsa_sample_a_kernel, past=past, ns=ns),
        grid=(DB,),
        in_specs=[per((NSA_HEADS, LANES))] + [per((ncs, KV_W))] * 4 + [per((1, 256))] * 3 + [win, win]
                 + [_const_spec((CMP_LEN, KV_W))] * 2 + [_const_spec((KV_W, KV_W))] * 2
                 + [_const_spec((1, KV_W))] * 2 + [_const_spec(m.shape)],
        out_specs=[per(s[1:]) for s in shapes],
        out_shape=[jax.ShapeDtypeStruct(s, d) for s, d in zip(shapes, dts)],
        compiler_params=_params("arbitrary"),
        name="nsa_sample_a",
    )(q3, *halves, nk3, nv3, wkv3, win_k, win_v, cw["wpk"], cw["wpv"], cw["wk"], cw["wv"],
      cw["bk"], cw["bv"], jnp.asarray(m, BF16))


def _nsa_sample_b_kernel(pt_ref, ix_ref, *refs, past, nbp, nsel):
    del pt_ref
    k_refs, v_refs = refs[:HKV * nsel], refs[HKV * nsel:2 * HKV * nsel]
    q_ref, ocw_ref, nk_ref, nv_ref, g_ref, o_ref = refs[2 * HKV * nsel:]
    b = pl.program_id(0)
    q8 = q_ref[0].astype(BF16)
    k_new = nk_ref[0][:, KV_W:].astype(BF16).astype(F32)
    v_new = nv_ref[0][:, KV_W:].astype(BF16).astype(F32)
    s_new = jnp.sum(q8.astype(F32) * k_new, axis=-1, keepdims=True)
    nk = nsel * SLC_BLOCK
    lane = lax.broadcasted_iota(jnp.int32, (1, nk), 1)
    row = lax.broadcasted_iota(jnp.int32, (NSA_HEADS, KV_W), 0)
    o_s = jnp.zeros((NSA_HEADS, KV_W), F32)
    for h in range(HKV):
        kb = jnp.concatenate([k_refs[h * nsel + r][...] for r in range(nsel)], axis=0).astype(BF16)
        vb = jnp.concatenate([v_refs[h * nsel + r][...] for r in range(nsel)], axis=0).astype(BF16)
        jv = jnp.full((1, nk), -1, jnp.int32)
        n_new = jnp.int32(0)
        for r in range(nsel):
            j = ix_ref[b, h, r]
            jv = jnp.where(lane // SLC_BLOCK == r, j, jv)
            n_new = n_new + (j == nbp).astype(jnp.int32)
        tok = jv * SLC_BLOCK + lane % SLC_BLOCK
        mask = (jv >= 0) & (jv < nbp) & (tok <= past)
        has_new = n_new > 0
        s = jnp.where(mask, _dot_nt(q8, kb), NEG)
        sn = jnp.where(has_new, s_new, NEG)
        mx = jnp.maximum(jnp.max(s, axis=-1, keepdims=True), sn)
        e = jnp.where(mask, jnp.exp(s - mx), 0.0)
        e_new = jnp.where(has_new, jnp.exp(sn - mx), 0.0)
        den = jnp.maximum(jnp.sum(e, axis=-1, keepdims=True) + e_new, jnp.finfo(jnp.float32).tiny)
        o_h = (_dot(e.astype(BF16), vb) + e_new * v_new) / den
        o_s = jnp.where(row // GRP == h, o_h, o_s)
    g = jax.nn.sigmoid(g_ref[0])
    ocw = ocw_ref[0]
    o = g[:, 0:1] * ocw[:, :KV_W] + g[:, 1:2] * o_s + g[:, 2:3] * ocw[:, KV_W:]
    lane_o = lax.broadcasted_iota(jnp.int32, (NSA_HEADS, KV_W), 1)
    o_ref[0] = jnp.where((lane_o // HD) == (row // GRP), o, 0.0)


def _nsa_sample_b(l, cache_k, cache_v, page_table, idx, q3, ocw, nk3, nv3, g3, past):
    DB = q3.shape[0]
    page = cache_k.shape[2]
    bpp = page // SLC_BLOCK
    nbp = past // SLC_BLOCK
    nsel = idx.shape[2]

    def blk_spec(h, r):
        def imap(b, pt, ix):
            j = jnp.clip(ix[b, h, r], 0, nbp - 1)
            return (l, pt[b, j // bpp], j % bpp, 1)
        return pl.BlockSpec((None, None, SLC_BLOCK, KV_W), imap)

    per = lambda shape: pl.BlockSpec((1,) + shape, lambda b, pt, ix: (b,) + (0,) * len(shape))
    kv_specs = [blk_spec(h, r) for h in range(HKV) for r in range(nsel)]
    grid_spec = pltpu.PrefetchScalarGridSpec(
        num_scalar_prefetch=2, grid=(DB,),
        in_specs=kv_specs * 2 + [per((NSA_HEADS, LANES)), per((NSA_HEADS, 2 * KV_W)), per((1, 256)), per((1, 256)),
                                 per((NSA_HEADS, 3))],
        out_specs=per((NSA_HEADS, KV_W)))
    n = HKV * nsel
    return pl.pallas_call(
        functools.partial(_nsa_sample_b_kernel, past=past, nbp=nbp, nsel=nsel),
        grid_spec=grid_spec,
        out_shape=jax.ShapeDtypeStruct((DB, NSA_HEADS, KV_W), F32),
        compiler_params=_params("arbitrary"),
        name="nsa_sample_b",
    )(page_table, idx, *([cache_k] * n), *([cache_v] * n), q3, ocw, nk3, nv3, g3)


def _row_to_col(rowv):
    n = rowv.shape[1]
    eye = lax.broadcasted_iota(jnp.int32, (n, n), 0) == lax.broadcasted_iota(jnp.int32, (n, n), 1)
    return jnp.sum(jnp.where(eye, jnp.broadcast_to(rowv, (n, n)), 0.0), axis=1, keepdims=True)


def _compress_sample_t_kernel(pt_ref, *refs, npg):
    del pt_ref
    k_refs, v_refs = refs[:npg], refs[npg:2 * npg]
    wk_ref, wv_ref, sg_ref, ok_ref, ov_ref = refs[2 * npg:]
    for src, w_ref, o_ref in ((k_refs, wk_ref, ok_ref), (v_refs, wv_ref, ov_ref)):
        wa = w_ref[0:HD, :]
        wb = w_ref[HD:2 * HD, :]
        for h in range(HKV):
            prods = []
            for i in range(npg):
                x = src[i][h]
                prods.append(jnp.concatenate([x * wa, x * wb], axis=0).astype(BF16))
            o_ref[0, h] = _dot(jnp.concatenate(prods, axis=1), sg_ref[...])


def _compress_sample_t(l, cache_kt, cache_vt, page_table, cw):
    DB, n_pages = page_table.shape
    page = cache_kt.shape[-1]
    npg = min(PAGES_PER_STEP, n_pages)
    assert n_pages % npg == 0 and page % CMP_STRIDE == 0
    spp = page // CMP_STRIDE
    nseg = n_pages * spp
    assert (npg * spp) % LANES == 0 or npg * spp == nseg
    sg = np.zeros((npg * page, npg * spp), np.float32)
    sg[np.arange(npg * page), np.arange(npg * page) // CMP_STRIDE] = 1.0

    def page_spec(i):
        return pl.BlockSpec((None, None, None, HKV, HD, page),
                            lambda b, c, pt: (l, pt[b, c * npg + i], 0, 0, 0, 0))

    out = pl.BlockSpec((1, HKV, 2 * HD, npg * spp), lambda b, c, pt: (b, 0, 0, c))
    cst = lambda shape: pl.BlockSpec(shape, lambda b, c, pt: (0,) * len(shape))
    grid_spec = pltpu.PrefetchScalarGridSpec(
        num_scalar_prefetch=1, grid=(DB, n_pages // npg),
        in_specs=[page_spec(i) for i in range(npg)] * 2
                 + [cst((2 * HD, page)), cst((2 * HD, page)), cst(sg.shape)],
        out_specs=[out] * 2)
    return pl.pallas_call(
        functools.partial(_compress_sample_t_kernel, npg=npg),
        grid_spec=grid_spec,
        out_shape=[jax.ShapeDtypeStruct((DB, HKV, 2 * HD, nseg), F32)] * 2,
        compiler_params=_params("arbitrary", "arbitrary"),
        name="compress_sample",
    )(page_table, *([cache_kt] * npg), *([cache_vt] * npg), cw["wpk_t"], cw["wpv_t"], jnp.asarray(sg, BF16))


def _nsa_sample_a_t_kernel(q_ref, hk_ref, hv_ref, nk_ref, nv_ref, wkv_ref, wink_ref, winv_ref,
                           wpk_ref, wpv_ref, wk_ref, wv_ref, bk_ref, bv_ref, m_ref,
                           idx_ref, ocw_ref, wko_ref, wvo_ref, *, past, ns):
    ncs = hk_ref.shape[3]
    wb = wink_ref.shape[3]
    q8 = q_ref[0].astype(BF16)
    row = lax.broadcasted_iota(jnp.int32, (NSA_HEADS, 1), 0)
    lane_c = lax.broadcasted_iota(jnp.int32, (HD, ncs), 1)
    lane_w = lax.broadcasted_iota(jnp.int32, (HD, wb), 1)
    n_idx = lax.broadcasted_iota(jnp.int32, (1, ncs), 1)

    def per_head(fn):
        a, b = fn(0), fn(1)
        return jnp.where(row // GRP == 0, a, b)

    def comp_t(h_ref, new_row, wp_ref, w_ref, bias_ref, h):
        at = h_ref[0, h, 0:HD, :]
        bt = h_ref[0, h, HD:2 * HD, :]
        b_new = _row_to_col(new_row[:, HD * h:HD * (h + 1)]) * wp_ref[...]
        pooled = at + jnp.where(lane_c == ncs - 1, b_new, pltpu.roll(bt, ncs - 1, 1))
        return (_dot(w_ref[...], pooled.astype(BF16)) + bias_ref[...]).astype(BF16)

    kct = [comp_t(hk_ref, nk_ref[0], wpk_ref, wk_ref, bk_ref, h) for h in range(HKV)]
    vct = [comp_t(hv_ref, nv_ref[0], wpv_ref, wv_ref, bv_ref, h) for h in range(HKV)]
    s = per_head(lambda h: _dot(q8, kct[h]))
    p = _masked_softmax(s, (CMP_STRIDE * n_idx + CMP_LEN - 1) <= past)
    pb = p.astype(BF16)
    o_c = per_head(lambda h: _dot_nt(pb, vct[h]))

    rowi = lax.broadcasted_iota(jnp.int32, p.shape, 0)
    p0 = jnp.sum(p[0:GRP], axis=0, keepdims=True)
    p1 = jnp.sum(p[GRP:2 * GRP], axis=0, keepdims=True)
    imp = _dot_split3(jnp.where(rowi == 0, p0, jnp.where(rowi == 1, p1, 0.0)), m_ref[...])
    lane = lax.broadcasted_iota(jnp.int32, imp.shape, 1)
    cur = past // SLC_BLOCK
    valid = lane * SLC_BLOCK <= past
    forced = (lane == 0) | (lane == cur) | (lane == cur - 1)
    score = jnp.where(valid, jnp.where(forced, FORCE, imp), -1.0)
    score = jnp.where(lane < ns, score, -jnp.inf)
    lane_o = lax.broadcasted_iota(jnp.int32, (SUBLANES, LANES), 1)
    idx = jnp.full((SUBLANES, LANES), -1, jnp.int32)
    for r in range(min(N_SEL, ns)):
        mx = jnp.max(score, axis=-1, keepdims=True)
        first = jnp.min(jnp.where(score == mx, lane, 1 << 20), axis=-1, keepdims=True)
        idx = jnp.where(lane_o == r, jnp.where(mx >= 0.0, first, -1), idx)
        score = jnp.where(lane == first, -jnp.inf, score)
    idx_ref[0] = idx

    sel_head = lambda t: jnp.where(row // GRP == 0, t[:, :HD], t[:, HD:])
    k_new = wkv_ref[0][:, :KV_W]
    v_new = wkv_ref[0][:, KV_W:]
    s_w = per_head(lambda h: _dot(q8, wink_ref[0, h].astype(BF16)))
    dlt = wb - lax.broadcasted_iota(jnp.int32, (1, wb), 1)
    mask = (dlt < WINDOW) & (past - dlt >= 0)
    s_new = jnp.sum(q8.astype(F32) * sel_head(k_new).astype(BF16).astype(F32), axis=-1, keepdims=True)
    s_w = jnp.where(mask, s_w, NEG)
    mx = jnp.maximum(jnp.max(s_w, axis=-1, keepdims=True), s_new)
    e = jnp.where(mask, jnp.exp(s_w - mx), 0.0)
    e_new = jnp.exp(s_new - mx)
    den = jnp.sum(e, axis=-1, keepdims=True) + e_new
    eb = e.astype(BF16)
    o_w = per_head(lambda h: _dot_nt(eb, winv_ref[0, h].astype(BF16)))
    o_w = (o_w + e_new * sel_head(v_new).astype(BF16).astype(F32)) / den
    ocw_ref[0] = jnp.concatenate([o_c, o_w], axis=-1)
    for h in range(HKV):
        for src, new, dst in ((wink_ref, k_new, wko_ref), (winv_ref, v_new, wvo_ref)):
            col = _row_to_col(new[:, HD * h:HD * (h + 1)])
            dst[0, h] = jnp.where(lane_w == wb - 1, col, pltpu.roll(src[0, h], wb - 1, 1))


NSA_A_SEQS = 4
N_SEQ_IN = 8
N_SEQ_OUT = 4


def _nsa_sample_a_t_multi(*refs, nsq, past, ns):
    for sq in range(nsq):
        one = lambda r: r.at[pl.ds(sq, 1)]
        _nsa_sample_a_t_kernel(*[one(r) for r in refs[:N_SEQ_IN]], *refs[N_SEQ_IN:-N_SEQ_OUT],
                               *[one(r) for r in refs[-N_SEQ_OUT:]], past=past, ns=ns)


def _nsa_sample_a_t(l, q3, hk, hv, nk3, nv3, wkv3, win_kt, win_vt, cw, past):
    DB = q3.shape[0]
    ncs = hk.shape[3]
    wb = win_kt.shape[-1]
    ns = -(-(past + 1) // SLC_BLOCK)
    nsl = -(-ns // LANES) * LANES
    m = _importance_matrix(ncs, ncs, ns, 0, nsl)
    nsq = NSA_A_SEQS if DB % NSA_A_SEQS == 0 else 1
    per = lambda shape: pl.BlockSpec((nsq,) + shape, lambda b: (b,) + (0,) * len(shape))
    win = pl.BlockSpec((None, nsq, HKV, HD, wb), lambda b: (l, b, 0, 0, 0))
    shapes = [(DB, SUBLANES, LANES), (DB, NSA_HEADS, 2 * HD), (DB, HKV, HD, wb), (DB, HKV, HD, wb)]
    dts = [jnp.int32, F32, F32, F32]
    return pl.pallas_call(
        functools.partial(_nsa_sample_a_t_multi, nsq=nsq, past=past, ns=ns),
        grid=(DB // nsq,),
        in_specs=[per((NSA_HEADS, HD))] + [per((HKV, 2 * HD, ncs))] * 2 + [per((1, 256))] * 3 + [win, win]
                 + [_const_spec((HD, 1))] * 2 + [_const_spec((HD, HD))] * 2
                 + [_const_spec((HD, 1))] * 2 + [_const_spec(m.shape)],
        out_specs=[per(s[1:]) for s in shapes],
        out_shape=[jax.ShapeDtypeStruct(s, d) for s, d in zip(shapes, dts)],
        compiler_params=_params("arbitrary"),
        name="nsa_sample_a",
    )(q3, hk, hv, nk3, nv3, wkv3, win_kt, win_vt, cw["wpk_b0"], cw["wpv_b0"], cw["wk_t"], cw["wv_t"],
      cw["bk_col"], cw["bv_col"], jnp.asarray(m, BF16))


def _nsa_sample_b_t_kernel(pt_ref, ix_ref, *refs, past, nbp, nsel, page):
    del pt_ref
    k_refs, v_refs = refs[:HKV * nsel], refs[HKV * nsel:2 * HKV * nsel]
    q_ref, ocw_ref, nk_ref, nv_ref, g_ref, o_ref = refs[2 * HKV * nsel:]
    b = pl.program_id(0)
    bpp = page // SLC_BLOCK
    q8 = q_ref[0].astype(BF16)
    row = lax.broadcasted_iota(jnp.int32, (NSA_HEADS, 1), 0)
    sel_head = lambda t: jnp.where(row // GRP == 0, t[:, :HD], t[:, HD:])
    k_new = sel_head(nk_ref[0][:, KV_W:]).astype(BF16).astype(F32)
    v_new = sel_head(nv_ref[0][:, KV_W:]).astype(BF16).astype(F32)
    s_new = jnp.sum(q8.astype(F32) * k_new, axis=-1, keepdims=True)
    nk = nsel * page
    lane = lax.broadcasted_iota(jnp.int32, (1, nk), 1)
    o_s = jnp.zeros((NSA_HEADS, HD), F32)
    for h in range(HKV):
        kt = jnp.concatenate([k_refs[h * nsel + r][...] for r in range(nsel)], axis=1).astype(BF16)
        vt = jnp.concatenate([v_refs[h * nsel + r][...] for r in range(nsel)], axis=1).astype(BF16)
        jv = jnp.full((1, nk), -1, jnp.int32)
        n_new = jnp.int32(0)
        for r in range(nsel):
            j = ix_ref[b, h, r]
            jv = jnp.where(lane // page == r, j, jv)
            n_new = n_new + (j == nbp).astype(jnp.int32)
        t_in = lane % page
        tok = (jv // bpp) * page + t_in
        mask = (jv >= 0) & (jv < nbp) & (t_in // SLC_BLOCK == jv % bpp) & (tok <= past)
        has_new = n_new > 0
        s = jnp.where(mask, _dot(q8, kt), NEG)
        sn = jnp.where(has_new, s_new, NEG)
        mx = jnp.maximum(jnp.max(s, axis=-1, keepdims=True), sn)
        e = jnp.where(mask, jnp.exp(s - mx), 0.0)
        e_new = jnp.where(has_new, jnp.exp(sn - mx), 0.0)
        den = jnp.maximum(jnp.sum(e, axis=-1, keepdims=True) + e_new, jnp.finfo(jnp.float32).tiny)
        o_h = (_dot_nt(e.astype(BF16), vt) + e_new * v_new) / den
        o_s = jnp.where(row // GRP == h, o_h, o_s)
    g = jax.nn.sigmoid(g_ref[0])
    ocw = ocw_ref[0]
    o_ref[0] = g[:, 0:1] * ocw[:, :HD] + g[:, 1:2] * o_s + g[:, 2:3] * ocw[:, HD:]


def _nsa_sample_b_t(l, cache_kt, cache_vt, page_table, idx, q3, ocw, nk3, nv3, g3, past):
    DB = q3.shape[0]
    page = cache_kt.shape[-1]
    bpp = page // SLC_BLOCK
    nbp = past // SLC_BLOCK
    nsel = idx.shape[2]

    def tile_spec(h, r):
        def imap(b, pt, ix):
            j = jnp.clip(ix[b, h, r], 0, nbp - 1)
            return (l, pt[b, j // bpp], 1, h, 0, 0)
        return pl.BlockSpec((None, None, None, None, HD, page), imap)

    per = lambda shape: pl.BlockSpec((1,) + shape, lambda b, pt, ix: (b,) + (0,) * len(shape))
    kv_specs = [tile_spec(h, r) for h in range(HKV) for r in range(nsel)]
    grid_spec = pltpu.PrefetchScalarGridSpec(
        num_scalar_prefetch=2, grid=(DB,),
        in_specs=kv_specs * 2 + [per((NSA_HEADS, HD)), per((NSA_HEADS, 2 * HD)), per((1, 256)), per((1, 256)),
                                 per((NSA_HEADS, 3))],
        out_specs=per((NSA_HEADS, HD)))
    n = HKV * nsel
    return pl.pallas_call(
        functools.partial(_nsa_sample_b_t_kernel, past=past, nbp=nbp, nsel=nsel, page=page),
        grid_spec=grid_spec,
        out_shape=jax.ShapeDtypeStruct((DB, NSA_HEADS, HD), F32),
        compiler_params=_params("arbitrary"),
        name="nsa_sample_b",
    )(page_table, idx, *([cache_kt] * n), *([cache_vt] * n), q3, ocw, nk3, nv3, g3)


SEQ_PER_STEP = 8


def _mlstm_sample_kernel(mqkv_ref, sm_ref, mo_ref, gb_ref, c_ref, n_ref, m_ref,
                         om_ref, c_out, n_out, m_out, *, nb):
    eye = (lax.broadcasted_iota(jnp.int32, (HD, HD), 0) == lax.broadcasted_iota(jnp.int32, (HD, HD), 1))
    lane = lax.broadcasted_iota(jnp.int32, (1, LANES), 1)

    def col(rowv):
        return jnp.sum(jnp.where(eye, jnp.broadcast_to(rowv, (HD, HD)), 0.0), axis=1, keepdims=True)

    for b in range(nb):
        g = sm_ref[b] + gb_ref[...]
        lf = _log_sigmoid(g)
        heads = []
        m_tile = jnp.zeros((1, LANES), F32)
        for hd in range(MH):
            ig = g[:, SM_IG + hd:SM_IG + hd + 1]
            a = lf[:, SM_FG + hd:SM_FG + hd + 1] + m_ref[b][:, hd:hd + 1]
            mt = jnp.maximum(a, ig)
            dm = jnp.exp(ig - mt)
            inter = jnp.exp(a - mt)
            q = mqkv_ref[b][:, HD * hd:HD * (hd + 1)]
            k = mqkv_ref[b][:, MLSTM_W + HD * hd:MLSTM_W + HD * (hd + 1)] * Q_SCALE
            v = mqkv_ref[b][:, 2 * MLSTM_W + HD * hd:2 * MLSTM_W + HD * (hd + 1)]
            c_prev = c_ref[b, hd]
            n_prev = n_ref[b][hd:hd + 1, :]
            qk = jnp.sum(q * k, axis=-1, keepdims=True) * dm
            q_c = jnp.sum(col(q) * c_prev, axis=0, keepdims=True)
            num = qk * v + inter * q_c
            den = qk + inter * jnp.sum(q * n_prev, axis=-1, keepdims=True)
            hout = num / jnp.maximum(jnp.abs(den), jnp.exp(-mt))
            heads.append(jax.nn.sigmoid(mo_ref[b][:, HD * hd:HD * (hd + 1)]) * hout)
            w = jnp.exp(ig - mt)
            decay = jnp.exp(a - mt)
            c_out[b, hd] = decay * c_prev + (w * col(k)) * v
            n_out[b, hd:hd + 1, :] = decay * n_prev + w * k
            m_tile = jnp.where(lane == hd, mt, m_tile)
        om_ref[b] = jnp.concatenate(heads, axis=-1)
        m_out[b] = m_tile


def _mlstm_sample(l, mqkv3, sm3, mo3, gate_b_tile, state_c, state_n, state_m4):
    DB = mqkv3.shape[0]
    nb = min(SEQ_PER_STEP, DB)
    assert DB % nb == 0
    per = lambda w: pl.BlockSpec((nb, 1, w), lambda i: (i, 0, 0))
    return pl.pallas_call(
        functools.partial(_mlstm_sample_kernel, nb=nb),
        grid=(DB // nb,),
        in_specs=[per(3 * MLSTM_W), per(LANES), per(MLSTM_W), _const_spec((1, LANES)),
                  pl.BlockSpec((None, nb, MH, HD, HD), lambda i: (l, i, 0, 0, 0)),
                  pl.BlockSpec((None, nb, MH, HD), lambda i: (l, i, 0, 0)),
                  pl.BlockSpec((None, nb, 1, MH), lambda i: (l, i, 0, 0))],
        out_specs=[per(MLSTM_W), pl.BlockSpec((nb, MH, HD, HD), lambda i: (i, 0, 0, 0)),
                   pl.BlockSpec((nb, MH, HD), lambda i: (i, 0, 0)), per(LANES)],
        out_shape=[jax.ShapeDtypeStruct((DB, 1, MLSTM_W), F32), jax.ShapeDtypeStruct((DB, MH, HD, HD), F32),
                   jax.ShapeDtypeStruct((DB, MH, HD), F32), jax.ShapeDtypeStruct((DB, 1, LANES), F32)],
        compiler_params=_params("arbitrary"),
        name="mlstm_sample",
    )(mqkv3, sm3, mo3, gate_b_tile, state_c, state_n, state_m4)


def _outffn_sample_kernel(x_ref, on_ref, om_ref, op_ref, gt1_ref, sh2_ref, sc2_ref, gt2_ref, g2_ref, fg_ref,
                          wout_ref, wup_ref, cw_ref, cb_ref, wdn_ref, prev_ref, y_ref, up_ref, *, d_ff, final):
    mix = (_dot(on_ref[...].astype(BF16), wout_ref[0:NSA_W, :])
           + _dot(om_ref[...].astype(BF16), wout_ref[NSA_W:NSA_W + MLSTM_W, :])
           + _dot(op_ref[...].astype(BF16), wout_ref[NSA_W + MLSTM_W:, :]))
    x1 = x_ref[...] + gt1_ref[...] * mix
    h2 = _norm_mod(x1, g2_ref[...], sc2_ref[...], sh2_ref[...]).astype(BF16)
    up = _dot(h2, wup_ref[...])
    up_ref[...] = up
    y = cb_ref[...] + cw_ref[FFN_CONV - 1:FFN_CONV, :] * up
    for j in range(FFN_CONV - 1):
        y = y + cw_ref[j:j + 1, :] * prev_ref[:, j, :]
    a, b = y[:, :d_ff], y[:, d_ff:]
    f = _dot((a * jax.nn.sigmoid(a) * b).astype(BF16), wdn_ref[...])
    x2 = x1 + gt2_ref[...] * f
    y_ref[...] = _rmsnorm(x2, fg_ref[...]) if final else x2


def _outffn_sample(l, x, on, om, op, gt1, sh2, sc2, gt2, g2, fg, w_out, w_up, conv_w, conv_b, w_dn,
                   state_ffn, final):
    DB, D = x.shape
    d_ff = w_dn.shape[0]
    full = lambda shape: pl.BlockSpec(shape, lambda i: (0,) * len(shape))
    args = (x, on, om, op, gt1, sh2, sc2, gt2, g2, fg, w_out, w_up, conv_w, conv_b, w_dn)
    return pl.pallas_call(
        functools.partial(_outffn_sample_kernel, d_ff=d_ff, final=final),
        grid=(1,),
        in_specs=[full(a.shape) for a in args]
                 + [pl.BlockSpec((None, DB, FFN_CONV - 1, 2 * d_ff), lambda i: (l, 0, 0, 0))],
        out_specs=[full((DB, D)), full((DB, 2 * d_ff))],
        out_shape=[jax.ShapeDtypeStruct((DB, D), F32), jax.ShapeDtypeStruct((DB, 2 * d_ff), F32)],
        compiler_params=_params("arbitrary"),
        name="outffn_sample",
    )(*args, state_ffn)


def _sample_layer(l, x, mod, lw, caches, final):
    DB, D = x.shape
    ckt, cvt, page_table, win_kt, win_vt, st_c, st_n, st_m, st_pool, st_ffn = caches
    past = page_table.shape[1] * ckt.shape[-1]
    sh1, sc1, gt1, sh2, sc2, gt2 = [mod[:, D * i:D * (i + 1)] for i in range(6)]
    q, nk, nv, wkv, mqkv, mo, op, sm, pu = _inproj_sample(
        l, x, lw["g1"], sh1, sc1, lw["w_b"], lw["w_q"], lw["pool_bd"], lw["pool_scale"], st_pool, past)
    r3 = lambda t: t[:, None, :]
    hk, hv = _compress_sample_t(l, ckt, cvt, page_table, lw["cw"])
    q3 = q.reshape(DB, NSA_HEADS, HD)
    idx, ocw, wk_new, wv_new = _nsa_sample_a_t(l, q3, hk, hv, r3(nk), r3(nv), r3(wkv), win_kt, win_vt,
                                               lw["cw"], past)
    g3 = sm[:, :3 * NSA_HEADS].reshape(DB, NSA_HEADS, 3)
    on = _nsa_sample_b_t(l, ckt, cvt, page_table, idx[:, :HKV, :N_SEL], q3, ocw, r3(nk), r3(nv), g3, past)
    om, c_new, n_new, m_new = _mlstm_sample(l, r3(mqkv), r3(sm), r3(mo), lw["gate_b"], st_c, st_n,
                                            st_m.reshape(st_m.shape[0], DB, 1, MH))
    y, up = _outffn_sample(l, x, on.reshape(DB, NSA_W), om[:, 0], op, gt1, sh2, sc2, gt2,
                           lw["g2"], lw["fg"], lw["w_out"], lw["w_up"], lw["conv_w"],
                           lw["conv_b"], lw["w_dn"], st_ffn, final)
    states = (nk.reshape(DB, 1, 2, HKV, HD), nv.reshape(DB, 1, 2, HKV, HD),
              jnp.transpose(wk_new, (0, 3, 1, 2)), jnp.transpose(wv_new, (0, 3, 1, 2)),
              c_new, n_new, m_new[:, 0, :MH],
              jnp.concatenate([st_pool[l][:, 1:], pu[:, None, :]], axis=1),
              jnp.concatenate([st_ffn[l][:, 1:], up[:, None, :]], axis=1))
    return y, states


def kernel(x_prompt, x_sample, cache_k, cache_v, cache_win_k, cache_win_v, state_mlstm_C, state_mlstm_n,
           state_mlstm_m, state_pool, state_ffn_conv, page_table, c_prompt, c_sample, norm1_g, norm2_g, ada_w,
           ada_b, w_in, nsa_cmp_pos, nsa_cmp_w, nsa_cmp_b, mlstm_gate_b, pool_w, pool_scale, w_out, ffn_w_up,
           ffn_conv_w, ffn_conv_b, ffn_w_down, final_g):
    B = x_prompt.shape[0]
    DB, DS, D = x_sample.shape
    assert DS == 1
    depth = w_in.shape[0]
    w = dict(norm1_g=norm1_g, norm2_g=norm2_g, w_in=w_in, nsa_cmp_pos=nsa_cmp_pos, nsa_cmp_w=nsa_cmp_w,
             nsa_cmp_b=nsa_cmp_b, mlstm_gate_b=mlstm_gate_b, pool_w=pool_w, pool_scale=pool_scale, w_out=w_out,
             ffn_w_up=ffn_w_up, ffn_conv_w=ffn_conv_w, ffn_conv_b=ffn_conv_b, ffn_w_down=ffn_w_down,
             final_g=final_g)
    mod = _ada_mod(jnp.concatenate([c_prompt, c_sample], axis=0), ada_w, ada_b)
    page = cache_k.shape[2]
    ckt = jnp.transpose(cache_k, (0, 1, 3, 4, 5, 2))
    cvt = jnp.transpose(cache_v, (0, 1, 3, 4, 5, 2))
    caches = (ckt, cvt, page_table, jnp.transpose(cache_win_k, (0, 1, 3, 4, 2)),
              jnp.transpose(cache_win_v, (0, 1, 3, 4, 2)),
              state_mlstm_C, state_mlstm_n, state_mlstm_m, state_pool, state_ffn_conv)
    xp, xs = x_prompt, x_sample[:, 0, :]
    acc_p = [[] for _ in range(9)]
    acc_s = [[] for _ in range(9)]
    for l in range(depth):
        lw = _prep_layer(l, w, page)
        final = l == depth - 1
        xp, st_p = _prompt_layer(xp, mod[l, :B][:, None, :], lw, final)
        xs, st_s = _sample_layer(l, xs, mod[l, B:], lw, caches, final)
        for a, v in zip(acc_p, st_p):
            a.append(v)
        for a, v in zip(acc_s, st_s):
            a.append(v)
    sp = [jnp.stack(a, axis=0) for a in acc_p]
    ss = [jnp.stack(a, axis=0) for a in acc_s]
    return (xp, xs[:, None, :], *sp, *ss)
```

```python
import functools

import numpy as np
import jax
import jax.numpy as jnp
from jax import lax
from jax.experimental import pallas as pl
from jax.experimental.pallas import tpu as pltpu

F32 = jnp.float32
BF16 = jnp.bfloat16

HD = 64
NSA_HEADS = 8
HKV = 2
GRP = NSA_HEADS // HKV
MH = 4
NSA_W = NSA_HEADS * HD
KV_W = HKV * HD
MLSTM_W = MH * HD
POOL_W = 4 * HD
CMP_STRIDE = 16
CMP_LEN = 32
SLC_BLOCK = 64
SLC_RATIO = SLC_BLOCK // CMP_STRIDE
N_SEL = 16
WINDOW = 512
POOL_WINDOWS = (2, 4, 8, 16)
POOL_STATE = 15
FFN_CONV = 3
RMS_EPS = 1e-6
NEG = -1e30
FORCE = 1e6
Q_SCALE = HD ** -0.5

LANES = 128
SUBLANES = 8
VMEM_LIMIT = 56 * 1024 * 1024

C_NK = 0
C_NV = C_NK + 256
C_KS = C_NV + 256
C_VS = C_KS + 256
C_VW = C_VS + 256
C_WKV = C_VW + 256
C_MQKV = C_WKV + 256
C_MO = C_MQKV + 3 * MLSTM_W
C_PU = C_MO + MLSTM_W
C_SM = C_PU + POOL_W
C_END = C_SM + LANES
SM_IG = 3 * NSA_HEADS
SM_FG = SM_IG + MH


def _dot(a, b):
    return jnp.dot(a, b, preferred_element_type=F32)


def _dot_nt(a, b):
    return lax.dot_general(a, b, (((1,), (1,)), ((), ())), preferred_element_type=F32)


def _dot_tn(a, b):
    return lax.dot_general(a, b, (((0,), (0,)), ((), ())), preferred_element_type=F32)


def _dot_split3(x, m):
    x1 = x.astype(BF16)
    r1 = x - x1.astype(F32)
    x2 = r1.astype(BF16)
    x3 = (r1 - x2.astype(F32)).astype(BF16)
    return _dot(x1, m) + _dot(x2, m) + _dot(x3, m)


def _masked_softmax(s, mask):
    s = jnp.where(mask, s, NEG)
    e = jnp.where(mask, jnp.exp(s - jnp.max(s, axis=-1, keepdims=True)), 0.0)
    return e / jnp.maximum(jnp.sum(e, axis=-1, keepdims=True), jnp.finfo(jnp.float32).tiny)


def _norm_mod(x, g, sc, sh):
    ms = jnp.mean(x * x, axis=-1, keepdims=True)
    return (x * lax.rsqrt(ms + RMS_EPS) * g) * (1.0 + sc) + sh


def _rmsnorm(x, g):
    ms = jnp.mean(x * x, axis=-1, keepdims=True)
    return x * lax.rsqrt(ms + RMS_EPS) * g


def _log_sigmoid(x):
    return jnp.minimum(x, 0.0) - jnp.log1p(jnp.exp(-jnp.abs(x)))


def _cumsum_rows(x):
    n = x.shape[0]
    row = lax.broadcasted_iota(jnp.int32, x.shape, 0)
    sh = 1
    while sh < n:
        x = x + jnp.where(row >= sh, pltpu.roll(x, sh, 0), 0.0)
        sh *= 2
    return x


def _const_spec(shape):
    nd = len(shape)
    return pl.BlockSpec(shape, lambda *_: (0,) * nd, pipeline_mode=pl.Buffered(1))


def _params(*sem):
    return pltpu.CompilerParams(dimension_semantics=sem, vmem_limit_bytes=VMEM_LIMIT)


def _ada_kernel(c_ref, w_ref, b_ref, o_ref):
    c = c_ref[...]
    s = c * jax.nn.sigmoid(c)
    o_ref[0] = _dot(s.astype(BF16), w_ref[0].astype(BF16)) + b_ref[0]


def _ada_mod(c_all, ada_w, ada_b):
    depth, d, n = ada_w.shape
    rows = c_all.shape[0]
    tn = 1536
    return pl.pallas_call(
        _ada_kernel,
        grid=(depth, n // tn),
        in_specs=[pl.BlockSpec((rows, d), lambda l, j: (0, 0)),
                  pl.BlockSpec((1, d, tn), lambda l, j: (l, 0, j)),
                  pl.BlockSpec((1, 1, tn), lambda l, j: (l, 0, j))],
        out_specs=pl.BlockSpec((1, rows, tn), lambda l, j: (l, 0, j)),
        out_shape=jax.ShapeDtypeStruct((depth, rows, n), F32),
        compiler_params=_params("arbitrary", "arbitrary"),
        name="ada_mod",
    )(c_all, ada_w, ada_b.reshape(depth, 1, n))


def _prep_w_in(w):
    d = w.shape[0]
    o = 0
    o += NSA_W
    nkv = w[:, o:o + 6 * KV_W]; o += 6 * KV_W
    ng = w[:, o:o + 3 * NSA_HEADS]; o += 3 * NSA_HEADS
    mqkv = w[:, o:o + 3 * MLSTM_W]; o += 3 * MLSTM_W
    mif = w[:, o:o + 2 * MH]; o += 2 * MH
    mo = w[:, o:o + MLSTM_W]; o += MLSTM_W
    pu = w[:, o:o + POOL_W]
    k_cmp, v_cmp, k_slc, v_slc, k_win, v_win = [nkv[:, KV_W * i:KV_W * (i + 1)] for i in range(6)]
    z = jnp.zeros((d, HD), w.dtype)
    h0 = lambda t: t[:, :HD]
    h1 = lambda t: t[:, HD:]
    cols = [
        k_cmp, k_slc, v_cmp, v_slc,
        h0(k_slc), z, z, h1(k_slc),
        h0(v_slc), z, h1(v_slc), z,
        h0(v_win), z, h1(v_win), z,
        k_win, v_win, mqkv, mo, pu,
        ng, mif, jnp.zeros((d, LANES - 3 * NSA_HEADS - 2 * MH), w.dtype)]
    out = jnp.concatenate(cols, axis=1).astype(BF16)
    assert out.shape[1] == C_END
    return out


def _pool_mix(pu, sums, cnt, pw_ref, pscale_ref):
    lane = lax.broadcasted_iota(jnp.int32, pu.shape, 1)
    grp = lane // HD
    mean = jnp.where(grp == 0, sums[2] / cnt[2],
                     jnp.where(grp == 1, sums[4] / cnt[4],
                               jnp.where(grp == 2, sums[8] / cnt[8], sums[16] / cnt[16])))
    d = mean - pu
    return _dot(d.astype(BF16), pw_ref[...]) * pscale_ref[...]


def _split2(x):
    hi = x.astype(BF16)
    return hi, (x - hi.astype(F32)).astype(BF16)


def _dot_split(a, b):
    return _dot(a[0], b[0]) + _dot(a[1], b[0]) + _dot(a[0], b[1])


def _inproj_prompt_kernel(x_ref, g_ref, sh_ref, sc_ref, w_ref, wqh_ref, wql_ref, pw_ref, pscale_ref,
                          q_ref, nk_ref, nv_ref, ks_ref, vs_ref, kw_ref, vw_ref, wkv_ref,
                          mqkv_ref, mo_ref, op_ref, sm_ref, ps_ref, zs_ref, *, tm):
    t = pl.program_id(1)
    hf = _norm_mod(x_ref[0], g_ref[...], sc_ref[0], sh_ref[0])
    h, h_lo = _split2(hf)

    def seg(a, b):
        return _dot(h, w_ref[:, a:b])

    q_ref[0] = _dot_split((h, h_lo), (wqh_ref[...], wql_ref[...])) * Q_SCALE
    nk_ref[0] = seg(C_NK, C_NV)
    nv_ref[0] = seg(C_NV, C_KS)
    row = t * tm + lax.broadcasted_iota(jnp.int32, (tm, 2 * LANES), 0)
    lane = lax.broadcasted_iota(jnp.int32, (tm, 2 * LANES), 1)
    is_aux = (lane >= HD) & (lane < 3 * HD)
    aux = jnp.where(lane < LANES, lane - HD, lane - LANES)
    onehot = jnp.where(is_aux & (aux == row // SLC_BLOCK), 1.0, 0.0)
    ks_ref[0] = (seg(C_KS, C_VS) + onehot).astype(BF16)
    ones_col = jnp.where(lane % LANES == HD, 1.0, 0.0)
    vs_ref[0] = (seg(C_VS, C_VW) + ones_col).astype(BF16)
    vw_ref[0] = (seg(C_VW, C_WKV) + ones_col).astype(BF16)
    wkv = seg(C_WKV, C_MQKV)
    wkv_ref[0] = wkv
    kw_ref[0] = wkv[:, :KV_W].astype(BF16)
    mqkv_ref[0] = seg(C_MQKV, C_MO)
    mo_ref[0] = seg(C_MO, C_PU)
    sm_ref[0] = seg(C_SM, C_END)

    pu = seg(C_PU, C_SM)
    halo = 2 * SUBLANES

    @pl.when(t == 0)
    def _():
        zs_ref[0:halo, :] = jnp.zeros((halo, POOL_W), F32)

    @pl.when(t > 0)
    def _():
        zs_ref[0:halo, :] = zs_ref[tm:tm + halo, :]

    zs_ref[halo:halo + tm, :] = pu
    acc = pu
    sums = {}
    for i in range(1, POOL_STATE + 1):
        acc = acc + zs_ref[pl.ds(halo - i, tm), :]
        if i + 1 in POOL_WINDOWS:
            sums[i + 1] = acc
    pos1 = (t * tm + lax.broadcasted_iota(jnp.int32, (tm, 1), 0) + 1).astype(F32)
    cnt = {w: jnp.minimum(float(w), pos1) for w in POOL_WINDOWS}
    op_ref[0] = _pool_mix(pu, sums, cnt, pw_ref, pscale_ref).astype(BF16)
    ps_ref[0] = zs_ref[tm:tm + halo, :]


def _inproj_prompt(x, g1, sh1, sc1, w_b, w_q2, pool_bd, pool_scale, tm=512):
    B, T, D = x.shape
    nT = T // tm
    assert T % tm == 0 and tm >= 2 * SUBLANES and WINDOW % tm == 0
    nwin = WINDOW // tm
    row = lambda w: pl.BlockSpec((1, tm, w), lambda b, t: (b, t, 0))
    mod = pl.BlockSpec((1, 1, D), lambda b, t: (b, 0, 0))
    outs = [
        (row(NSA_W), (B, T, NSA_W), F32),
        (row(256), (B, T, 256), F32),
        (row(256), (B, T, 256), F32),
        (row(256), (B, T, 256), BF16),
        (row(256), (B, T, 256), BF16),
        (row(KV_W), (B, T, KV_W), BF16),
        (row(256), (B, T, 256), BF16),
        (pl.BlockSpec((1, tm, 256), lambda b, t: (b, jnp.maximum(t - (nT - nwin), 0), 0)),
         (B, WINDOW, 256), F32),
        (row(3 * MLSTM_W), (B, T, 3 * MLSTM_W), F32),
        (row(MLSTM_W), (B, T, MLSTM_W), F32),
        (row(POOL_W), (B, T, POOL_W), BF16),
        (row(LANES), (B, T, LANES), F32),
        (pl.BlockSpec((1, 2 * SUBLANES, POOL_W), lambda b, t: (b, 0, 0)), (B, 2 * SUBLANES, POOL_W), F32),
    ]
    return pl.pallas_call(
        functools.partial(_inproj_prompt_kernel, tm=tm),
        grid=(B, nT),
        in_specs=[pl.BlockSpec((1, tm, D), lambda b, t: (b, t, 0)),
                  _const_spec((1, D)), mod, mod,
                  _const_spec(w_b.shape), _const_spec(w_q2[0].shape), _const_spec(w_q2[1].shape),
                  _const_spec(pool_bd.shape), _const_spec((1, POOL_W))],
        out_specs=[o[0] for o in outs],
        out_shape=[jax.ShapeDtypeStruct(o[1], o[2]) for o in outs],
        scratch_shapes=[pltpu.VMEM((2 * SUBLANES + tm, POOL_W), F32)],
        compiler_params=_params("arbitrary", "arbitrary"),
        name="inproj_prompt",
    )(x, g1, sh1, sc1, w_b, w_q2[0], w_q2[1], pool_bd, pool_scale)


def _compress_prompt_kernel(k_ref, v_ref, wpk_ref, wpv_ref, wk_ref, wkl_ref, wv_ref, bk_ref, bv_ref,
                            kc_ref, vc_ref, *, nseg):
    def pooled(src_ref, wp_ref):
        a = jnp.zeros((nseg, KV_W), F32)
        b = jnp.zeros((nseg, KV_W), F32)
        for j in range(CMP_STRIDE):
            xj = src_ref[0, pl.ds(j, nseg, stride=CMP_STRIDE), :]
            a = a + xj * wp_ref[j:j + 1, :]
            b = b + xj * wp_ref[CMP_STRIDE + j:CMP_STRIDE + j + 1, :]
        return a + pltpu.roll(b, nseg - 1, 0)

    kc_ref[0] = _dot_split(_split2(pooled(k_ref, wpk_ref)), (wk_ref[...], wkl_ref[...])) + bk_ref[...]
    vc_ref[0] = (_dot(pooled(v_ref, wpv_ref).astype(BF16), wv_ref[...]) + bv_ref[...]).astype(BF16)


def _compress_prompt(nk, nv, cw):
    B, T, _ = nk.shape
    nseg = T // CMP_STRIDE
    src = pl.BlockSpec((1, T, KV_W), lambda b: (b, 0, 0))
    return pl.pallas_call(
        functools.partial(_compress_prompt_kernel, nseg=nseg),
        grid=(B,),
        in_specs=[src, src, _const_spec((CMP_LEN, KV_W)), _const_spec((CMP_LEN, KV_W)),
                  _const_spec((KV_W, KV_W)), _const_spec((KV_W, KV_W)), _const_spec((KV_W, 2 * KV_W)),
                  _const_spec((1, KV_W)), _const_spec((1, 2 * KV_W))],
        out_specs=[pl.BlockSpec((1, nseg, KV_W), lambda b: (b, 0, 0)),
                   pl.BlockSpec((1, nseg, 2 * KV_W), lambda b: (b, 0, 0))],
        out_shape=[jax.ShapeDtypeStruct((B, nseg, KV_W), F32),
                   jax.ShapeDtypeStruct((B, nseg, 2 * KV_W), BF16)],
        compiler_params=_params("arbitrary"),
        name="compress_prompt",
    )(nk, nv, cw["wpk"], cw["wpv"], cw["wk"], cw["wk_lo"], cw["wv_dup"], cw["bk"], cw["bv_dup"])


def _prep_compress(cmp_pos, cmp_w, cmp_b, page):
    tile2 = lambda t: jnp.concatenate([t, t], axis=-1)
    z = jnp.zeros((HD, HD), F32)
    wk, wv = cmp_w[0], cmp_w[1]
    wk_bd = jnp.block([[wk, z], [z, wk]])
    wv_dup = jnp.block([[wv, z, z, z], [z, z, wv, z]])

    def pos_t(p):
        reps = page // CMP_STRIDE
        return jnp.concatenate([jnp.tile(p[:CMP_STRIDE].T, (1, reps)), jnp.tile(p[CMP_STRIDE:].T, (1, reps))], axis=0)

    return dict(
        wpk=tile2(cmp_pos[0]), wpv=tile2(cmp_pos[1]),
        wk=wk_bd.astype(BF16), wk_lo=(wk_bd - wk_bd.astype(BF16).astype(F32)).astype(BF16),
        wv_dup=wv_dup.astype(BF16),
        bk=tile2(cmp_b[0])[None, :],
        bv_dup=jnp.concatenate([cmp_b[1], jnp.zeros((HD,), F32)] * 2)[None, :],
        wpk_t=pos_t(cmp_pos[0]), wpv_t=pos_t(cmp_pos[1]),
        wpk_b0=cmp_pos[0][CMP_STRIDE][:, None], wpv_b0=cmp_pos[1][CMP_STRIDE][:, None],
        wk_t=wk.T.astype(BF16), wv_t=wv.T.astype(BF16),
        bk_col=cmp_b[0][:, None], bv_col=cmp_b[1][:, None])


def _importance_matrix(nc_rows, nc_valid, ns, lane_off, width):
    m = np.zeros((nc_rows, width), np.float32)
    for j in range(ns):
        for n in range(SLC_RATIO * j - 1, SLC_RATIO * j + SLC_RATIO):
            if 0 <= n < nc_valid:
                m[n, lane_off + j] = 1.0
    return m


def _select_blocks_t(st_ref, h, ns):
    nb, nq = st_ref.shape[1], st_ref.shape[2]
    ngrp = nb // SUBLANES
    groups = [st_ref[h, SUBLANES * r:SUBLANES * (r + 1), :] for r in range(ngrp)]
    ranks = [jnp.zeros((SUBLANES, nq), F32) for _ in range(ngrp)]
    jrow = lax.broadcasted_iota(jnp.int32, (SUBLANES, nq), 0)
    for jp in range(min(ns, nb)):
        row = jnp.broadcast_to(st_ref[h, pl.ds(jp, 1), :], (SUBLANES, nq))
        for r in range(ngrp):
            ge = jnp.where(row >= groups[r], 1.0, 0.0)
            gt = jnp.where(row > groups[r], 1.0, 0.0)
            if jp < SUBLANES * r:
                inc = ge
            elif jp >= SUBLANES * (r + 1):
                inc = gt
            else:
                inc = jnp.where(jrow + SUBLANES * r > jp, ge, gt)
            ranks[r] = ranks[r] + inc
    return jnp.concatenate(
        [jnp.where((ranks[r] < N_SEL) & (groups[r] >= 0.0), 0.0, NEG) for r in range(ngrp)], axis=0)


def _nsa_prompt_kernel(q_ref, sm_ref, kc_ref, vc_ref, ks_ref, vs_ref, kw_ref, vw_ref, mt_ref,
                       o_ref, acc_ref, m_ref, qa_ref, st_ref, *, T, tk, qb):
    i = pl.program_id(1)
    ns = T // SLC_BLOCK
    nc = kc_ref.shape[1]
    rows = GRP * qb
    qpos = i * qb + lax.broadcasted_iota(jnp.int32, (qb, 1), 0)
    qpos4 = jnp.concatenate([qpos] * GRP, axis=0)
    qpos_t = i * qb + lax.broadcasted_iota(jnp.int32, (HD, qb), 1)
    jb_t = lax.broadcasted_iota(jnp.int32, (HD, qb), 0)
    sig = jax.nn.sigmoid(sm_ref[0])
    lane = lax.broadcasted_iota(jnp.int32, (qb, LANES), 1)
    wk = WINDOW + qb
    n_idx = lax.broadcasted_iota(jnp.int32, (1, nc), 1)
    cmp_mask = (CMP_STRIDE * n_idx + CMP_LEN - 1) <= qpos4
    wstart = pl.multiple_of(jnp.maximum(i - WINDOW // qb, 0) * qb, qb)
    dlt = qpos - (wstart + lax.broadcasted_iota(jnp.int32, (1, wk), 1))
    wbias = jnp.where((dlt >= 0) & (dlt < WINDOW), 0.0, NEG)
    cur_t = qpos_t // SLC_BLOCK
    valid_t = (jb_t * SLC_BLOCK <= qpos_t) & (jb_t < ns)
    forced_t = (jb_t == 0) | (jb_t == cur_t) | (jb_t == cur_t - 1)
    zpad = jnp.zeros((LANES - HD, qb), F32)
    kc_hi, kc_lo = _split2(kc_ref[0])

    o_cs, o_ws = [], []
    for h in range(HKV):
        parts = []
        for g in range(GRP):
            hd = GRP * h + g
            src = q_ref[0, :, LANES * (hd // 2):LANES * (hd // 2 + 1)]
            if hd % 2 != h:
                src = pltpu.roll(src, HD, 1)
            parts.append(jnp.where((lane < HD) if h == 0 else (lane >= HD), src, 0.0))
        qh, qh_lo = _split2(jnp.concatenate(parts, axis=0))

        p_c = _masked_softmax(_dot_nt(qh, kc_hi) + _dot_nt(qh_lo, kc_hi) + _dot_nt(qh, kc_lo), cmp_mask)
        o_cs.append(_dot(p_c.astype(BF16), vc_ref[0, :, LANES * h:LANES * (h + 1)]))

        psum = p_c[0:qb] + p_c[qb:2 * qb] + p_c[2 * qb:3 * qb] + p_c[3 * qb:4 * qb]
        x1 = psum.astype(BF16)
        r1 = psum - x1.astype(F32)
        x2 = r1.astype(BF16)
        x3 = (r1 - x2.astype(F32)).astype(BF16)
        mt = mt_ref[...]
        imp_t = _dot_nt(mt, x1) + _dot_nt(mt, x2) + _dot_nt(mt, x3)
        score_t = jnp.where(valid_t, jnp.where(forced_t, FORCE, imp_t), -1.0)
        st_ref[h] = jnp.where(jb_t < ns, score_t, -jnp.inf)
        sel_t = _select_blocks_t(st_ref, h, ns)
        selneg = (jnp.concatenate([zpad, sel_t], axis=0) if h == 0
                  else jnp.concatenate([sel_t, zpad], axis=0)).T
        qa_ref[h] = qh + jnp.concatenate([selneg.astype(BF16)] * GRP, axis=0)

        sw = _dot_nt(qh, kw_ref[0, pl.ds(wstart, wk), :])
        sw = jnp.concatenate([sw[qb * g:qb * (g + 1)] + wbias for g in range(GRP)], axis=0)
        e = jnp.exp(sw - jnp.max(sw, axis=-1, keepdims=True))
        o_w = _dot(e.astype(BF16), vw_ref[0, pl.ds(wstart, wk), LANES * h:LANES * (h + 1)])
        o_ws.append(o_w / o_w[:, HD:HD + 1])

    m_ref[...] = jnp.full(m_ref.shape, -jnp.inf, F32)
    acc_ref[...] = jnp.zeros(acc_ref.shape, F32)

    def chunk(c, diagonal):
        start = pl.multiple_of(c * tk, tk)
        for h in range(HKV):
            k = ks_ref[0, pl.ds(start, tk), LANES * h:LANES * (h + 1)]
            v = vs_ref[0, pl.ds(start, tk), LANES * h:LANES * (h + 1)]
            sc = _dot_nt(qa_ref[h], k)
            if diagonal:
                tok = start + lax.broadcasted_iota(jnp.int32, (1, tk), 1)
                sc = jnp.where(tok <= qpos4, sc, NEG)
            m_prev = m_ref[h]
            m_new = jnp.maximum(m_prev, jnp.max(sc, axis=-1, keepdims=True))
            alpha = jnp.exp(m_prev - m_new)
            p = jnp.exp(sc - m_new[:, 0:1])
            acc_ref[h] = alpha * acc_ref[h] + _dot(p.astype(BF16), v)
            m_ref[h] = m_new

    n_full = (i * qb) // tk

    def full_chunk(c, carry):
        chunk(c, False)
        return carry

    lax.fori_loop(0, n_full, full_chunk, 0)
    chunk(n_full, True)

    for h in range(HKV):
        acc = acc_ref[h]
        o_s = acc / acc[:, HD:HD + 1]
        outs = []
        for g in range(GRP):
            c0 = 3 * (GRP * h + g)
            r = slice(qb * g, qb * (g + 1))
            outs.append(sig[:, c0:c0 + 1] * o_cs[h][r] + sig[:, c0 + 1:c0 + 2] * o_s[r]
                        + sig[:, c0 + 2:c0 + 3] * o_ws[h][r])
        base = GRP * HD * h
        for pair in range(GRP // 2):
            hi = pltpu.roll(outs[2 * pair + 1], HD, 1)
            o_ref[0, :, base + LANES * pair:base + LANES * (pair + 1)] = jnp.where(
                lane < HD, outs[2 * pair], hi).astype(BF16)


def _nsa_prompt(q, sm, kc, vc, ks, vs, kw, vw, qb=256, tk=512):
    B, T, _ = q.shape
    ns = T // SLC_BLOCK
    nc = kc.shape[1]
    assert ns <= HD and T % tk == 0 and T >= WINDOW + qb and WINDOW % qb == 0 and tk % qb == 0
    m2 = _importance_matrix(nc, nc - 1, ns, 0, HD).T
    full = lambda w: pl.BlockSpec((1, T, w), lambda b, i: (b, 0, 0))
    return pl.pallas_call(
        functools.partial(_nsa_prompt_kernel, T=T, tk=tk, qb=qb),
        grid=(B, T // qb),
        in_specs=[pl.BlockSpec((1, qb, NSA_W), lambda b, i: (b, i, 0)),
                  pl.BlockSpec((1, qb, LANES), lambda b, i: (b, i, 0)),
                  pl.BlockSpec((1, nc, KV_W), lambda b, i: (b, 0, 0)),
                  pl.BlockSpec((1, nc, 2 * KV_W), lambda b, i: (b, 0, 0)),
                  full(256), full(256), full(KV_W), full(256),
                  _const_spec(m2.shape)],
        out_specs=pl.BlockSpec((1, qb, NSA_W), lambda b, i: (b, i, 0)),
        out_shape=jax.ShapeDtypeStruct((B, T, NSA_W), BF16),
        scratch_shapes=[pltpu.VMEM((HKV, GRP * qb, LANES), F32)] * 2
                       + [pltpu.VMEM((HKV, GRP * qb, LANES), BF16), pltpu.VMEM((HKV, HD, qb), F32)],
        compiler_params=_params("arbitrary", "arbitrary"),
        name="nsa_prompt",
    )(q, sm, kc, vc, ks, vs, kw, vw, jnp.asarray(m2, BF16))


def _mlstm_gates(g):
    b = pltpu.roll(_cumsum_rows(_log_sigmoid(g)), LANES - MH, 1)
    return b, g - b


def _mlstm_prompt_kernel(mqkv_ref, sm_ref, mo_ref, gb_ref, om_ref, c_out, n_out, m_out,
                         c_s, n_s, m_s, *, B, L):
    t = pl.program_id(0)

    @pl.when(t == 0)
    def _():
        c_s[...] = jnp.zeros(c_s.shape, F32)
        n_s[...] = jnp.zeros(n_s.shape, F32)
        m_s[...] = jnp.zeros(m_s.shape, F32)

    li = lax.broadcasted_iota(jnp.int32, (L, L), 0)
    si = lax.broadcasted_iota(jnp.int32, (L, L), 1)
    for b in range(B):
        b_al, r = _mlstm_gates(sm_ref[b] + gb_ref[...])
        r_t = jnp.concatenate([r, jnp.zeros((LANES - L, LANES), F32)], axis=0).T if L < LANES else r.T
        heads = []
        for hd in range(MH):
            idx = b * MH + hd
            bcol = b_al[:, SM_IG + hd:SM_IG + hd + 1]
            rcol = r[:, SM_IG + hd:SM_IG + hd + 1]
            rrow = r_t[SM_IG + hd:SM_IG + hd + 1, 0:L]
            mprev = m_s[idx][:, 0:1]
            acol = bcol + mprev
            logd = jnp.where(si <= li, bcol + rrow, NEG)
            mt = jnp.maximum(acol, jnp.max(logd, axis=-1, keepdims=True))
            dm = jnp.exp(logd - mt)
            inter = jnp.exp(acol - mt)
            q = mqkv_ref[b, :, HD * hd:HD * (hd + 1)]
            k = mqkv_ref[b, :, MLSTM_W + HD * hd:MLSTM_W + HD * (hd + 1)] * Q_SCALE
            v = mqkv_ref[b, :, 2 * MLSTM_W + HD * hd:2 * MLSTM_W + HD * (hd + 1)]
            qb, kb = q.astype(BF16), k.astype(BF16)
            qk = _dot_nt(qb, kb) * dm
            c_prev = c_s[idx]
            n_prev = n_s[idx]
            num = _dot(qk.astype(BF16), v.astype(BF16)) + inter * _dot(qb, c_prev.astype(BF16))
            den = jnp.sum(qk, axis=-1, keepdims=True) + inter * jnp.sum(q * n_prev, axis=-1, keepdims=True)
            hout = num / jnp.maximum(jnp.abs(den), jnp.exp(-mt))
            heads.append(jax.nn.sigmoid(mo_ref[b, :, HD * hd:HD * (hd + 1)]) * hout)
            m_last = mt[L - 1:L, :]
            b_last = bcol[L - 1:L, :]
            wcol = jnp.exp(b_last + rcol - m_last)
            decay = jnp.exp(b_last + mprev - m_last)
            c_s[idx] = decay * c_prev + _dot_tn(kb, (wcol * v).astype(BF16))
            n_s[idx] = decay * n_prev + jnp.sum(wcol * k, axis=0, keepdims=True)
            m_s[idx] = jnp.broadcast_to(m_last, (1, LANES))
        om_ref[b] = jnp.concatenate(heads, axis=-1).astype(BF16)

    @pl.when(t == pl.num_programs(0) - 1)
    def _():
        c_out[...] = c_s[...]
        n_out[...] = n_s[...]
        m_out[...] = m_s[...]


def _mlstm_prompt(mqkv, sm, mo, gate_b_tile, L=256):
    B, T, _ = mqkv.shape
    assert T % L == 0 and (L <= LANES or L % LANES == 0)
    blk = lambda w: pl.BlockSpec((B, L, w), lambda t: (0, t, 0))
    st = lambda shape: pl.BlockSpec(shape, lambda t: (0,) * len(shape))
    shapes = [(B * MH, HD, HD), (B * MH, 1, HD), (B * MH, 1, LANES)]
    return pl.pallas_call(
        functools.partial(_mlstm_prompt_kernel, B=B, L=L),
        grid=(T // L,),
        in_specs=[blk(3 * MLSTM_W), blk(LANES), blk(MLSTM_W), _const_spec((1, LANES))],
        out_specs=[blk(MLSTM_W)] + [st(s) for s in shapes],
        out_shape=[jax.ShapeDtypeStruct((B, T, MLSTM_W), BF16)] + [jax.ShapeDtypeStruct(s, F32) for s in shapes],
        scratch_shapes=[pltpu.VMEM(s, F32) for s in shapes],
        compiler_params=_params("arbitrary"),
        name="mlstm_prompt",
    )(mqkv, sm, mo, gate_b_tile)


FF_CHUNK = 1408


def _outffn_prompt_kernel(x_ref, on_ref, om_ref, op_ref, gt1_ref, sh2_ref, sc2_ref, gt2_ref,
                          g2_ref, fg_ref, wout_ref, wup_ref, cw_ref, cb_ref, wdn_ref,
                          y_ref, fs_ref, prev_ref, es_ref, *, tm, d_ff, final):
    t = pl.program_id(1)
    mix = (_dot(on_ref[0], wout_ref[0:NSA_W, :])
           + _dot(om_ref[0], wout_ref[NSA_W:NSA_W + MLSTM_W, :])
           + _dot(op_ref[0], wout_ref[NSA_W + MLSTM_W:, :]))
    x1 = x_ref[0] + gt1_ref[0] * mix
    h2 = _norm_mod(x1, g2_ref[...], sc2_ref[0], sh2_ref[0]).astype(BF16)

    @pl.when(t == 0)
    def _():
        prev_ref[...] = jnp.zeros(prev_ref.shape, F32)

    f = jnp.zeros(x1.shape, F32)
    w = FF_CHUNK
    for c in range(d_ff // w):
        for half, off in ((0, c * w), (1, d_ff + c * w)):
            es_ref[0:SUBLANES, half * w:(half + 1) * w] = prev_ref[:, off:off + w]
            es_ref[SUBLANES:SUBLANES + tm, half * w:(half + 1) * w] = _dot(h2, wup_ref[:, off:off + w])
            prev_ref[:, off:off + w] = es_ref[tm:tm + SUBLANES, half * w:(half + 1) * w]
        ys = []
        for half, off in ((0, c * w), (1, d_ff + c * w)):
            y = cb_ref[:, off:off + w]
            for j in range(FFN_CONV):
                y = y + cw_ref[j:j + 1, off:off + w] * es_ref[pl.ds(SUBLANES - (FFN_CONV - 1) + j, tm), half * w:(half + 1) * w]
            ys.append(y)
        act = ys[0] * jax.nn.sigmoid(ys[0]) * ys[1]
        f = f + _dot(act.astype(BF16), wdn_ref[c * w:(c + 1) * w, :])
    x2 = x1 + gt2_ref[0] * f
    y_ref[0] = _rmsnorm(x2, fg_ref[...]) if final else x2
    fs_ref[0] = prev_ref[...]


def _outffn_prompt(x, on, om, op, gt1, sh2, sc2, gt2, g2, fg, w_out, w_up, conv_w, conv_b, w_dn, final, tm=512):
    B, T, D = x.shape
    d_ff = w_dn.shape[0]
    assert T % tm == 0 and d_ff % FF_CHUNK == 0
    row = lambda w: pl.BlockSpec((1, tm, w), lambda b, t: (b, t, 0))
    mod = pl.BlockSpec((1, 1, D), lambda b, t: (b, 0, 0))
    return pl.pallas_call(
        functools.partial(_outffn_prompt_kernel, tm=tm, d_ff=d_ff, final=final),
        grid=(B, T // tm),
        in_specs=[row(D), row(NSA_W), row(MLSTM_W), row(POOL_W), mod, mod, mod, mod,
                  _const_spec((1, D)), _const_spec((1, D)), _const_spec(w_out.shape), _const_spec(w_up.shape),
                  _const_spec(conv_w.shape), _const_spec(conv_b.shape), _const_spec(w_dn.shape)],
        out_specs=[row(D), pl.BlockSpec((1, SUBLANES, 2 * d_ff), lambda b, t: (b, 0, 0))],
        out_shape=[jax.ShapeDtypeStruct((B, T, D), F32), jax.ShapeDtypeStruct((B, SUBLANES, 2 * d_ff), F32)],
        scratch_shapes=[pltpu.VMEM((SUBLANES, 2 * d_ff), F32), pltpu.VMEM((SUBLANES + tm, 2 * FF_CHUNK), F32)],
        compiler_params=_params("arbitrary", "arbitrary"),
        name="outffn_prompt",
    )(x, on, om, op, gt1, sh2, sc2, gt2, g2, fg, w_out, w_up, conv_w, conv_b, w_dn)


def _prep_layer(l, w, page):
    gate_b = jnp.zeros((1, LANES), F32).at[0, SM_IG:SM_IG + 2 * MH].set(w["mlstm_gate_b"][l])
    pw = w["pool_w"][l]
    z = jnp.zeros((HD, HD), F32)
    pool_bd = jnp.block([[pw[i] if i == j else z for j in range(4)] for i in range(4)]).astype(BF16)
    wq = w["w_in"][l][:, :NSA_W]
    wq_hi = wq.astype(BF16)
    return dict(
        w_q=wq_hi, w_q2=(wq_hi, (wq - wq_hi.astype(F32)).astype(BF16)),
        g1=w["norm1_g"][l][None], g2=w["norm2_g"][l][None],
        w_b=_prep_w_in(w["w_in"][l]), pool_bd=pool_bd, pool_scale=w["pool_scale"][l][None],
        cw=_prep_compress(w["nsa_cmp_pos"][l], w["nsa_cmp_w"][l], w["nsa_cmp_b"][l], page),
        gate_b=gate_b, w_out=w["w_out"][l].astype(BF16), w_up=w["ffn_w_up"][l].astype(BF16),
        conv_w=w["ffn_conv_w"][l], conv_b=w["ffn_conv_b"][l][None], w_dn=w["ffn_w_down"][l].astype(BF16),
        fg=w["final_g"][None])


def _prompt_layer(x, mod, lw, final):
    B, T, D = x.shape
    sh1, sc1, gt1, sh2, sc2, gt2 = [mod[:, :, D * i:D * (i + 1)] for i in range(6)]
    (q, nk, nv, ks, vs, kw, vw, wkv, mqkv, mo, op, sm, ps) = _inproj_prompt(
        x, lw["g1"], sh1, sc1, lw["w_b"], lw["w_q2"], lw["pool_bd"], lw["pool_scale"])
    kc, vc = _compress_prompt(nk, nv, lw["cw"])
    on = _nsa_prompt(q, sm, kc, vc, ks, vs, kw, vw)
    om, c_st, n_st, m_st = _mlstm_prompt(mqkv, sm, mo, lw["gate_b"])
    y, fs = _outffn_prompt(x, on, om, op, gt1, sh2, sc2, gt2, lw["g2"], lw["fg"], lw["w_out"],
                           lw["w_up"], lw["conv_w"], lw["conv_b"], lw["w_dn"], final)
    wb = wkv.shape[1]
    states = (nk.reshape(B, T, 2, HKV, HD), nv.reshape(B, T, 2, HKV, HD),
              wkv[:, :, :KV_W].reshape(B, wb, HKV, HD), wkv[:, :, KV_W:].reshape(B, wb, HKV, HD),
              c_st.reshape(B, MH, HD, HD), n_st.reshape(B, MH, HD), m_st[:, 0, 0].reshape(B, MH),
              ps[:, 2 * SUBLANES - POOL_STATE:], fs[:, SUBLANES - (FFN_CONV - 1):])
    return y, states


def _inproj_sample_kernel(x_ref, g_ref, sh_ref, sc_ref, w_ref, wq_ref, pw_ref, pscale_ref, prev_ref,
                          q_ref, nk_ref, nv_ref, wkv_ref, mqkv_ref, mo_ref, op_ref, sm_ref, pu_ref, *, pos0):
    h = _norm_mod(x_ref[...], g_ref[...], sc_ref[...], sh_ref[...]).astype(BF16)

    def seg(a, b):
        return _dot(h, w_ref[:, a:b])

    q_ref[...] = _dot(h, wq_ref[...]) * Q_SCALE
    nk_ref[...] = seg(C_NK, C_NV)
    nv_ref[...] = seg(C_NV, C_KS)
    wkv_ref[...] = seg(C_WKV, C_MQKV)
    mqkv_ref[...] = seg(C_MQKV, C_MO)
    mo_ref[...] = seg(C_MO, C_PU)
    sm_ref[...] = seg(C_SM, C_END)
    pu = seg(C_PU, C_SM)
    pu_ref[...] = pu
    acc = pu
    sums = {}
    for i in range(1, POOL_STATE + 1):
        acc = acc + prev_ref[:, POOL_STATE - i, :]
        if i + 1 in POOL_WINDOWS:
            sums[i + 1] = acc
    cnt = {w: float(min(w, pos0 + 1)) for w in POOL_WINDOWS}
    op_ref[...] = _pool_mix(pu, sums, cnt, pw_ref, pscale_ref)


def _inproj_sample(l, x, g1, sh1, sc1, w_b, w_q, pool_bd, pool_scale, state_pool, pos0):
    DB, D = x.shape
    full = lambda shape: pl.BlockSpec(shape, lambda i: (0,) * len(shape))
    widths = [NSA_W, 256, 256, 256, 3 * MLSTM_W, MLSTM_W, POOL_W, LANES, POOL_W]
    return pl.pallas_call(
        functools.partial(_inproj_sample_kernel, pos0=pos0),
        grid=(1,),
        in_specs=[full((DB, D)), full((1, D)), full((DB, D)), full((DB, D)), full(w_b.shape),
                  full(w_q.shape), full(pool_bd.shape), full((1, POOL_W)),
                  pl.BlockSpec((None, DB, POOL_STATE, POOL_W), lambda i: (l, 0, 0, 0))],
        out_specs=[full((DB, w)) for w in widths],
        out_shape=[jax.ShapeDtypeStruct((DB, w), F32) for w in widths],
        compiler_params=_params("arbitrary"),
        name="inproj_sample",
    )(x, g1, sh1, sc1, w_b, w_q, pool_bd, pool_scale, state_pool)


PAGES_PER_STEP = 32


def _row_to_col(rowv):
    n = rowv.shape[1]
    eye = lax.broadcasted_iota(jnp.int32, (n, n), 0) == lax.broadcasted_iota(jnp.int32, (n, n), 1)
    return jnp.sum(jnp.where(eye, jnp.broadcast_to(rowv, (n, n)), 0.0), axis=1, keepdims=True)


def _compress_sample_t_kernel(pt_ref, *refs, npg):
    del pt_ref
    k_refs, v_refs = refs[:npg], refs[npg:2 * npg]
    wk_ref, wv_ref, sg_ref, ok_ref, ov_ref = refs[2 * npg:]
    for src, w_ref, o_ref in ((k_refs, wk_ref, ok_ref), (v_refs, wv_ref, ov_ref)):
        wa = w_ref[0:HD, :]
        wb = w_ref[HD:2 * HD, :]
        for h in range(HKV):
            prods = []
            for i in range(npg):
                x = src[i][h]
                prods.append(jnp.concatenate([x * wa, x * wb], axis=0).astype(BF16))
            o_ref[0, h] = _dot(jnp.concatenate(prods, axis=1), sg_ref[...])


def _compress_sample_t(l, cache_kt, cache_vt, page_table, cw):
    DB, n_pages = page_table.shape
    page = cache_kt.shape[-1]
    npg = min(PAGES_PER_STEP, n_pages)
    assert n_pages % npg == 0 and page % CMP_STRIDE == 0
    spp = page // CMP_STRIDE
    nseg = n_pages * spp
    assert (npg * spp) % LANES == 0 or npg * spp == nseg
    sg = np.zeros((npg * page, npg * spp), np.float32)
    sg[np.arange(npg * page), np.arange(npg * page) // CMP_STRIDE] = 1.0

    def page_spec(i):
        return pl.BlockSpec((None, None, None, HKV, HD, page),
                            lambda b, c, pt: (l, pt[b, c * npg + i], 0, 0, 0, 0))

    out = pl.BlockSpec((1, HKV, 2 * HD, npg * spp), lambda b, c, pt: (b, 0, 0, c))
    cst = lambda shape: pl.BlockSpec(shape, lambda b, c, pt: (0,) * len(shape))
    grid_spec = pltpu.PrefetchScalarGridSpec(
        num_scalar_prefetch=1, grid=(DB, n_pages // npg),
        in_specs=[page_spec(i) for i in range(npg)] * 2
                 + [cst((2 * HD, page)), cst((2 * HD, page)), cst(sg.shape)],
        out_specs=[out] * 2)
    return pl.pallas_call(
        functools.partial(_compress_sample_t_kernel, npg=npg),
        grid_spec=grid_spec,
        out_shape=[jax.ShapeDtypeStruct((DB, HKV, 2 * HD, nseg), F32)] * 2,
        compiler_params=_params("arbitrary", "arbitrary"),
        name="compress_sample",
    )(page_table, *([cache_kt] * npg), *([cache_vt] * npg), cw["wpk_t"], cw["wpv_t"], jnp.asarray(sg, BF16))


def _nsa_sample_a_t_kernel(q_ref, hk_ref, hv_ref, nk_ref, nv_ref, wkv_ref, wink_ref, winv_ref,
                           wpk_ref, wpv_ref, wk_ref, wv_ref, bk_ref, bv_ref, m_ref,
                           idx_ref, ocw_ref, wko_ref, wvo_ref, *, past, ns):
    ncs = hk_ref.shape[3]
    wb = wink_ref.shape[3]
    q8 = q_ref[0].astype(BF16)
    row = lax.broadcasted_iota(jnp.int32, (NSA_HEADS, 1), 0)
    lane_c = lax.broadcasted_iota(jnp.int32, (HD, ncs), 1)
    lane_w = lax.broadcasted_iota(jnp.int32, (HD, wb), 1)
    n_idx = lax.broadcasted_iota(jnp.int32, (1, ncs), 1)

    def per_head(fn):
        a, b = fn(0), fn(1)
        return jnp.where(row // GRP == 0, a, b)

    def comp_t(h_ref, new_row, wp_ref, w_ref, bias_ref, h):
        at = h_ref[0, h, 0:HD, :]
        bt = h_ref[0, h, HD:2 * HD, :]
        b_new = _row_to_col(new_row[:, HD * h:HD * (h + 1)]) * wp_ref[...]
        pooled = at + jnp.where(lane_c == ncs - 1, b_new, pltpu.roll(bt, ncs - 1, 1))
        return (_dot(w_ref[...], pooled.astype(BF16)) + bias_ref[...]).astype(BF16)

    kct = [comp_t(hk_ref, nk_ref[0], wpk_ref, wk_ref, bk_ref, h) for h in range(HKV)]
    vct = [comp_t(hv_ref, nv_ref[0], wpv_ref, wv_ref, bv_ref, h) for h in range(HKV)]
    s = per_head(lambda h: _dot(q8, kct[h]))
    p = _masked_softmax(s, (CMP_STRIDE * n_idx + CMP_LEN - 1) <= past)
    pb = p.astype(BF16)
    o_c = per_head(lambda h: _dot_nt(pb, vct[h]))

    rowi = lax.broadcasted_iota(jnp.int32, p.shape, 0)
    p0 = jnp.sum(p[0:GRP], axis=0, keepdims=True)
    p1 = jnp.sum(p[GRP:2 * GRP], axis=0, keepdims=True)
    imp = _dot_split3(jnp.where(rowi == 0, p0, jnp.where(rowi == 1, p1, 0.0)), m_ref[...])
    lane = lax.broadcasted_iota(jnp.int32, imp.shape, 1)
    cur = past // SLC_BLOCK
    valid = lane * SLC_BLOCK <= past
    forced = (lane == 0) | (lane == cur) | (lane == cur - 1)
    score = jnp.where(valid, jnp.where(forced, FORCE, imp), -1.0)
    score = jnp.where(lane < ns, score, -jnp.inf)
    lane_o = lax.broadcasted_iota(jnp.int32, (SUBLANES, LANES), 1)
    idx = jnp.full((SUBLANES, LANES), -1, jnp.int32)
    for r in range(min(N_SEL, ns)):
        mx = jnp.max(score, axis=-1, keepdims=True)
        first = jnp.min(jnp.where(score == mx, lane, 1 << 20), axis=-1, keepdims=True)
        idx = jnp.where(lane_o == r, jnp.where(mx >= 0.0, first, -1), idx)
        score = jnp.where(lane == first, -jnp.inf, score)
    idx_ref[0] = idx

    sel_head = lambda t: jnp.where(row // GRP == 0, t[:, :HD], t[:, HD:])
    k_new = wkv_ref[0][:, :KV_W]
    v_new = wkv_ref[0][:, KV_W:]
    s_w = per_head(lambda h: _dot(q8, wink_ref[0, h].astype(BF16)))
    dlt = wb - lax.broadcasted_iota(jnp.int32, (1, wb), 1)
    mask = (dlt < WINDOW) & (past - dlt >= 0)
    s_new = jnp.sum(q8.astype(F32) * sel_head(k_new).astype(BF16).astype(F32), axis=-1, keepdims=True)
    s_w = jnp.where(mask, s_w, NEG)
    mx = jnp.maximum(jnp.max(s_w, axis=-1, keepdims=True), s_new)
    e = jnp.where(mask, jnp.exp(s_w - mx), 0.0)
    e_new = jnp.exp(s_new - mx)
    den = jnp.sum(e, axis=-1, keepdims=True) + e_new
    eb = e.astype(BF16)
    o_w = per_head(lambda h: _dot_nt(eb, winv_ref[0, h].astype(BF16)))
    o_w = (o_w + e_new * sel_head(v_new).astype(BF16).astype(F32)) / den
    ocw_ref[0] = jnp.concatenate([o_c, o_w], axis=-1)
    for h in range(HKV):
        for src, new, dst in ((wink_ref, k_new, wko_ref), (winv_ref, v_new, wvo_ref)):
            col = _row_to_col(new[:, HD * h:HD * (h + 1)])
            dst[0, h] = jnp.where(lane_w == wb - 1, col, pltpu.roll(src[0, h], wb - 1, 1))


NSA_A_SEQS = 4
N_SEQ_IN = 8
N_SEQ_OUT = 4


def _nsa_sample_a_t_multi(*refs, nsq, past, ns):
    for sq in range(nsq):
        one = lambda r: r.at[pl.ds(sq, 1)]
        _nsa_sample_a_t_kernel(*[one(r) for r in refs[:N_SEQ_IN]], *refs[N_SEQ_IN:-N_SEQ_OUT],
                               *[one(r) for r in refs[-N_SEQ_OUT:]], past=past, ns=ns)


def _nsa_sample_a_t(l, q3, hk, hv, nk3, nv3, wkv3, win_kt, win_vt, cw, past):
    DB = q3.shape[0]
    ncs = hk.shape[3]
    wb = win_kt.shape[-1]
    ns = -(-(past + 1) // SLC_BLOCK)
    nsl = -(-ns // LANES) * LANES
    m = _importance_matrix(ncs, ncs, ns, 0, nsl)
    nsq = NSA_A_SEQS if DB % NSA_A_SEQS == 0 else 1
    per = lambda shape: pl.BlockSpec((nsq,) + shape, lambda b: (b,) + (0,) * len(shape))
    win = pl.BlockSpec((None, nsq, HKV, HD, wb), lambda b: (l, b, 0, 0, 0))
    shapes = [(DB, SUBLANES, LANES), (DB, NSA_HEADS, 2 * HD), (DB, HKV, HD, wb), (DB, HKV, HD, wb)]
    dts = [jnp.int32, F32, F32, F32]
    return pl.pallas_call(
        functools.partial(_nsa_sample_a_t_multi, nsq=nsq, past=past, ns=ns),
        grid=(DB // nsq,),
        in_specs=[per((NSA_HEADS, HD))] + [per((HKV, 2 * HD, ncs))] * 2 + [per((1, 256))] * 3 + [win, win]
                 + [_const_spec((HD, 1))] * 2 + [_const_spec((HD, HD))] * 2
                 + [_const_spec((HD, 1))] * 2 + [_const_spec(m.shape)],
        out_specs=[per(s[1:]) for s in shapes],
        out_shape=[jax.ShapeDtypeStruct(s, d) for s, d in zip(shapes, dts)],
        compiler_params=_params("arbitrary"),
        name="nsa_sample_a",
    )(q3, hk, hv, nk3, nv3, wkv3, win_kt, win_vt, cw["wpk_b0"], cw["wpv_b0"], cw["wk_t"], cw["wv_t"],
      cw["bk_col"], cw["bv_col"], jnp.asarray(m, BF16))


def _nsa_sample_b_t_kernel(pt_ref, ix_ref, *refs, past, nbp, nsel, page):
    del pt_ref
    k_refs, v_refs = refs[:HKV * nsel], refs[HKV * nsel:2 * HKV * nsel]
    q_ref, ocw_ref, nk_ref, nv_ref, g_ref, o_ref = refs[2 * HKV * nsel:]
    b = pl.program_id(0)
    bpp = page // SLC_BLOCK
    q8 = q_ref[0].astype(BF16)
    row = lax.broadcasted_iota(jnp.int32, (NSA_HEADS, 1), 0)
    sel_head = lambda t: jnp.where(row // GRP == 0, t[:, :HD], t[:, HD:])
    k_new = sel_head(nk_ref[0][:, KV_W:]).astype(BF16).astype(F32)
    v_new = sel_head(nv_ref[0][:, KV_W:]).astype(BF16).astype(F32)
    s_new = jnp.sum(q8.astype(F32) * k_new, axis=-1, keepdims=True)
    nk = nsel * page
    lane = lax.broadcasted_iota(jnp.int32, (1, nk), 1)
    o_s = jnp.zeros((NSA_HEADS, HD), F32)
    for h in range(HKV):
        kt = jnp.concatenate([k_refs[h * nsel + r][...] for r in range(nsel)], axis=1).astype(BF16)
        vt = jnp.concatenate([v_refs[h * nsel + r][...] for r in range(nsel)], axis=1).astype(BF16)
        jv = jnp.full((1, nk), -1, jnp.int32)
        n_new = jnp.int32(0)
        for r in range(nsel):
            j = ix_ref[b, h, r]
            jv = jnp.where(lane // page == r, j, jv)
            n_new = n_new + (j == nbp).astype(jnp.int32)
        t_in = lane % page
        tok = (jv // bpp) * page + t_in
        mask = (jv >= 0) & (jv < nbp) & (t_in // SLC_BLOCK == jv % bpp) & (tok <= past)
        has_new = n_new > 0
        s = jnp.where(mask, _dot(q8, kt), NEG)
        sn = jnp.where(has_new, s_new, NEG)
        mx = jnp.maximum(jnp.max(s, axis=-1, keepdims=True), sn)
        e = jnp.where(mask, jnp.exp(s - mx), 0.0)
        e_new = jnp.where(has_new, jnp.exp(sn - mx), 0.0)
        den = jnp.maximum(jnp.sum(e, axis=-1, keepdims=True) + e_new, jnp.finfo(jnp.float32).tiny)
        o_h = (_dot_nt(e.astype(BF16), vt) + e_new * v_new) / den
        o_s = jnp.where(row // GRP == h, o_h, o_s)
    g = jax.nn.sigmoid(g_ref[0])
    ocw = ocw_ref[0]
    o_ref[0] = g[:, 0:1] * ocw[:, :HD] + g[:, 1:2] * o_s + g[:, 2:3] * ocw[:, HD:]


def _nsa_sample_b_t(l, cache_kt, cache_vt, page_table, idx, q3, ocw, nk3, nv3, g3, past):
    DB = q3.shape[0]
    page = cache_kt.shape[-1]
    bpp = page // SLC_BLOCK
    nbp = past // SLC_BLOCK
    nsel = idx.shape[2]

    def tile_spec(h, r):
        def imap(b, pt, ix):
            j = jnp.clip(ix[b, h, r], 0, nbp - 1)
            return (l, pt[b, j // bpp], 1, h, 0, 0)
        return pl.BlockSpec((None, None, None, None, HD, page), imap)

    per = lambda shape: pl.BlockSpec((1,) + shape, lambda b, pt, ix: (b,) + (0,) * len(shape))
    kv_specs = [tile_spec(h, r) for h in range(HKV) for r in range(nsel)]
    grid_spec = pltpu.PrefetchScalarGridSpec(
        num_scalar_prefetch=2, grid=(DB,),
        in_specs=kv_specs * 2 + [per((NSA_HEADS, HD)), per((NSA_HEADS, 2 * HD)), per((1, 256)), per((1, 256)),
                                 per((NSA_HEADS, 3))],
        out_specs=per((NSA_HEADS, HD)))
    n = HKV * nsel
    return pl.pallas_call(
        functools.partial(_nsa_sample_b_t_kernel, past=past, nbp=nbp, nsel=nsel, page=page),
        grid_spec=grid_spec,
        out_shape=jax.ShapeDtypeStruct((DB, NSA_HEADS, HD), F32),
        compiler_params=_params("arbitrary"),
        name="nsa_sample_b",
    )(page_table, idx, *([cache_kt] * n), *([cache_vt] * n), q3, ocw, nk3, nv3, g3)


SEQ_PER_STEP = 8


def _mlstm_sample_kernel(mqkv_ref, sm_ref, mo_ref, gb_ref, c_ref, n_ref, m_ref,
                         om_ref, c_out, n_out, m_out, *, nb):
    eye = (lax.broadcasted_iota(jnp.int32, (HD, HD), 0) == lax.broadcasted_iota(jnp.int32, (HD, HD), 1))
    lane = lax.broadcasted_iota(jnp.int32, (1, LANES), 1)

    def col(rowv):
        return jnp.sum(jnp.where(eye, jnp.broadcast_to(rowv, (HD, HD)), 0.0), axis=1, keepdims=True)

    for b in range(nb):
        g = sm_ref[b] + gb_ref[...]
        lf = _log_sigmoid(g)
        heads = []
        m_tile = jnp.zeros((1, LANES), F32)
        for hd in range(MH):
            ig = g[:, SM_IG + hd:SM_IG + hd + 1]
            a = lf[:, SM_FG + hd:SM_FG + hd + 1] + m_ref[b][:, hd:hd + 1]
            mt = jnp.maximum(a, ig)
            dm = jnp.exp(ig - mt)
            inter = jnp.exp(a - mt)
            q = mqkv_ref[b][:, HD * hd:HD * (hd + 1)]
            k = mqkv_ref[b][:, MLSTM_W + HD * hd:MLSTM_W + HD * (hd + 1)] * Q_SCALE
            v = mqkv_ref[b][:, 2 * MLSTM_W + HD * hd:2 * MLSTM_W + HD * (hd + 1)]
            c_prev = c_ref[b, hd]
            n_prev = n_ref[b][hd:hd + 1, :]
            qk = jnp.sum(q * k, axis=-1, keepdims=True) * dm
            q_c = jnp.sum(col(q) * c_prev, axis=0, keepdims=True)
            num = qk * v + inter * q_c
            den = qk + inter * jnp.sum(q * n_prev, axis=-1, keepdims=True)
            hout = num / jnp.maximum(jnp.abs(den), jnp.exp(-mt))
            heads.append(jax.nn.sigmoid(mo_ref[b][:, HD * hd:HD * (hd + 1)]) * hout)
            w = jnp.exp(ig - mt)
            decay = jnp.exp(a - mt)
            c_out[b, hd] = decay * c_prev + (w * col(k)) * v
            n_out[b, hd:hd + 1, :] = decay * n_prev + w * k
            m_tile = jnp.where(lane == hd, mt, m_tile)
        om_ref[b] = jnp.concatenate(heads, axis=-1)
        m_out[b] = m_tile


def _mlstm_sample(l, mqkv3, sm3, mo3, gate_b_tile, state_c, state_n, state_m4):
    DB = mqkv3.shape[0]
    nb = min(SEQ_PER_STEP, DB)
    assert DB % nb == 0
    per = lambda w: pl.BlockSpec((nb, 1, w), lambda i: (i, 0, 0))
    return pl.pallas_call(
        functools.partial(_mlstm_sample_kernel, nb=nb),
        grid=(DB // nb,),
        in_specs=[per(3 * MLSTM_W), per(LANES), per(MLSTM_W), _const_spec((1, LANES)),
                  pl.BlockSpec((None, nb, MH, HD, HD), lambda i: (l, i, 0, 0, 0)),
                  pl.BlockSpec((None, nb, MH, HD), lambda i: (l, i, 0, 0)),
                  pl.BlockSpec((None, nb, 1, MH), lambda i: (l, i, 0, 0))],
        out_specs=[per(MLSTM_W), pl.BlockSpec((nb, MH, HD, HD), lambda i: (i, 0, 0, 0)),
                   pl.BlockSpec((nb, MH, HD), lambda i: (i, 0, 0)), per(LANES)],
        out_shape=[jax.ShapeDtypeStruct((DB, 1, MLSTM_W), F32), jax.ShapeDtypeStruct((DB, MH, HD, HD), F32),
                   jax.ShapeDtypeStruct((DB, MH, HD), F32), jax.ShapeDtypeStruct((DB, 1, LANES), F32)],
        compiler_params=_params("arbitrary"),
        name="mlstm_sample",
    )(mqkv3, sm3, mo3, gate_b_tile, state_c, state_n, state_m4)


def _outffn_sample_kernel(x_ref, on_ref, om_ref, op_ref, gt1_ref, sh2_ref, sc2_ref, gt2_ref, g2_ref, fg_ref,
                          wout_ref, wup_ref, cw_ref, cb_ref, wdn_ref, prev_ref, y_ref, up_ref, *, d_ff, final):
    mix = (_dot(on_ref[...].astype(BF16), wout_ref[0:NSA_W, :])
           + _dot(om_ref[...].astype(BF16), wout_ref[NSA_W:NSA_W + MLSTM_W, :])
           + _dot(op_ref[...].astype(BF16), wout_ref[NSA_W + MLSTM_W:, :]))
    x1 = x_ref[...] + gt1_ref[...] * mix
    h2 = _norm_mod(x1, g2_ref[...], sc2_ref[...], sh2_ref[...]).astype(BF16)
    up = _dot(h2, wup_ref[...])
    up_ref[...] = up
    y = cb_ref[...] + cw_ref[FFN_CONV - 1:FFN_CONV, :] * up
    for j in range(FFN_CONV - 1):
        y = y + cw_ref[j:j + 1, :] * prev_ref[:, j, :]
    a, b = y[:, :d_ff], y[:, d_ff:]
    f = _dot((a * jax.nn.sigmoid(a) * b).astype(BF16), wdn_ref[...])
    x2 = x1 + gt2_ref[...] * f
    y_ref[...] = _rmsnorm(x2, fg_ref[...]) if final else x2


def _outffn_sample(l, x, on, om, op, gt1, sh2, sc2, gt2, g2, fg, w_out, w_up, conv_w, conv_b, w_dn,
                   state_ffn, final):
    DB, D = x.shape
    d_ff = w_dn.shape[0]
    full = lambda shape: pl.BlockSpec(shape, lambda i: (0,) * len(shape))
    args = (x, on, om, op, gt1, sh2, sc2, gt2, g2, fg, w_out, w_up, conv_w, conv_b, w_dn)
    return pl.pallas_call(
        functools.partial(_outffn_sample_kernel, d_ff=d_ff, final=final),
        grid=(1,),
        in_specs=[full(a.shape) for a in args]
                 + [pl.BlockSpec((None, DB, FFN_CONV - 1, 2 * d_ff), lambda i: (l, 0, 0, 0))],
        out_specs=[full((DB, D)), full((DB, 2 * d_ff))],
        out_shape=[jax.ShapeDtypeStruct((DB, D), F32), jax.ShapeDtypeStruct((DB, 2 * d_ff), F32)],
        compiler_params=_params("arbitrary"),
        name="outffn_sample",
    )(*args, state_ffn)


def _sample_layer(l, x, mod, lw, caches, final):
    DB, D = x.shape
    ckt, cvt, page_table, win_kt, win_vt, st_c, st_n, st_m, st_pool, st_ffn = caches
    past = page_table.shape[1] * ckt.shape[-1]
    sh1, sc1, gt1, sh2, sc2, gt2 = [mod[:, D * i:D * (i + 1)] for i in range(6)]
    q, nk, nv, wkv, mqkv, mo, op, sm, pu = _inproj_sample(
        l, x, lw["g1"], sh1, sc1, lw["w_b"], lw["w_q"], lw["pool_bd"], lw["pool_scale"], st_pool, past)
    r3 = lambda t: t[:, None, :]
    hk, hv = _compress_sample_t(l, ckt, cvt, page_table, lw["cw"])
    q3 = q.reshape(DB, NSA_HEADS, HD)
    idx, ocw, wk_new, wv_new = _nsa_sample_a_t(l, q3, hk, hv, r3(nk), r3(nv), r3(wkv), win_kt, win_vt,
                                               lw["cw"], past)
    g3 = sm[:, :3 * NSA_HEADS].reshape(DB, NSA_HEADS, 3)
    on = _nsa_sample_b_t(l, ckt, cvt, page_table, idx[:, :HKV, :N_SEL], q3, ocw, r3(nk), r3(nv), g3, past)
    om, c_new, n_new, m_new = _mlstm_sample(l, r3(mqkv), r3(sm), r3(mo), lw["gate_b"], st_c, st_n,
                                            st_m.reshape(st_m.shape[0], DB, 1, MH))
    y, up = _outffn_sample(l, x, on.reshape(DB, NSA_W), om[:, 0], op, gt1, sh2, sc2, gt2,
                           lw["g2"], lw["fg"], lw["w_out"], lw["w_up"], lw["conv_w"],
                           lw["conv_b"], lw["w_dn"], st_ffn, final)
    states = (nk.reshape(DB, 1, 2, HKV, HD), nv.reshape(DB, 1, 2, HKV, HD),
              jnp.transpose(wk_new, (0, 3, 1, 2)), jnp.transpose(wv_new, (0, 3, 1, 2)),
              c_new, n_new, m_new[:, 0, :MH],
              jnp.concatenate([st_pool[l][:, 1:], pu[:, None, :]], axis=1),
              jnp.concatenate([st_ffn[l][:, 1:], up[:, None, :]], axis=1))
    return y, states


def kernel(x_prompt, x_sample, cache_k, cache_v, cache_win_k, cache_win_v, state_mlstm_C, state_mlstm_n,
           state_mlstm_m, state_pool, state_ffn_conv, page_table, c_prompt, c_sample, norm1_g, norm2_g, ada_w,
           ada_b, w_in, nsa_cmp_pos, nsa_cmp_w, nsa_cmp_b, mlstm_gate_b, pool_w, pool_scale, w_out, ffn_w_up,
           ffn_conv_w, ffn_conv_b, ffn_w_down, final_g):
    B = x_prompt.shape[0]
    DB, DS, D = x_sample.shape
    assert DS == 1
    depth = w_in.shape[0]
    w = dict(norm1_g=norm1_g, norm2_g=norm2_g, w_in=w_in, nsa_cmp_pos=nsa_cmp_pos, nsa_cmp_w=nsa_cmp_w,
             nsa_cmp_b=nsa_cmp_b, mlstm_gate_b=mlstm_gate_b, pool_w=pool_w, pool_scale=pool_scale, w_out=w_out,
             ffn_w_up=ffn_w_up, ffn_conv_w=ffn_conv_w, ffn_conv_b=ffn_conv_b, ffn_w_down=ffn_w_down,
             final_g=final_g)
    mod = _ada_mod(jnp.concatenate([c_prompt, c_sample], axis=0), ada_w, ada_b)
    page = cache_k.shape[2]
    ckt = jnp.transpose(cache_k, (0, 1, 3, 4, 5, 2))
    cvt = jnp.transpose(cache_v, (0, 1, 3, 4, 5, 2))
    caches = (ckt, cvt, page_table, jnp.transpose(cache_win_k, (0, 1, 3, 4, 2)),
              jnp.transpose(cache_win_v, (0, 1, 3, 4, 2)),
              state_mlstm_C, state_mlstm_n, state_mlstm_m, state_pool, state_ffn_conv)
    xp, xs = x_prompt, x_sample[:, 0, :]
    acc_p = [[] for _ in range(9)]
    acc_s = [[] for _ in range(9)]
    for l in range(depth):
        lw = _prep_layer(l, w, page)
        final = l == depth - 1
        xp, st_p = _prompt_layer(xp, mod[l, :B][:, None, :], lw, final)
        xs, st_s = _sample_layer(l, xs, mod[l, B:], lw, caches, final)
        for a, v in zip(acc_p, st_p):
            a.append(v)
        for a, v in zip(acc_s, st_s):
            a.append(v)
    sp = [jnp.stack(a, axis=0) for a in acc_p]
    ss = [jnp.stack(a, axis=0) for a in acc_s]
    return (xp, xs[:, None, :], *sp, *ss)
```

```python
import functools

import numpy as np
import jax
import jax.numpy as jnp
from jax import lax
from jax.experimental import pallas as pl
from jax.experimental.pallas import tpu as pltpu

F32 = jnp.float32
BF16 = jnp.bfloat16

HD = 64
NSA_HEADS = 8
HKV = 2
GRP = NSA_HEADS // HKV
MH = 4
NSA_W = NSA_HEADS * HD
KV_W = HKV * HD
MLSTM_W = MH * HD
POOL_W = 4 * HD
CMP_STRIDE = 16
CMP_LEN = 32
SLC_BLOCK = 64
SLC_RATIO = SLC_BLOCK // CMP_STRIDE
N_SEL = 16
WINDOW = 512
POOL_WINDOWS = (2, 4, 8, 16)
POOL_STATE = 15
FFN_CONV = 3
RMS_EPS = 1e-6
NEG = -1e30
FORCE = 1e6
Q_SCALE = HD ** -0.5

LANES = 128
SUBLANES = 8
VMEM_LIMIT = 56 * 1024 * 1024

C_NK = 0
C_NV = C_NK + 256
C_KS = C_NV + 256
C_VS = C_KS + 256
C_VW = C_VS + 256
C_WKV = C_VW + 256
C_MQKV = C_WKV + 256
C_MO = C_MQKV + 3 * MLSTM_W
C_PU = C_MO + MLSTM_W
C_SM = C_PU + POOL_W
C_END = C_SM + LANES
SM_IG = 3 * NSA_HEADS
SM_FG = SM_IG + MH


def _dot(a, b):
    return jnp.dot(a, b, preferred_element_type=F32)


def _dot_nt(a, b):
    return lax.dot_general(a, b, (((1,), (1,)), ((), ())), preferred_element_type=F32)


def _dot_tn(a, b):
    return lax.dot_general(a, b, (((0,), (0,)), ((), ())), preferred_element_type=F32)


def _dot_split3(x, m):
    x1 = x.astype(BF16)
    r1 = x - x1.astype(F32)
    x2 = r1.astype(BF16)
    x3 = (r1 - x2.astype(F32)).astype(BF16)
    return _dot(x1, m) + _dot(x2, m) + _dot(x3, m)


def _masked_softmax(s, mask):
    s = jnp.where(mask, s, NEG)
    e = jnp.where(mask, jnp.exp(s - jnp.max(s, axis=-1, keepdims=True)), 0.0)
    return e / jnp.maximum(jnp.sum(e, axis=-1, keepdims=True), jnp.finfo(jnp.float32).tiny)


def _norm_mod(x, g, sc, sh):
    ms = jnp.mean(x * x, axis=-1, keepdims=True)
    return (x * lax.rsqrt(ms + RMS_EPS) * g) * (1.0 + sc) + sh


def _rmsnorm(x, g):
    ms = jnp.mean(x * x, axis=-1, keepdims=True)
    return x * lax.rsqrt(ms + RMS_EPS) * g


def _log_sigmoid(x):
    return jnp.minimum(x, 0.0) - jnp.log1p(jnp.exp(-jnp.abs(x)))


def _cumsum_rows(x):
    n = x.shape[0]
    row = lax.broadcasted_iota(jnp.int32, x.shape, 0)
    sh = 1
    while sh < n:
        x = x + jnp.where(row >= sh, pltpu.roll(x, sh, 0), 0.0)
        sh *= 2
    return x


def _const_spec(shape):
    nd = len(shape)
    return pl.BlockSpec(shape, lambda *_: (0,) * nd, pipeline_mode=pl.Buffered(1))


def _params(*sem):
    return pltpu.CompilerParams(dimension_semantics=sem, vmem_limit_bytes=VMEM_LIMIT)


def _ada_kernel(c_ref, w_ref, b_ref, o_ref):
    c = c_ref[...]
    s = c * jax.nn.sigmoid(c)
    o_ref[0] = _dot(s.astype(BF16), w_ref[0].astype(BF16)) + b_ref[0]


def _ada_mod(c_all, ada_w, ada_b):
    depth, d, n = ada_w.shape
    rows = c_all.shape[0]
    tn = 1536
    return pl.pallas_call(
        _ada_kernel,
        grid=(depth, n // tn),
        in_specs=[pl.BlockSpec((rows, d), lambda l, j: (0, 0)),
                  pl.BlockSpec((1, d, tn), lambda l, j: (l, 0, j)),
                  pl.BlockSpec((1, 1, tn), lambda l, j: (l, 0, j))],
        out_specs=pl.BlockSpec((1, rows, tn), lambda l, j: (l, 0, j)),
        out_shape=jax.ShapeDtypeStruct((depth, rows, n), F32),
        compiler_params=_params("arbitrary", "arbitrary"),
        name="ada_mod",
    )(c_all, ada_w, ada_b.reshape(depth, 1, n))


def _prep_w_in(w):
    d = w.shape[0]
    o = 0
    o += NSA_W
    nkv = w[:, o:o + 6 * KV_W]; o += 6 * KV_W
    ng = w[:, o:o + 3 * NSA_HEADS]; o += 3 * NSA_HEADS
    mqkv = w[:, o:o + 3 * MLSTM_W]; o += 3 * MLSTM_W
    mif = w[:, o:o + 2 * MH]; o += 2 * MH
    mo = w[:, o:o + MLSTM_W]; o += MLSTM_W
    pu = w[:, o:o + POOL_W]
    k_cmp, v_cmp, k_slc, v_slc, k_win, v_win = [nkv[:, KV_W * i:KV_W * (i + 1)] for i in range(6)]
    z = jnp.zeros((d, HD), w.dtype)
    h0 = lambda t: t[:, :HD]
    h1 = lambda t: t[:, HD:]
    cols = [
        k_cmp, k_slc, v_cmp, v_slc,
        h0(k_slc), z, z, h1(k_slc),
        h0(v_slc), z, h1(v_slc), z,
        h0(v_win), z, h1(v_win), z,
        k_win, v_win, mqkv, mo, pu,
        ng, mif, jnp.zeros((d, LANES - 3 * NSA_HEADS - 2 * MH), w.dtype)]
    out = jnp.concatenate(cols, axis=1).astype(BF16)
    assert out.shape[1] == C_END
    return out


def _pool_mix(pu, sums, cnt, pw_ref, pscale_ref):
    lane = lax.broadcasted_iota(jnp.int32, pu.shape, 1)
    grp = lane // HD
    mean = jnp.where(grp == 0, sums[2] / cnt[2],
                     jnp.where(grp == 1, sums[4] / cnt[4],
                               jnp.where(grp == 2, sums[8] / cnt[8], sums[16] / cnt[16])))
    d = mean - pu
    return _dot(d.astype(BF16), pw_ref[...]) * pscale_ref[...]


def _split2(x):
    hi = x.astype(BF16)
    return hi, (x - hi.astype(F32)).astype(BF16)


def _dot_split(a, b):
    return _dot(a[0], b[0]) + _dot(a[1], b[0]) + _dot(a[0], b[1])


def _inproj_prompt_kernel(x_ref, g_ref, sh_ref, sc_ref, w_ref, wqh_ref, wql_ref, pw_ref, pscale_ref,
                          q_ref, nk_ref, nv_ref, ks_ref, vs_ref, kw_ref, vw_ref, wkv_ref,
                          mqkv_ref, mo_ref, op_ref, sm_ref, ps_ref, zs_ref, *, tm):
    t = pl.program_id(1)
    hf = _norm_mod(x_ref[0], g_ref[...], sc_ref[0], sh_ref[0])
    h, h_lo = _split2(hf)

    def seg(a, b):
        return _dot(h, w_ref[:, a:b])

    q_ref[0] = _dot_split((h, h_lo), (wqh_ref[...], wql_ref[...])) * Q_SCALE
    nk_ref[0] = seg(C_NK, C_NV)
    nv_ref[0] = seg(C_NV, C_KS)
    row = t * tm + lax.broadcasted_iota(jnp.int32, (tm, 2 * LANES), 0)
    lane = lax.broadcasted_iota(jnp.int32, (tm, 2 * LANES), 1)
    is_aux = (lane >= HD) & (lane < 3 * HD)
    aux = jnp.where(lane < LANES, lane - HD, lane - LANES)
    onehot = jnp.where(is_aux & (aux == row // SLC_BLOCK), 1.0, 0.0)
    ks_ref[0] = (seg(C_KS, C_VS) + onehot).astype(BF16)
    ones_col = jnp.where(lane % LANES == HD, 1.0, 0.0)
    vs_ref[0] = (seg(C_VS, C_VW) + ones_col).astype(BF16)
    vw_ref[0] = (seg(C_VW, C_WKV) + ones_col).astype(BF16)
    wkv = seg(C_WKV, C_MQKV)
    wkv_ref[0] = wkv
    kw_ref[0] = wkv[:, :KV_W].astype(BF16)
    mqkv_ref[0] = seg(C_MQKV, C_MO)
    mo_ref[0] = seg(C_MO, C_PU)
    sm_ref[0] = seg(C_SM, C_END)

    pu = seg(C_PU, C_SM)
    halo = 2 * SUBLANES

    @pl.when(t == 0)
    def _():
        zs_ref[0:halo, :] = jnp.zeros((halo, POOL_W), F32)

    @pl.when(t > 0)
    def _():
        zs_ref[0:halo, :] = zs_ref[tm:tm + halo, :]

    zs_ref[halo:halo + tm, :] = pu
    acc = pu
    sums = {}
    for i in range(1, POOL_STATE + 1):
        acc = acc + zs_ref[pl.ds(halo - i, tm), :]
        if i + 1 in POOL_WINDOWS:
            sums[i + 1] = acc
    pos1 = (t * tm + lax.broadcasted_iota(jnp.int32, (tm, 1), 0) + 1).astype(F32)
    cnt = {w: jnp.minimum(float(w), pos1) for w in POOL_WINDOWS}
    op_ref[0] = _pool_mix(pu, sums, cnt, pw_ref, pscale_ref).astype(BF16)
    ps_ref[0] = zs_ref[tm:tm + halo, :]


def _inproj_prompt(x, g1, sh1, sc1, w_b, w_q2, pool_bd, pool_scale, tm=512):
    B, T, D = x.shape
    nT = T // tm
    assert T % tm == 0 and tm >= 2 * SUBLANES and WINDOW % tm == 0
    nwin = WINDOW // tm
    row = lambda w: pl.BlockSpec((1, tm, w), lambda b, t: (b, t, 0))
    mod = pl.BlockSpec((1, 1, D), lambda b, t: (b, 0, 0))
    outs = [
        (row(NSA_W), (B, T, NSA_W), F32),
        (row(256), (B, T, 256), F32),
        (row(256), (B, T, 256), F32),
        (row(256), (B, T, 256), BF16),
        (row(256), (B, T, 256), BF16),
        (row(KV_W), (B, T, KV_W), BF16),
        (row(256), (B, T, 256), BF16),
        (pl.BlockSpec((1, tm, 256), lambda b, t: (b, jnp.maximum(t - (nT - nwin), 0), 0)),
         (B, WINDOW, 256), F32),
        (row(3 * MLSTM_W), (B, T, 3 * MLSTM_W), F32),
        (row(MLSTM_W), (B, T, MLSTM_W), F32),
        (row(POOL_W), (B, T, POOL_W), BF16),
        (row(LANES), (B, T, LANES), F32),
        (pl.BlockSpec((1, 2 * SUBLANES, POOL_W), lambda b, t: (b, 0, 0)), (B, 2 * SUBLANES, POOL_W), F32),
    ]
    return pl.pallas_call(
        functools.partial(_inproj_prompt_kernel, tm=tm),
        grid=(B, nT),
        in_specs=[pl.BlockSpec((1, tm, D), lambda b, t: (b, t, 0)),
                  _const_spec((1, D)), mod, mod,
                  _const_spec(w_b.shape), _const_spec(w_q2[0].shape), _const_spec(w_q2[1].shape),
                  _const_spec(pool_bd.shape), _const_spec((1, POOL_W))],
        out_specs=[o[0] for o in outs],
        out_shape=[jax.ShapeDtypeStruct(o[1], o[2]) for o in outs],
        scratch_shapes=[pltpu.VMEM((2 * SUBLANES + tm, POOL_W), F32)],
        compiler_params=_params("arbitrary", "arbitrary"),
        name="inproj_prompt",
    )(x, g1, sh1, sc1, w_b, w_q2[0], w_q2[1], pool_bd, pool_scale)


def _compress_prompt_kernel(k_ref, v_ref, wpk_ref, wpv_ref, wk_ref, wkl_ref, wv_ref, bk_ref, bv_ref,
                            kc_ref, vc_ref, *, nseg):
    def pooled(src_ref, wp_ref):
        a = jnp.zeros((nseg, KV_W), F32)
        b = jnp.zeros((nseg, KV_W), F32)
        for j in range(CMP_STRIDE):
            xj = src_ref[0, pl.ds(j, nseg, stride=CMP_STRIDE), :]
            a = a + xj * wp_ref[j:j + 1, :]
            b = b + xj * wp_ref[CMP_STRIDE + j:CMP_STRIDE + j + 1, :]
        return a + pltpu.roll(b, nseg - 1, 0)

    kc_ref[0] = _dot_split(_split2(pooled(k_ref, wpk_ref)), (wk_ref[...], wkl_ref[...])) + bk_ref[...]
    vc_ref[0] = (_dot(pooled(v_ref, wpv_ref).astype(BF16), wv_ref[...]) + bv_ref[...]).astype(BF16)


def _compress_prompt(nk, nv, cw):
    B, T, _ = nk.shape
    nseg = T // CMP_STRIDE
    src = pl.BlockSpec((1, T, KV_W), lambda b: (b, 0, 0))
    return pl.pallas_call(
        functools.partial(_compress_prompt_kernel, nseg=nseg),
        grid=(B,),
        in_specs=[src, src, _const_spec((CMP_LEN, KV_W)), _const_spec((CMP_LEN, KV_W)),
                  _const_spec((KV_W, KV_W)), _const_spec((KV_W, KV_W)), _const_spec((KV_W, 2 * KV_W)),
                  _const_spec((1, KV_W)), _const_spec((1, 2 * KV_W))],
        out_specs=[pl.BlockSpec((1, nseg, KV_W), lambda b: (b, 0, 0)),
                   pl.BlockSpec((1, nseg, 2 * KV_W), lambda b: (b, 0, 0))],
        out_shape=[jax.ShapeDtypeStruct((B, nseg, KV_W), F32),
                   jax.ShapeDtypeStruct((B, nseg, 2 * KV_W), BF16)],
        compiler_params=_params("arbitrary"),
        name="compress_prompt",
    )(nk, nv, cw["wpk"], cw["wpv"], cw["wk"], cw["wk_lo"], cw["wv_dup"], cw["bk"], cw["bv_dup"])


def _prep_compress(cmp_pos, cmp_w, cmp_b, page):
    tile2 = lambda t: jnp.concatenate([t, t], axis=-1)
    z = jnp.zeros((HD, HD), F32)
    wk, wv = cmp_w[0], cmp_w[1]
    wk_bd = jnp.block([[wk, z], [z, wk]])
    wv_dup = jnp.block([[wv, z, z, z], [z, z, wv, z]])

    def pos_t(p):
        reps = page // CMP_STRIDE
        return jnp.concatenate([jnp.tile(p[:CMP_STRIDE].T, (1, reps)), jnp.tile(p[CMP_STRIDE:].T, (1, reps))], axis=0)

    return dict(
        wpk=tile2(cmp_pos[0]), wpv=tile2(cmp_pos[1]),
        wk=wk_bd.astype(BF16), wk_lo=(wk_bd - wk_bd.astype(BF16).astype(F32)).astype(BF16),
        wv_dup=wv_dup.astype(BF16),
        bk=tile2(cmp_b[0])[None, :],
        bv_dup=jnp.concatenate([cmp_b[1], jnp.zeros((HD,), F32)] * 2)[None, :],
        wpk_t=pos_t(cmp_pos[0]), wpv_t=pos_t(cmp_pos[1]),
        wpk_b0=cmp_pos[0][CMP_STRIDE][:, None], wpv_b0=cmp_pos[1][CMP_STRIDE][:, None],
        wk_t=wk.T.astype(BF16), wv_t=wv.T.astype(BF16),
        bk_col=cmp_b[0][:, None], bv_col=cmp_b[1][:, None])


def _importance_matrix(nc_rows, nc_valid, ns, lane_off, width):
    m = np.zeros((nc_rows, width), np.float32)
    for j in range(ns):
        for n in range(SLC_RATIO * j - 1, SLC_RATIO * j + SLC_RATIO):
            if 0 <= n < nc_valid:
                m[n, lane_off + j] = 1.0
    return m


def _select_blocks_t(st_ref, h, ns):
    nb, nq = st_ref.shape[1], st_ref.shape[2]
    ngrp = nb // SUBLANES
    groups = [st_ref[h, SUBLANES * r:SUBLANES * (r + 1), :] for r in range(ngrp)]
    ranks = [jnp.zeros((SUBLANES, nq), F32) for _ in range(ngrp)]
    jrow = lax.broadcasted_iota(jnp.int32, (SUBLANES, nq), 0)
    for jp in range(min(ns, nb)):
        row = jnp.broadcast_to(st_ref[h, pl.ds(jp, 1), :], (SUBLANES, nq))
        for r in range(ngrp):
            ge = jnp.where(row >= groups[r], 1.0, 0.0)
            gt = jnp.where(row > groups[r], 1.0, 0.0)
            if jp < SUBLANES * r:
                inc = ge
            elif jp >= SUBLANES * (r + 1):
                inc = gt
            else:
                inc = jnp.where(jrow + SUBLANES * r > jp, ge, gt)
            ranks[r] = ranks[r] + inc
    return jnp.concatenate(
        [jnp.where((ranks[r] < N_SEL) & (groups[r] >= 0.0), 0.0, NEG) for r in range(ngrp)], axis=0)


def _nsa_prompt_kernel(q_ref, sm_ref, kc_ref, vc_ref, ks_ref, vs_ref, kw_ref, vw_ref, mt_ref,
                       o_ref, acc_ref, m_ref, qa_ref, st_ref, sa_ref, sb_ref, *, T, tk, qb):
    i = pl.program_id(1)
    ns = T // SLC_BLOCK
    nc = kc_ref.shape[1]
    rows = GRP * qb
    qpos = i * qb + lax.broadcasted_iota(jnp.int32, (qb, 1), 0)
    qpos4 = jnp.concatenate([qpos] * GRP, axis=0)
    qpos_t = i * qb + lax.broadcasted_iota(jnp.int32, (HD, qb), 1)
    jb_t = lax.broadcasted_iota(jnp.int32, (HD, qb), 0)
    sig = jax.nn.sigmoid(sm_ref[0])
    lane = lax.broadcasted_iota(jnp.int32, (qb, LANES), 1)
    wk = WINDOW + qb
    n_idx = lax.broadcasted_iota(jnp.int32, (1, nc), 1)
    cmp_mask = (CMP_STRIDE * n_idx + CMP_LEN - 1) <= qpos4
    wstart = pl.multiple_of(jnp.maximum(i - WINDOW // qb, 0) * qb, qb)
    dlt = qpos - (wstart + lax.broadcasted_iota(jnp.int32, (1, wk), 1))
    wbias = jnp.where((dlt >= 0) & (dlt < WINDOW), 0.0, NEG)
    cur_t = qpos_t // SLC_BLOCK
    valid_t = (jb_t * SLC_BLOCK <= qpos_t) & (jb_t < ns)
    forced_t = (jb_t == 0) | (jb_t == cur_t) | (jb_t == cur_t - 1)
    zpad = jnp.zeros((LANES - HD, qb), F32)
    kc_hi, kc_lo = _split2(kc_ref[0])

    o_cs, o_ws = [], []
    for h in range(HKV):
        parts = []
        for g in range(GRP):
            hd = GRP * h + g
            src = q_ref[0, :, LANES * (hd // 2):LANES * (hd // 2 + 1)]
            if hd % 2 != h:
                src = pltpu.roll(src, HD, 1)
            parts.append(jnp.where((lane < HD) if h == 0 else (lane >= HD), src, 0.0))
        qh, qh_lo = _split2(jnp.concatenate(parts, axis=0))

        p_c = _masked_softmax(_dot_nt(qh, kc_hi) + _dot_nt(qh_lo, kc_hi) + _dot_nt(qh, kc_lo), cmp_mask)
        o_cs.append(_dot(p_c.astype(BF16), vc_ref[0, :, LANES * h:LANES * (h + 1)]))

        psum = p_c[0:qb] + p_c[qb:2 * qb] + p_c[2 * qb:3 * qb] + p_c[3 * qb:4 * qb]
        x1 = psum.astype(BF16)
        r1 = psum - x1.astype(F32)
        x2 = r1.astype(BF16)
        x3 = (r1 - x2.astype(F32)).astype(BF16)
        mt = mt_ref[...]
        imp_t = _dot_nt(mt, x1) + _dot_nt(mt, x2) + _dot_nt(mt, x3)
        score_t = jnp.where(valid_t, jnp.where(forced_t, FORCE, imp_t), -1.0)
        st_ref[h] = jnp.where(jb_t < ns, score_t, -jnp.inf)
        sel_t = _select_blocks_t(st_ref, h, ns)
        selneg = (jnp.concatenate([zpad, sel_t], axis=0) if h == 0
                  else jnp.concatenate([sel_t, zpad], axis=0)).T
        qa_ref[h] = qh + jnp.concatenate([selneg.astype(BF16)] * GRP, axis=0)

        sw = _dot_nt(qh, kw_ref[0, pl.ds(wstart, wk), :])
        sw = jnp.concatenate([sw[qb * g:qb * (g + 1)] + wbias for g in range(GRP)], axis=0)
        e = jnp.exp(sw - jnp.max(sw, axis=-1, keepdims=True))
        o_w = _dot(e.astype(BF16), vw_ref[0, pl.ds(wstart, wk), LANES * h:LANES * (h + 1)])
        o_ws.append(o_w / o_w[:, HD:HD + 1])

    m_ref[...] = jnp.full(m_ref.shape, -jnp.inf, F32)
    acc_ref[...] = jnp.zeros(acc_ref.shape, F32)

    def scores(c, s_ref):
        start = pl.multiple_of(c * tk, tk)
        for h in range(HKV):
            s_ref[h] = _dot_nt(qa_ref[h], ks_ref[0, pl.ds(start, tk), LANES * h:LANES * (h + 1)])

    def accumulate(c, s_ref, diagonal):
        start = pl.multiple_of(c * tk, tk)
        for h in range(HKV):
            sc = s_ref[h]
            if diagonal:
                tok = start + lax.broadcasted_iota(jnp.int32, (1, tk), 1)
                sc = jnp.where(tok <= qpos4, sc, NEG)
            m_prev = m_ref[h]
            m_new = jnp.maximum(m_prev, jnp.max(sc, axis=-1, keepdims=True))
            alpha = jnp.exp(m_prev - m_new)
            p = jnp.exp(sc - m_new[:, 0:1])
            v = vs_ref[0, pl.ds(start, tk), LANES * h:LANES * (h + 1)]
            acc_ref[h] = alpha * acc_ref[h] + _dot(p.astype(BF16), v)
            m_ref[h] = m_new

    n_full = (i * qb) // tk
    scores(0, sa_ref)

    def chunk_pair(cc, carry):
        c0 = 2 * cc
        scores(c0 + 1, sb_ref)
        accumulate(c0, sa_ref, False)
        scores(c0 + 2, sa_ref)
        accumulate(c0 + 1, sb_ref, False)
        return carry

    lax.fori_loop(0, n_full // 2, chunk_pair, 0)

    @pl.when(n_full % 2 == 0)
    def _():
        accumulate(n_full, sa_ref, True)

    @pl.when(n_full % 2 == 1)
    def _():
        scores(n_full, sb_ref)
        accumulate(n_full - 1, sa_ref, False)
        accumulate(n_full, sb_ref, True)

    for h in range(HKV):
        acc = acc_ref[h]
        o_s = acc / acc[:, HD:HD + 1]
        outs = []
        for g in range(GRP):
            c0 = 3 * (GRP * h + g)
            r = slice(qb * g, qb * (g + 1))
            outs.append(sig[:, c0:c0 + 1] * o_cs[h][r] + sig[:, c0 + 1:c0 + 2] * o_s[r]
                        + sig[:, c0 + 2:c0 + 3] * o_ws[h][r])
        base = GRP * HD * h
        for pair in range(GRP // 2):
            hi = pltpu.roll(outs[2 * pair + 1], HD, 1)
            o_ref[0, :, base + LANES * pair:base + LANES * (pair + 1)] = jnp.where(
                lane < HD, outs[2 * pair], hi).astype(BF16)


def _nsa_prompt(q, sm, kc, vc, ks, vs, kw, vw, qb=256, tk=512):
    B, T, _ = q.shape
    ns = T // SLC_BLOCK
    nc = kc.shape[1]
    assert ns <= HD and T % tk == 0 and T >= WINDOW + qb and WINDOW % qb == 0 and tk % qb == 0
    m2 = _importance_matrix(nc, nc - 1, ns, 0, HD).T
    full = lambda w: pl.BlockSpec((1, T, w), lambda b, i: (b, 0, 0))
    return pl.pallas_call(
        functools.partial(_nsa_prompt_kernel, T=T, tk=tk, qb=qb),
        grid=(B, T // qb),
        in_specs=[pl.BlockSpec((1, qb, NSA_W), lambda b, i: (b, i, 0)),
                  pl.BlockSpec((1, qb, LANES), lambda b, i: (b, i, 0)),
                  pl.BlockSpec((1, nc, KV_W), lambda b, i: (b, 0, 0)),
                  pl.BlockSpec((1, nc, 2 * KV_W), lambda b, i: (b, 0, 0)),
                  full(256), full(256), full(KV_W), full(256),
                  _const_spec(m2.shape)],
        out_specs=pl.BlockSpec((1, qb, NSA_W), lambda b, i: (b, i, 0)),
        out_shape=jax.ShapeDtypeStruct((B, T, NSA_W), BF16),
        scratch_shapes=[pltpu.VMEM((HKV, GRP * qb, LANES), F32)] * 2
                       + [pltpu.VMEM((HKV, GRP * qb, LANES), BF16), pltpu.VMEM((HKV, HD, qb), F32)]
                       + [pltpu.VMEM((HKV, GRP * qb, tk), F32)] * 2,
        compiler_params=_params("arbitrary", "arbitrary"),
        name="nsa_prompt",
    )(q, sm, kc, vc, ks, vs, kw, vw, jnp.asarray(m2, BF16))


def _mlstm_gates(g):
    b = pltpu.roll(_cumsum_rows(_log_sigmoid(g)), LANES - MH, 1)
    return b, g - b


def _mlstm_prompt_kernel(mqkv_ref, sm_ref, mo_ref, gb_ref, om_ref, c_out, n_out, m_out,
                         c_s, n_s, m_s, *, B, L):
    t = pl.program_id(0)

    @pl.when(t == 0)
    def _():
        c_s[...] = jnp.zeros(c_s.shape, F32)
        n_s[...] = jnp.zeros(n_s.shape, F32)
        m_s[...] = jnp.zeros(m_s.shape, F32)

    li = lax.broadcasted_iota(jnp.int32, (L, L), 0)
    si = lax.broadcasted_iota(jnp.int32, (L, L), 1)
    for b in range(B):
        b_al, r = _mlstm_gates(sm_ref[b] + gb_ref[...])
        r_t = jnp.concatenate([r, jnp.zeros((LANES - L, LANES), F32)], axis=0).T if L < LANES else r.T
        heads = []
        for hd in range(MH):
            idx = b * MH + hd
            bcol = b_al[:, SM_IG + hd:SM_IG + hd + 1]
            rcol = r[:, SM_IG + hd:SM_IG + hd + 1]
            rrow = r_t[SM_IG + hd:SM_IG + hd + 1, 0:L]
            mprev = m_s[idx][:, 0:1]
            acol = bcol + mprev
            logd = jnp.where(si <= li, bcol + rrow, NEG)
            mt = jnp.maximum(acol, jnp.max(logd, axis=-1, keepdims=True))
            dm = jnp.exp(logd - mt)
            inter = jnp.exp(acol - mt)
            q = mqkv_ref[b, :, HD * hd:HD * (hd + 1)]
            k = mqkv_ref[b, :, MLSTM_W + HD * hd:MLSTM_W + HD * (hd + 1)] * Q_SCALE
            v = mqkv_ref[b, :, 2 * MLSTM_W + HD * hd:2 * MLSTM_W + HD * (hd + 1)]
            qb, kb = q.astype(BF16), k.astype(BF16)
            qk = _dot_nt(qb, kb) * dm
            c_prev = c_s[idx]
            n_prev = n_s[idx]
            num = _dot(qk.astype(BF16), v.astype(BF16)) + inter * _dot(qb, c_prev.astype(BF16))
            den = jnp.sum(qk, axis=-1, keepdims=True) + inter * jnp.sum(q * n_prev, axis=-1, keepdims=True)
            hout = num / jnp.maximum(jnp.abs(den), jnp.exp(-mt))
            heads.append(jax.nn.sigmoid(mo_ref[b, :, HD * hd:HD * (hd + 1)]) * hout)
            m_last = mt[L - 1:L, :]
            b_last = bcol[L - 1:L, :]
            wcol = jnp.exp(b_last + rcol - m_last)
            decay = jnp.exp(b_last + mprev - m_last)
            c_s[idx] = decay * c_prev + _dot_tn(kb, (wcol * v).astype(BF16))
            n_s[idx] = decay * n_prev + jnp.sum(wcol * k, axis=0, keepdims=True)
            m_s[idx] = jnp.broadcast_to(m_last, (1, LANES))
        om_ref[b] = jnp.concatenate(heads, axis=-1).astype(BF16)

    @pl.when(t == pl.num_programs(0) - 1)
    def _():
        c_out[...] = c_s[...]
        n_out[...] = n_s[...]
        m_out[...] = m_s[...]


def _mlstm_prompt(mqkv, sm, mo, gate_b_tile, L=256):
    B, T, _ = mqkv.shape
    assert T % L == 0 and (L <= LANES or L % LANES == 0)
    blk = lambda w: pl.BlockSpec((B, L, w), lambda t: (0, t, 0))
    st = lambda shape: pl.BlockSpec(shape, lambda t: (0,) * len(shape))
    shapes = [(B * MH, HD, HD), (B * MH, 1, HD), (B * MH, 1, LANES)]
    return pl.pallas_call(
        functools.partial(_mlstm_prompt_kernel, B=B, L=L),
        grid=(T // L,),
        in_specs=[blk(3 * MLSTM_W), blk(LANES), blk(MLSTM_W), _const_spec((1, LANES))],
        out_specs=[blk(MLSTM_W)] + [st(s) for s in shapes],
        out_shape=[jax.ShapeDtypeStruct((B, T, MLSTM_W), BF16)] + [jax.ShapeDtypeStruct(s, F32) for s in shapes],
        scratch_shapes=[pltpu.VMEM(s, F32) for s in shapes],
        compiler_params=_params("arbitrary"),
        name="mlstm_prompt",
    )(mqkv, sm, mo, gate_b_tile)


FF_CHUNK = 1408


def _outffn_prompt_kernel(x_ref, on_ref, om_ref, op_ref, gt1_ref, sh2_ref, sc2_ref, gt2_ref,
                          g2_ref, fg_ref, wout_ref, wup_ref, cw_ref, cb_ref, wdn_ref,
                          y_ref, fs_ref, prev_ref, es_ref, *, tm, d_ff, final):
    t = pl.program_id(1)
    mix = (_dot(on_ref[0], wout_ref[0:NSA_W, :])
           + _dot(om_ref[0], wout_ref[NSA_W:NSA_W + MLSTM_W, :])
           + _dot(op_ref[0], wout_ref[NSA_W + MLSTM_W:, :]))
    x1 = x_ref[0] + gt1_ref[0] * mix
    h2 = _norm_mod(x1, g2_ref[...], sc2_ref[0], sh2_ref[0]).astype(BF16)

    @pl.when(t == 0)
    def _():
        prev_ref[...] = jnp.zeros(prev_ref.shape, F32)

    f = jnp.zeros(x1.shape, F32)
    w = FF_CHUNK
    for c in range(d_ff // w):
        for half, off in ((0, c * w), (1, d_ff + c * w)):
            es_ref[0:SUBLANES, half * w:(half + 1) * w] = prev_ref[:, off:off + w]
            es_ref[SUBLANES:SUBLANES + tm, half * w:(half + 1) * w] = _dot(h2, wup_ref[:, off:off + w])
            prev_ref[:, off:off + w] = es_ref[tm:tm + SUBLANES, half * w:(half + 1) * w]
        ys = []
        for half, off in ((0, c * w), (1, d_ff + c * w)):
            y = cb_ref[:, off:off + w]
            for j in range(FFN_CONV):
                y = y + cw_ref[j:j + 1, off:off + w] * es_ref[pl.ds(SUBLANES - (FFN_CONV - 1) + j, tm), half * w:(half + 1) * w]
            ys.append(y)
        act = ys[0] * jax.nn.sigmoid(ys[0]) * ys[1]
        f = f + _dot(act.astype(BF16), wdn_ref[c * w:(c + 1) * w, :])
    x2 = x1 + gt2_ref[0] * f
    y_ref[0] = _rmsnorm(x2, fg_ref[...]) if final else x2
    fs_ref[0] = prev_ref[...]


def _outffn_prompt(x, on, om, op, gt1, sh2, sc2, gt2, g2, fg, w_out, w_up, conv_w, conv_b, w_dn, final, tm=512):
    B, T, D = x.shape
    d_ff = w_dn.shape[0]
    assert T % tm == 0 and d_ff % FF_CHUNK == 0
    row = lambda w: pl.BlockSpec((1, tm, w), lambda b, t: (b, t, 0))
    mod = pl.BlockSpec((1, 1, D), lambda b, t: (b, 0, 0))
    return pl.pallas_call(
        functools.partial(_outffn_prompt_kernel, tm=tm, d_ff=d_ff, final=final),
        grid=(B, T // tm),
        in_specs=[row(D), row(NSA_W), row(MLSTM_W), row(POOL_W), mod, mod, mod, mod,
                  _const_spec((1, D)), _const_spec((1, D)), _const_spec(w_out.shape), _const_spec(w_up.shape),
                  _const_spec(conv_w.shape), _const_spec(conv_b.shape), _const_spec(w_dn.shape)],
        out_specs=[row(D), pl.BlockSpec((1, SUBLANES, 2 * d_ff), lambda b, t: (b, 0, 0))],
        out_shape=[jax.ShapeDtypeStruct((B, T, D), F32), jax.ShapeDtypeStruct((B, SUBLANES, 2 * d_ff), F32)],
        scratch_shapes=[pltpu.VMEM((SUBLANES, 2 * d_ff), F32), pltpu.VMEM((SUBLANES + tm, 2 * FF_CHUNK), F32)],
        compiler_params=_params("arbitrary", "arbitrary"),
        name="outffn_prompt",
    )(x, on, om, op, gt1, sh2, sc2, gt2, g2, fg, w_out, w_up, conv_w, conv_b, w_dn)


def _prep_layer(l, w, page):
    gate_b = jnp.zeros((1, LANES), F32).at[0, SM_IG:SM_IG + 2 * MH].set(w["mlstm_gate_b"][l])
    pw = w["pool_w"][l]
    z = jnp.zeros((HD, HD), F32)
    pool_bd = jnp.block([[pw[i] if i == j else z for j in range(4)] for i in range(4)]).astype(BF16)
    wq = w["w_in"][l][:, :NSA_W]
    wq_hi = wq.astype(BF16)
    return dict(
        w_q=wq_hi, w_q2=(wq_hi, (wq - wq_hi.astype(F32)).astype(BF16)),
        g1=w["norm1_g"][l][None], g2=w["norm2_g"][l][None],
        w_b=_prep_w_in(w["w_in"][l]), pool_bd=pool_bd, pool_scale=w["pool_scale"][l][None],
        cw=_prep_compress(w["nsa_cmp_pos"][l], w["nsa_cmp_w"][l], w["nsa_cmp_b"][l], page),
        gate_b=gate_b, w_out=w["w_out"][l].astype(BF16), w_up=w["ffn_w_up"][l].astype(BF16),
        conv_w=w["ffn_conv_w"][l], conv_b=w["ffn_conv_b"][l][None], w_dn=w["ffn_w_down"][l].astype(BF16),
        fg=w["final_g"][None])


def _prompt_layer(x, mod, lw, final):
    B, T, D = x.shape
    sh1, sc1, gt1, sh2, sc2, gt2 = [mod[:, :, D * i:D * (i + 1)] for i in range(6)]
    (q, nk, nv, ks, vs, kw, vw, wkv, mqkv, mo, op, sm, ps) = _inproj_prompt(
        x, lw["g1"], sh1, sc1, lw["w_b"], lw["w_q2"], lw["pool_bd"], lw["pool_scale"])
    kc, vc = _compress_prompt(nk, nv, lw["cw"])
    on = _nsa_prompt(q, sm, kc, vc, ks, vs, kw, vw)
    om, c_st, n_st, m_st = _mlstm_prompt(mqkv, sm, mo, lw["gate_b"])
    y, fs = _outffn_prompt(x, on, om, op, gt1, sh2, sc2, gt2, lw["g2"], lw["fg"], lw["w_out"],
                           lw["w_up"], lw["conv_w"], lw["conv_b"], lw["w_dn"], final)
    wb = wkv.shape[1]
    states = (nk.reshape(B, T, 2, HKV, HD), nv.reshape(B, T, 2, HKV, HD),
              wkv[:, :, :KV_W].reshape(B, wb, HKV, HD), wkv[:, :, KV_W:].reshape(B, wb, HKV, HD),
              c_st.reshape(B, MH, HD, HD), n_st.reshape(B, MH, HD), m_st[:, 0, 0].reshape(B, MH),
              ps[:, 2 * SUBLANES - POOL_STATE:], fs[:, SUBLANES - (FFN_CONV - 1):])
    return y, states


def _inproj_sample_kernel(x_ref, g_ref, sh_ref, sc_ref, w_ref, wq_ref, pw_ref, pscale_ref, prev_ref,
                          q_ref, nk_ref, nv_ref, wkv_ref, mqkv_ref, mo_ref, op_ref, sm_ref, pu_ref, *, pos0):
    h = _norm_mod(x_ref[...], g_ref[...], sc_ref[...], sh_ref[...]).astype(BF16)

    def seg(a, b):
        return _dot(h, w_ref[:, a:b])

    q_ref[...] = _dot(h, wq_ref[...]) * Q_SCALE
    nk_ref[...] = seg(C_NK, C_NV)
    nv_ref[...] = seg(C_NV, C_KS)
    wkv_ref[...] = seg(C_WKV, C_MQKV)
    mqkv_ref[...] = seg(C_MQKV, C_MO)
    mo_ref[...] = seg(C_MO, C_PU)
    sm_ref[...] = seg(C_SM, C_END)
    pu = seg(C_PU, C_SM)
    pu_ref[...] = pu
    acc = pu
    sums = {}
    for i in range(1, POOL_STATE + 1):
        acc = acc + prev_ref[:, POOL_STATE - i, :]
        if i + 1 in POOL_WINDOWS:
            sums[i + 1] = acc
    cnt = {w: float(min(w, pos0 + 1)) for w in POOL_WINDOWS}
    op_ref[...] = _pool_mix(pu, sums, cnt, pw_ref, pscale_ref)


def _inproj_sample(l, x, g1, sh1, sc1, w_b, w_q, pool_bd, pool_scale, state_pool, pos0):
    DB, D = x.shape
    full = lambda shape: pl.BlockSpec(shape, lambda i: (0,) * len(shape))
    widths = [NSA_W, 256, 256, 256, 3 * MLSTM_W, MLSTM_W, POOL_W, LANES, POOL_W]
    return pl.pallas_call(
        functools.partial(_inproj_sample_kernel, pos0=pos0),
        grid=(1,),
        in_specs=[full((DB, D)), full((1, D)), full((DB, D)), full((DB, D)), full(w_b.shape),
                  full(w_q.shape), full(pool_bd.shape), full((1, POOL_W)),
                  pl.BlockSpec((None, DB, POOL_STATE, POOL_W), lambda i: (l, 0, 0, 0))],
        out_specs=[full((DB, w)) for w in widths],
        out_shape=[jax.ShapeDtypeStruct((DB, w), F32) for w in widths],
        compiler_params=_params("arbitrary"),
        name="inproj_sample",
    )(x, g1, sh1, sc1, w_b, w_q, pool_bd, pool_scale, state_pool)


PAGES_PER_STEP = 32


def _row_to_col(rowv):
    n = rowv.shape[1]
    eye = lax.broadcasted_iota(jnp.int32, (n, n), 0) == lax.broadcasted_iota(jnp.int32, (n, n), 1)
    return jnp.sum(jnp.where(eye, jnp.broadcast_to(rowv, (n, n)), 0.0), axis=1, keepdims=True)


def _compress_sample_t_kernel(pt_ref, *refs, npg):
    del pt_ref
    k_refs, v_refs = refs[:npg], refs[npg:2 * npg]
    wk_ref, wv_ref, sg_ref, ok_ref, ov_ref = refs[2 * npg:]
    for src, w_ref, o_ref in ((k_refs, wk_ref, ok_ref), (v_refs, wv_ref, ov_ref)):
        wa = w_ref[0:HD, :]
        wb = w_ref[HD:2 * HD, :]
        for h in range(HKV):
            prods = []
            for i in range(npg):
                x = src[i][h]
                prods.append(jnp.concatenate([x * wa, x * wb], axis=0).astype(BF16))
            o_ref[0, h] = _dot(jnp.concatenate(prods, axis=1), sg_ref[...])


def _compress_sample_t(l, cache_kt, cache_vt, page_table, cw):
    DB, n_pages = page_table.shape
    page = cache_kt.shape[-1]
    npg = min(PAGES_PER_STEP, n_pages)
    assert n_pages % npg == 0 and page % CMP_STRIDE == 0
    spp = page // CMP_STRIDE
    nseg = n_pages * spp
    assert (npg * spp) % LANES == 0 or npg * spp == nseg
    sg = np.zeros((npg * page, npg * spp), np.float32)
    sg[np.arange(npg * page), np.arange(npg * page) // CMP_STRIDE] = 1.0

    def page_spec(i):
        return pl.BlockSpec((None, None, None, HKV, HD, page),
                            lambda b, c, pt: (l, pt[b, c * npg + i], 0, 0, 0, 0))

    out = pl.BlockSpec((1, HKV, 2 * HD, npg * spp), lambda b, c, pt: (b, 0, 0, c))
    cst = lambda shape: pl.BlockSpec(shape, lambda b, c, pt: (0,) * len(shape))
    grid_spec = pltpu.PrefetchScalarGridSpec(
        num_scalar_prefetch=1, grid=(DB, n_pages // npg),
        in_specs=[page_spec(i) for i in range(npg)] * 2
                 + [cst((2 * HD, page)), cst((2 * HD, page)), cst(sg.shape)],
        out_specs=[out] * 2)
    return pl.pallas_call(
        functools.partial(_compress_sample_t_kernel, npg=npg),
        grid_spec=grid_spec,
        out_shape=[jax.ShapeDtypeStruct((DB, HKV, 2 * HD, nseg), F32)] * 2,
        compiler_params=_params("arbitrary", "arbitrary"),
        name="compress_sample",
    )(page_table, *([cache_kt] * npg), *([cache_vt] * npg), cw["wpk_t"], cw["wpv_t"], jnp.asarray(sg, BF16))


def _nsa_sample_a_t_kernel(q_ref, hk_ref, hv_ref, nk_ref, nv_ref, wkv_ref, wink_ref, winv_ref,
                           wpk_ref, wpv_ref, wk_ref, wv_ref, bk_ref, bv_ref, m_ref,
                           idx_ref, ocw_ref, wko_ref, wvo_ref, *, past, ns):
    ncs = hk_ref.shape[3]
    wb = wink_ref.shape[3]
    q8 = q_ref[0].astype(BF16)
    row = lax.broadcasted_iota(jnp.int32, (NSA_HEADS, 1), 0)
    lane_c = lax.broadcasted_iota(jnp.int32, (HD, ncs), 1)
    lane_w = lax.broadcasted_iota(jnp.int32, (HD, wb), 1)
    n_idx = lax.broadcasted_iota(jnp.int32, (1, ncs), 1)

    def per_head(fn):
        a, b = fn(0), fn(1)
        return jnp.where(row // GRP == 0, a, b)

    def comp_t(h_ref, new_row, wp_ref, w_ref, bias_ref, h):
        at = h_ref[0, h, 0:HD, :]
        bt = h_ref[0, h, HD:2 * HD, :]
        b_new = _row_to_col(new_row[:, HD * h:HD * (h + 1)]) * wp_ref[...]
        pooled = at + jnp.where(lane_c == ncs - 1, b_new, pltpu.roll(bt, ncs - 1, 1))
        return (_dot(w_ref[...], pooled.astype(BF16)) + bias_ref[...]).astype(BF16)

    kct = [comp_t(hk_ref, nk_ref[0], wpk_ref, wk_ref, bk_ref, h) for h in range(HKV)]
    vct = [comp_t(hv_ref, nv_ref[0], wpv_ref, wv_ref, bv_ref, h) for h in range(HKV)]
    s = per_head(lambda h: _dot(q8, kct[h]))
    p = _masked_softmax(s, (CMP_STRIDE * n_idx + CMP_LEN - 1) <= past)
    pb = p.astype(BF16)
    o_c = per_head(lambda h: _dot_nt(pb, vct[h]))

    rowi = lax.broadcasted_iota(jnp.int32, p.shape, 0)
    p0 = jnp.sum(p[0:GRP], axis=0, keepdims=True)
    p1 = jnp.sum(p[GRP:2 * GRP], axis=0, keepdims=True)
    imp = _dot_split3(jnp.where(rowi == 0, p0, jnp.where(rowi == 1, p1, 0.0)), m_ref[...])
    lane = lax.broadcasted_iota(jnp.int32, imp.shape, 1)
    cur = past // SLC_BLOCK
    valid = lane * SLC_BLOCK <= past
    forced = (lane == 0) | (lane == cur) | (lane == cur - 1)
    score = jnp.where(valid, jnp.where(forced, FORCE, imp), -1.0)
    score = jnp.where(lane < ns, score, -jnp.inf)
    lane_o = lax.broadcasted_iota(jnp.int32, (SUBLANES, LANES), 1)
    idx = jnp.full((SUBLANES, LANES), -1, jnp.int32)
    for r in range(min(N_SEL, ns)):
        mx = jnp.max(score, axis=-1, keepdims=True)
        first = jnp.min(jnp.where(score == mx, lane, 1 << 20), axis=-1, keepdims=True)
        idx = jnp.where(lane_o == r, jnp.where(mx >= 0.0, first, -1), idx)
        score = jnp.where(lane == first, -jnp.inf, score)
    idx_ref[0] = idx

    sel_head = lambda t: jnp.where(row // GRP == 0, t[:, :HD], t[:, HD:])
    k_new = wkv_ref[0][:, :KV_W]
    v_new = wkv_ref[0][:, KV_W:]
    s_w = per_head(lambda h: _dot(q8, wink_ref[0, h].astype(BF16)))
    dlt = wb - lax.broadcasted_iota(jnp.int32, (1, wb), 1)
    mask = (dlt < WINDOW) & (past - dlt >= 0)
    s_new = jnp.sum(q8.astype(F32) * sel_head(k_new).astype(BF16).astype(F32), axis=-1, keepdims=True)
    s_w = jnp.where(mask, s_w, NEG)
    mx = jnp.maximum(jnp.max(s_w, axis=-1, keepdims=True), s_new)
    e = jnp.where(mask, jnp.exp(s_w - mx), 0.0)
    e_new = jnp.exp(s_new - mx)
    den = jnp.sum(e, axis=-1, keepdims=True) + e_new
    eb = e.astype(BF16)
    o_w = per_head(lambda h: _dot_nt(eb, winv_ref[0, h].astype(BF16)))
    o_w = (o_w + e_new * sel_head(v_new).astype(BF16).astype(F32)) / den
    ocw_ref[0] = jnp.concatenate([o_c, o_w], axis=-1)
    for h in range(HKV):
        for src, new, dst in ((wink_ref, k_new, wko_ref), (winv_ref, v_new, wvo_ref)):
            col = _row_to_col(new[:, HD * h:HD * (h + 1)])
            dst[0, h] = jnp.where(lane_w == wb - 1, col, pltpu.roll(src[0, h], wb - 1, 1))


NSA_A_SEQS = 4
N_SEQ_IN = 8
N_SEQ_OUT = 4


def _nsa_sample_a_t_multi(*refs, nsq, past, ns):
    for sq in range(nsq):
        one = lambda r: r.at[pl.ds(sq, 1)]
        _nsa_sample_a_t_kernel(*[one(r) for r in refs[:N_SEQ_IN]], *refs[N_SEQ_IN:-N_SEQ_OUT],
                               *[one(r) for r in refs[-N_SEQ_OUT:]], past=past, ns=ns)


def _nsa_sample_a_t(l, q3, hk, hv, nk3, nv3, wkv3, win_kt, win_vt, cw, past):
    DB = q3.shape[0]
    ncs = hk.shape[3]
    wb = win_kt.shape[-1]
    ns = -(-(past + 1) // SLC_BLOCK)
    nsl = -(-ns // LANES) * LANES
    m = _importance_matrix(ncs, ncs, ns, 0, nsl)
    nsq = NSA_A_SEQS if DB % NSA_A_SEQS == 0 else 1
    per = lambda shape: pl.BlockSpec((nsq,) + shape, lambda b: (b,) + (0,) * len(shape))
    win = pl.BlockSpec((None, nsq, HKV, HD, wb), lambda b: (l, b, 0, 0, 0))
    shapes = [(DB, SUBLANES, LANES), (DB, NSA_HEADS, 2 * HD), (DB, HKV, HD, wb), (DB, HKV, HD, wb)]
    dts = [jnp.int32, F32, F32, F32]
    return pl.pallas_call(
        functools.partial(_nsa_sample_a_t_multi, nsq=nsq, past=past, ns=ns),
        grid=(DB // nsq,),
        in_specs=[per((NSA_HEADS, HD))] + [per((HKV, 2 * HD, ncs))] * 2 + [per((1, 256))] * 3 + [win, win]
                 + [_const_spec((HD, 1))] * 2 + [_const_spec((HD, HD))] * 2
                 + [_const_spec((HD, 1))] * 2 + [_const_spec(m.shape)],
        out_specs=[per(s[1:]) for s in shapes],
        out_shape=[jax.ShapeDtypeStruct(s, d) for s, d in zip(shapes, dts)],
        compiler_params=_params("arbitrary"),
        name="nsa_sample_a",
    )(q3, hk, hv, nk3, nv3, wkv3, win_kt, win_vt, cw["wpk_b0"], cw["wpv_b0"], cw["wk_t"], cw["wv_t"],
      cw["bk_col"], cw["bv_col"], jnp.asarray(m, BF16))


def _nsa_sample_b_t_kernel(pt_ref, ix_ref, *refs, past, nbp, nsel, page):
    del pt_ref
    k_refs, v_refs = refs[:HKV * nsel], refs[HKV * nsel:2 * HKV * nsel]
    q_ref, ocw_ref, nk_ref, nv_ref, g_ref, o_ref = refs[2 * HKV * nsel:]
    b = pl.program_id(0)
    bpp = page // SLC_BLOCK
    q8 = q_ref[0].astype(BF16)
    row = lax.broadcasted_iota(jnp.int32, (NSA_HEADS, 1), 0)
    sel_head = lambda t: jnp.where(row // GRP == 0, t[:, :HD], t[:, HD:])
    k_new = sel_head(nk_ref[0][:, KV_W:]).astype(BF16).astype(F32)
    v_new = sel_head(nv_ref[0][:, KV_W:]).astype(BF16).astype(F32)
    s_new = jnp.sum(q8.astype(F32) * k_new, axis=-1, keepdims=True)
    nk = nsel * page
    lane = lax.broadcasted_iota(jnp.int32, (1, nk), 1)
    o_s = jnp.zeros((NSA_HEADS, HD), F32)
    for h in range(HKV):
        kt = jnp.concatenate([k_refs[h * nsel + r][...] for r in range(nsel)], axis=1).astype(BF16)
        vt = jnp.concatenate([v_refs[h * nsel + r][...] for r in range(nsel)], axis=1).astype(BF16)
        jv = jnp.full((1, nk), -1, jnp.int32)
        n_new = jnp.int32(0)
        for r in range(nsel):
            j = ix_ref[b, h, r]
            jv = jnp.where(lane // page == r, j, jv)
            n_new = n_new + (j == nbp).astype(jnp.int32)
        t_in = lane % page
        tok = (jv // bpp) * page + t_in
        mask = (jv >= 0) & (jv < nbp) & (t_in // SLC_BLOCK == jv % bpp) & (tok <= past)
        has_new = n_new > 0
        s = jnp.where(mask, _dot(q8, kt), NEG)
        sn = jnp.where(has_new, s_new, NEG)
        mx = jnp.maximum(jnp.max(s, axis=-1, keepdims=True), sn)
        e = jnp.where(mask, jnp.exp(s - mx), 0.0)
        e_new = jnp.where(has_new, jnp.exp(sn - mx), 0.0)
        den = jnp.maximum(jnp.sum(e, axis=-1, keepdims=True) + e_new, jnp.finfo(jnp.float32).tiny)
        o_h = (_dot_nt(e.astype(BF16), vt) + e_new * v_new) / den
        o_s = jnp.where(row // GRP == h, o_h, o_s)
    g = jax.nn.sigmoid(g_ref[0])
    ocw = ocw_ref[0]
    o_ref[0] = g[:, 0:1] * ocw[:, :HD] + g[:, 1:2] * o_s + g[:, 2:3] * ocw[:, HD:]


def _nsa_sample_b_t(l, cache_kt, cache_vt, page_table, idx, q3, ocw, nk3, nv3, g3, past):
    DB = q3.shape[0]
    page = cache_kt.shape[-1]
    bpp = page // SLC_BLOCK
    nbp = past // SLC_BLOCK
    nsel = idx.shape[2]

    def tile_spec(h, r):
        def imap(b, pt, ix):
            j = jnp.clip(ix[b, h, r], 0, nbp - 1)
            return (l, pt[b, j // bpp], 1, h, 0, 0)
        return pl.BlockSpec((None, None, None, None, HD, page), imap)

    per = lambda shape: pl.BlockSpec((1,) + shape, lambda b, pt, ix: (b,) + (0,) * len(shape))
    kv_specs = [tile_spec(h, r) for h in range(HKV) for r in range(nsel)]
    grid_spec = pltpu.PrefetchScalarGridSpec(
        num_scalar_prefetch=2, grid=(DB,),
        in_specs=kv_specs * 2 + [per((NSA_HEADS, HD)), per((NSA_HEADS, 2 * HD)), per((1, 256)), per((1, 256)),
                                 per((NSA_HEADS, 3))],
        out_specs=per((NSA_HEADS, HD)))
    n = HKV * nsel
    return pl.pallas_call(
        functools.partial(_nsa_sample_b_t_kernel, past=past, nbp=nbp, nsel=nsel, page=page),
        grid_spec=grid_spec,
        out_shape=jax.ShapeDtypeStruct((DB, NSA_HEADS, HD), F32),
        compiler_params=_params("arbitrary"),
        name="nsa_sample_b",
    )(page_table, idx, *([cache_kt] * n), *([cache_vt] * n), q3, ocw, nk3, nv3, g3)


SEQ_PER_STEP = 8


def _mlstm_sample_kernel(mqkv_ref, sm_ref, mo_ref, gb_ref, c_ref, n_ref, m_ref,
                         om_ref, c_out, n_out, m_out, *, nb):
    eye = (lax.broadcasted_iota(jnp.int32, (HD, HD), 0) == lax.broadcasted_iota(jnp.int32, (HD, HD), 1))
    lane = lax.broadcasted_iota(jnp.int32, (1, LANES), 1)

    def col(rowv):
        return jnp.sum(jnp.where(eye, jnp.broadcast_to(rowv, (HD, HD)), 0.0), axis=1, keepdims=True)

    for b in range(nb):
        g = sm_ref[b] + gb_ref[...]
        lf = _log_sigmoid(g)
        heads = []
        m_tile = jnp.zeros((1, LANES), F32)
        for hd in range(MH):
            ig = g[:, SM_IG + hd:SM_IG + hd + 1]
            a = lf[:, SM_FG + hd:SM_FG + hd + 1] + m_ref[b][:, hd:hd + 1]
            mt = jnp.maximum(a, ig)
            dm = jnp.exp(ig - mt)
            inter = jnp.exp(a - mt)
            q = mqkv_ref[b][:, HD * hd:HD * (hd + 1)]
            k = mqkv_ref[b][:, MLSTM_W + HD * hd:MLSTM_W + HD * (hd + 1)] * Q_SCALE
            v = mqkv_ref[b][:, 2 * MLSTM_W + HD * hd:2 * MLSTM_W + HD * (hd + 1)]
            c_prev = c_ref[b, hd]
            n_prev = n_ref[b][hd:hd + 1, :]
            qk = jnp.sum(q * k, axis=-1, keepdims=True) * dm
            q_c = jnp.sum(col(q) * c_prev, axis=0, keepdims=True)
            num = qk * v + inter * q_c
            den = qk + inter * jnp.sum(q * n_prev, axis=-1, keepdims=True)
            hout = num / jnp.maximum(jnp.abs(den), jnp.exp(-mt))
            heads.append(jax.nn.sigmoid(mo_ref[b][:, HD * hd:HD * (hd + 1)]) * hout)
            w = jnp.exp(ig - mt)
            decay = jnp.exp(a - mt)
            c_out[b, hd] = decay * c_prev + (w * col(k)) * v
            n_out[b, hd:hd + 1, :] = decay * n_prev + w * k
            m_tile = jnp.where(lane == hd, mt, m_tile)
        om_ref[b] = jnp.concatenate(heads, axis=-1)
        m_out[b] = m_tile


def _mlstm_sample(l, mqkv3, sm3, mo3, gate_b_tile, state_c, state_n, state_m4):
    DB = mqkv3.shape[0]
    nb = min(SEQ_PER_STEP, DB)
    assert DB % nb == 0
    per = lambda w: pl.BlockSpec((nb, 1, w), lambda i: (i, 0, 0))
    return pl.pallas_call(
        functools.partial(_mlstm_sample_kernel, nb=nb),
        grid=(DB // nb,),
        in_specs=[per(3 * MLSTM_W), per(LANES), per(MLSTM_W), _const_spec((1, LANES)),
                  pl.BlockSpec((None, nb, MH, HD, HD), lambda i: (l, i, 0, 0, 0)),
                  pl.BlockSpec((None, nb, MH, HD), lambda i: (l, i, 0, 0)),
                  pl.BlockSpec((None, nb, 1, MH), lambda i: (l, i, 0, 0))],
        out_specs=[per(MLSTM_W), pl.BlockSpec((nb, MH, HD, HD), lambda i: (i, 0, 0, 0)),
                   pl.BlockSpec((nb, MH, HD), lambda i: (i, 0, 0)), per(LANES)],
        out_shape=[jax.ShapeDtypeStruct((DB, 1, MLSTM_W), F32), jax.ShapeDtypeStruct((DB, MH, HD, HD), F32),
                   jax.ShapeDtypeStruct((DB, MH, HD), F32), jax.ShapeDtypeStruct((DB, 1, LANES), F32)],
        compiler_params=_params("arbitrary"),
        name="mlstm_sample",
    )(mqkv3, sm3, mo3, gate_b_tile, state_c, state_n, state_m4)


def _outffn_sample_kernel(x_ref, on_ref, om_ref, op_ref, gt1_ref, sh2_ref, sc2_ref, gt2_ref, g2_ref, fg_ref,
                          wout_ref, wup_ref, cw_ref, cb_ref, wdn_ref, prev_ref, y_ref, up_ref, *, d_ff, final):
    mix = (_dot(on_ref[...].astype(BF16), wout_ref[0:NSA_W, :])
           + _dot(om_ref[...].astype(BF16), wout_ref[NSA_W:NSA_W + MLSTM_W, :])
           + _dot(op_ref[...].astype(BF16), wout_ref[NSA_W + MLSTM_W:, :]))
    x1 = x_ref[...] + gt1_ref[...] * mix
    h2 = _norm_mod(x1, g2_ref[...], sc2_ref[...], sh2_ref[...]).astype(BF16)
    up = _dot(h2, wup_ref[...])
    up_ref[...] = up
    y = cb_ref[...] + cw_ref[FFN_CONV - 1:FFN_CONV, :] * up
    for j in range(FFN_CONV - 1):
        y = y + cw_ref[j:j + 1, :] * prev_ref[:, j, :]
    a, b = y[:, :d_ff], y[:, d_ff:]
    f = _dot((a * jax.nn.sigmoid(a) * b).astype(BF16), wdn_ref[...])
    x2 = x1 + gt2_ref[...] * f
    y_ref[...] = _rmsnorm(x2, fg_ref[...]) if final else x2


def _outffn_sample(l, x, on, om, op, gt1, sh2, sc2, gt2, g2, fg, w_out, w_up, conv_w, conv_b, w_dn,
                   state_ffn, final):
    DB, D = x.shape
    d_ff = w_dn.shape[0]
    full = lambda shape: pl.BlockSpec(shape, lambda i: (0,) * len(shape))
    args = (x, on, om, op, gt1, sh2, sc2, gt2, g2, fg, w_out, w_up, conv_w, conv_b, w_dn)
    return pl.pallas_call(
        functools.partial(_outffn_sample_kernel, d_ff=d_ff, final=final),
        grid=(1,),
        in_specs=[full(a.shape) for a in args]
                 + [pl.BlockSpec((None, DB, FFN_CONV - 1, 2 * d_ff), lambda i: (l, 0, 0, 0))],
        out_specs=[full((DB, D)), full((DB, 2 * d_ff))],
        out_shape=[jax.ShapeDtypeStruct((DB, D), F32), jax.ShapeDtypeStruct((DB, 2 * d_ff), F32)],
        compiler_params=_params("arbitrary"),
        name="outffn_sample",
    )(*args, state_ffn)


def _sample_layer(l, x, mod, lw, caches, final):
    DB, D = x.shape
    ckt, cvt, page_table, win_kt, win_vt, st_c, st_n, st_m, st_pool, st_ffn = caches
    past = page_table.shape[1] * ckt.shape[-1]
    sh1, sc1, gt1, sh2, sc2, gt2 = [mod[:, D * i:D * (i + 1)] for i in range(6)]
    q, nk, nv, wkv, mqkv, mo, op, sm, pu = _inproj_sample(
        l, x, lw["g1"], sh1, sc1, lw["w_b"], lw["w_q"], lw["pool_bd"], lw["pool_scale"], st_pool, past)
    r3 = lambda t: t[:, None, :]
    hk, hv = _compress_sample_t(l, ckt, cvt, page_table, lw["cw"])
    q3 = q.reshape(DB, NSA_HEADS, HD)
    idx, ocw, wk_new, wv_new = _nsa_sample_a_t(l, q3, hk, hv, r3(nk), r3(nv), r3(wkv), win_kt, win_vt,
                                               lw["cw"], past)
    g3 = sm[:, :3 * NSA_HEADS].reshape(DB, NSA_HEADS, 3)
    on = _nsa_sample_b_t(l, ckt, cvt, page_table, idx[:, :HKV, :N_SEL], q3, ocw, r3(nk), r3(nv), g3, past)
    om, c_new, n_new, m_new = _mlstm_sample(l, r3(mqkv), r3(sm), r3(mo), lw["gate_b"], st_c, st_n,
                                            st_m.reshape(st_m.shape[0], DB, 1, MH))
    y, up = _outffn_sample(l, x, on.reshape(DB, NSA_W), om[:, 0], op, gt1, sh2, sc2, gt2,
                           lw["g2"], lw["fg"], lw["w_out"], lw["w_up"], lw["conv_w"],
                           lw["conv_b"], lw["w_dn"], st_ffn, final)
    states = (nk.reshape(DB, 1, 2, HKV, HD), nv.reshape(DB, 1, 2, HKV, HD),
              jnp.transpose(wk_new, (0, 3, 1, 2)), jnp.transpose(wv_new, (0, 3, 1, 2)),
              c_new, n_new, m_new[:, 0, :MH],
              jnp.concatenate([st_pool[l][:, 1:], pu[:, None, :]], axis=1),
              jnp.concatenate([st_ffn[l][:, 1:], up[:, None, :]], axis=1))
    return y, states


def kernel(x_prompt, x_sample, cache_k, cache_v, cache_win_k, cache_win_v, state_mlstm_C, state_mlstm_n,
           state_mlstm_m, state_pool, state_ffn_conv, page_table, c_prompt, c_sample, norm1_g, norm2_g, ada_w,
           ada_b, w_in, nsa_cmp_pos, nsa_cmp_w, nsa_cmp_b, mlstm_gate_b, pool_w, pool_scale, w_out, ffn_w_up,
           ffn_conv_w, ffn_conv_b, ffn_w_down, final_g):
    B = x_prompt.shape[0]
    DB, DS, D = x_sample.shape
    assert DS == 1
    depth = w_in.shape[0]
    w = dict(norm1_g=norm1_g, norm2_g=norm2_g, w_in=w_in, nsa_cmp_pos=nsa_cmp_pos, nsa_cmp_w=nsa_cmp_w,
             nsa_cmp_b=nsa_cmp_b, mlstm_gate_b=mlstm_gate_b, pool_w=pool_w, pool_scale=pool_scale, w_out=w_out,
             ffn_w_up=ffn_w_up, ffn_conv_w=ffn_conv_w, ffn_conv_b=ffn_conv_b, ffn_w_down=ffn_w_down,
             final_g=final_g)
    mod = _ada_mod(jnp.concatenate([c_prompt, c_sample], axis=0), ada_w, ada_b)
    page = cache_k.shape[2]
    ckt = jnp.transpose(cache_k, (0, 1, 3, 4, 5, 2))
    cvt = jnp.transpose(cache_v, (0, 1, 3, 4, 5, 2))
    caches = (ckt, cvt, page_table, jnp.transpose(cache_win_k, (0, 1, 3, 4, 2)),
              jnp.transpose(cache_win_v, (0, 1, 3, 4, 2)),
              state_mlstm_C, state_mlstm_n, state_mlstm_m, state_pool, state_ffn_conv)
    xp, xs = x_prompt, x_sample[:, 0, :]
    acc_p = [[] for _ in range(9)]
    acc_s = [[] for _ in range(9)]
    for l in range(depth):
        lw = _prep_layer(l, w, page)
        final = l == depth - 1
        xp, st_p = _prompt_layer(xp, mod[l, :B][:, None, :], lw, final)
        xs, st_s = _sample_layer(l, xs, mod[l, B:], lw, caches, final)
        for a, v in zip(acc_p, st_p):
            a.append(v)
        for a, v in zip(acc_s, st_s):
            a.append(v)
    sp = [jnp.stack(a, axis=0) for a in acc_p]
    ss = [jnp.stack(a, axis=0) for a in acc_s]
    return (xp, xs[:, None, :], *sp, *ss)
```

```python
import functools

import numpy as np
import jax
import jax.numpy as jnp
from jax import lax
from jax.experimental import pallas as pl
from jax.experimental.pallas import tpu as pltpu

F32 = jnp.float32
BF16 = jnp.bfloat16

HD = 64
NSA_HEADS = 8
HKV = 2
GRP = NSA_HEADS // HKV
MH = 4
NSA_W = NSA_HEADS * HD
KV_W = HKV * HD
MLSTM_W = MH * HD
POOL_W = 4 * HD
CMP_STRIDE = 16
CMP_LEN = 32
SLC_BLOCK = 64
SLC_RATIO = SLC_BLOCK // CMP_STRIDE
N_SEL = 16
WINDOW = 512
POOL_WINDOWS = (2, 4, 8, 16)
POOL_STATE = 15
FFN_CONV = 3
RMS_EPS = 1e-6
NEG = -1e30
FORCE = 1e6
Q_SCALE = HD ** -0.5

LANES = 128
SUBLANES = 8
VMEM_LIMIT = 56 * 1024 * 1024

C_NK = 0
C_NV = C_NK + 256
C_WKV = C_NV + 256
C_MQKV = C_WKV + 256
C_MO = C_MQKV + 3 * MLSTM_W
C_PU = C_MO + MLSTM_W
C_SM = C_PU + POOL_W
C_END = C_SM + LANES
SM_IG = 3 * NSA_HEADS
SM_FG = SM_IG + MH


def _dot(a, b):
    return jnp.dot(a, b, preferred_element_type=F32)


def _dot_nt(a, b):
    return lax.dot_general(a, b, (((1,), (1,)), ((), ())), preferred_element_type=F32)


def _dot_tn(a, b):
    return lax.dot_general(a, b, (((0,), (0,)), ((), ())), preferred_element_type=F32)


def _dot_split3(x, m):
    x1 = x.astype(BF16)
    r1 = x - x1.astype(F32)
    x2 = r1.astype(BF16)
    x3 = (r1 - x2.astype(F32)).astype(BF16)
    return _dot(x1, m) + _dot(x2, m) + _dot(x3, m)


def _masked_softmax(s, mask):
    s = jnp.where(mask, s, NEG)
    e = jnp.where(mask, jnp.exp(s - jnp.max(s, axis=-1, keepdims=True)), 0.0)
    return e / jnp.maximum(jnp.sum(e, axis=-1, keepdims=True), jnp.finfo(jnp.float32).tiny)


def _norm_mod(x, g, sc, sh):
    ms = jnp.mean(x * x, axis=-1, keepdims=True)
    return (x * lax.rsqrt(ms + RMS_EPS) * g) * (1.0 + sc) + sh


def _rmsnorm(x, g):
    ms = jnp.mean(x * x, axis=-1, keepdims=True)
    return x * lax.rsqrt(ms + RMS_EPS) * g


def _log_sigmoid(x):
    return jnp.minimum(x, 0.0) - jnp.log1p(jnp.exp(-jnp.abs(x)))


def _cumsum_rows(x):
    n = x.shape[0]
    row = lax.broadcasted_iota(jnp.int32, x.shape, 0)
    sh = 1
    while sh < n:
        x = x + jnp.where(row >= sh, pltpu.roll(x, sh, 0), 0.0)
        sh *= 2
    return x


def _const_spec(shape):
    nd = len(shape)
    return pl.BlockSpec(shape, lambda *_: (0,) * nd, pipeline_mode=pl.Buffered(1))


def _params(*sem):
    return pltpu.CompilerParams(dimension_semantics=sem, vmem_limit_bytes=VMEM_LIMIT)


def _ada_kernel(c_ref, w_ref, b_ref, o_ref):
    c = c_ref[...]
    s = c * jax.nn.sigmoid(c)
    o_ref[0] = _dot(s.astype(BF16), w_ref[0].astype(BF16)) + b_ref[0]


def _ada_mod(c_all, ada_w, ada_b):
    depth, d, n = ada_w.shape
    rows = c_all.shape[0]
    tn = 1536
    return pl.pallas_call(
        _ada_kernel,
        grid=(depth, n // tn),
        in_specs=[pl.BlockSpec((rows, d), lambda l, j: (0, 0)),
                  pl.BlockSpec((1, d, tn), lambda l, j: (l, 0, j)),
                  pl.BlockSpec((1, 1, tn), lambda l, j: (l, 0, j))],
        out_specs=pl.BlockSpec((1, rows, tn), lambda l, j: (l, 0, j)),
        out_shape=jax.ShapeDtypeStruct((depth, rows, n), F32),
        compiler_params=_params("arbitrary", "arbitrary"),
        name="ada_mod",
    )(c_all, ada_w, ada_b.reshape(depth, 1, n))


def _prep_w_in(w):
    d = w.shape[0]
    o = 0
    o += NSA_W
    nkv = w[:, o:o + 6 * KV_W]; o += 6 * KV_W
    ng = w[:, o:o + 3 * NSA_HEADS]; o += 3 * NSA_HEADS
    mqkv = w[:, o:o + 3 * MLSTM_W]; o += 3 * MLSTM_W
    mif = w[:, o:o + 2 * MH]; o += 2 * MH
    mo = w[:, o:o + MLSTM_W]; o += MLSTM_W
    pu = w[:, o:o + POOL_W]
    k_cmp, v_cmp, k_slc, v_slc, k_win, v_win = [nkv[:, KV_W * i:KV_W * (i + 1)] for i in range(6)]
    cols = [
        k_cmp, k_slc, v_cmp, v_slc, k_win, v_win, mqkv, mo, pu,
        ng, mif, jnp.zeros((d, LANES - 3 * NSA_HEADS - 2 * MH), w.dtype)]
    out = jnp.concatenate(cols, axis=1).astype(BF16)
    assert out.shape[1] == C_END
    return out


def _pool_mix(pu, sums, cnt, pw_ref, pscale_ref):
    lane = lax.broadcasted_iota(jnp.int32, pu.shape, 1)
    grp = lane // HD
    mean = jnp.where(grp == 0, sums[2] / cnt[2],
                     jnp.where(grp == 1, sums[4] / cnt[4],
                               jnp.where(grp == 2, sums[8] / cnt[8], sums[16] / cnt[16])))
    d = mean - pu
    return _dot(d.astype(BF16), pw_ref[...]) * pscale_ref[...]


def _split2(x):
    hi = x.astype(BF16)
    return hi, (x - hi.astype(F32)).astype(BF16)


def _dot_split(a, b):
    return _dot(a[0], b[0]) + _dot(a[1], b[0]) + _dot(a[0], b[1])


def _inproj_prompt_kernel(x_ref, g_ref, sh_ref, sc_ref, w_ref, wqh_ref, wql_ref, pw_ref, pscale_ref,
                          q_ref, nk_ref, nv_ref, ks_ref, vs_ref, kw_ref, vw_ref, wkv_ref,
                          mqkv_ref, mo_ref, op_ref, sm_ref, ps_ref, zs_ref, *, tm):
    t = pl.program_id(1)
    hf = _norm_mod(x_ref[0], g_ref[...], sc_ref[0], sh_ref[0])
    h, h_lo = _split2(hf)

    def seg(a, b):
        return _dot(h, w_ref[:, a:b])

    q_ref[0] = _dot_split((h, h_lo), (wqh_ref[...], wql_ref[...])) * Q_SCALE
    nk = seg(C_NK, C_NV)
    nv = seg(C_NV, C_WKV)
    wkv = seg(C_WKV, C_MQKV)
    nk_ref[0] = nk
    nv_ref[0] = nv
    wkv_ref[0] = wkv
    kw_ref[0] = wkv[:, :KV_W].astype(BF16)
    blk = (t * tm + lax.broadcasted_iota(jnp.int32, (tm, LANES), 0)) // SLC_BLOCK
    lane = lax.broadcasted_iota(jnp.int32, (tm, LANES), 1)
    low = lane < HD

    def tiles(x, spare0, spare1):
        return (jnp.where(low, x, spare0).astype(BF16), jnp.where(low, spare1, x).astype(BF16))

    k_slc, v_slc, v_win = nk[:, KV_W:], nv[:, KV_W:], wkv[:, KV_W:]
    one0 = jnp.where(lane == HD, 1.0, 0.0)
    one1 = jnp.where(lane == 0, 1.0, 0.0)
    for dst, pair in ((ks_ref, tiles(k_slc, jnp.where(lane - HD == blk, 1.0, 0.0), jnp.where(lane == blk, 1.0, 0.0))),
                      (vs_ref, tiles(v_slc, one0, one1)), (vw_ref, tiles(v_win, one0, one1))):
        dst[0, :, 0:LANES] = pair[0]
        dst[0, :, LANES:2 * LANES] = pair[1]
    mqkv_ref[0] = seg(C_MQKV, C_MO)
    mo_ref[0] = seg(C_MO, C_PU)
    sm_ref[0] = seg(C_SM, C_END)

    pu = seg(C_PU, C_SM)
    halo = 2 * SUBLANES

    @pl.when(t == 0)
    def _():
        zs_ref[0:halo, :] = jnp.zeros((halo, POOL_W), F32)

    @pl.when(t > 0)
    def _():
        zs_ref[0:halo, :] = zs_ref[tm:tm + halo, :]

    zs_ref[halo:halo + tm, :] = pu
    acc = pu
    sums = {}
    for i in range(1, POOL_STATE + 1):
        acc = acc + zs_ref[pl.ds(halo - i, tm), :]
        if i + 1 in POOL_WINDOWS:
            sums[i + 1] = acc
    pos1 = (t * tm + lax.broadcasted_iota(jnp.int32, (tm, 1), 0) + 1).astype(F32)
    cnt = {w: jnp.minimum(float(w), pos1) for w in POOL_WINDOWS}
    op_ref[0] = _pool_mix(pu, sums, cnt, pw_ref, pscale_ref).astype(BF16)
    ps_ref[0] = zs_ref[tm:tm + halo, :]


def _inproj_prompt(x, g1, sh1, sc1, w_b, w_q2, pool_bd, pool_scale, tm=512):
    B, T, D = x.shape
    nT = T // tm
    assert T % tm == 0 and tm >= 2 * SUBLANES and WINDOW % tm == 0
    nwin = WINDOW // tm
    row = lambda w: pl.BlockSpec((1, tm, w), lambda b, t: (b, t, 0))
    mod = pl.BlockSpec((1, 1, D), lambda b, t: (b, 0, 0))
    outs = [
        (row(NSA_W), (B, T, NSA_W), F32),
        (row(256), (B, T, 256), F32),
        (row(256), (B, T, 256), F32),
        (row(256), (B, T, 256), BF16),
        (row(256), (B, T, 256), BF16),
        (row(KV_W), (B, T, KV_W), BF16),
        (row(256), (B, T, 256), BF16),
        (pl.BlockSpec((1, tm, 256), lambda b, t: (b, jnp.maximum(t - (nT - nwin), 0), 0)),
         (B, WINDOW, 256), F32),
        (row(3 * MLSTM_W), (B, T, 3 * MLSTM_W), F32),
        (row(MLSTM_W), (B, T, MLSTM_W), F32),
        (row(POOL_W), (B, T, POOL_W), BF16),
        (row(LANES), (B, T, LANES), F32),
        (pl.BlockSpec((1, 2 * SUBLANES, POOL_W), lambda b, t: (b, 0, 0)), (B, 2 * SUBLANES, POOL_W), F32),
    ]
    return pl.pallas_call(
        functools.partial(_inproj_prompt_kernel, tm=tm),
        grid=(B, nT),
        in_specs=[pl.BlockSpec((1, tm, D), lambda b, t: (b, t, 0)),
                  _const_spec((1, D)), mod, mod,
                  _const_spec(w_b.shape), _const_spec(w_q2[0].shape), _const_spec(w_q2[1].shape),
                  _const_spec(pool_bd.shape), _const_spec((1, POOL_W))],
        out_specs=[o[0] for o in outs],
        out_shape=[jax.ShapeDtypeStruct(o[1], o[2]) for o in outs],
        scratch_shapes=[pltpu.VMEM((2 * SUBLANES + tm, POOL_W), F32)],
        compiler_params=_params("arbitrary", "arbitrary"),
        name="inproj_prompt",
    )(x, g1, sh1, sc1, w_b, w_q2[0], w_q2[1], pool_bd, pool_scale)


def _compress_prompt_kernel(k_ref, v_ref, wpk_ref, wpv_ref, wk_ref, wkl_ref, wv_ref, bk_ref, bv_ref,
                            kc_ref, vc_ref, *, nseg):
    def pooled(src_ref, wp_ref):
        a = jnp.zeros((nseg, KV_W), F32)
        b = jnp.zeros((nseg, KV_W), F32)
        for j in range(CMP_STRIDE):
            xj = src_ref[0, pl.ds(j, nseg, stride=CMP_STRIDE), :]
            a = a + xj * wp_ref[j:j + 1, :]
            b = b + xj * wp_ref[CMP_STRIDE + j:CMP_STRIDE + j + 1, :]
        return a + pltpu.roll(b, nseg - 1, 0)

    kc_ref[0] = _dot_split(_split2(pooled(k_ref, wpk_ref)), (wk_ref[...], wkl_ref[...])) + bk_ref[...]
    vc_ref[0] = (_dot(pooled(v_ref, wpv_ref).astype(BF16), wv_ref[...]) + bv_ref[...]).astype(BF16)


def _compress_prompt(nk, nv, cw):
    B, T, _ = nk.shape
    nseg = T // CMP_STRIDE
    src = pl.BlockSpec((1, T, KV_W), lambda b: (b, 0, 0))
    return pl.pallas_call(
        functools.partial(_compress_prompt_kernel, nseg=nseg),
        grid=(B,),
        in_specs=[src, src, _const_spec((CMP_LEN, KV_W)), _const_spec((CMP_LEN, KV_W)),
                  _const_spec((KV_W, KV_W)), _const_spec((KV_W, KV_W)), _const_spec((KV_W, 2 * KV_W)),
                  _const_spec((1, KV_W)), _const_spec((1, 2 * KV_W))],
        out_specs=[pl.BlockSpec((1, nseg, KV_W), lambda b: (b, 0, 0)),
                   pl.BlockSpec((1, nseg, 2 * KV_W), lambda b: (b, 0, 0))],
        out_shape=[jax.ShapeDtypeStruct((B, nseg, KV_W), F32),
                   jax.ShapeDtypeStruct((B, nseg, 2 * KV_W), BF16)],
        compiler_params=_params("arbitrary"),
        name="compress_prompt",
    )(nk, nv, cw["wpk"], cw["wpv"], cw["wk"], cw["wk_lo"], cw["wv_dup"], cw["bk"], cw["bv_dup"])


def _prep_compress(cmp_pos, cmp_w, cmp_b, page):
    tile2 = lambda t: jnp.concatenate([t, t], axis=-1)
    z = jnp.zeros((HD, HD), F32)
    wk, wv = cmp_w[0], cmp_w[1]
    wk_bd = jnp.block([[wk, z], [z, wk]])
    wv_dup = jnp.block([[wv, z, z, z], [z, z, z, wv]])

    def pos_t(p):
        reps = page // CMP_STRIDE
        return jnp.concatenate([jnp.tile(p[:CMP_STRIDE].T, (1, reps)), jnp.tile(p[CMP_STRIDE:].T, (1, reps))], axis=0)

    return dict(
        wpk=tile2(cmp_pos[0]), wpv=tile2(cmp_pos[1]),
        wk=wk_bd.astype(BF16), wk_lo=(wk_bd - wk_bd.astype(BF16).astype(F32)).astype(BF16),
        wv_dup=wv_dup.astype(BF16),
        bk=tile2(cmp_b[0])[None, :],
        bv_dup=jnp.concatenate([cmp_b[1], jnp.zeros((2 * HD,), F32), cmp_b[1]])[None, :],
        wpk_t=pos_t(cmp_pos[0]), wpv_t=pos_t(cmp_pos[1]),
        wpk_b0=cmp_pos[0][CMP_STRIDE][:, None], wpv_b0=cmp_pos[1][CMP_STRIDE][:, None],
        wk_t=wk.T.astype(BF16), wv_t=wv.T.astype(BF16),
        bk_col=cmp_b[0][:, None], bv_col=cmp_b[1][:, None])


def _importance_matrix(nc_rows, nc_valid, ns, lane_off, width):
    m = np.zeros((nc_rows, width), np.float32)
    for j in range(ns):
        for n in range(SLC_RATIO * j - 1, SLC_RATIO * j + SLC_RATIO):
            if 0 <= n < nc_valid:
                m[n, lane_off + j] = 1.0
    return m


def _select_blocks_t(st_ref, h, ns):
    nb, nq = st_ref.shape[1], st_ref.shape[2]
    ngrp = nb // SUBLANES
    groups = [st_ref[h, SUBLANES * r:SUBLANES * (r + 1), :] for r in range(ngrp)]
    ranks = [jnp.zeros((SUBLANES, nq), F32) for _ in range(ngrp)]
    jrow = lax.broadcasted_iota(jnp.int32, (SUBLANES, nq), 0)
    for jp in range(min(ns, nb)):
        row = jnp.broadcast_to(st_ref[h, pl.ds(jp, 1), :], (SUBLANES, nq))
        for r in range(ngrp):
            ge = jnp.where(row >= groups[r], 1.0, 0.0)
            gt = jnp.where(row > groups[r], 1.0, 0.0)
            if jp < SUBLANES * r:
                inc = ge
            elif jp >= SUBLANES * (r + 1):
                inc = gt
            else:
                inc = jnp.where(jrow + SUBLANES * r > jp, ge, gt)
            ranks[r] = ranks[r] + inc
    return jnp.concatenate(
        [jnp.where((ranks[r] < N_SEL) & (groups[r] >= 0.0), 0.0, NEG) for r in range(ngrp)], axis=0)


def _nsa_prompt_kernel(q_ref, sm_ref, kc_ref, vc_ref, ks_ref, vs_ref, kw_ref, vw_ref, mt_ref,
                       o_ref, acc_ref, m_ref, qa_ref, st_ref, sa_ref, sb_ref, *, T, tk, qb):
    i = pl.program_id(1)
    ns = T // SLC_BLOCK
    nc = kc_ref.shape[1]
    rows = GRP * qb
    qpos = i * qb + lax.broadcasted_iota(jnp.int32, (qb, 1), 0)
    qpos4 = jnp.concatenate([qpos] * GRP, axis=0)
    qpos_t = i * qb + lax.broadcasted_iota(jnp.int32, (HD, qb), 1)
    jb_t = lax.broadcasted_iota(jnp.int32, (HD, qb), 0)
    sig = jax.nn.sigmoid(sm_ref[0])
    lane = lax.broadcasted_iota(jnp.int32, (qb, LANES), 1)
    wk = WINDOW + qb
    n_idx = lax.broadcasted_iota(jnp.int32, (1, nc), 1)
    cmp_mask = (CMP_STRIDE * n_idx + CMP_LEN - 1) <= qpos4
    wstart = pl.multiple_of(jnp.maximum(i - WINDOW // qb, 0) * qb, qb)
    dlt = qpos - (wstart + lax.broadcasted_iota(jnp.int32, (1, wk), 1))
    wbias = jnp.where((dlt >= 0) & (dlt < WINDOW), 0.0, NEG)
    cur_t = qpos_t // SLC_BLOCK
    valid_t = (jb_t * SLC_BLOCK <= qpos_t) & (jb_t < ns)
    forced_t = (jb_t == 0) | (jb_t == cur_t) | (jb_t == cur_t - 1)
    zpad = jnp.zeros((LANES - HD, qb), F32)
    kc_hi, kc_lo = _split2(kc_ref[0])
    sum_lane = (HD, 0)

    o_cs, o_ws = [], []
    for h in range(HKV):
        parts = []
        for g in range(GRP):
            hd = GRP * h + g
            src = q_ref[0, :, LANES * (hd // 2):LANES * (hd // 2 + 1)]
            if hd % 2 != h:
                src = pltpu.roll(src, HD, 1)
            parts.append(jnp.where((lane < HD) if h == 0 else (lane >= HD), src, 0.0))
        qh, qh_lo = _split2(jnp.concatenate(parts, axis=0))

        p_c = _masked_softmax(_dot_nt(qh, kc_hi) + _dot_nt(qh_lo, kc_hi) + _dot_nt(qh, kc_lo), cmp_mask)
        o_cs.append(_dot(p_c.astype(BF16), vc_ref[0, :, LANES * h:LANES * (h + 1)]))

        psum = p_c[0:qb] + p_c[qb:2 * qb] + p_c[2 * qb:3 * qb] + p_c[3 * qb:4 * qb]
        x1 = psum.astype(BF16)
        r1 = psum - x1.astype(F32)
        x2 = r1.astype(BF16)
        x3 = (r1 - x2.astype(F32)).astype(BF16)
        mt = mt_ref[...]
        imp_t = _dot_nt(mt, x1) + _dot_nt(mt, x2) + _dot_nt(mt, x3)
        score_t = jnp.where(valid_t, jnp.where(forced_t, FORCE, imp_t), -1.0)
        st_ref[h] = jnp.where(jb_t < ns, score_t, -jnp.inf)
        sel_t = _select_blocks_t(st_ref, h, ns)
        selneg = (jnp.concatenate([zpad, sel_t], axis=0) if h == 0
                  else jnp.concatenate([sel_t, zpad], axis=0)).T
        qa_ref[h] = qh + jnp.concatenate([selneg.astype(BF16)] * GRP, axis=0)

        sw = _dot_nt(qh, kw_ref[0, pl.ds(wstart, wk), :])
        sw = jnp.concatenate([sw[qb * g:qb * (g + 1)] + wbias for g in range(GRP)], axis=0)
        e = jnp.exp(sw - jnp.max(sw, axis=-1, keepdims=True))
        o_w = _dot(e.astype(BF16), vw_ref[0, pl.ds(wstart, wk), LANES * h:LANES * (h + 1)])
        o_ws.append(o_w / o_w[:, sum_lane[h]:sum_lane[h] + 1])

    m_ref[...] = jnp.full(m_ref.shape, -jnp.inf, F32)
    acc_ref[...] = jnp.zeros(acc_ref.shape, F32)

    def scores(c, s_ref):
        start = pl.multiple_of(c * tk, tk)
        for h in range(HKV):
            s_ref[h] = _dot_nt(qa_ref[h], ks_ref[0, pl.ds(start, tk), LANES * h:LANES * (h + 1)])

    def accumulate(c, s_ref, diagonal):
        start = pl.multiple_of(c * tk, tk)
        for h in range(HKV):
            sc = s_ref[h]
            if diagonal:
                tok = start + lax.broadcasted_iota(jnp.int32, (1, tk), 1)
                sc = jnp.where(tok <= qpos4, sc, NEG)
            m_prev = m_ref[h]
            m_new = jnp.maximum(m_prev, jnp.max(sc, axis=-1, keepdims=True))
            alpha = jnp.exp(m_prev - m_new)
            p = jnp.exp(sc - m_new[:, 0:1])
            v = vs_ref[0, pl.ds(start, tk), LANES * h:LANES * (h + 1)]
            acc_ref[h] = alpha * acc_ref[h] + _dot(p.astype(BF16), v)
            m_ref[h] = m_new

    n_full = (i * qb) // tk
    scores(0, sa_ref)

    def chunk_pair(cc, carry):
        c0 = 2 * cc
        scores(c0 + 1, sb_ref)
        accumulate(c0, sa_ref, False)
        scores(c0 + 2, sa_ref)
        accumulate(c0 + 1, sb_ref, False)
        return carry

    lax.fori_loop(0, n_full // 2, chunk_pair, 0)

    @pl.when(n_full % 2 == 0)
    def _():
        accumulate(n_full, sa_ref, True)

    @pl.when(n_full % 2 == 1)
    def _():
        scores(n_full, sb_ref)
        accumulate(n_full - 1, sa_ref, False)
        accumulate(n_full, sb_ref, True)

    for h in range(HKV):
        acc = acc_ref[h]
        o_s = acc / acc[:, sum_lane[h]:sum_lane[h] + 1]
        outs = []
        for g in range(GRP):
            c0 = 3 * (GRP * h + g)
            r = slice(qb * g, qb * (g + 1))
            outs.append(sig[:, c0:c0 + 1] * o_cs[h][r] + sig[:, c0 + 1:c0 + 2] * o_s[r]
                        + sig[:, c0 + 2:c0 + 3] * o_ws[h][r])
        base = GRP * HD * h
        for pair in range(GRP // 2):
            lo, hi = outs[2 * pair], outs[2 * pair + 1]
            if h == 0:
                hi = pltpu.roll(hi, HD, 1)
            else:
                lo = pltpu.roll(lo, HD, 1)
            o_ref[0, :, base + LANES * pair:base + LANES * (pair + 1)] = jnp.where(lane < HD, lo, hi).astype(BF16)


def _nsa_prompt(q, sm, kc, vc, ks, vs, kw, vw, qb=256, tk=512):
    B, T, _ = q.shape
    ns = T // SLC_BLOCK
    nc = kc.shape[1]
    assert ns <= HD and T % tk == 0 and T >= WINDOW + qb and WINDOW % qb == 0 and tk % qb == 0
    m2 = _importance_matrix(nc, nc - 1, ns, 0, HD).T
    full = lambda w: pl.BlockSpec((1, T, w), lambda b, i: (b, 0, 0))
    return pl.pallas_call(
        functools.partial(_nsa_prompt_kernel, T=T, tk=tk, qb=qb),
        grid=(B, T // qb),
        in_specs=[pl.BlockSpec((1, qb, NSA_W), lambda b, i: (b, i, 0)),
                  pl.BlockSpec((1, qb, LANES), lambda b, i: (b, i, 0)),
                  pl.BlockSpec((1, nc, KV_W), lambda b, i: (b, 0, 0)),
                  pl.BlockSpec((1, nc, 2 * KV_W), lambda b, i: (b, 0, 0)),
                  full(256), full(256), full(KV_W), full(256),
                  _const_spec(m2.shape)],
        out_specs=pl.BlockSpec((1, qb, NSA_W), lambda b, i: (b, i, 0)),
        out_shape=jax.ShapeDtypeStruct((B, T, NSA_W), BF16),
        scratch_shapes=[pltpu.VMEM((HKV, GRP * qb, LANES), F32)] * 2
                       + [pltpu.VMEM((HKV, GRP * qb, LANES), BF16), pltpu.VMEM((HKV, HD, qb), F32)]
                       + [pltpu.VMEM((HKV, GRP * qb, tk), F32)] * 2,
        compiler_params=_params("arbitrary", "arbitrary"),
        name="nsa_prompt",
    )(q, sm, kc, vc, ks, vs, kw, vw, jnp.asarray(m2, BF16))


def _mlstm_gates(g):
    b = pltpu.roll(_cumsum_rows(_log_sigmoid(g)), LANES - MH, 1)
    return b, g - b


def _mlstm_prompt_kernel(mqkv_ref, sm_ref, mo_ref, gb_ref, om_ref, c_out, n_out, m_out,
                         c_s, n_s, m_s, *, B, L):
    t = pl.program_id(0)

    @pl.when(t == 0)
    def _():
        c_s[...] = jnp.zeros(c_s.shape, F32)
        n_s[...] = jnp.zeros(n_s.shape, F32)
        m_s[...] = jnp.zeros(m_s.shape, F32)

    li = lax.broadcasted_iota(jnp.int32, (L, L), 0)
    si = lax.broadcasted_iota(jnp.int32, (L, L), 1)
    for b in range(B):
        b_al, r = _mlstm_gates(sm_ref[b] + gb_ref[...])
        r_t = jnp.concatenate([r, jnp.zeros((LANES - L, LANES), F32)], axis=0).T if L < LANES else r.T
        heads = []
        for hd in range(MH):
            idx = b * MH + hd
            bcol = b_al[:, SM_IG + hd:SM_IG + hd + 1]
            rcol = r[:, SM_IG + hd:SM_IG + hd + 1]
            rrow = r_t[SM_IG + hd:SM_IG + hd + 1, 0:L]
            mprev = m_s[idx][:, 0:1]
            acol = bcol + mprev
            logd = jnp.where(si <= li, bcol + rrow, NEG)
            mt = jnp.maximum(acol, jnp.max(logd, axis=-1, keepdims=True))
            dm = jnp.exp(logd - mt)
            inter = jnp.exp(acol - mt)
            q = mqkv_ref[b, :, HD * hd:HD * (hd + 1)]
            k = mqkv_ref[b, :, MLSTM_W + HD * hd:MLSTM_W + HD * (hd + 1)] * Q_SCALE
            v = mqkv_ref[b, :, 2 * MLSTM_W + HD * hd:2 * MLSTM_W + HD * (hd + 1)]
            qb, kb = q.astype(BF16), k.astype(BF16)
            qk = _dot_nt(qb, kb) * dm
            c_prev = c_s[idx]
            n_prev = n_s[idx]
            num = _dot(qk.astype(BF16), v.astype(BF16)) + inter * _dot(qb, c_prev.astype(BF16))
            den = jnp.sum(qk, axis=-1, keepdims=True) + inter * jnp.sum(q * n_prev, axis=-1, keepdims=True)
            hout = num / jnp.maximum(jnp.abs(den), jnp.exp(-mt))
            heads.append(jax.nn.sigmoid(mo_ref[b, :, HD * hd:HD * (hd + 1)]) * hout)
            m_last = mt[L - 1:L, :]
            b_last = bcol[L - 1:L, :]
            wcol = jnp.exp(b_last + rcol - m_last)
            decay = jnp.exp(b_last + mprev - m_last)
            c_s[idx] = decay * c_prev + _dot_tn(kb, (wcol * v).astype(BF16))
            n_s[idx] = decay * n_prev + jnp.sum(wcol * k, axis=0, keepdims=True)
            m_s[idx] = jnp.broadcast_to(m_last, (1, LANES))
        om_ref[b] = jnp.concatenate(heads, axis=-1).astype(BF16)

    @pl.when(t == pl.num_programs(0) - 1)
    def _():
        c_out[...] = c_s[...]
        n_out[...] = n_s[...]
        m_out[...] = m_s[...]


def _mlstm_prompt(mqkv, sm, mo, gate_b_tile, L=256):
    B, T, _ = mqkv.shape
    assert T % L == 0 and (L <= LANES or L % LANES == 0)
    blk = lambda w: pl.BlockSpec((B, L, w), lambda t: (0, t, 0))
    st = lambda shape: pl.BlockSpec(shape, lambda t: (0,) * len(shape))
    shapes = [(B * MH, HD, HD), (B * MH, 1, HD), (B * MH, 1, LANES)]
    return pl.pallas_call(
        functools.partial(_mlstm_prompt_kernel, B=B, L=L),
        grid=(T // L,),
        in_specs=[blk(3 * MLSTM_W), blk(LANES), blk(MLSTM_W), _const_spec((1, LANES))],
        out_specs=[blk(MLSTM_W)] + [st(s) for s in shapes],
        out_shape=[jax.ShapeDtypeStruct((B, T, MLSTM_W), BF16)] + [jax.ShapeDtypeStruct(s, F32) for s in shapes],
        scratch_shapes=[pltpu.VMEM(s, F32) for s in shapes],
        compiler_params=_params("arbitrary"),
        name="mlstm_prompt",
    )(mqkv, sm, mo, gate_b_tile)


FF_CHUNK = 2816


def _outffn_prompt_kernel(x_ref, on_ref, om_ref, op_ref, gt1_ref, sh2_ref, sc2_ref, gt2_ref,
                          g2_ref, fg_ref, wout_ref, wup_ref, cw_ref, cb_ref, wdn_ref,
                          y_ref, fs_ref, prev_ref, es_ref, *, tm, d_ff, final):
    t = pl.program_id(1)
    mix = (_dot(on_ref[0], wout_ref[0:NSA_W, :])
           + _dot(om_ref[0], wout_ref[NSA_W:NSA_W + MLSTM_W, :])
           + _dot(op_ref[0], wout_ref[NSA_W + MLSTM_W:, :]))
    x1 = x_ref[0] + gt1_ref[0] * mix
    h2 = _norm_mod(x1, g2_ref[...], sc2_ref[0], sh2_ref[0]).astype(BF16)

    @pl.when(t == 0)
    def _():
        prev_ref[...] = jnp.zeros(prev_ref.shape, F32)

    f = jnp.zeros(x1.shape, F32)
    w = FF_CHUNK
    for c in range(d_ff // w):
        for half, off in ((0, c * w), (1, d_ff + c * w)):
            es_ref[0:SUBLANES, half * w:(half + 1) * w] = prev_ref[:, off:off + w]
            es_ref[SUBLANES:SUBLANES + tm, half * w:(half + 1) * w] = _dot(h2, wup_ref[:, off:off + w])
            prev_ref[:, off:off + w] = es_ref[tm:tm + SUBLANES, half * w:(half + 1) * w]
        ys = []
        for half, off in ((0, c * w), (1, d_ff + c * w)):
            y = cb_ref[:, off:off + w]
            for j in range(FFN_CONV):
                y = y + cw_ref[j:j + 1, off:off + w] * es_ref[pl.ds(SUBLANES - (FFN_CONV - 1) + j, tm), half * w:(half + 1) * w]
            ys.append(y)
        act = ys[0] * jax.nn.sigmoid(ys[0]) * ys[1]
        f = f + _dot(act.astype(BF16), wdn_ref[c * w:(c + 1) * w, :])
    x2 = x1 + gt2_ref[0] * f
    y_ref[0] = _rmsnorm(x2, fg_ref[...]) if final else x2
    fs_ref[0] = prev_ref[...]


def _outffn_prompt(x, on, om, op, gt1, sh2, sc2, gt2, g2, fg, w_out, w_up, conv_w, conv_b, w_dn, final, tm=512):
    B, T, D = x.shape
    d_ff = w_dn.shape[0]
    assert T % tm == 0 and d_ff % FF_CHUNK == 0
    row = lambda w: pl.BlockSpec((1, tm, w), lambda b, t: (b, t, 0))
    mod = pl.BlockSpec((1, 1, D), lambda b, t: (b, 0, 0))
    return pl.pallas_call(
        functools.partial(_outffn_prompt_kernel, tm=tm, d_ff=d_ff, final=final),
        grid=(B, T // tm),
        in_specs=[row(D), row(NSA_W), row(MLSTM_W), row(POOL_W), mod, mod, mod, mod,
                  _const_spec((1, D)), _const_spec((1, D)), _const_spec(w_out.shape), _const_spec(w_up.shape),
                  _const_spec(conv_w.shape), _const_spec(conv_b.shape), _const_spec(w_dn.shape)],
        out_specs=[row(D), pl.BlockSpec((1, SUBLANES, 2 * d_ff), lambda b, t: (b, 0, 0))],
        out_shape=[jax.ShapeDtypeStruct((B, T, D), F32), jax.ShapeDtypeStruct((B, SUBLANES, 2 * d_ff), F32)],
        scratch_shapes=[pltpu.VMEM((SUBLANES, 2 * d_ff), F32), pltpu.VMEM((SUBLANES + tm, 2 * FF_CHUNK), F32)],
        compiler_params=_params("arbitrary", "arbitrary"),
        name="outffn_prompt",
    )(x, on, om, op, gt1, sh2, sc2, gt2, g2, fg, w_out, w_up, conv_w, conv_b, w_dn)


def _prep_layer(l, w, page):
    gate_b = jnp.zeros((1, LANES), F32).at[0, SM_IG:SM_IG + 2 * MH].set(w["mlstm_gate_b"][l])
    pw = w["pool_w"][l]
    z = jnp.zeros((HD, HD), F32)
    pool_bd = jnp.block([[pw[i] if i == j else z for j in range(4)] for i in range(4)]).astype(BF16)
    wq = w["w_in"][l][:, :NSA_W]
    wq_hi = wq.astype(BF16)
    return dict(
        w_q=wq_hi, w_q2=(wq_hi, (wq - wq_hi.astype(F32)).astype(BF16)),
        g1=w["norm1_g"][l][None], g2=w["norm2_g"][l][None],
        w_b=_prep_w_in(w["w_in"][l]), pool_bd=pool_bd, pool_scale=w["pool_scale"][l][None],
        cw=_prep_compress(w["nsa_cmp_pos"][l], w["nsa_cmp_w"][l], w["nsa_cmp_b"][l], page),
        gate_b=gate_b, w_out=w["w_out"][l].astype(BF16), w_up=w["ffn_w_up"][l].astype(BF16),
        conv_w=w["ffn_conv_w"][l], conv_b=w["ffn_conv_b"][l][None], w_dn=w["ffn_w_down"][l].astype(BF16),
        fg=w["final_g"][None])


def _prompt_layer(x, mod, lw, final):
    B, T, D = x.shape
    sh1, sc1, gt1, sh2, sc2, gt2 = [mod[:, :, D * i:D * (i + 1)] for i in range(6)]
    (q, nk, nv, ks, vs, kw, vw, wkv, mqkv, mo, op, sm, ps) = _inproj_prompt(
        x, lw["g1"], sh1, sc1, lw["w_b"], lw["w_q2"], lw["pool_bd"], lw["pool_scale"])
    kc, vc = _compress_prompt(nk, nv, lw["cw"])
    on = _nsa_prompt(q, sm, kc, vc, ks, vs, kw, vw)
    om, c_st, n_st, m_st = _mlstm_prompt(mqkv, sm, mo, lw["gate_b"])
    y, fs = _outffn_prompt(x, on, om, op, gt1, sh2, sc2, gt2, lw["g2"], lw["fg"], lw["w_out"],
                           lw["w_up"], lw["conv_w"], lw["conv_b"], lw["w_dn"], final)
    wb = wkv.shape[1]
    states = (nk.reshape(B, T, 2, HKV, HD), nv.reshape(B, T, 2, HKV, HD),
              wkv[:, :, :KV_W].reshape(B, wb, HKV, HD), wkv[:, :, KV_W:].reshape(B, wb, HKV, HD),
              c_st.reshape(B, MH, HD, HD), n_st.reshape(B, MH, HD), m_st[:, 0, 0].reshape(B, MH),
              ps[:, 2 * SUBLANES - POOL_STATE:], fs[:, SUBLANES - (FFN_CONV - 1):])
    return y, states


def _inproj_sample_kernel(x_ref, g_ref, sh_ref, sc_ref, w_ref, wq_ref, pw_ref, pscale_ref, prev_ref,
                          q_ref, nk_ref, nv_ref, wkv_ref, mqkv_ref, mo_ref, op_ref, sm_ref, pu_ref, *, pos0):
    h = _norm_mod(x_ref[...], g_ref[...], sc_ref[...], sh_ref[...]).astype(BF16)

    def seg(a, b):
        return _dot(h, w_ref[:, a:b])

    q_ref[...] = _dot(h, wq_ref[...]) * Q_SCALE
    nk_ref[...] = seg(C_NK, C_NV)
    nv_ref[...] = seg(C_NV, C_WKV)
    wkv_ref[...] = seg(C_WKV, C_MQKV)
    mqkv_ref[...] = seg(C_MQKV, C_MO)
    mo_ref[...] = seg(C_MO, C_PU)
    sm_ref[...] = seg(C_SM, C_END)
    pu = seg(C_PU, C_SM)
    pu_ref[...] = pu
    acc = pu
    sums = {}
    for i in range(1, POOL_STATE + 1):
        acc = acc + prev_ref[:, POOL_STATE - i, :]
        if i + 1 in POOL_WINDOWS:
            sums[i + 1] = acc
    cnt = {w: float(min(w, pos0 + 1)) for w in POOL_WINDOWS}
    op_ref[...] = _pool_mix(pu, sums, cnt, pw_ref, pscale_ref)


def _inproj_sample(l, x, g1, sh1, sc1, w_b, w_q, pool_bd, pool_scale, state_pool, pos0):
    DB, D = x.shape
    full = lambda shape: pl.BlockSpec(shape, lambda i: (0,) * len(shape))
    widths = [NSA_W, 256, 256, 256, 3 * MLSTM_W, MLSTM_W, POOL_W, LANES, POOL_W]
    return pl.pallas_call(
        functools.partial(_inproj_sample_kernel, pos0=pos0),
        grid=(1,),
        in_specs=[full((DB, D)), full((1, D)), full((DB, D)), full((DB, D)), full(w_b.shape),
                  full(w_q.shape), full(pool_bd.shape), full((1, POOL_W)),
                  pl.BlockSpec((None, DB, POOL_STATE, POOL_W), lambda i: (l, 0, 0, 0))],
        out_specs=[full((DB, w)) for w in widths],
        out_shape=[jax.ShapeDtypeStruct((DB, w), F32) for w in widths],
        compiler_params=_params("arbitrary"),
        name="inproj_sample",
    )(x, g1, sh1, sc1, w_b, w_q, pool_bd, pool_scale, state_pool)


PAGES_PER_STEP = 32


def _row_to_col(rowv):
    n = rowv.shape[1]
    eye = lax.broadcasted_iota(jnp.int32, (n, n), 0) == lax.broadcasted_iota(jnp.int32, (n, n), 1)
    return jnp.sum(jnp.where(eye, jnp.broadcast_to(rowv, (n, n)), 0.0), axis=1, keepdims=True)


def _compress_sample_t_kernel(pt_ref, *refs, npg):
    del pt_ref
    k_refs, v_refs = refs[:npg], refs[npg:2 * npg]
    wk_ref, wv_ref, sg_ref, ok_ref, ov_ref = refs[2 * npg:]
    for src, w_ref, o_ref in ((k_refs, wk_ref, ok_ref), (v_refs, wv_ref, ov_ref)):
        wa = w_ref[0:HD, :]
        wb = w_ref[HD:2 * HD, :]
        for h in range(HKV):
            prods = []
            for i in range(npg):
                x = src[i][h]
                prods.append(jnp.concatenate([x * wa, x * wb], axis=0).astype(BF16))
            o_ref[0, h] = _dot(jnp.concatenate(prods, axis=1), sg_ref[...])


def _compress_sample_t(l, cache_kt, cache_vt, page_table, cw):
    DB, n_pages = page_table.shape
    page = cache_kt.shape[-1]
    npg = min(PAGES_PER_STEP, n_pages)
    assert n_pages % npg == 0 and page % CMP_STRIDE == 0
    spp = page // CMP_STRIDE
    nseg = n_pages * spp
    assert (npg * spp) % LANES == 0 or npg * spp == nseg
    sg = np.zeros((npg * page, npg * spp), np.float32)
    sg[np.arange(npg * page), np.arange(npg * page) // CMP_STRIDE] = 1.0

    def page_spec(i):
        return pl.BlockSpec((None, None, None, HKV, HD, page),
                            lambda b, c, pt: (l, pt[b, c * npg + i], 0, 0, 0, 0))

    out = pl.BlockSpec((1, HKV, 2 * HD, npg * spp), lambda b, c, pt: (b, 0, 0, c))
    cst = lambda shape: pl.BlockSpec(shape, lambda b, c, pt: (0,) * len(shape))
    grid_spec = pltpu.PrefetchScalarGridSpec(
        num_scalar_prefetch=1, grid=(DB, n_pages // npg),
        in_specs=[page_spec(i) for i in range(npg)] * 2
                 + [cst((2 * HD, page)), cst((2 * HD, page)), cst(sg.shape)],
        out_specs=[out] * 2)
    return pl.pallas_call(
        functools.partial(_compress_sample_t_kernel, npg=npg),
        grid_spec=grid_spec,
        out_shape=[jax.ShapeDtypeStruct((DB, HKV, 2 * HD, nseg), F32)] * 2,
        compiler_params=_params("arbitrary", "arbitrary"),
        name="compress_sample",
    )(page_table, *([cache_kt] * npg), *([cache_vt] * npg), cw["wpk_t"], cw["wpv_t"], jnp.asarray(sg, BF16))


def _nsa_sample_a_t_kernel(q_ref, hk_ref, hv_ref, nk_ref, nv_ref, wkv_ref, wink_ref, winv_ref,
                           wpk_ref, wpv_ref, wk_ref, wv_ref, bk_ref, bv_ref, m_ref,
                           idx_ref, ocw_ref, wko_ref, wvo_ref, *, past, ns):
    ncs = hk_ref.shape[3]
    wb = wink_ref.shape[3]
    q8 = q_ref[0].astype(BF16)
    row = lax.broadcasted_iota(jnp.int32, (NSA_HEADS, 1), 0)
    lane_c = lax.broadcasted_iota(jnp.int32, (HD, ncs), 1)
    lane_w = lax.broadcasted_iota(jnp.int32, (HD, wb), 1)
    n_idx = lax.broadcasted_iota(jnp.int32, (1, ncs), 1)

    def per_head(fn):
        a, b = fn(0), fn(1)
        return jnp.where(row // GRP == 0, a, b)

    def comp_t(h_ref, new_row, wp_ref, w_ref, bias_ref, h):
        at = h_ref[0, h, 0:HD, :]
        bt = h_ref[0, h, HD:2 * HD, :]
        b_new = _row_to_col(new_row[:, HD * h:HD * (h + 1)]) * wp_ref[...]
        pooled = at + jnp.where(lane_c == ncs - 1, b_new, pltpu.roll(bt, ncs - 1, 1))
        return (_dot(w_ref[...], pooled.astype(BF16)) + bias_ref[...]).astype(BF16)

    kct = [comp_t(hk_ref, nk_ref[0], wpk_ref, wk_ref, bk_ref, h) for h in range(HKV)]
    vct = [comp_t(hv_ref, nv_ref[0], wpv_ref, wv_ref, bv_ref, h) for h in range(HKV)]
    s = per_head(lambda h: _dot(q8, kct[h]))
    p = _masked_softmax(s, (CMP_STRIDE * n_idx + CMP_LEN - 1) <= past)
    pb = p.astype(BF16)
    o_c = per_head(lambda h: _dot_nt(pb, vct[h]))

    rowi = lax.broadcasted_iota(jnp.int32, p.shape, 0)
    p0 = jnp.sum(p[0:GRP], axis=0, keepdims=True)
    p1 = jnp.sum(p[GRP:2 * GRP], axis=0, keepdims=True)
    imp = _dot_split3(jnp.where(rowi == 0, p0, jnp.where(rowi == 1, p1, 0.0)), m_ref[...])
    lane = lax.broadcasted_iota(jnp.int32, imp.shape, 1)
    cur = past // SLC_BLOCK
    valid = lane * SLC_BLOCK <= past
    forced = (lane == 0) | (lane == cur) | (lane == cur - 1)
    score = jnp.where(valid, jnp.where(forced, FORCE, imp), -1.0)
    score = jnp.where(lane < ns, score, -jnp.inf)
    lane_o = lax.broadcasted_iota(jnp.int32, (SUBLANES, LANES), 1)
    idx = jnp.full((SUBLANES, LANES), -1, jnp.int32)
    for r in range(min(N_SEL, ns)):
        mx = jnp.max(score, axis=-1, keepdims=True)
        first = jnp.min(jnp.where(score == mx, lane, 1 << 20), axis=-1, keepdims=True)
        idx = jnp.where(lane_o == r, jnp.where(mx >= 0.0, first, -1), idx)
        score = jnp.where(lane == first, -jnp.inf, score)
    idx_ref[0] = idx

    sel_head = lambda t: jnp.where(row // GRP == 0, t[:, :HD], t[:, HD:])
    k_new = wkv_ref[0][:, :KV_W]
    v_new = wkv_ref[0][:, KV_W:]
    s_w = per_head(lambda h: _dot(q8, wink_ref[0, h].astype(BF16)))
    dlt = wb - lax.broadcasted_iota(jnp.int32, (1, wb), 1)
    mask = (dlt < WINDOW) & (past - dlt >= 0)
    s_new = jnp.sum(q8.astype(F32) * sel_head(k_new).astype(BF16).astype(F32), axis=-1, keepdims=True)
    s_w = jnp.where(mask, s_w, NEG)
    mx = jnp.maximum(jnp.max(s_w, axis=-1, keepdims=True), s_new)
    e = jnp.where(mask, jnp.exp(s_w - mx), 0.0)
    e_new = jnp.exp(s_new - mx)
    den = jnp.sum(e, axis=-1, keepdims=True) + e_new
    eb = e.astype(BF16)
    o_w = per_head(lambda h: _dot_nt(eb, winv_ref[0, h].astype(BF16)))
    o_w = (o_w + e_new * sel_head(v_new).astype(BF16).astype(F32)) / den
    ocw_ref[0] = jnp.concatenate([o_c, o_w], axis=-1)
    for h in range(HKV):
        for src, new, dst in ((wink_ref, k_new, wko_ref), (winv_ref, v_new, wvo_ref)):
            col = _row_to_col(new[:, HD * h:HD * (h + 1)])
            dst[0, h] = jnp.where(lane_w == wb - 1, col, pltpu.roll(src[0, h], wb - 1, 1))


NSA_A_SEQS = 4
N_SEQ_IN = 8
N_SEQ_OUT = 4


def _nsa_sample_a_t_multi(*refs, nsq, past, ns):
    for sq in range(nsq):
        one = lambda r: r.at[pl.ds(sq, 1)]
        _nsa_sample_a_t_kernel(*[one(r) for r in refs[:N_SEQ_IN]], *refs[N_SEQ_IN:-N_SEQ_OUT],
                               *[one(r) for r in refs[-N_SEQ_OUT:]], past=past, ns=ns)


def _nsa_sample_a_t(l, q3, hk, hv, nk3, nv3, wkv3, win_kt, win_vt, cw, past):
    DB = q3.shape[0]
    ncs = hk.shape[3]
    wb = win_kt.shape[-1]
    ns = -(-(past + 1) // SLC_BLOCK)
    nsl = -(-ns // LANES) * LANES
    m = _importance_matrix(ncs, ncs, ns, 0, nsl)
    nsq = NSA_A_SEQS if DB % NSA_A_SEQS == 0 else 1
    per = lambda shape: pl.BlockSpec((nsq,) + shape, lambda b: (b,) + (0,) * len(shape))
    win = pl.BlockSpec((None, nsq, HKV, HD, wb), lambda b: (l, b, 0, 0, 0))
    shapes = [(DB, SUBLANES, LANES), (DB, NSA_HEADS, 2 * HD), (DB, HKV, HD, wb), (DB, HKV, HD, wb)]
    dts = [jnp.int32, F32, F32, F32]
    return pl.pallas_call(
        functools.partial(_nsa_sample_a_t_multi, nsq=nsq, past=past, ns=ns),
        grid=(DB // nsq,),
        in_specs=[per((NSA_HEADS, HD))] + [per((HKV, 2 * HD, ncs))] * 2 + [per((1, 256))] * 3 + [win, win]
                 + [_const_spec((HD, 1))] * 2 + [_const_spec((HD, HD))] * 2
                 + [_const_spec((HD, 1))] * 2 + [_const_spec(m.shape)],
        out_specs=[per(s[1:]) for s in shapes],
        out_shape=[jax.ShapeDtypeStruct(s, d) for s, d in zip(shapes, dts)],
        compiler_params=_params("arbitrary"),
        name="nsa_sample_a",
    )(q3, hk, hv, nk3, nv3, wkv3, win_kt, win_vt, cw["wpk_b0"], cw["wpv_b0"], cw["wk_t"], cw["wv_t"],
      cw["bk_col"], cw["bv_col"], jnp.asarray(m, BF16))


def _nsa_sample_b_t_kernel(pt_ref, ix_ref, *refs, past, nbp, nsel, page):
    del pt_ref
    k_refs, v_refs = refs[:HKV * nsel], refs[HKV * nsel:2 * HKV * nsel]
    q_ref, ocw_ref, nk_ref, nv_ref, g_ref, o_ref = refs[2 * HKV * nsel:]
    b = pl.program_id(0)
    bpp = page // SLC_BLOCK
    q8 = q_ref[0].astype(BF16)
    row = lax.broadcasted_iota(jnp.int32, (NSA_HEADS, 1), 0)
    sel_head = lambda t: jnp.where(row // GRP == 0, t[:, :HD], t[:, HD:])
    k_new = sel_head(nk_ref[0][:, KV_W:]).astype(BF16).astype(F32)
    v_new = sel_head(nv_ref[0][:, KV_W:]).astype(BF16).astype(F32)
    s_new = jnp.sum(q8.astype(F32) * k_new, axis=-1, keepdims=True)
    nk = nsel * page
    lane = lax.broadcasted_iota(jnp.int32, (1, nk), 1)
    o_s = jnp.zeros((NSA_HEADS, HD), F32)
    for h in range(HKV):
        kt = jnp.concatenate([k_refs[h * nsel + r][...] for r in range(nsel)], axis=1).astype(BF16)
        vt = jnp.concatenate([v_refs[h * nsel + r][...] for r in range(nsel)], axis=1).astype(BF16)
        jv = jnp.full((1, nk), -1, jnp.int32)
        n_new = jnp.int32(0)
        for r in range(nsel):
            j = ix_ref[b, h, r]
            jv = jnp.where(lane // page == r, j, jv)
            n_new = n_new + (j == nbp).astype(jnp.int32)
        t_in = lane % page
        tok = (jv // bpp) * page + t_in
        mask = (jv >= 0) & (jv < nbp) & (t_in // SLC_BLOCK == jv % bpp) & (tok <= past)
        has_new = n_new > 0
        s = jnp.where(mask, _dot(q8, kt), NEG)
        sn = jnp.where(has_new, s_new, NEG)
        mx = jnp.maximum(jnp.max(s, axis=-1, keepdims=True), sn)
        e = jnp.where(mask, jnp.exp(s - mx), 0.0)
        e_new = jnp.where(has_new, jnp.exp(sn - mx), 0.0)
        den = jnp.maximum(jnp.sum(e, axis=-1, keepdims=True) + e_new, jnp.finfo(jnp.float32).tiny)
        o_h = (_dot_nt(e.astype(BF16), vt) + e_new * v_new) / den
        o_s = jnp.where(row // GRP == h, o_h, o_s)
    g = jax.nn.sigmoid(g_ref[0])
    ocw = ocw_ref[0]
    o_ref[0] = g[:, 0:1] * ocw[:, :HD] + g[:, 1:2] * o_s + g[:, 2:3] * ocw[:, HD:]


def _nsa_sample_b_t(l, cache_kt, cache_vt, page_table, idx, q3, ocw, nk3, nv3, g3, past):
    DB = q3.shape[0]
    page = cache_kt.shape[-1]
    bpp = page // SLC_BLOCK
    nbp = past // SLC_BLOCK
    nsel = idx.shape[2]

    def tile_spec(h, r):
        def imap(b, pt, ix):
            j = jnp.clip(ix[b, h, r], 0, nbp - 1)
            return (l, pt[b, j // bpp], 1, h, 0, 0)
        return pl.BlockSpec((None, None, None, None, HD, page), imap)

    per = lambda shape: pl.BlockSpec((1,) + shape, lambda b, pt, ix: (b,) + (0,) * len(shape))
    kv_specs = [tile_spec(h, r) for h in range(HKV) for r in range(nsel)]
    grid_spec = pltpu.PrefetchScalarGridSpec(
        num_scalar_prefetch=2, grid=(DB,),
        in_specs=kv_specs * 2 + [per((NSA_HEADS, HD)), per((NSA_HEADS, 2 * HD)), per((1, 256)), per((1, 256)),
                                 per((NSA_HEADS, 3))],
        out_specs=per((NSA_HEADS, HD)))
    n = HKV * nsel
    return pl.pallas_call(
        functools.partial(_nsa_sample_b_t_kernel, past=past, nbp=nbp, nsel=nsel, page=page),
        grid_spec=grid_spec,
        out_shape=jax.ShapeDtypeStruct((DB, NSA_HEADS, HD), F32),
        compiler_params=_params("arbitrary"),
        name="nsa_sample_b",
    )(page_table, idx, *([cache_kt] * n), *([cache_vt] * n), q3, ocw, nk3, nv3, g3)


SEQ_PER_STEP = 8


def _mlstm_sample_kernel(mqkv_ref, sm_ref, mo_ref, gb_ref, c_ref, n_ref, m_ref,
                         om_ref, c_out, n_out, m_out, *, nb):
    eye = (lax.broadcasted_iota(jnp.int32, (HD, HD), 0) == lax.broadcasted_iota(jnp.int32, (HD, HD), 1))
    lane = lax.broadcasted_iota(jnp.int32, (1, LANES), 1)

    def col(rowv):
        return jnp.sum(jnp.where(eye, jnp.broadcast_to(rowv, (HD, HD)), 0.0), axis=1, keepdims=True)

    for b in range(nb):
        g = sm_ref[b] + gb_ref[...]
        lf = _log_sigmoid(g)
        heads = []
        m_tile = jnp.zeros((1, LANES), F32)
        for hd in range(MH):
            ig = g[:, SM_IG + hd:SM_IG + hd + 1]
            a = lf[:, SM_FG + hd:SM_FG + hd + 1] + m_ref[b][:, hd:hd + 1]
            mt = jnp.maximum(a, ig)
            dm = jnp.exp(ig - mt)
            inter = jnp.exp(a - mt)
            q = mqkv_ref[b][:, HD * hd:HD * (hd + 1)]
            k = mqkv_ref[b][:, MLSTM_W + HD * hd:MLSTM_W + HD * (hd + 1)] * Q_SCALE
            v = mqkv_ref[b][:, 2 * MLSTM_W + HD * hd:2 * MLSTM_W + HD * (hd + 1)]
            c_prev = c_ref[b, hd]
            n_prev = n_ref[b][hd:hd + 1, :]
            qk = jnp.sum(q * k, axis=-1, keepdims=True) * dm
            q_c = jnp.sum(col(q) * c_prev, axis=0, keepdims=True)
            num = qk * v + inter * q_c
            den = qk + inter * jnp.sum(q * n_prev, axis=-1, keepdims=True)
            hout = num / jnp.maximum(jnp.abs(den), jnp.exp(-mt))
            heads.append(jax.nn.sigmoid(mo_ref[b][:, HD * hd:HD * (hd + 1)]) * hout)
            w = jnp.exp(ig - mt)
            decay = jnp.exp(a - mt)
            c_out[b, hd] = decay * c_prev + (w * col(k)) * v
            n_out[b, hd:hd + 1, :] = decay * n_prev + w * k
            m_tile = jnp.where(lane == hd, mt, m_tile)
        om_ref[b] = jnp.concatenate(heads, axis=-1)
        m_out[b] = m_tile


def _mlstm_sample(l, mqkv3, sm3, mo3, gate_b_tile, state_c, state_n, state_m4):
    DB = mqkv3.shape[0]
    nb = min(SEQ_PER_STEP, DB)
    assert DB % nb == 0
    per = lambda w: pl.BlockSpec((nb, 1, w), lambda i: (i, 0, 0))
    return pl.pallas_call(
        functools.partial(_mlstm_sample_kernel, nb=nb),
        grid=(DB // nb,),
        in_specs=[per(3 * MLSTM_W), per(LANES), per(MLSTM_W), _const_spec((1, LANES)),
                  pl.BlockSpec((None, nb, MH, HD, HD), lambda i: (l, i, 0, 0, 0)),
                  pl.BlockSpec((None, nb, MH, HD), lambda i: (l, i, 0, 0)),
                  pl.BlockSpec((None, nb, 1, MH), lambda i: (l, i, 0, 0))],
        out_specs=[per(MLSTM_W), pl.BlockSpec((nb, MH, HD, HD), lambda i: (i, 0, 0, 0)),
                   pl.BlockSpec((nb, MH, HD), lambda i: (i, 0, 0)), per(LANES)],
        out_shape=[jax.ShapeDtypeStruct((DB, 1, MLSTM_W), F32), jax.ShapeDtypeStruct((DB, MH, HD, HD), F32),
                   jax.ShapeDtypeStruct((DB, MH, HD), F32), jax.ShapeDtypeStruct((DB, 1, LANES), F32)],
        compiler_params=_params("arbitrary"),
        name="mlstm_sample",
    )(mqkv3, sm3, mo3, gate_b_tile, state_c, state_n, state_m4)


def _outffn_sample_kernel(x_ref, on_ref, om_ref, op_ref, gt1_ref, sh2_ref, sc2_ref, gt2_ref, g2_ref, fg_ref,
                          wout_ref, wup_ref, cw_ref, cb_ref, wdn_ref, prev_ref, y_ref, up_ref, *, d_ff, final):
    mix = (_dot(on_ref[...].astype(BF16), wout_ref[0:NSA_W, :])
           + _dot(om_ref[...].astype(BF16), wout_ref[NSA_W:NSA_W + MLSTM_W, :])
           + _dot(op_ref[...].astype(BF16), wout_ref[NSA_W + MLSTM_W:, :]))
    x1 = x_ref[...] + gt1_ref[...] * mix
    h2 = _norm_mod(x1, g2_ref[...], sc2_ref[...], sh2_ref[...]).astype(BF16)
    up = _dot(h2, wup_ref[...])
    up_ref[...] = up
    y = cb_ref[...] + cw_ref[FFN_CONV - 1:FFN_CONV, :] * up
    for j in range(FFN_CONV - 1):
        y = y + cw_ref[j:j + 1, :] * prev_ref[:, j, :]
    a, b = y[:, :d_ff], y[:, d_ff:]
    f = _dot((a * jax.nn.sigmoid(a) * b).astype(BF16), wdn_ref[...])
    x2 = x1 + gt2_ref[...] * f
    y_ref[...] = _rmsnorm(x2, fg_ref[...]) if final else x2


def _outffn_sample(l, x, on, om, op, gt1, sh2, sc2, gt2, g2, fg, w_out, w_up, conv_w, conv_b, w_dn,
                   state_ffn, final):
    DB, D = x.shape
    d_ff = w_dn.shape[0]
    full = lambda shape: pl.BlockSpec(shape, lambda i: (0,) * len(shape))
    args = (x, on, om, op, gt1, sh2, sc2, gt2, g2, fg, w_out, w_up, conv_w, conv_b, w_dn)
    return pl.pallas_call(
        functools.partial(_outffn_sample_kernel, d_ff=d_ff, final=final),
        grid=(1,),
        in_specs=[full(a.shape) for a in args]
                 + [pl.BlockSpec((None, DB, FFN_CONV - 1, 2 * d_ff), lambda i: (l, 0, 0, 0))],
        out_specs=[full((DB, D)), full((DB, 2 * d_ff))],
        out_shape=[jax.ShapeDtypeStruct((DB, D), F32), jax.ShapeDtypeStruct((DB, 2 * d_ff), F32)],
        compiler_params=_params("arbitrary"),
        name="outffn_sample",
    )(*args, state_ffn)


def _sample_layer(l, x, mod, lw, caches, final):
    DB, D = x.shape
    ckt, cvt, page_table, win_kt, win_vt, st_c, st_n, st_m, st_pool, st_ffn = caches
    past = page_table.shape[1] * ckt.shape[-1]
    sh1, sc1, gt1, sh2, sc2, gt2 = [mod[:, D * i:D * (i + 1)] for i in range(6)]
    q, nk, nv, wkv, mqkv, mo, op, sm, pu = _inproj_sample(
        l, x, lw["g1"], sh1, sc1, lw["w_b"], lw["w_q"], lw["pool_bd"], lw["pool_scale"], st_pool, past)
    r3 = lambda t: t[:, None, :]
    hk, hv = _compress_sample_t(l, ckt, cvt, page_table, lw["cw"])
    q3 = q.reshape(DB, NSA_HEADS, HD)
    idx, ocw, wk_new, wv_new = _nsa_sample_a_t(l, q3, hk, hv, r3(nk), r3(nv), r3(wkv), win_kt, win_vt,
                                               lw["cw"], past)
    g3 = sm[:, :3 * NSA_HEADS].reshape(DB, NSA_HEADS, 3)
    on = _nsa_sample_b_t(l, ckt, cvt, page_table, idx[:, :HKV, :N_SEL], q3, ocw, r3(nk), r3(nv), g3, past)
    om, c_new, n_new, m_new = _mlstm_sample(l, r3(mqkv), r3(sm), r3(mo), lw["gate_b"], st_c, st_n,
                                            st_m.reshape(st_m.shape[0], DB, 1, MH))
    y, up = _outffn_sample(l, x, on.reshape(DB, NSA_W), om[:, 0], op, gt1, sh2, sc2, gt2,
                           lw["g2"], lw["fg"], lw["w_out"], lw["w_up"], lw["conv_w"],
                           lw["conv_b"], lw["w_dn"], st_ffn, final)
    states = (nk.reshape(DB, 1, 2, HKV, HD), nv.reshape(DB, 1, 2, HKV, HD),
              jnp.transpose(wk_new, (0, 3, 1, 2)), jnp.transpose(wv_new, (0, 3, 1, 2)),
              c_new, n_new, m_new[:, 0, :MH],
              jnp.concatenate([st_pool[l][:, 1:], pu[:, None, :]], axis=1),
              jnp.concatenate([st_ffn[l][:, 1:], up[:, None, :]], axis=1))
    return y, states


def kernel(x_prompt, x_sample, cache_k, cache_v, cache_win_k, cache_win_v, state_mlstm_C, state_mlstm_n,
           state_mlstm_m, state_pool, state_ffn_conv, page_table, c_prompt, c_sample, norm1_g, norm2_g, ada_w,
           ada_b, w_in, nsa_cmp_pos, nsa_cmp_w, nsa_cmp_b, mlstm_gate_b, pool_w, pool_scale, w_out, ffn_w_up,
           ffn_conv_w, ffn_conv_b, ffn_w_down, final_g):
    B = x_prompt.shape[0]
    DB, DS, D = x_sample.shape
    assert DS == 1
    depth = w_in.shape[0]
    w = dict(norm1_g=norm1_g, norm2_g=norm2_g, w_in=w_in, nsa_cmp_pos=nsa_cmp_pos, nsa_cmp_w=nsa_cmp_w,
             nsa_cmp_b=nsa_cmp_b, mlstm_gate_b=mlstm_gate_b, pool_w=pool_w, pool_scale=pool_scale, w_out=w_out,
             ffn_w_up=ffn_w_up, ffn_conv_w=ffn_conv_w, ffn_conv_b=ffn_conv_b, ffn_w_down=ffn_w_down,
             final_g=final_g)
    mod = _ada_mod(jnp.concatenate([c_prompt, c_sample], axis=0), ada_w, ada_b)
    page = cache_k.shape[2]
    ckt = jnp.transpose(cache_k, (0, 1, 3, 4, 5, 2))
    cvt = jnp.transpose(cache_v, (0, 1, 3, 4, 5, 2))
    caches = (ckt, cvt, page_table, jnp.transpose(cache_win_k, (0, 1, 3, 4, 2)),
              jnp.transpose(cache_win_v, (0, 1, 3, 4, 2)),
              state_mlstm_C, state_mlstm_n, state_mlstm_m, state_pool, state_ffn_conv)
    xp, xs = x_prompt, x_sample[:, 0, :]
    acc_p = [[] for _ in range(9)]
    acc_s = [[] for _ in range(9)]
    for l in range(depth):
        lw = _prep_layer(l, w, page)
        final = l == depth - 1
        xp, st_p = _prompt_layer(xp, mod[l, :B][:, None, :], lw, final)
        xs, st_s = _sample_layer(l, xs, mod[l, B:], lw, caches, final)
        for a, v in zip(acc_p, st_p):
            a.append(v)
        for a, v in zip(acc_s, st_s):
            a.append(v)
    sp = [jnp.stack(a, axis=0) for a in acc_p]
    ss = [jnp.stack(a, axis=0) for a in acc_s]
    return (xp, xs[:, None, :], *sp, *ss)
```

```python
import functools

import numpy as np
import jax
import jax.numpy as jnp
from jax import lax
from jax.experimental import pallas as pl
from jax.experimental.pallas import tpu as pltpu

F32 = jnp.float32
BF16 = jnp.bfloat16

HD = 64
NSA_HEADS = 8
HKV = 2
GRP = NSA_HEADS // HKV
MH = 4
NSA_W = NSA_HEADS * HD
KV_W = HKV * HD
MLSTM_W = MH * HD
POOL_W = 4 * HD
CMP_STRIDE = 16
CMP_LEN = 32
SLC_BLOCK = 64
SLC_RATIO = SLC_BLOCK // CMP_STRIDE
N_SEL = 16
WINDOW = 512
POOL_WINDOWS = (2, 4, 8, 16)
POOL_STATE = 15
FFN_CONV = 3
RMS_EPS = 1e-6
NEG = -1e30
FORCE = 1e6
Q_SCALE = HD ** -0.5

LANES = 128
SUBLANES = 8
VMEM_LIMIT = 56 * 1024 * 1024

C_NK = 0
C_NV = C_NK + 256
C_WKV = C_NV + 256
C_MQKV = C_WKV + 256
C_MO = C_MQKV + 3 * MLSTM_W
C_PU = C_MO + MLSTM_W
C_SM = C_PU + POOL_W
C_END = C_SM + LANES
SM_IG = 3 * NSA_HEADS
SM_FG = SM_IG + MH


def _dot(a, b):
    return jnp.dot(a, b, preferred_element_type=F32)


def _dot_nt(a, b):
    return lax.dot_general(a, b, (((1,), (1,)), ((), ())), preferred_element_type=F32)


def _dot_tn(a, b):
    return lax.dot_general(a, b, (((0,), (0,)), ((), ())), preferred_element_type=F32)


def _dot_split3(x, m):
    x1 = x.astype(BF16)
    r1 = x - x1.astype(F32)
    x2 = r1.astype(BF16)
    x3 = (r1 - x2.astype(F32)).astype(BF16)
    return _dot(x1, m) + _dot(x2, m) + _dot(x3, m)


def _masked_softmax(s, mask):
    s = jnp.where(mask, s, NEG)
    e = jnp.where(mask, jnp.exp(s - jnp.max(s, axis=-1, keepdims=True)), 0.0)
    return e / jnp.maximum(jnp.sum(e, axis=-1, keepdims=True), jnp.finfo(jnp.float32).tiny)


def _norm_mod(x, g, sc, sh):
    ms = jnp.mean(x * x, axis=-1, keepdims=True)
    return (x * lax.rsqrt(ms + RMS_EPS) * g) * (1.0 + sc) + sh


def _rmsnorm(x, g):
    ms = jnp.mean(x * x, axis=-1, keepdims=True)
    return x * lax.rsqrt(ms + RMS_EPS) * g


def _log_sigmoid(x):
    return jnp.minimum(x, 0.0) - jnp.log1p(jnp.exp(-jnp.abs(x)))


def _cumsum_rows(x):
    n = x.shape[0]
    row = lax.broadcasted_iota(jnp.int32, x.shape, 0)
    sh = 1
    while sh < n:
        x = x + jnp.where(row >= sh, pltpu.roll(x, sh, 0), 0.0)
        sh *= 2
    return x


def _const_spec(shape):
    nd = len(shape)
    return pl.BlockSpec(shape, lambda *_: (0,) * nd, pipeline_mode=pl.Buffered(1))


def _params(*sem):
    return pltpu.CompilerParams(dimension_semantics=sem, vmem_limit_bytes=VMEM_LIMIT)


def _ada_kernel(c_ref, w_ref, b_ref, o_ref):
    c = c_ref[...]
    s = c * jax.nn.sigmoid(c)
    o_ref[0] = _dot(s.astype(BF16), w_ref[0].astype(BF16)) + b_ref[0]


def _ada_mod(c_all, ada_w, ada_b):
    depth, d, n = ada_w.shape
    rows = c_all.shape[0]
    tn = 1536
    return pl.pallas_call(
        _ada_kernel,
        grid=(depth, n // tn),
        in_specs=[pl.BlockSpec((rows, d), lambda l, j: (0, 0)),
                  pl.BlockSpec((1, d, tn), lambda l, j: (l, 0, j)),
                  pl.BlockSpec((1, 1, tn), lambda l, j: (l, 0, j))],
        out_specs=pl.BlockSpec((1, rows, tn), lambda l, j: (l, 0, j)),
        out_shape=jax.ShapeDtypeStruct((depth, rows, n), F32),
        compiler_params=_params("arbitrary", "arbitrary"),
        name="ada_mod",
    )(c_all, ada_w, ada_b.reshape(depth, 1, n))


def _prep_w_in(w):
    d = w.shape[0]
    o = 0
    o += NSA_W
    nkv = w[:, o:o + 6 * KV_W]; o += 6 * KV_W
    ng = w[:, o:o + 3 * NSA_HEADS]; o += 3 * NSA_HEADS
    mqkv = w[:, o:o + 3 * MLSTM_W]; o += 3 * MLSTM_W
    mif = w[:, o:o + 2 * MH]; o += 2 * MH
    mo = w[:, o:o + MLSTM_W]; o += MLSTM_W
    pu = w[:, o:o + POOL_W]
    k_cmp, v_cmp, k_slc, v_slc, k_win, v_win = [nkv[:, KV_W * i:KV_W * (i + 1)] for i in range(6)]
    cols = [
        k_cmp, k_slc, v_cmp, v_slc, k_win, v_win, mqkv, mo, pu,
        ng, mif, jnp.zeros((d, LANES - 3 * NSA_HEADS - 2 * MH), w.dtype)]
    out = jnp.concatenate(cols, axis=1).astype(BF16)
    assert out.shape[1] == C_END
    return out


def _pool_mix(pu, sums, cnt, pw_ref, pscale_ref):
    lane = lax.broadcasted_iota(jnp.int32, pu.shape, 1)
    grp = lane // HD
    mean = jnp.where(grp == 0, sums[2] / cnt[2],
                     jnp.where(grp == 1, sums[4] / cnt[4],
                               jnp.where(grp == 2, sums[8] / cnt[8], sums[16] / cnt[16])))
    d = mean - pu
    return _dot(d.astype(BF16), pw_ref[...]) * pscale_ref[...]


def _split2(x):
    hi = x.astype(BF16)
    return hi, (x - hi.astype(F32)).astype(BF16)


def _dot_split(a, b):
    return _dot(a[0], b[0]) + _dot(a[1], b[0]) + _dot(a[0], b[1])


def _inproj_prompt_kernel(x_ref, g_ref, sh_ref, sc_ref, w_ref, wqh_ref, wql_ref, pw_ref, pscale_ref,
                          q_ref, nk_ref, nv_ref, ks_ref, vs_ref, kw_ref, vw_ref, wkv_ref,
                          mqkv_ref, mo_ref, op_ref, sm_ref, ps_ref, zs_ref, *, tm):
    t = pl.program_id(1)
    hf = _norm_mod(x_ref[0], g_ref[...], sc_ref[0], sh_ref[0])
    h, h_lo = _split2(hf)

    def seg(a, b):
        return _dot(h, w_ref[:, a:b])

    q_ref[0] = _dot_split((h, h_lo), (wqh_ref[...], wql_ref[...])) * Q_SCALE
    nk = seg(C_NK, C_NV)
    nv = seg(C_NV, C_WKV)
    wkv = seg(C_WKV, C_MQKV)
    nk_ref[0] = nk
    nv_ref[0] = nv
    wkv_ref[0] = wkv
    kw_ref[0] = wkv[:, :KV_W].astype(BF16)
    blk = (t * tm + lax.broadcasted_iota(jnp.int32, (tm, LANES), 0)) // SLC_BLOCK
    lane = lax.broadcasted_iota(jnp.int32, (tm, LANES), 1)
    low = lane < HD

    def tiles(x, spare0, spare1):
        return (jnp.where(low, x, spare0).astype(BF16), jnp.where(low, spare1, x).astype(BF16))

    k_slc, v_slc, v_win = nk[:, KV_W:], nv[:, KV_W:], wkv[:, KV_W:]
    one0 = jnp.where(lane == HD, 1.0, 0.0)
    one1 = jnp.where(lane == 0, 1.0, 0.0)
    for dst, pair in ((ks_ref, tiles(k_slc, jnp.where(lane - HD == blk, 1.0, 0.0), jnp.where(lane == blk, 1.0, 0.0))),
                      (vs_ref, tiles(v_slc, one0, one1)), (vw_ref, tiles(v_win, one0, one1))):
        dst[0, :, 0:LANES] = pair[0]
        dst[0, :, LANES:2 * LANES] = pair[1]
    mqkv_ref[0] = seg(C_MQKV, C_MO)
    mo_ref[0] = seg(C_MO, C_PU)
    sm_ref[0] = seg(C_SM, C_END)

    pu = seg(C_PU, C_SM)
    halo = 2 * SUBLANES

    @pl.when(t == 0)
    def _():
        zs_ref[0:halo, :] = jnp.zeros((halo, POOL_W), F32)

    @pl.when(t > 0)
    def _():
        zs_ref[0:halo, :] = zs_ref[tm:tm + halo, :]

    zs_ref[halo:halo + tm, :] = pu
    acc = pu
    sums = {}
    for i in range(1, POOL_STATE + 1):
        acc = acc + zs_ref[pl.ds(halo - i, tm), :]
        if i + 1 in POOL_WINDOWS:
            sums[i + 1] = acc
    pos1 = (t * tm + lax.broadcasted_iota(jnp.int32, (tm, 1), 0) + 1).astype(F32)
    cnt = {w: jnp.minimum(float(w), pos1) for w in POOL_WINDOWS}
    op_ref[0] = _pool_mix(pu, sums, cnt, pw_ref, pscale_ref).astype(BF16)
    ps_ref[0] = zs_ref[tm:tm + halo, :]


def _inproj_prompt(x, g1, sh1, sc1, w_b, w_q2, pool_bd, pool_scale, tm=512):
    B, T, D = x.shape
    nT = T // tm
    assert T % tm == 0 and tm >= 2 * SUBLANES and WINDOW % tm == 0
    nwin = WINDOW // tm
    row = lambda w: pl.BlockSpec((1, tm, w), lambda b, t: (b, t, 0))
    mod = pl.BlockSpec((1, 1, D), lambda b, t: (b, 0, 0))
    outs = [
        (row(NSA_W), (B, T, NSA_W), F32),
        (row(256), (B, T, 256), F32),
        (row(256), (B, T, 256), F32),
        (row(256), (B, T, 256), BF16),
        (row(256), (B, T, 256), BF16),
        (row(KV_W), (B, T, KV_W), BF16),
        (row(256), (B, T, 256), BF16),
        (pl.BlockSpec((1, tm, 256), lambda b, t: (b, jnp.maximum(t - (nT - nwin), 0), 0)),
         (B, WINDOW, 256), F32),
        (row(3 * MLSTM_W), (B, T, 3 * MLSTM_W), F32),
        (row(MLSTM_W), (B, T, MLSTM_W), F32),
        (row(POOL_W), (B, T, POOL_W), BF16),
        (row(LANES), (B, T, LANES), F32),
        (pl.BlockSpec((1, 2 * SUBLANES, POOL_W), lambda b, t: (b, 0, 0)), (B, 2 * SUBLANES, POOL_W), F32),
    ]
    return pl.pallas_call(
        functools.partial(_inproj_prompt_kernel, tm=tm),
        grid=(B, nT),
        in_specs=[pl.BlockSpec((1, tm, D), lambda b, t: (b, t, 0)),
                  _const_spec((1, D)), mod, mod,
                  _const_spec(w_b.shape), _const_spec(w_q2[0].shape), _const_spec(w_q2[1].shape),
                  _const_spec(pool_bd.shape), _const_spec((1, POOL_W))],
        out_specs=[o[0] for o in outs],
        out_shape=[jax.ShapeDtypeStruct(o[1], o[2]) for o in outs],
        scratch_shapes=[pltpu.VMEM((2 * SUBLANES + tm, POOL_W), F32)],
        compiler_params=_params("arbitrary", "arbitrary"),
        name="inproj_prompt",
    )(x, g1, sh1, sc1, w_b, w_q2[0], w_q2[1], pool_bd, pool_scale)


def _compress_prompt_kernel(k_ref, v_ref, wpk_ref, wpv_ref, wk_ref, wkl_ref, wv_ref, bk_ref, bv_ref,
                            kc_ref, vc_ref, *, nseg):
    def pooled(src_ref, wp_ref):
        a = jnp.zeros((nseg, KV_W), F32)
        b = jnp.zeros((nseg, KV_W), F32)
        for j in range(CMP_STRIDE):
            xj = src_ref[0, pl.ds(j, nseg, stride=CMP_STRIDE), :]
            a = a + xj * wp_ref[j:j + 1, :]
            b = b + xj * wp_ref[CMP_STRIDE + j:CMP_STRIDE + j + 1, :]
        return a + pltpu.roll(b, nseg - 1, 0)

    kc_ref[0] = _dot_split(_split2(pooled(k_ref, wpk_ref)), (wk_ref[...], wkl_ref[...])) + bk_ref[...]
    vc_ref[0] = (_dot(pooled(v_ref, wpv_ref).astype(BF16), wv_ref[...]) + bv_ref[...]).astype(BF16)


def _compress_prompt(nk, nv, cw):
    B, T, _ = nk.shape
    nseg = T // CMP_STRIDE
    src = pl.BlockSpec((1, T, KV_W), lambda b: (b, 0, 0))
    return pl.pallas_call(
        functools.partial(_compress_prompt_kernel, nseg=nseg),
        grid=(B,),
        in_specs=[src, src, _const_spec((CMP_LEN, KV_W)), _const_spec((CMP_LEN, KV_W)),
                  _const_spec((KV_W, KV_W)), _const_spec((KV_W, KV_W)), _const_spec((KV_W, 2 * KV_W)),
                  _const_spec((1, KV_W)), _const_spec((1, 2 * KV_W))],
        out_specs=[pl.BlockSpec((1, nseg, KV_W), lambda b: (b, 0, 0)),
                   pl.BlockSpec((1, nseg, 2 * KV_W), lambda b: (b, 0, 0))],
        out_shape=[jax.ShapeDtypeStruct((B, nseg, KV_W), F32),
                   jax.ShapeDtypeStruct((B, nseg, 2 * KV_W), BF16)],
        compiler_params=_params("arbitrary"),
        name="compress_prompt",
    )(nk, nv, cw["wpk"], cw["wpv"], cw["wk"], cw["wk_lo"], cw["wv_dup"], cw["bk"], cw["bv_dup"])


def _prep_compress(cmp_pos, cmp_w, cmp_b, page):
    tile2 = lambda t: jnp.concatenate([t, t], axis=-1)
    z = jnp.zeros((HD, HD), F32)
    wk, wv = cmp_w[0], cmp_w[1]
    wk_bd = jnp.block([[wk, z], [z, wk]])
    wv_dup = jnp.block([[wv, z, z, z], [z, z, z, wv]])

    def pos_t(p):
        reps = page // CMP_STRIDE
        return jnp.concatenate([jnp.tile(p[:CMP_STRIDE].T, (1, reps)), jnp.tile(p[CMP_STRIDE:].T, (1, reps))], axis=0)

    return dict(
        wpk=tile2(cmp_pos[0]), wpv=tile2(cmp_pos[1]),
        wk=wk_bd.astype(BF16), wk_lo=(wk_bd - wk_bd.astype(BF16).astype(F32)).astype(BF16),
        wv_dup=wv_dup.astype(BF16),
        bk=tile2(cmp_b[0])[None, :],
        bv_dup=jnp.concatenate([cmp_b[1], jnp.zeros((2 * HD,), F32), cmp_b[1]])[None, :],
        wpk_t=pos_t(cmp_pos[0]), wpv_t=pos_t(cmp_pos[1]),
        wpk_b0=cmp_pos[0][CMP_STRIDE][:, None], wpv_b0=cmp_pos[1][CMP_STRIDE][:, None],
        wk_t=wk.T.astype(BF16), wv_t=wv.T.astype(BF16),
        bk_col=cmp_b[0][:, None], bv_col=cmp_b[1][:, None])


def _importance_matrix(nc_rows, nc_valid, ns, lane_off, width):
    m = np.zeros((nc_rows, width), np.float32)
    for j in range(ns):
        for n in range(SLC_RATIO * j - 1, SLC_RATIO * j + SLC_RATIO):
            if 0 <= n < nc_valid:
                m[n, lane_off + j] = 1.0
    return m


def _select_blocks_t(st_ref, h, ns):
    nb, nq = st_ref.shape[1], st_ref.shape[2]
    ngrp = nb // SUBLANES
    groups = [st_ref[h, SUBLANES * r:SUBLANES * (r + 1), :] for r in range(ngrp)]
    ranks = [jnp.zeros((SUBLANES, nq), F32) for _ in range(ngrp)]
    jrow = lax.broadcasted_iota(jnp.int32, (SUBLANES, nq), 0)
    for jp in range(min(ns, nb)):
        row = jnp.broadcast_to(st_ref[h, pl.ds(jp, 1), :], (SUBLANES, nq))
        for r in range(ngrp):
            ge = jnp.where(row >= groups[r], 1.0, 0.0)
            gt = jnp.where(row > groups[r], 1.0, 0.0)
            if jp < SUBLANES * r:
                inc = ge
            elif jp >= SUBLANES * (r + 1):
                inc = gt
            else:
                inc = jnp.where(jrow + SUBLANES * r > jp, ge, gt)
            ranks[r] = ranks[r] + inc
    return jnp.concatenate(
        [jnp.where((ranks[r] < N_SEL) & (groups[r] >= 0.0), 0.0, NEG) for r in range(ngrp)], axis=0)


def _nsa_prompt_kernel(q_ref, sm_ref, kc_ref, vc_ref, ks_ref, vs_ref, kw_ref, vw_ref, mt_ref,
                       o_ref, acc_ref, m_ref, qa_ref, st_ref, sa_ref, sb_ref, *, T, tk, qb):
    i = pl.program_id(1)
    ns = T // SLC_BLOCK
    nc = kc_ref.shape[1]
    rows = GRP * qb
    qpos = i * qb + lax.broadcasted_iota(jnp.int32, (qb, 1), 0)
    qpos4 = jnp.concatenate([qpos] * GRP, axis=0)
    qpos_t = i * qb + lax.broadcasted_iota(jnp.int32, (HD, qb), 1)
    jb_t = lax.broadcasted_iota(jnp.int32, (HD, qb), 0)
    sig = jax.nn.sigmoid(sm_ref[0])
    lane = lax.broadcasted_iota(jnp.int32, (qb, LANES), 1)
    wk = WINDOW + qb
    n_idx = lax.broadcasted_iota(jnp.int32, (1, nc), 1)
    cmp_mask = (CMP_STRIDE * n_idx + CMP_LEN - 1) <= qpos4
    wstart = pl.multiple_of(jnp.maximum(i - WINDOW // qb, 0) * qb, qb)
    dlt = qpos - (wstart + lax.broadcasted_iota(jnp.int32, (1, wk), 1))
    wbias = jnp.where((dlt >= 0) & (dlt < WINDOW), 0.0, NEG)
    cur_t = qpos_t // SLC_BLOCK
    valid_t = (jb_t * SLC_BLOCK <= qpos_t) & (jb_t < ns)
    forced_t = (jb_t == 0) | (jb_t == cur_t) | (jb_t == cur_t - 1)
    zpad = jnp.zeros((LANES - HD, qb), F32)
    kc_hi, kc_lo = _split2(kc_ref[0])
    sum_lane = (HD, 0)

    o_cs, o_ws = [], []
    for h in range(HKV):
        parts = []
        for g in range(GRP):
            hd = GRP * h + g
            src = q_ref[0, :, LANES * (hd // 2):LANES * (hd // 2 + 1)]
            if hd % 2 != h:
                src = pltpu.roll(src, HD, 1)
            parts.append(jnp.where((lane < HD) if h == 0 else (lane >= HD), src, 0.0))
        qh, qh_lo = _split2(jnp.concatenate(parts, axis=0))

        p_c = _masked_softmax(_dot_nt(qh, kc_hi) + _dot_nt(qh_lo, kc_hi) + _dot_nt(qh, kc_lo), cmp_mask)
        o_cs.append(_dot(p_c.astype(BF16), vc_ref[0, :, LANES * h:LANES * (h + 1)]))

        psum = p_c[0:qb] + p_c[qb:2 * qb] + p_c[2 * qb:3 * qb] + p_c[3 * qb:4 * qb]
        x1 = psum.astype(BF16)
        r1 = psum - x1.astype(F32)
        x2 = r1.astype(BF16)
        x3 = (r1 - x2.astype(F32)).astype(BF16)
        mt = mt_ref[...]
        imp_t = _dot_nt(mt, x1) + _dot_nt(mt, x2) + _dot_nt(mt, x3)
        score_t = jnp.where(valid_t, jnp.where(forced_t, FORCE, imp_t), -1.0)
        st_ref[h] = jnp.where(jb_t < ns, score_t, -jnp.inf)
        sel_t = _select_blocks_t(st_ref, h, ns)
        selneg = (jnp.concatenate([zpad, sel_t], axis=0) if h == 0
                  else jnp.concatenate([sel_t, zpad], axis=0)).T
        qa_ref[h] = qh + jnp.concatenate([selneg.astype(BF16)] * GRP, axis=0)

        sw = _dot_nt(qh, kw_ref[0, pl.ds(wstart, wk), :])
        sw = jnp.concatenate([sw[qb * g:qb * (g + 1)] + wbias for g in range(GRP)], axis=0)
        e = jnp.exp(sw - jnp.max(sw, axis=-1, keepdims=True))
        o_w = _dot(e.astype(BF16), vw_ref[0, pl.ds(wstart, wk), LANES * h:LANES * (h + 1)])
        o_ws.append(o_w / o_w[:, sum_lane[h]:sum_lane[h] + 1])

    m_ref[...] = jnp.full(m_ref.shape, -jnp.inf, F32)
    acc_ref[...] = jnp.zeros(acc_ref.shape, F32)

    def scores(c, s_ref):
        start = pl.multiple_of(c * tk, tk)
        for h in range(HKV):
            s_ref[h] = _dot_nt(qa_ref[h], ks_ref[0, pl.ds(start, tk), LANES * h:LANES * (h + 1)])

    def accumulate(c, s_ref, diagonal):
        start = pl.multiple_of(c * tk, tk)
        for h in range(HKV):
            sc = s_ref[h]
            if diagonal:
                tok = start + lax.broadcasted_iota(jnp.int32, (1, tk), 1)
                sc = jnp.where(tok <= qpos4, sc, NEG)
            m_prev = m_ref[h]
            m_new = jnp.maximum(m_prev, jnp.max(sc, axis=-1, keepdims=True))
            alpha = jnp.exp(m_prev - m_new)
            p = jnp.exp(sc - m_new[:, 0:1])
            v = vs_ref[0, pl.ds(start, tk), LANES * h:LANES * (h + 1)]
            acc_ref[h] = alpha * acc_ref[h] + _dot(p.astype(BF16), v)
            m_ref[h] = m_new

    n_full = (i * qb) // tk
    scores(0, sa_ref)

    def chunk_pair(cc, carry):
        c0 = 2 * cc
        scores(c0 + 1, sb_ref)
        accumulate(c0, sa_ref, False)
        scores(c0 + 2, sa_ref)
        accumulate(c0 + 1, sb_ref, False)
        return carry

    lax.fori_loop(0, n_full // 2, chunk_pair, 0)

    @pl.when(n_full % 2 == 0)
    def _():
        accumulate(n_full, sa_ref, True)

    @pl.when(n_full % 2 == 1)
    def _():
        scores(n_full, sb_ref)
        accumulate(n_full - 1, sa_ref, False)
        accumulate(n_full, sb_ref, True)

    for h in range(HKV):
        acc = acc_ref[h]
        o_s = acc / acc[:, sum_lane[h]:sum_lane[h] + 1]
        outs = []
        for g in range(GRP):
            c0 = 3 * (GRP * h + g)
            r = slice(qb * g, qb * (g + 1))
            outs.append(sig[:, c0:c0 + 1] * o_cs[h][r] + sig[:, c0 + 1:c0 + 2] * o_s[r]
                        + sig[:, c0 + 2:c0 + 3] * o_ws[h][r])
        base = GRP * HD * h
        for pair in range(GRP // 2):
            lo, hi = outs[2 * pair], outs[2 * pair + 1]
            if h == 0:
                hi = pltpu.roll(hi, HD, 1)
            else:
                lo = pltpu.roll(lo, HD, 1)
            o_ref[0, :, base + LANES * pair:base + LANES * (pair + 1)] = jnp.where(lane < HD, lo, hi).astype(BF16)


def _nsa_prompt(q, sm, kc, vc, ks, vs, kw, vw, qb=256, tk=512):
    B, T, _ = q.shape
    ns = T // SLC_BLOCK
    nc = kc.shape[1]
    assert ns <= HD and T % tk == 0 and T >= WINDOW + qb and WINDOW % qb == 0 and tk % qb == 0
    m2 = _importance_matrix(nc, nc - 1, ns, 0, HD).T
    full = lambda w: pl.BlockSpec((1, T, w), lambda b, i: (b, 0, 0))
    return pl.pallas_call(
        functools.partial(_nsa_prompt_kernel, T=T, tk=tk, qb=qb),
        grid=(B, T // qb),
        in_specs=[pl.BlockSpec((1, qb, NSA_W), lambda b, i: (b, i, 0)),
                  pl.BlockSpec((1, qb, LANES), lambda b, i: (b, i, 0)),
                  pl.BlockSpec((1, nc, KV_W), lambda b, i: (b, 0, 0)),
                  pl.BlockSpec((1, nc, 2 * KV_W), lambda b, i: (b, 0, 0)),
                  full(256), full(256), full(KV_W), full(256),
                  _const_spec(m2.shape)],
        out_specs=pl.BlockSpec((1, qb, NSA_W), lambda b, i: (b, i, 0)),
        out_shape=jax.ShapeDtypeStruct((B, T, NSA_W), BF16),
        scratch_shapes=[pltpu.VMEM((HKV, GRP * qb, LANES), F32)] * 2
                       + [pltpu.VMEM((HKV, GRP * qb, LANES), BF16), pltpu.VMEM((HKV, HD, qb), F32)]
                       + [pltpu.VMEM((HKV, GRP * qb, tk), F32)] * 2,
        compiler_params=_params("arbitrary", "arbitrary"),
        name="nsa_prompt",
    )(q, sm, kc, vc, ks, vs, kw, vw, jnp.asarray(m2, BF16))


def _mlstm_gates(g):
    b = pltpu.roll(_cumsum_rows(_log_sigmoid(g)), LANES - MH, 1)
    return b, g - b


def _mlstm_prompt_kernel(mqkv_ref, sm_ref, mo_ref, gb_ref, om_ref, c_out, n_out, m_out,
                         c_s, n_s, m_s, *, B, L):
    t = pl.program_id(0)

    @pl.when(t == 0)
    def _():
        c_s[...] = jnp.zeros(c_s.shape, F32)
        n_s[...] = jnp.zeros(n_s.shape, F32)
        m_s[...] = jnp.zeros(m_s.shape, F32)

    li = lax.broadcasted_iota(jnp.int32, (L, L), 0)
    si = lax.broadcasted_iota(jnp.int32, (L, L), 1)
    for b in range(B):
        b_al, r = _mlstm_gates(sm_ref[b] + gb_ref[...])
        r_t = jnp.concatenate([r, jnp.zeros((LANES - L, LANES), F32)], axis=0).T if L < LANES else r.T
        heads = []
        for hd in range(MH):
            idx = b * MH + hd
            bcol = b_al[:, SM_IG + hd:SM_IG + hd + 1]
            rcol = r[:, SM_IG + hd:SM_IG + hd + 1]
            rrow = r_t[SM_IG + hd:SM_IG + hd + 1, 0:L]
            mprev = m_s[idx][:, 0:1]
            acol = bcol + mprev
            logd = jnp.where(si <= li, bcol + rrow, NEG)
            mt = jnp.maximum(acol, jnp.max(logd, axis=-1, keepdims=True))
            dm = jnp.exp(logd - mt)
            inter = jnp.exp(acol - mt)
            q = mqkv_ref[b, :, HD * hd:HD * (hd + 1)]
            k = mqkv_ref[b, :, MLSTM_W + HD * hd:MLSTM_W + HD * (hd + 1)] * Q_SCALE
            v = mqkv_ref[b, :, 2 * MLSTM_W + HD * hd:2 * MLSTM_W + HD * (hd + 1)]
            qb, kb = q.astype(BF16), k.astype(BF16)
            qk = _dot_nt(qb, kb) * dm
            c_prev = c_s[idx]
            n_prev = n_s[idx]
            num = _dot(qk.astype(BF16), v.astype(BF16)) + inter * _dot(qb, c_prev.astype(BF16))
            den = jnp.sum(qk, axis=-1, keepdims=True) + inter * jnp.sum(q * n_prev, axis=-1, keepdims=True)
            hout = num / jnp.maximum(jnp.abs(den), jnp.exp(-mt))
            heads.append(jax.nn.sigmoid(mo_ref[b, :, HD * hd:HD * (hd + 1)]) * hout)
            m_last = mt[L - 1:L, :]
            b_last = bcol[L - 1:L, :]
            wcol = jnp.exp(b_last + rcol - m_last)
            decay = jnp.exp(b_last + mprev - m_last)
            c_s[idx] = decay * c_prev + _dot_tn(kb, (wcol * v).astype(BF16))
            n_s[idx] = decay * n_prev + jnp.sum(wcol * k, axis=0, keepdims=True)
            m_s[idx] = jnp.broadcast_to(m_last, (1, LANES))
        om_ref[b] = jnp.concatenate(heads, axis=-1).astype(BF16)

    @pl.when(t == pl.num_programs(0) - 1)
    def _():
        c_out[...] = c_s[...]
        n_out[...] = n_s[...]
        m_out[...] = m_s[...]


def _mlstm_prompt(mqkv, sm, mo, gate_b_tile, L=256):
    B, T, _ = mqkv.shape
    assert T % L == 0 and (L <= LANES or L % LANES == 0)
    blk = lambda w: pl.BlockSpec((B, L, w), lambda t: (0, t, 0))
    st = lambda shape: pl.BlockSpec(shape, lambda t: (0,) * len(shape))
    shapes = [(B * MH, HD, HD), (B * MH, 1, HD), (B * MH, 1, LANES)]
    return pl.pallas_call(
        functools.partial(_mlstm_prompt_kernel, B=B, L=L),
        grid=(T // L,),
        in_specs=[blk(3 * MLSTM_W), blk(LANES), blk(MLSTM_W), _const_spec((1, LANES))],
        out_specs=[blk(MLSTM_W)] + [st(s) for s in shapes],
        out_shape=[jax.ShapeDtypeStruct((B, T, MLSTM_W), BF16)] + [jax.ShapeDtypeStruct(s, F32) for s in shapes],
        scratch_shapes=[pltpu.VMEM(s, F32) for s in shapes],
        compiler_params=_params("arbitrary"),
        name="mlstm_prompt",
    )(mqkv, sm, mo, gate_b_tile)


FF_CHUNK = 2816


def _outffn_prompt_kernel(x_ref, on_ref, om_ref, op_ref, gt1_ref, sh2_ref, sc2_ref, gt2_ref,
                          g2_ref, fg_ref, wout_ref, wup_ref, cw_ref, cb_ref, wdn_ref,
                          y_ref, fs_ref, prev_ref, es_ref, *, tm, d_ff, final):
    t = pl.program_id(1)
    mix = (_dot(on_ref[0], wout_ref[0:NSA_W, :])
           + _dot(om_ref[0], wout_ref[NSA_W:NSA_W + MLSTM_W, :])
           + _dot(op_ref[0], wout_ref[NSA_W + MLSTM_W:, :]))
    x1 = x_ref[0] + gt1_ref[0] * mix
    h2 = _norm_mod(x1, g2_ref[...], sc2_ref[0], sh2_ref[0]).astype(BF16)

    @pl.when(t == 0)
    def _():
        prev_ref[...] = jnp.zeros(prev_ref.shape, F32)

    f = jnp.zeros(x1.shape, F32)
    w = FF_CHUNK
    for c in range(d_ff // w):
        for half, off in ((0, c * w), (1, d_ff + c * w)):
            es_ref[0:SUBLANES, half * w:(half + 1) * w] = prev_ref[:, off:off + w]
            es_ref[SUBLANES:SUBLANES + tm, half * w:(half + 1) * w] = _dot(h2, wup_ref[:, off:off + w])
            prev_ref[:, off:off + w] = es_ref[tm:tm + SUBLANES, half * w:(half + 1) * w]
        ys = []
        for half, off in ((0, c * w), (1, d_ff + c * w)):
            y = cb_ref[:, off:off + w]
            for j in range(FFN_CONV):
                y = y + cw_ref[j:j + 1, off:off + w] * es_ref[pl.ds(SUBLANES - (FFN_CONV - 1) + j, tm), half * w:(half + 1) * w]
            ys.append(y)
        act = ys[0] * jax.nn.sigmoid(ys[0]) * ys[1]
        f = f + _dot(act.astype(BF16), wdn_ref[c * w:(c + 1) * w, :])
    x2 = x1 + gt2_ref[0] * f
    y_ref[0] = _rmsnorm(x2, fg_ref[...]) if final else x2
    fs_ref[0] = prev_ref[...]


def _outffn_prompt(x, on, om, op, gt1, sh2, sc2, gt2, g2, fg, w_out, w_up, conv_w, conv_b, w_dn, final, tm=512):
    B, T, D = x.shape
    d_ff = w_dn.shape[0]
    assert T % tm == 0 and d_ff % FF_CHUNK == 0
    row = lambda w: pl.BlockSpec((1, tm, w), lambda b, t: (b, t, 0))
    mod = pl.BlockSpec((1, 1, D), lambda b, t: (b, 0, 0))
    return pl.pallas_call(
        functools.partial(_outffn_prompt_kernel, tm=tm, d_ff=d_ff, final=final),
        grid=(B, T // tm),
        in_specs=[row(D), row(NSA_W), row(MLSTM_W), row(POOL_W), mod, mod, mod, mod,
                  _const_spec((1, D)), _const_spec((1, D)), _const_spec(w_out.shape), _const_spec(w_up.shape),
                  _const_spec(conv_w.shape), _const_spec(conv_b.shape), _const_spec(w_dn.shape)],
        out_specs=[row(D), pl.BlockSpec((1, SUBLANES, 2 * d_ff), lambda b, t: (b, 0, 0))],
        out_shape=[jax.ShapeDtypeStruct((B, T, D), F32), jax.ShapeDtypeStruct((B, SUBLANES, 2 * d_ff), F32)],
        scratch_shapes=[pltpu.VMEM((SUBLANES, 2 * d_ff), F32), pltpu.VMEM((SUBLANES + tm, 2 * FF_CHUNK), F32)],
        compiler_params=_params("arbitrary", "arbitrary"),
        name="outffn_prompt",
    )(x, on, om, op, gt1, sh2, sc2, gt2, g2, fg, w_out, w_up, conv_w, conv_b, w_dn)


def _prep_layer(l, w, page):
    gate_b = jnp.zeros((1, LANES), F32).at[0, SM_IG:SM_IG + 2 * MH].set(w["mlstm_gate_b"][l])
    pw = w["pool_w"][l]
    z = jnp.zeros((HD, HD), F32)
    pool_bd = jnp.block([[pw[i] if i == j else z for j in range(4)] for i in range(4)]).astype(BF16)
    wq = w["w_in"][l][:, :NSA_W]
    wq_hi = wq.astype(BF16)
    return dict(
        w_q=wq_hi, w_q2=(wq_hi, (wq - wq_hi.astype(F32)).astype(BF16)),
        g1=w["norm1_g"][l][None], g2=w["norm2_g"][l][None],
        w_b=_prep_w_in(w["w_in"][l]), pool_bd=pool_bd, pool_scale=w["pool_scale"][l][None],
        cw=_prep_compress(w["nsa_cmp_pos"][l], w["nsa_cmp_w"][l], w["nsa_cmp_b"][l], page),
        gate_b=gate_b, w_out=w["w_out"][l].astype(BF16), w_up=w["ffn_w_up"][l].astype(BF16),
        conv_w=w["ffn_conv_w"][l], conv_b=w["ffn_conv_b"][l][None], w_dn=w["ffn_w_down"][l].astype(BF16),
        fg=w["final_g"][None])


def _prompt_layer(x, mod, lw, final):
    B, T, D = x.shape
    sh1, sc1, gt1, sh2, sc2, gt2 = [mod[:, :, D * i:D * (i + 1)] for i in range(6)]
    (q, nk, nv, ks, vs, kw, vw, wkv, mqkv, mo, op, sm, ps) = _inproj_prompt(
        x, lw["g1"], sh1, sc1, lw["w_b"], lw["w_q2"], lw["pool_bd"], lw["pool_scale"])
    kc, vc = _compress_prompt(nk, nv, lw["cw"])
    on = _nsa_prompt(q, sm, kc, vc, ks, vs, kw, vw)
    om, c_st, n_st, m_st = _mlstm_prompt(mqkv, sm, mo, lw["gate_b"])
    y, fs = _outffn_prompt(x, on, om, op, gt1, sh2, sc2, gt2, lw["g2"], lw["fg"], lw["w_out"],
                           lw["w_up"], lw["conv_w"], lw["conv_b"], lw["w_dn"], final)
    wb = wkv.shape[1]
    states = (nk.reshape(B, T, 2, HKV, HD), nv.reshape(B, T, 2, HKV, HD),
              wkv[:, :, :KV_W].reshape(B, wb, HKV, HD), wkv[:, :, KV_W:].reshape(B, wb, HKV, HD),
              c_st.reshape(B, MH, HD, HD), n_st.reshape(B, MH, HD), m_st[:, 0, 0].reshape(B, MH),
              ps[:, 2 * SUBLANES - POOL_STATE:], fs[:, SUBLANES - (FFN_CONV - 1):])
    return y, states


def _inproj_sample_kernel(x_ref, g_ref, sh_ref, sc_ref, w_ref, wq_ref, pw_ref, pscale_ref, prev_ref,
                          q_ref, nk_ref, nv_ref, wkv_ref, mqkv_ref, mo_ref, op_ref, sm_ref, pu_ref, *, pos0):
    h = _norm_mod(x_ref[...], g_ref[...], sc_ref[...], sh_ref[...]).astype(BF16)

    def seg(a, b):
        return _dot(h, w_ref[:, a:b])

    q_ref[...] = _dot(h, wq_ref[...]) * Q_SCALE
    nk_ref[...] = seg(C_NK, C_NV)
    nv_ref[...] = seg(C_NV, C_WKV)
    wkv_ref[...] = seg(C_WKV, C_MQKV)
    mqkv_ref[...] = seg(C_MQKV, C_MO)
    mo_ref[...] = seg(C_MO, C_PU)
    sm_ref[...] = seg(C_SM, C_END)
    pu = seg(C_PU, C_SM)
    pu_ref[...] = pu
    acc = pu
    sums = {}
    for i in range(1, POOL_STATE + 1):
        acc = acc + prev_ref[:, POOL_STATE - i, :]
        if i + 1 in POOL_WINDOWS:
            sums[i + 1] = acc
    cnt = {w: float(min(w, pos0 + 1)) for w in POOL_WINDOWS}
    op_ref[...] = _pool_mix(pu, sums, cnt, pw_ref, pscale_ref)


def _inproj_sample(l, x, g1, sh1, sc1, w_b, w_q, pool_bd, pool_scale, state_pool, pos0):
    DB, D = x.shape
    full = lambda shape: pl.BlockSpec(shape, lambda i: (0,) * len(shape))
    widths = [NSA_W, 256, 256, 256, 3 * MLSTM_W, MLSTM_W, POOL_W, LANES, POOL_W]
    return pl.pallas_call(
        functools.partial(_inproj_sample_kernel, pos0=pos0),
        grid=(1,),
        in_specs=[full((DB, D)), full((1, D)), full((DB, D)), full((DB, D)), full(w_b.shape),
                  full(w_q.shape), full(pool_bd.shape), full((1, POOL_W)),
                  pl.BlockSpec((None, DB, POOL_STATE, POOL_W), lambda i: (l, 0, 0, 0))],
        out_specs=[full((DB, w)) for w in widths],
        out_shape=[jax.ShapeDtypeStruct((DB, w), F32) for w in widths],
        compiler_params=_params("arbitrary"),
        name="inproj_sample",
    )(x, g1, sh1, sc1, w_b, w_q, pool_bd, pool_scale, state_pool)


PAGES_PER_STEP = 32


def _row_to_col(rowv):
    n = rowv.shape[1]
    eye = lax.broadcasted_iota(jnp.int32, (n, n), 0) == lax.broadcasted_iota(jnp.int32, (n, n), 1)
    return jnp.sum(jnp.where(eye, jnp.broadcast_to(rowv, (n, n)), 0.0), axis=1, keepdims=True)


def _compress_sample_t_kernel(pt_ref, ck_ref, cv_ref, wk_ref, wv_ref, sg_ref, ok_ref, ov_ref,
                              kbuf, vbuf, sem, *, l, npg, nc):
    step = pl.program_id(0) * nc + pl.program_id(1)
    nsteps = pl.num_programs(0) * nc
    slot = step % 2

    def copies(s, sl):
        b, c = s // nc, s % nc
        out = []
        for i in range(npg):
            p = pt_ref[b, c * npg + i]
            out.append(pltpu.make_async_copy(ck_ref.at[l, p, 0], kbuf.at[sl, i], sem.at[0, sl]))
            out.append(pltpu.make_async_copy(cv_ref.at[l, p, 0], vbuf.at[sl, i], sem.at[1, sl]))
        return out

    @pl.when(step == 0)
    def _():
        for cp in copies(step, slot):
            cp.start()

    @pl.when(step + 1 < nsteps)
    def _():
        for cp in copies(step + 1, 1 - slot):
            cp.start()

    for cp in copies(step, slot):
        cp.wait()

    for buf, w_ref, o_ref in ((kbuf, wk_ref, ok_ref), (vbuf, wv_ref, ov_ref)):
        wa = w_ref[0:HD, :]
        wb = w_ref[HD:2 * HD, :]
        for h in range(HKV):
            prods = []
            for i in range(npg):
                x = buf[slot, i, h]
                prods.append(jnp.concatenate([x * wa, x * wb], axis=0).astype(BF16))
            o_ref[0, h] = _dot(jnp.concatenate(prods, axis=1), sg_ref[...])


def _compress_sample_t(l, cache_kt, cache_vt, page_table, cw):
    DB, n_pages = page_table.shape
    page = cache_kt.shape[-1]
    npg = min(PAGES_PER_STEP, n_pages)
    assert n_pages % npg == 0 and page % CMP_STRIDE == 0
    spp = page // CMP_STRIDE
    nseg = n_pages * spp
    assert (npg * spp) % LANES == 0 or npg * spp == nseg
    sg = np.zeros((npg * page, npg * spp), np.float32)
    sg[np.arange(npg * page), np.arange(npg * page) // CMP_STRIDE] = 1.0

    nc = n_pages // npg
    out = pl.BlockSpec((1, HKV, 2 * HD, npg * spp), lambda b, c, pt: (b, 0, 0, c))
    cst = lambda shape: pl.BlockSpec(shape, lambda b, c, pt: (0,) * len(shape))
    hbm = pl.BlockSpec(memory_space=pl.ANY)
    grid_spec = pltpu.PrefetchScalarGridSpec(
        num_scalar_prefetch=1, grid=(DB, nc),
        in_specs=[hbm, hbm, cst((2 * HD, page)), cst((2 * HD, page)), cst(sg.shape)],
        out_specs=[out] * 2,
        scratch_shapes=[pltpu.VMEM((2, npg, HKV, HD, page), F32), pltpu.VMEM((2, npg, HKV, HD, page), F32),
                        pltpu.SemaphoreType.DMA((2, 2))])
    return pl.pallas_call(
        functools.partial(_compress_sample_t_kernel, l=l, npg=npg, nc=nc),
        grid_spec=grid_spec,
        out_shape=[jax.ShapeDtypeStruct((DB, HKV, 2 * HD, nseg), F32)] * 2,
        compiler_params=_params("arbitrary", "arbitrary"),
        name="compress_sample",
    )(page_table, cache_kt, cache_vt, cw["wpk_t"], cw["wpv_t"], jnp.asarray(sg, BF16))


def _nsa_sample_a_t_kernel(q_ref, hk_ref, hv_ref, nk_ref, nv_ref, wkv_ref, wink_ref, winv_ref,
                           wpk_ref, wpv_ref, wk_ref, wv_ref, bk_ref, bv_ref, m_ref,
                           idx_ref, ocw_ref, wko_ref, wvo_ref, *, past, ns):
    ncs = hk_ref.shape[3]
    wb = wink_ref.shape[3]
    q8 = q_ref[0].astype(BF16)
    row = lax.broadcasted_iota(jnp.int32, (NSA_HEADS, 1), 0)
    lane_c = lax.broadcasted_iota(jnp.int32, (HD, ncs), 1)
    lane_w = lax.broadcasted_iota(jnp.int32, (HD, wb), 1)
    n_idx = lax.broadcasted_iota(jnp.int32, (1, ncs), 1)

    def per_head(fn):
        a, b = fn(0), fn(1)
        return jnp.where(row // GRP == 0, a, b)

    def comp_t(h_ref, new_row, wp_ref, w_ref, bias_ref, h):
        at = h_ref[0, h, 0:HD, :]
        bt = h_ref[0, h, HD:2 * HD, :]
        b_new = _row_to_col(new_row[:, HD * h:HD * (h + 1)]) * wp_ref[...]
        pooled = at + jnp.where(lane_c == ncs - 1, b_new, pltpu.roll(bt, ncs - 1, 1))
        return (_dot(w_ref[...], pooled.astype(BF16)) + bias_ref[...]).astype(BF16)

    kct = [comp_t(hk_ref, nk_ref[0], wpk_ref, wk_ref, bk_ref, h) for h in range(HKV)]
    vct = [comp_t(hv_ref, nv_ref[0], wpv_ref, wv_ref, bv_ref, h) for h in range(HKV)]
    s = per_head(lambda h: _dot(q8, kct[h]))
    p = _masked_softmax(s, (CMP_STRIDE * n_idx + CMP_LEN - 1) <= past)
    pb = p.astype(BF16)
    o_c = per_head(lambda h: _dot_nt(pb, vct[h]))

    rowi = lax.broadcasted_iota(jnp.int32, p.shape, 0)
    p0 = jnp.sum(p[0:GRP], axis=0, keepdims=True)
    p1 = jnp.sum(p[GRP:2 * GRP], axis=0, keepdims=True)
    imp = _dot_split3(jnp.where(rowi == 0, p0, jnp.where(rowi == 1, p1, 0.0)), m_ref[...])
    lane = lax.broadcasted_iota(jnp.int32, imp.shape, 1)
    cur = past // SLC_BLOCK
    valid = lane * SLC_BLOCK <= past
    forced = (lane == 0) | (lane == cur) | (lane == cur - 1)
    score = jnp.where(valid, jnp.where(forced, FORCE, imp), -1.0)
    score = jnp.where(lane < ns, score, -jnp.inf)
    lane_o = lax.broadcasted_iota(jnp.int32, (SUBLANES, LANES), 1)
    idx = jnp.full((SUBLANES, LANES), -1, jnp.int32)
    for r in range(min(N_SEL, ns)):
        mx = jnp.max(score, axis=-1, keepdims=True)
        first = jnp.min(jnp.where(score == mx, lane, 1 << 20), axis=-1, keepdims=True)
        idx = jnp.where(lane_o == r, jnp.where(mx >= 0.0, first, -1), idx)
        score = jnp.where(lane == first, -jnp.inf, score)
    idx_ref[0] = idx

    sel_head = lambda t: jnp.where(row // GRP == 0, t[:, :HD], t[:, HD:])
    k_new = wkv_ref[0][:, :KV_W]
    v_new = wkv_ref[0][:, KV_W:]
    s_w = per_head(lambda h: _dot(q8, wink_ref[0, h].astype(BF16)))
    dlt = wb - lax.broadcasted_iota(jnp.int32, (1, wb), 1)
    mask = (dlt < WINDOW) & (past - dlt >= 0)
    s_new = jnp.sum(q8.astype(F32) * sel_head(k_new).astype(BF16).astype(F32), axis=-1, keepdims=True)
    s_w = jnp.where(mask, s_w, NEG)
    mx = jnp.maximum(jnp.max(s_w, axis=-1, keepdims=True), s_new)
    e = jnp.where(mask, jnp.exp(s_w - mx), 0.0)
    e_new = jnp.exp(s_new - mx)
    den = jnp.sum(e, axis=-1, keepdims=True) + e_new
    eb = e.astype(BF16)
    o_w = per_head(lambda h: _dot_nt(eb, winv_ref[0, h].astype(BF16)))
    o_w = (o_w + e_new * sel_head(v_new).astype(BF16).astype(F32)) / den
    ocw_ref[0] = jnp.concatenate([o_c, o_w], axis=-1)
    for h in range(HKV):
        for src, new, dst in ((wink_ref, k_new, wko_ref), (winv_ref, v_new, wvo_ref)):
            col = _row_to_col(new[:, HD * h:HD * (h + 1)])
            dst[0, h] = jnp.where(lane_w == wb - 1, col, pltpu.roll(src[0, h], wb - 1, 1))


NSA_A_SEQS = 4
N_SEQ_IN = 8
N_SEQ_OUT = 4


def _nsa_sample_a_t_multi(*refs, nsq, past, ns):
    for sq in range(nsq):
        one = lambda r: r.at[pl.ds(sq, 1)]
        _nsa_sample_a_t_kernel(*[one(r) for r in refs[:N_SEQ_IN]], *refs[N_SEQ_IN:-N_SEQ_OUT],
                               *[one(r) for r in refs[-N_SEQ_OUT:]], past=past, ns=ns)


def _nsa_sample_a_t(l, q3, hk, hv, nk3, nv3, wkv3, win_kt, win_vt, cw, past):
    DB = q3.shape[0]
    ncs = hk.shape[3]
    wb = win_kt.shape[-1]
    ns = -(-(past + 1) // SLC_BLOCK)
    nsl = -(-ns // LANES) * LANES
    m = _importance_matrix(ncs, ncs, ns, 0, nsl)
    nsq = NSA_A_SEQS if DB % NSA_A_SEQS == 0 else 1
    per = lambda shape: pl.BlockSpec((nsq,) + shape, lambda b: (b,) + (0,) * len(shape))
    win = pl.BlockSpec((None, nsq, HKV, HD, wb), lambda b: (l, b, 0, 0, 0))
    shapes = [(DB, SUBLANES, LANES), (DB, NSA_HEADS, 2 * HD), (DB, HKV, HD, wb), (DB, HKV, HD, wb)]
    dts = [jnp.int32, F32, F32, F32]
    return pl.pallas_call(
        functools.partial(_nsa_sample_a_t_multi, nsq=nsq, past=past, ns=ns),
        grid=(DB // nsq,),
        in_specs=[per((NSA_HEADS, HD))] + [per((HKV, 2 * HD, ncs))] * 2 + [per((1, 256))] * 3 + [win, win]
                 + [_const_spec((HD, 1))] * 2 + [_const_spec((HD, HD))] * 2
                 + [_const_spec((HD, 1))] * 2 + [_const_spec(m.shape)],
        out_specs=[per(s[1:]) for s in shapes],
        out_shape=[jax.ShapeDtypeStruct(s, d) for s, d in zip(shapes, dts)],
        compiler_params=_params("arbitrary"),
        name="nsa_sample_a",
    )(q3, hk, hv, nk3, nv3, wkv3, win_kt, win_vt, cw["wpk_b0"], cw["wpv_b0"], cw["wk_t"], cw["wv_t"],
      cw["bk_col"], cw["bv_col"], jnp.asarray(m, BF16))


def _nsa_sample_b_t_kernel(pt_ref, ix_ref, *refs, past, nbp, nsel, page):
    del pt_ref
    k_refs, v_refs = refs[:HKV * nsel], refs[HKV * nsel:2 * HKV * nsel]
    q_ref, ocw_ref, nk_ref, nv_ref, g_ref, o_ref = refs[2 * HKV * nsel:]
    b = pl.program_id(0)
    bpp = page // SLC_BLOCK
    q8 = q_ref[0].astype(BF16)
    row = lax.broadcasted_iota(jnp.int32, (NSA_HEADS, 1), 0)
    sel_head = lambda t: jnp.where(row // GRP == 0, t[:, :HD], t[:, HD:])
    k_new = sel_head(nk_ref[0][:, KV_W:]).astype(BF16).astype(F32)
    v_new = sel_head(nv_ref[0][:, KV_W:]).astype(BF16).astype(F32)
    s_new = jnp.sum(q8.astype(F32) * k_new, axis=-1, keepdims=True)
    nk = nsel * page
    lane = lax.broadcasted_iota(jnp.int32, (1, nk), 1)
    o_s = jnp.zeros((NSA_HEADS, HD), F32)
    for h in range(HKV):
        kt = jnp.concatenate([k_refs[h * nsel + r][...] for r in range(nsel)], axis=1).astype(BF16)
        vt = jnp.concatenate([v_refs[h * nsel + r][...] for r in range(nsel)], axis=1).astype(BF16)
        jv = jnp.full((1, nk), -1, jnp.int32)
        n_new = jnp.int32(0)
        for r in range(nsel):
            j = ix_ref[b, h, r]
            jv = jnp.where(lane // page == r, j, jv)
            n_new = n_new + (j == nbp).astype(jnp.int32)
        t_in = lane % page
        tok = (jv // bpp) * page + t_in
        mask = (jv >= 0) & (jv < nbp) & (t_in // SLC_BLOCK == jv % bpp) & (tok <= past)
        has_new = n_new > 0
        s = jnp.where(mask, _dot(q8, kt), NEG)
        sn = jnp.where(has_new, s_new, NEG)
        mx = jnp.maximum(jnp.max(s, axis=-1, keepdims=True), sn)
        e = jnp.where(mask, jnp.exp(s - mx), 0.0)
        e_new = jnp.where(has_new, jnp.exp(sn - mx), 0.0)
        den = jnp.maximum(jnp.sum(e, axis=-1, keepdims=True) + e_new, jnp.finfo(jnp.float32).tiny)
        o_h = (_dot_nt(e.astype(BF16), vt) + e_new * v_new) / den
        o_s = jnp.where(row // GRP == h, o_h, o_s)
    g = jax.nn.sigmoid(g_ref[0])
    ocw = ocw_ref[0]
    o_ref[0] = g[:, 0:1] * ocw[:, :HD] + g[:, 1:2] * o_s + g[:, 2:3] * ocw[:, HD:]


def _nsa_sample_b_t(l, cache_kt, cache_vt, page_table, idx, q3, ocw, nk3, nv3, g3, past):
    DB = q3.shape[0]
    page = cache_kt.shape[-1]
    bpp = page // SLC_BLOCK
    nbp = past // SLC_BLOCK
    nsel = idx.shape[2]

    def tile_spec(h, r):
        def imap(b, pt, ix):
            j = jnp.clip(ix[b, h, r], 0, nbp - 1)
            return (l, pt[b, j // bpp], 1, h, 0, 0)
        return pl.BlockSpec((None, None, None, None, HD, page), imap)

    per = lambda shape: pl.BlockSpec((1,) + shape, lambda b, pt, ix: (b,) + (0,) * len(shape))
    kv_specs = [tile_spec(h, r) for h in range(HKV) for r in range(nsel)]
    grid_spec = pltpu.PrefetchScalarGridSpec(
        num_scalar_prefetch=2, grid=(DB,),
        in_specs=kv_specs * 2 + [per((NSA_HEADS, HD)), per((NSA_HEADS, 2 * HD)), per((1, 256)), per((1, 256)),
                                 per((NSA_HEADS, 3))],
        out_specs=per((NSA_HEADS, HD)))
    n = HKV * nsel
    return pl.pallas_call(
        functools.partial(_nsa_sample_b_t_kernel, past=past, nbp=nbp, nsel=nsel, page=page),
        grid_spec=grid_spec,
        out_shape=jax.ShapeDtypeStruct((DB, NSA_HEADS, HD), F32),
        compiler_params=_params("arbitrary"),
        name="nsa_sample_b",
    )(page_table, idx, *([cache_kt] * n), *([cache_vt] * n), q3, ocw, nk3, nv3, g3)


SEQ_PER_STEP = 8


def _mlstm_sample_kernel(mqkv_ref, sm_ref, mo_ref, gb_ref, c_ref, n_ref, m_ref,
                         om_ref, c_out, n_out, m_out, *, nb):
    eye = (lax.broadcasted_iota(jnp.int32, (HD, HD), 0) == lax.broadcasted_iota(jnp.int32, (HD, HD), 1))
    lane = lax.broadcasted_iota(jnp.int32, (1, LANES), 1)

    def col(rowv):
        return jnp.sum(jnp.where(eye, jnp.broadcast_to(rowv, (HD, HD)), 0.0), axis=1, keepdims=True)

    for b in range(nb):
        g = sm_ref[b] + gb_ref[...]
        lf = _log_sigmoid(g)
        heads = []
        m_tile = jnp.zeros((1, LANES), F32)
        for hd in range(MH):
            ig = g[:, SM_IG + hd:SM_IG + hd + 1]
            a = lf[:, SM_FG + hd:SM_FG + hd + 1] + m_ref[b][:, hd:hd + 1]
            mt = jnp.maximum(a, ig)
            dm = jnp.exp(ig - mt)
            inter = jnp.exp(a - mt)
            q = mqkv_ref[b][:, HD * hd:HD * (hd + 1)]
            k = mqkv_ref[b][:, MLSTM_W + HD * hd:MLSTM_W + HD * (hd + 1)] * Q_SCALE
            v = mqkv_ref[b][:, 2 * MLSTM_W + HD * hd:2 * MLSTM_W + HD * (hd + 1)]
            c_prev = c_ref[b, hd]
            n_prev = n_ref[b][hd:hd + 1, :]
            qk = jnp.sum(q * k, axis=-1, keepdims=True) * dm
            q_c = jnp.sum(col(q) * c_prev, axis=0, keepdims=True)
            num = qk * v + inter * q_c
            den = qk + inter * jnp.sum(q * n_prev, axis=-1, keepdims=True)
            hout = num / jnp.maximum(jnp.abs(den), jnp.exp(-mt))
            heads.append(jax.nn.sigmoid(mo_ref[b][:, HD * hd:HD * (hd + 1)]) * hout)
            w = jnp.exp(ig - mt)
            decay = jnp.exp(a - mt)
            c_out[b, hd] = decay * c_prev + (w * col(k)) * v
            n_out[b, hd:hd + 1, :] = decay * n_prev + w * k
            m_tile = jnp.where(lane == hd, mt, m_tile)
        om_ref[b] = jnp.concatenate(heads, axis=-1)
        m_out[b] = m_tile


def _mlstm_sample(l, mqkv3, sm3, mo3, gate_b_tile, state_c, state_n, state_m4):
    DB = mqkv3.shape[0]
    nb = min(SEQ_PER_STEP, DB)
    assert DB % nb == 0
    per = lambda w: pl.BlockSpec((nb, 1, w), lambda i: (i, 0, 0))
    return pl.pallas_call(
        functools.partial(_mlstm_sample_kernel, nb=nb),
        grid=(DB // nb,),
        in_specs=[per(3 * MLSTM_W), per(LANES), per(MLSTM_W), _const_spec((1, LANES)),
                  pl.BlockSpec((None, nb, MH, HD, HD), lambda i: (l, i, 0, 0, 0)),
                  pl.BlockSpec((None, nb, MH, HD), lambda i: (l, i, 0, 0)),
                  pl.BlockSpec((None, nb, 1, MH), lambda i: (l, i, 0, 0))],
        out_specs=[per(MLSTM_W), pl.BlockSpec((nb, MH, HD, HD), lambda i: (i, 0, 0, 0)),
                   pl.BlockSpec((nb, MH, HD), lambda i: (i, 0, 0)), per(LANES)],
        out_shape=[jax.ShapeDtypeStruct((DB, 1, MLSTM_W), F32), jax.ShapeDtypeStruct((DB, MH, HD, HD), F32),
                   jax.ShapeDtypeStruct((DB, MH, HD), F32), jax.ShapeDtypeStruct((DB, 1, LANES), F32)],
        compiler_params=_params("arbitrary"),
        name="mlstm_sample",
    )(mqkv3, sm3, mo3, gate_b_tile, state_c, state_n, state_m4)


def _outffn_sample_kernel(x_ref, on_ref, om_ref, op_ref, gt1_ref, sh2_ref, sc2_ref, gt2_ref, g2_ref, fg_ref,
                          wout_ref, wup_ref, cw_ref, cb_ref, wdn_ref, prev_ref, y_ref, up_ref, *, d_ff, final):
    mix = (_dot(on_ref[...].astype(BF16), wout_ref[0:NSA_W, :])
           + _dot(om_ref[...].astype(BF16), wout_ref[NSA_W:NSA_W + MLSTM_W, :])
           + _dot(op_ref[...].astype(BF16), wout_ref[NSA_W + MLSTM_W:, :]))
    x1 = x_ref[...] + gt1_ref[...] * mix
    h2 = _norm_mod(x1, g2_ref[...], sc2_ref[...], sh2_ref[...]).astype(BF16)
    up = _dot(h2, wup_ref[...])
    up_ref[...] = up
    y = cb_ref[...] + cw_ref[FFN_CONV - 1:FFN_CONV, :] * up
    for j in range(FFN_CONV - 1):
        y = y + cw_ref[j:j + 1, :] * prev_ref[:, j, :]
    a, b = y[:, :d_ff], y[:, d_ff:]
    f = _dot((a * jax.nn.sigmoid(a) * b).astype(BF16), wdn_ref[...])
    x2 = x1 + gt2_ref[...] * f
    y_ref[...] = _rmsnorm(x2, fg_ref[...]) if final else x2


def _outffn_sample(l, x, on, om, op, gt1, sh2, sc2, gt2, g2, fg, w_out, w_up, conv_w, conv_b, w_dn,
                   state_ffn, final):
    DB, D = x.shape
    d_ff = w_dn.shape[0]
    full = lambda shape: pl.BlockSpec(shape, lambda i: (0,) * len(shape))
    args = (x, on, om, op, gt1, sh2, sc2, gt2, g2, fg, w_out, w_up, conv_w, conv_b, w_dn)
    return pl.pallas_call(
        functools.partial(_outffn_sample_kernel, d_ff=d_ff, final=final),
        grid=(1,),
        in_specs=[full(a.shape) for a in args]
                 + [pl.BlockSpec((None, DB, FFN_CONV - 1, 2 * d_ff), lambda i: (l, 0, 0, 0))],
        out_specs=[full((DB, D)), full((DB, 2 * d_ff))],
        out_shape=[jax.ShapeDtypeStruct((DB, D), F32), jax.ShapeDtypeStruct((DB, 2 * d_ff), F32)],
        compiler_params=_params("arbitrary"),
        name="outffn_sample",
    )(*args, state_ffn)


def _sample_layer(l, x, mod, lw, caches, final):
    DB, D = x.shape
    ckt, cvt, page_table, win_kt, win_vt, st_c, st_n, st_m, st_pool, st_ffn = caches
    past = page_table.shape[1] * ckt.shape[-1]
    sh1, sc1, gt1, sh2, sc2, gt2 = [mod[:, D * i:D * (i + 1)] for i in range(6)]
    q, nk, nv, wkv, mqkv, mo, op, sm, pu = _inproj_sample(
        l, x, lw["g1"], sh1, sc1, lw["w_b"], lw["w_q"], lw["pool_bd"], lw["pool_scale"], st_pool, past)
    r3 = lambda t: t[:, None, :]
    hk, hv = _compress_sample_t(l, ckt, cvt, page_table, lw["cw"])
    q3 = q.reshape(DB, NSA_HEADS, HD)
    idx, ocw, wk_new, wv_new = _nsa_sample_a_t(l, q3, hk, hv, r3(nk), r3(nv), r3(wkv), win_kt, win_vt,
                                               lw["cw"], past)
    g3 = sm[:, :3 * NSA_HEADS].reshape(DB, NSA_HEADS, 3)
    on = _nsa_sample_b_t(l, ckt, cvt, page_table, idx[:, :HKV, :N_SEL], q3, ocw, r3(nk), r3(nv), g3, past)
    om, c_new, n_new, m_new = _mlstm_sample(l, r3(mqkv), r3(sm), r3(mo), lw["gate_b"], st_c, st_n,
                                            st_m.reshape(st_m.shape[0], DB, 1, MH))
    y, up = _outffn_sample(l, x, on.reshape(DB, NSA_W), om[:, 0], op, gt1, sh2, sc2, gt2,
                           lw["g2"], lw["fg"], lw["w_out"], lw["w_up"], lw["conv_w"],
                           lw["conv_b"], lw["w_dn"], st_ffn, final)
    states = (nk.reshape(DB, 1, 2, HKV, HD), nv.reshape(DB, 1, 2, HKV, HD),
              jnp.transpose(wk_new, (0, 3, 1, 2)), jnp.transpose(wv_new, (0, 3, 1, 2)),
              c_new, n_new, m_new[:, 0, :MH],
              jnp.concatenate([st_pool[l][:, 1:], pu[:, None, :]], axis=1),
              jnp.concatenate([st_ffn[l][:, 1:], up[:, None, :]], axis=1))
    return y, states


def kernel(x_prompt, x_sample, cache_k, cache_v, cache_win_k, cache_win_v, state_mlstm_C, state_mlstm_n,
           state_mlstm_m, state_pool, state_ffn_conv, page_table, c_prompt, c_sample, norm1_g, norm2_g, ada_w,
           ada_b, w_in, nsa_cmp_pos, nsa_cmp_w, nsa_cmp_b, mlstm_gate_b, pool_w, pool_scale, w_out, ffn_w_up,
           ffn_conv_w, ffn_conv_b, ffn_w_down, final_g):
    B = x_prompt.shape[0]
    DB, DS, D = x_sample.shape
    assert DS == 1
    depth = w_in.shape[0]
    w = dict(norm1_g=norm1_g, norm2_g=norm2_g, w_in=w_in, nsa_cmp_pos=nsa_cmp_pos, nsa_cmp_w=nsa_cmp_w,
             nsa_cmp_b=nsa_cmp_b, mlstm_gate_b=mlstm_gate_b, pool_w=pool_w, pool_scale=pool_scale, w_out=w_out,
             ffn_w_up=ffn_w_up, ffn_conv_w=ffn_conv_w, ffn_conv_b=ffn_conv_b, ffn_w_down=ffn_w_down,
             final_g=final_g)
    mod = _ada_mod(jnp.concatenate([c_prompt, c_sample], axis=0), ada_w, ada_b)
    page = cache_k.shape[2]
    ckt = jnp.transpose(cache_k, (0, 1, 3, 4, 5, 2))
    cvt = jnp.transpose(cache_v, (0, 1, 3, 4, 5, 2))
    caches = (ckt, cvt, page_table, jnp.transpose(cache_win_k, (0, 1, 3, 4, 2)),
              jnp.transpose(cache_win_v, (0, 1, 3, 4, 2)),
              state_mlstm_C, state_mlstm_n, state_mlstm_m, state_pool, state_ffn_conv)
    xp, xs = x_prompt, x_sample[:, 0, :]
    acc_p = [[] for _ in range(9)]
    acc_s = [[] for _ in range(9)]
    for l in range(depth):
        lw = _prep_layer(l, w, page)
        final = l == depth - 1
        xp, st_p = _prompt_layer(xp, mod[l, :B][:, None, :], lw, final)
        xs, st_s = _sample_layer(l, xs, mod[l, B:], lw, caches, final)
        for a, v in zip(acc_p, st_p):
            a.append(v)
        for a, v in zip(acc_s, st_s):
            a.append(v)
    sp = [jnp.stack(a, axis=0) for a in acc_p]
    ss = [jnp.stack(a, axis=0) for a in acc_s]
    return (xp, xs[:, None, :], *sp, *ss)
```

```python
import functools

import numpy as np
import jax
import jax.numpy as jnp
from jax import lax
from jax.experimental import pallas as pl
from jax.experimental.pallas import tpu as pltpu

F32 = jnp.float32
BF16 = jnp.bfloat16

HD = 64
NSA_HEADS = 8
HKV = 2
GRP = NSA_HEADS // HKV
MH = 4
NSA_W = NSA_HEADS * HD
KV_W = HKV * HD
MLSTM_W = MH * HD
POOL_W = 4 * HD
CMP_STRIDE = 16
CMP_LEN = 32
SLC_BLOCK = 64
SLC_RATIO = SLC_BLOCK // CMP_STRIDE
N_SEL = 16
WINDOW = 512
POOL_WINDOWS = (2, 4, 8, 16)
POOL_STATE = 15
FFN_CONV = 3
RMS_EPS = 1e-6
NEG = -1e30
FORCE = 1e6
Q_SCALE = HD ** -0.5

LANES = 128
SUBLANES = 8
VMEM_LIMIT = 56 * 1024 * 1024

C_NK = 0
C_NV = C_NK + 256
C_WKV = C_NV + 256
C_MQKV = C_WKV + 256
C_MO = C_MQKV + 3 * MLSTM_W
C_PU = C_MO + MLSTM_W
C_SM = C_PU + POOL_W
C_END = C_SM + LANES
SM_IG = 3 * NSA_HEADS
SM_FG = SM_IG + MH


def _dot(a, b):
    return jnp.dot(a, b, preferred_element_type=F32)


def _dot_nt(a, b):
    return lax.dot_general(a, b, (((1,), (1,)), ((), ())), preferred_element_type=F32)


def _dot_tn(a, b):
    return lax.dot_general(a, b, (((0,), (0,)), ((), ())), preferred_element_type=F32)


def _dot_split3(x, m):
    x1 = x.astype(BF16)
    r1 = x - x1.astype(F32)
    x2 = r1.astype(BF16)
    x3 = (r1 - x2.astype(F32)).astype(BF16)
    return _dot(x1, m) + _dot(x2, m) + _dot(x3, m)


def _masked_softmax(s, mask):
    s = jnp.where(mask, s, NEG)
    e = jnp.where(mask, jnp.exp(s - jnp.max(s, axis=-1, keepdims=True)), 0.0)
    return e / jnp.maximum(jnp.sum(e, axis=-1, keepdims=True), jnp.finfo(jnp.float32).tiny)


def _norm_mod(x, g, sc, sh):
    ms = jnp.mean(x * x, axis=-1, keepdims=True)
    return (x * lax.rsqrt(ms + RMS_EPS) * g) * (1.0 + sc) + sh


def _rmsnorm(x, g):
    ms = jnp.mean(x * x, axis=-1, keepdims=True)
    return x * lax.rsqrt(ms + RMS_EPS) * g


def _log_sigmoid(x):
    return jnp.minimum(x, 0.0) - jnp.log1p(jnp.exp(-jnp.abs(x)))


def _cumsum_rows(x):
    n = x.shape[0]
    row = lax.broadcasted_iota(jnp.int32, x.shape, 0)
    sh = 1
    while sh < n:
        x = x + jnp.where(row >= sh, pltpu.roll(x, sh, 0), 0.0)
        sh *= 2
    return x


def _const_spec(shape):
    nd = len(shape)
    return pl.BlockSpec(shape, lambda *_: (0,) * nd, pipeline_mode=pl.Buffered(1))


def _params(*sem):
    return pltpu.CompilerParams(dimension_semantics=sem, vmem_limit_bytes=VMEM_LIMIT)


def _ada_kernel(c_ref, w_ref, b_ref, o_ref):
    c = c_ref[...]
    s = c * jax.nn.sigmoid(c)
    o_ref[0] = _dot(s.astype(BF16), w_ref[0].astype(BF16)) + b_ref[0]


def _ada_mod(c_all, ada_w, ada_b):
    depth, d, n = ada_w.shape
    rows = c_all.shape[0]
    tn = 1536
    return pl.pallas_call(
        _ada_kernel,
        grid=(depth, n // tn),
        in_specs=[pl.BlockSpec((rows, d), lambda l, j: (0, 0)),
                  pl.BlockSpec((1, d, tn), lambda l, j: (l, 0, j)),
                  pl.BlockSpec((1, 1, tn), lambda l, j: (l, 0, j))],
        out_specs=pl.BlockSpec((1, rows, tn), lambda l, j: (l, 0, j)),
        out_shape=jax.ShapeDtypeStruct((depth, rows, n), F32),
        compiler_params=_params("arbitrary", "arbitrary"),
        name="ada_mod",
    )(c_all, ada_w, ada_b.reshape(depth, 1, n))


def _prep_w_in(w):
    d = w.shape[0]
    o = 0
    o += NSA_W
    nkv = w[:, o:o + 6 * KV_W]; o += 6 * KV_W
    ng = w[:, o:o + 3 * NSA_HEADS]; o += 3 * NSA_HEADS
    mqkv = w[:, o:o + 3 * MLSTM_W]; o += 3 * MLSTM_W
    mif = w[:, o:o + 2 * MH]; o += 2 * MH
    mo = w[:, o:o + MLSTM_W]; o += MLSTM_W
    pu = w[:, o:o + POOL_W]
    k_cmp, v_cmp, k_slc, v_slc, k_win, v_win = [nkv[:, KV_W * i:KV_W * (i + 1)] for i in range(6)]
    cols = [
        k_cmp, k_slc, v_cmp, v_slc, k_win, v_win, mqkv, mo, pu,
        ng, mif, jnp.zeros((d, LANES - 3 * NSA_HEADS - 2 * MH), w.dtype)]
    out = jnp.concatenate(cols, axis=1).astype(BF16)
    assert out.shape[1] == C_END
    return out


def _pool_mix(pu, sums, cnt, pw_ref, pscale_ref):
    lane = lax.broadcasted_iota(jnp.int32, pu.shape, 1)
    grp = lane // HD
    mean = jnp.where(grp == 0, sums[2] / cnt[2],
                     jnp.where(grp == 1, sums[4] / cnt[4],
                               jnp.where(grp == 2, sums[8] / cnt[8], sums[16] / cnt[16])))
    d = mean - pu
    return _dot(d.astype(BF16), pw_ref[...]) * pscale_ref[...]


def _split2(x):
    hi = x.astype(BF16)
    return hi, (x - hi.astype(F32)).astype(BF16)


def _dot_split(a, b):
    return _dot(a[0], b[0]) + _dot(a[1], b[0]) + _dot(a[0], b[1])


def _inproj_prompt_kernel(x_ref, g_ref, sh_ref, sc_ref, w_ref, wqh_ref, wql_ref, pw_ref, pscale_ref,
                          q_ref, nk_ref, nv_ref, ks_ref, vs_ref, kw_ref, vw_ref, wkv_ref,
                          mqkv_ref, mo_ref, op_ref, sm_ref, ps_ref, zs_ref, *, tm):
    t = pl.program_id(1)
    hf = _norm_mod(x_ref[0], g_ref[...], sc_ref[0], sh_ref[0])
    h, h_lo = _split2(hf)

    def seg(a, b):
        return _dot(h, w_ref[:, a:b])

    q_ref[0] = _dot_split((h, h_lo), (wqh_ref[...], wql_ref[...])) * Q_SCALE
    nk = seg(C_NK, C_NV)
    nv = seg(C_NV, C_WKV)
    wkv = seg(C_WKV, C_MQKV)
    nk_ref[0] = nk
    nv_ref[0] = nv
    wkv_ref[0] = wkv
    kw_ref[0] = wkv[:, :KV_W].astype(BF16)
    blk = (t * tm + lax.broadcasted_iota(jnp.int32, (tm, LANES), 0)) // SLC_BLOCK
    lane = lax.broadcasted_iota(jnp.int32, (tm, LANES), 1)
    low = lane < HD

    def tiles(x, spare0, spare1):
        return (jnp.where(low, x, spare0).astype(BF16), jnp.where(low, spare1, x).astype(BF16))

    k_slc, v_slc, v_win = nk[:, KV_W:], nv[:, KV_W:], wkv[:, KV_W:]
    one0 = jnp.where(lane == HD, 1.0, 0.0)
    one1 = jnp.where(lane == 0, 1.0, 0.0)
    for dst, pair in ((ks_ref, tiles(k_slc, jnp.where(lane - HD == blk, 1.0, 0.0), jnp.where(lane == blk, 1.0, 0.0))),
                      (vs_ref, tiles(v_slc, one0, one1)), (vw_ref, tiles(v_win, one0, one1))):
        dst[0, :, 0:LANES] = pair[0]
        dst[0, :, LANES:2 * LANES] = pair[1]
    mqkv_ref[0] = seg(C_MQKV, C_MO)
    mo_ref[0] = seg(C_MO, C_PU)
    sm_ref[0] = seg(C_SM, C_END)

    pu = seg(C_PU, C_SM)
    halo = 2 * SUBLANES

    @pl.when(t == 0)
    def _():
        zs_ref[0:halo, :] = jnp.zeros((halo, POOL_W), F32)

    @pl.when(t > 0)
    def _():
        zs_ref[0:halo, :] = zs_ref[tm:tm + halo, :]

    zs_ref[halo:halo + tm, :] = pu
    acc = pu
    sums = {}
    for i in range(1, POOL_STATE + 1):
        acc = acc + zs_ref[pl.ds(halo - i, tm), :]
        if i + 1 in POOL_WINDOWS:
            sums[i + 1] = acc
    pos1 = (t * tm + lax.broadcasted_iota(jnp.int32, (tm, 1), 0) + 1).astype(F32)
    cnt = {w: jnp.minimum(float(w), pos1) for w in POOL_WINDOWS}
    op_ref[0] = _pool_mix(pu, sums, cnt, pw_ref, pscale_ref).astype(BF16)
    ps_ref[0] = zs_ref[tm:tm + halo, :]


def _inproj_prompt(x, g1, sh1, sc1, w_b, w_q2, pool_bd, pool_scale, tm=512):
    B, T, D = x.shape
    nT = T // tm
    assert T % tm == 0 and tm >= 2 * SUBLANES and WINDOW % tm == 0
    nwin = WINDOW // tm
    row = lambda w: pl.BlockSpec((1, tm, w), lambda b, t: (b, t, 0))
    mod = pl.BlockSpec((1, 1, D), lambda b, t: (b, 0, 0))
    outs = [
        (row(NSA_W), (B, T, NSA_W), F32),
        (row(256), (B, T, 256), F32),
        (row(256), (B, T, 256), F32),
        (row(256), (B, T, 256), BF16),
        (row(256), (B, T, 256), BF16),
        (row(KV_W), (B, T, KV_W), BF16),
        (row(256), (B, T, 256), BF16),
        (pl.BlockSpec((1, tm, 256), lambda b, t: (b, jnp.maximum(t - (nT - nwin), 0), 0)),
         (B, WINDOW, 256), F32),
        (row(3 * MLSTM_W), (B, T, 3 * MLSTM_W), F32),
        (row(MLSTM_W), (B, T, MLSTM_W), F32),
        (row(POOL_W), (B, T, POOL_W), BF16),
        (row(LANES), (B, T, LANES), F32),
        (pl.BlockSpec((1, 2 * SUBLANES, POOL_W), lambda b, t: (b, 0, 0)), (B, 2 * SUBLANES, POOL_W), F32),
    ]
    return pl.pallas_call(
        functools.partial(_inproj_prompt_kernel, tm=tm),
        grid=(B, nT),
        in_specs=[pl.BlockSpec((1, tm, D), lambda b, t: (b, t, 0)),
                  _const_spec((1, D)), mod, mod,
                  _const_spec(w_b.shape), _const_spec(w_q2[0].shape), _const_spec(w_q2[1].shape),
                  _const_spec(pool_bd.shape), _const_spec((1, POOL_W))],
        out_specs=[o[0] for o in outs],
        out_shape=[jax.ShapeDtypeStruct(o[1], o[2]) for o in outs],
        scratch_shapes=[pltpu.VMEM((2 * SUBLANES + tm, POOL_W), F32)],
        compiler_params=_params("arbitrary", "arbitrary"),
        name="inproj_prompt",
    )(x, g1, sh1, sc1, w_b, w_q2[0], w_q2[1], pool_bd, pool_scale)


def _compress_prompt_kernel(k_ref, v_ref, wpk_ref, wpv_ref, wk_ref, wkl_ref, wv_ref, bk_ref, bv_ref,
                            kc_ref, vc_ref, *, nseg):
    def pooled(src_ref, wp_ref):
        a = jnp.zeros((nseg, KV_W), F32)
        b = jnp.zeros((nseg, KV_W), F32)
        for j in range(CMP_STRIDE):
            xj = src_ref[0, pl.ds(j, nseg, stride=CMP_STRIDE), :]
            a = a + xj * wp_ref[j:j + 1, :]
            b = b + xj * wp_ref[CMP_STRIDE + j:CMP_STRIDE + j + 1, :]
        return a + pltpu.roll(b, nseg - 1, 0)

    kc_ref[0] = _dot_split(_split2(pooled(k_ref, wpk_ref)), (wk_ref[...], wkl_ref[...])) + bk_ref[...]
    vc_ref[0] = (_dot(pooled(v_ref, wpv_ref).astype(BF16), wv_ref[...]) + bv_ref[...]).astype(BF16)


def _compress_prompt(nk, nv, cw):
    B, T, _ = nk.shape
    nseg = T // CMP_STRIDE
    src = pl.BlockSpec((1, T, KV_W), lambda b: (b, 0, 0))
    return pl.pallas_call(
        functools.partial(_compress_prompt_kernel, nseg=nseg),
        grid=(B,),
        in_specs=[src, src, _const_spec((CMP_LEN, KV_W)), _const_spec((CMP_LEN, KV_W)),
                  _const_spec((KV_W, KV_W)), _const_spec((KV_W, KV_W)), _const_spec((KV_W, 2 * KV_W)),
                  _const_spec((1, KV_W)), _const_spec((1, 2 * KV_W))],
        out_specs=[pl.BlockSpec((1, nseg, KV_W), lambda b: (b, 0, 0)),
                   pl.BlockSpec((1, nseg, 2 * KV_W), lambda b: (b, 0, 0))],
        out_shape=[jax.ShapeDtypeStruct((B, nseg, KV_W), F32),
                   jax.ShapeDtypeStruct((B, nseg, 2 * KV_W), BF16)],
        compiler_params=_params("arbitrary"),
        name="compress_prompt",
    )(nk, nv, cw["wpk"], cw["wpv"], cw["wk"], cw["wk_lo"], cw["wv_dup"], cw["bk"], cw["bv_dup"])


def _prep_compress(cmp_pos, cmp_w, cmp_b, page):
    tile2 = lambda t: jnp.concatenate([t, t], axis=-1)
    z = jnp.zeros((HD, HD), F32)
    wk, wv = cmp_w[0], cmp_w[1]
    wk_bd = jnp.block([[wk, z], [z, wk]])
    wv_dup = jnp.block([[wv, z, z, z], [z, z, z, wv]])

    def pos_t(p):
        reps = page // CMP_STRIDE
        return jnp.concatenate([jnp.tile(p[:CMP_STRIDE].T, (1, reps)), jnp.tile(p[CMP_STRIDE:].T, (1, reps))], axis=0)

    return dict(
        wpk=tile2(cmp_pos[0]), wpv=tile2(cmp_pos[1]),
        wk=wk_bd.astype(BF16), wk_lo=(wk_bd - wk_bd.astype(BF16).astype(F32)).astype(BF16),
        wv_dup=wv_dup.astype(BF16),
        bk=tile2(cmp_b[0])[None, :],
        bv_dup=jnp.concatenate([cmp_b[1], jnp.zeros((2 * HD,), F32), cmp_b[1]])[None, :],
        wpk_t=pos_t(cmp_pos[0]), wpv_t=pos_t(cmp_pos[1]),
        wpk_b0=cmp_pos[0][CMP_STRIDE][:, None], wpv_b0=cmp_pos[1][CMP_STRIDE][:, None],
        wk_t=wk.T.astype(BF16), wv_t=wv.T.astype(BF16),
        bk_col=cmp_b[0][:, None], bv_col=cmp_b[1][:, None])


def _importance_matrix(nc_rows, nc_valid, ns, lane_off, width):
    m = np.zeros((nc_rows, width), np.float32)
    for j in range(ns):
        for n in range(SLC_RATIO * j - 1, SLC_RATIO * j + SLC_RATIO):
            if 0 <= n < nc_valid:
                m[n, lane_off + j] = 1.0
    return m


def _select_blocks_t(st_ref, h, ns):
    nb, nq = st_ref.shape[1], st_ref.shape[2]
    ngrp = nb // SUBLANES
    groups = [st_ref[h, SUBLANES * r:SUBLANES * (r + 1), :] for r in range(ngrp)]
    ranks = [jnp.zeros((SUBLANES, nq), F32) for _ in range(ngrp)]
    jrow = lax.broadcasted_iota(jnp.int32, (SUBLANES, nq), 0)
    for jp in range(min(ns, nb)):
        row = jnp.broadcast_to(st_ref[h, pl.ds(jp, 1), :], (SUBLANES, nq))
        for r in range(ngrp):
            ge = jnp.where(row >= groups[r], 1.0, 0.0)
            gt = jnp.where(row > groups[r], 1.0, 0.0)
            if jp < SUBLANES * r:
                inc = ge
            elif jp >= SUBLANES * (r + 1):
                inc = gt
            else:
                inc = jnp.where(jrow + SUBLANES * r > jp, ge, gt)
            ranks[r] = ranks[r] + inc
    return jnp.concatenate(
        [jnp.where((ranks[r] < N_SEL) & (groups[r] >= 0.0), 0.0, NEG) for r in range(ngrp)], axis=0)


def _nsa_prompt_kernel(q_ref, sm_ref, kc_ref, vc_ref, ks_ref, vs_ref, kw_ref, vw_ref, mt_ref,
                       o_ref, acc_ref, m_ref, qa_ref, st_ref, sa_ref, sb_ref, *, T, tk, qb):
    i = pl.program_id(1)
    ns = T // SLC_BLOCK
    nc = kc_ref.shape[1]
    rows = GRP * qb
    qpos = i * qb + lax.broadcasted_iota(jnp.int32, (qb, 1), 0)
    qpos4 = jnp.concatenate([qpos] * GRP, axis=0)
    qpos_t = i * qb + lax.broadcasted_iota(jnp.int32, (HD, qb), 1)
    jb_t = lax.broadcasted_iota(jnp.int32, (HD, qb), 0)
    sig = jax.nn.sigmoid(sm_ref[0])
    lane = lax.broadcasted_iota(jnp.int32, (qb, LANES), 1)
    wk = WINDOW + qb
    n_idx = lax.broadcasted_iota(jnp.int32, (1, nc), 1)
    cmp_mask = (CMP_STRIDE * n_idx + CMP_LEN - 1) <= qpos4
    wstart = pl.multiple_of(jnp.maximum(i - WINDOW // qb, 0) * qb, qb)
    dlt = qpos - (wstart + lax.broadcasted_iota(jnp.int32, (1, wk), 1))
    wbias = jnp.where((dlt >= 0) & (dlt < WINDOW), 0.0, NEG)
    cur_t = qpos_t // SLC_BLOCK
    valid_t = (jb_t * SLC_BLOCK <= qpos_t) & (jb_t < ns)
    forced_t = (jb_t == 0) | (jb_t == cur_t) | (jb_t == cur_t - 1)
    zpad = jnp.zeros((LANES - HD, qb), F32)
    kc_hi, kc_lo = _split2(kc_ref[0])
    sum_lane = (HD, 0)

    o_cs, o_ws = [], []
    for h in range(HKV):
        parts = []
        for g in range(GRP):
            hd = GRP * h + g
            src = q_ref[0, :, LANES * (hd // 2):LANES * (hd // 2 + 1)]
            if hd % 2 != h:
                src = pltpu.roll(src, HD, 1)
            parts.append(jnp.where((lane < HD) if h == 0 else (lane >= HD), src, 0.0))
        qh, qh_lo = _split2(jnp.concatenate(parts, axis=0))

        p_c = _masked_softmax(_dot_nt(qh, kc_hi) + _dot_nt(qh_lo, kc_hi) + _dot_nt(qh, kc_lo), cmp_mask)
        o_cs.append(_dot(p_c.astype(BF16), vc_ref[0, :, LANES * h:LANES * (h + 1)]))

        psum = p_c[0:qb] + p_c[qb:2 * qb] + p_c[2 * qb:3 * qb] + p_c[3 * qb:4 * qb]
        x1 = psum.astype(BF16)
        r1 = psum - x1.astype(F32)
        x2 = r1.astype(BF16)
        x3 = (r1 - x2.astype(F32)).astype(BF16)
        mt = mt_ref[...]
        imp_t = _dot_nt(mt, x1) + _dot_nt(mt, x2) + _dot_nt(mt, x3)
        score_t = jnp.where(valid_t, jnp.where(forced_t, FORCE, imp_t), -1.0)
        st_ref[h] = jnp.where(jb_t < ns, score_t, -jnp.inf)
        sel_t = _select_blocks_t(st_ref, h, ns)
        selneg = (jnp.concatenate([zpad, sel_t], axis=0) if h == 0
                  else jnp.concatenate([sel_t, zpad], axis=0)).T
        qa_ref[h] = qh + jnp.concatenate([selneg.astype(BF16)] * GRP, axis=0)

        sw = _dot_nt(qh, kw_ref[0, pl.ds(wstart, wk), :])
        sw = jnp.concatenate([sw[qb * g:qb * (g + 1)] + wbias for g in range(GRP)], axis=0)
        e = jnp.exp(sw - jnp.max(sw, axis=-1, keepdims=True))
        o_w = _dot(e.astype(BF16), vw_ref[0, pl.ds(wstart, wk), LANES * h:LANES * (h + 1)])
        o_ws.append(o_w / o_w[:, sum_lane[h]:sum_lane[h] + 1])

    m_ref[...] = jnp.full(m_ref.shape, -jnp.inf, F32)
    acc_ref[...] = jnp.zeros(acc_ref.shape, F32)

    def scores(c, s_ref):
        start = pl.multiple_of(c * tk, tk)
        for h in range(HKV):
            s_ref[h] = _dot_nt(qa_ref[h], ks_ref[0, pl.ds(start, tk), LANES * h:LANES * (h + 1)])

    def accumulate(c, s_ref, diagonal):
        start = pl.multiple_of(c * tk, tk)
        for h in range(HKV):
            sc = s_ref[h]
            if diagonal:
                tok = start + lax.broadcasted_iota(jnp.int32, (1, tk), 1)
                sc = jnp.where(tok <= qpos4, sc, NEG)
            m_prev = m_ref[h]
            m_new = jnp.maximum(m_prev, jnp.max(sc, axis=-1, keepdims=True))
            alpha = jnp.exp(m_prev - m_new)
            p = jnp.exp(sc - m_new[:, 0:1])
            v = vs_ref[0, pl.ds(start, tk), LANES * h:LANES * (h + 1)]
            acc_ref[h] = alpha * acc_ref[h] + _dot(p.astype(BF16), v)
            m_ref[h] = m_new

    n_full = (i * qb) // tk
    scores(0, sa_ref)

    def chunk_pair(cc, carry):
        c0 = 2 * cc
        scores(c0 + 1, sb_ref)
        accumulate(c0, sa_ref, False)
        scores(c0 + 2, sa_ref)
        accumulate(c0 + 1, sb_ref, False)
        return carry

    lax.fori_loop(0, n_full // 2, chunk_pair, 0)

    @pl.when(n_full % 2 == 0)
    def _():
        accumulate(n_full, sa_ref, True)

    @pl.when(n_full % 2 == 1)
    def _():
        scores(n_full, sb_ref)
        accumulate(n_full - 1, sa_ref, False)
        accumulate(n_full, sb_ref, True)

    for h in range(HKV):
        acc = acc_ref[h]
        o_s = acc / acc[:, sum_lane[h]:sum_lane[h] + 1]
        outs = []
        for g in range(GRP):
            c0 = 3 * (GRP * h + g)
            r = slice(qb * g, qb * (g + 1))
            outs.append(sig[:, c0:c0 + 1] * o_cs[h][r] + sig[:, c0 + 1:c0 + 2] * o_s[r]
                        + sig[:, c0 + 2:c0 + 3] * o_ws[h][r])
        base = GRP * HD * h
        for pair in range(GRP // 2):
            lo, hi = outs[2 * pair], outs[2 * pair + 1]
            if h == 0:
                hi = pltpu.roll(hi, HD, 1)
            else:
                lo = pltpu.roll(lo, HD, 1)
            o_ref[0, :, base + LANES * pair:base + LANES * (pair + 1)] = jnp.where(lane < HD, lo, hi).astype(BF16)


def _nsa_prompt(q, sm, kc, vc, ks, vs, kw, vw, qb=256, tk=512):
    B, T, _ = q.shape
    ns = T // SLC_BLOCK
    nc = kc.shape[1]
    assert ns <= HD and T % tk == 0 and T >= WINDOW + qb and WINDOW % qb == 0 and tk % qb == 0
    m2 = _importance_matrix(nc, nc - 1, ns, 0, HD).T
    full = lambda w: pl.BlockSpec((1, T, w), lambda b, i: (b, 0, 0))
    return pl.pallas_call(
        functools.partial(_nsa_prompt_kernel, T=T, tk=tk, qb=qb),
        grid=(B, T // qb),
        in_specs=[pl.BlockSpec((1, qb, NSA_W), lambda b, i: (b, i, 0)),
                  pl.BlockSpec((1, qb, LANES), lambda b, i: (b, i, 0)),
                  pl.BlockSpec((1, nc, KV_W), lambda b, i: (b, 0, 0)),
                  pl.BlockSpec((1, nc, 2 * KV_W), lambda b, i: (b, 0, 0)),
                  full(256), full(256), full(KV_W), full(256),
                  _const_spec(m2.shape)],
        out_specs=pl.BlockSpec((1, qb, NSA_W), lambda b, i: (b, i, 0)),
        out_shape=jax.ShapeDtypeStruct((B, T, NSA_W), BF16),
        scratch_shapes=[pltpu.VMEM((HKV, GRP * qb, LANES), F32)] * 2
                       + [pltpu.VMEM((HKV, GRP * qb, LANES), BF16), pltpu.VMEM((HKV, HD, qb), F32)]
                       + [pltpu.VMEM((HKV, GRP * qb, tk), F32)] * 2,
        compiler_params=_params("arbitrary", "arbitrary"),
        name="nsa_prompt",
    )(q, sm, kc, vc, ks, vs, kw, vw, jnp.asarray(m2, BF16))


def _mlstm_gates(g):
    b = pltpu.roll(_cumsum_rows(_log_sigmoid(g)), LANES - MH, 1)
    return b, g - b


def _mlstm_prompt_kernel(mqkv_ref, sm_ref, mo_ref, gb_ref, om_ref, c_out, n_out, m_out,
                         c_s, n_s, m_s, *, B, L):
    t = pl.program_id(0)

    @pl.when(t == 0)
    def _():
        c_s[...] = jnp.zeros(c_s.shape, F32)
        n_s[...] = jnp.zeros(n_s.shape, F32)
        m_s[...] = jnp.zeros(m_s.shape, F32)

    li = lax.broadcasted_iota(jnp.int32, (L, L), 0)
    si = lax.broadcasted_iota(jnp.int32, (L, L), 1)
    for b in range(B):
        b_al, r = _mlstm_gates(sm_ref[b] + gb_ref[...])
        r_t = jnp.concatenate([r, jnp.zeros((LANES - L, LANES), F32)], axis=0).T if L < LANES else r.T
        heads = []
        for hd in range(MH):
            idx = b * MH + hd
            bcol = b_al[:, SM_IG + hd:SM_IG + hd + 1]
            rcol = r[:, SM_IG + hd:SM_IG + hd + 1]
            rrow = r_t[SM_IG + hd:SM_IG + hd + 1, 0:L]
            mprev = m_s[idx][:, 0:1]
            acol = bcol + mprev
            logd = jnp.where(si <= li, bcol + rrow, NEG)
            mt = jnp.maximum(acol, jnp.max(logd, axis=-1, keepdims=True))
            dm = jnp.exp(logd - mt)
            inter = jnp.exp(acol - mt)
            q = mqkv_ref[b, :, HD * hd:HD * (hd + 1)]
            k = mqkv_ref[b, :, MLSTM_W + HD * hd:MLSTM_W + HD * (hd + 1)] * Q_SCALE
            v = mqkv_ref[b, :, 2 * MLSTM_W + HD * hd:2 * MLSTM_W + HD * (hd + 1)]
            qb, kb = q.astype(BF16), k.astype(BF16)
            qk = _dot_nt(qb, kb) * dm
            c_prev = c_s[idx]
            n_prev = n_s[idx]
            num = _dot(qk.astype(BF16), v.astype(BF16)) + inter * _dot(qb, c_prev.astype(BF16))
            den = jnp.sum(qk, axis=-1, keepdims=True) + inter * jnp.sum(q * n_prev, axis=-1, keepdims=True)
            hout = num / jnp.maximum(jnp.abs(den), jnp.exp(-mt))
            heads.append(jax.nn.sigmoid(mo_ref[b, :, HD * hd:HD * (hd + 1)]) * hout)
            m_last = mt[L - 1:L, :]
            b_last = bcol[L - 1:L, :]
            wcol = jnp.exp(b_last + rcol - m_last)
            decay = jnp.exp(b_last + mprev - m_last)
            c_s[idx] = decay * c_prev + _dot_tn(kb, (wcol * v).astype(BF16))
            n_s[idx] = decay * n_prev + jnp.sum(wcol * k, axis=0, keepdims=True)
            m_s[idx] = jnp.broadcast_to(m_last, (1, LANES))
        om_ref[b] = jnp.concatenate(heads, axis=-1).astype(BF16)

    @pl.when(t == pl.num_programs(0) - 1)
    def _():
        c_out[...] = c_s[...]
        n_out[...] = n_s[...]
        m_out[...] = m_s[...]


def _mlstm_prompt(mqkv, sm, mo, gate_b_tile, L=256):
    B, T, _ = mqkv.shape
    assert T % L == 0 and (L <= LANES or L % LANES == 0)
    blk = lambda w: pl.BlockSpec((B, L, w), lambda t: (0, t, 0))
    st = lambda shape: pl.BlockSpec(shape, lambda t: (0,) * len(shape))
    shapes = [(B * MH, HD, HD), (B * MH, 1, HD), (B * MH, 1, LANES)]
    return pl.pallas_call(
        functools.partial(_mlstm_prompt_kernel, B=B, L=L),
        grid=(T // L,),
        in_specs=[blk(3 * MLSTM_W), blk(LANES), blk(MLSTM_W), _const_spec((1, LANES))],
        out_specs=[blk(MLSTM_W)] + [st(s) for s in shapes],
        out_shape=[jax.ShapeDtypeStruct((B, T, MLSTM_W), BF16)] + [jax.ShapeDtypeStruct(s, F32) for s in shapes],
        scratch_shapes=[pltpu.VMEM(s, F32) for s in shapes],
        compiler_params=_params("arbitrary"),
        name="mlstm_prompt",
    )(mqkv, sm, mo, gate_b_tile)


FF_CHUNK = 2816


def _outffn_prompt_kernel(x_ref, on_ref, om_ref, op_ref, gt1_ref, sh2_ref, sc2_ref, gt2_ref,
                          g2_ref, fg_ref, wout_ref, wup_ref, cw_ref, cb_ref, wdn_ref,
                          y_ref, fs_ref, prev_ref, es_ref, *, tm, d_ff, final):
    t = pl.program_id(1)
    mix = (_dot(on_ref[0], wout_ref[0:NSA_W, :])
           + _dot(om_ref[0], wout_ref[NSA_W:NSA_W + MLSTM_W, :])
           + _dot(op_ref[0], wout_ref[NSA_W + MLSTM_W:, :]))
    x1 = x_ref[0] + gt1_ref[0] * mix
    h2 = _norm_mod(x1, g2_ref[...], sc2_ref[0], sh2_ref[0]).astype(BF16)

    @pl.when(t == 0)
    def _():
        prev_ref[...] = jnp.zeros(prev_ref.shape, F32)

    f = jnp.zeros(x1.shape, F32)
    w = FF_CHUNK
    for c in range(d_ff // w):
        for half, off in ((0, c * w), (1, d_ff + c * w)):
            es_ref[0:SUBLANES, half * w:(half + 1) * w] = prev_ref[:, off:off + w]
            es_ref[SUBLANES:SUBLANES + tm, half * w:(half + 1) * w] = _dot(h2, wup_ref[:, off:off + w])
            prev_ref[:, off:off + w] = es_ref[tm:tm + SUBLANES, half * w:(half + 1) * w]
        ys = []
        for half, off in ((0, c * w), (1, d_ff + c * w)):
            y = cb_ref[:, off:off + w]
            for j in range(FFN_CONV):
                y = y + cw_ref[j:j + 1, off:off + w] * es_ref[pl.ds(SUBLANES - (FFN_CONV - 1) + j, tm), half * w:(half + 1) * w]
            ys.append(y)
        act = ys[0] * jax.nn.sigmoid(ys[0]) * ys[1]
        f = f + _dot(act.astype(BF16), wdn_ref[c * w:(c + 1) * w, :])
    x2 = x1 + gt2_ref[0] * f
    y_ref[0] = _rmsnorm(x2, fg_ref[...]) if final else x2
    fs_ref[0] = prev_ref[...]


def _outffn_prompt(x, on, om, op, gt1, sh2, sc2, gt2, g2, fg, w_out, w_up, conv_w, conv_b, w_dn, final, tm=512):
    B, T, D = x.shape
    d_ff = w_dn.shape[0]
    assert T % tm == 0 and d_ff % FF_CHUNK == 0
    row = lambda w: pl.BlockSpec((1, tm, w), lambda b, t: (b, t, 0))
    mod = pl.BlockSpec((1, 1, D), lambda b, t: (b, 0, 0))
    return pl.pallas_call(
        functools.partial(_outffn_prompt_kernel, tm=tm, d_ff=d_ff, final=final),
        grid=(B, T // tm),
        in_specs=[row(D), row(NSA_W), row(MLSTM_W), row(POOL_W), mod, mod, mod, mod,
                  _const_spec((1, D)), _const_spec((1, D)), _const_spec(w_out.shape), _const_spec(w_up.shape),
                  _const_spec(conv_w.shape), _const_spec(conv_b.shape), _const_spec(w_dn.shape)],
        out_specs=[row(D), pl.BlockSpec((1, SUBLANES, 2 * d_ff), lambda b, t: (b, 0, 0))],
        out_shape=[jax.ShapeDtypeStruct((B, T, D), F32), jax.ShapeDtypeStruct((B, SUBLANES, 2 * d_ff), F32)],
        scratch_shapes=[pltpu.VMEM((SUBLANES, 2 * d_ff), F32), pltpu.VMEM((SUBLANES + tm, 2 * FF_CHUNK), F32)],
        compiler_params=_params("arbitrary", "arbitrary"),
        name="outffn_prompt",
    )(x, on, om, op, gt1, sh2, sc2, gt2, g2, fg, w_out, w_up, conv_w, conv_b, w_dn)


def _prep_layer(l, w, page):
    gate_b = jnp.zeros((1, LANES), F32).at[0, SM_IG:SM_IG + 2 * MH].set(w["mlstm_gate_b"][l])
    pw = w["pool_w"][l]
    z = jnp.zeros((HD, HD), F32)
    pool_bd = jnp.block([[pw[i] if i == j else z for j in range(4)] for i in range(4)]).astype(BF16)
    wq = w["w_in"][l][:, :NSA_W]
    wq_hi = wq.astype(BF16)
    return dict(
        w_q=wq_hi, w_q2=(wq_hi, (wq - wq_hi.astype(F32)).astype(BF16)),
        g1=w["norm1_g"][l][None], g2=w["norm2_g"][l][None],
        w_b=_prep_w_in(w["w_in"][l]), pool_bd=pool_bd, pool_scale=w["pool_scale"][l][None],
        cw=_prep_compress(w["nsa_cmp_pos"][l], w["nsa_cmp_w"][l], w["nsa_cmp_b"][l], page),
        gate_b=gate_b, w_out=w["w_out"][l].astype(BF16), w_up=w["ffn_w_up"][l].astype(BF16),
        conv_w=w["ffn_conv_w"][l], conv_b=w["ffn_conv_b"][l][None], w_dn=w["ffn_w_down"][l].astype(BF16),
        fg=w["final_g"][None])


def _prompt_layer(x, mod, lw, final):
    B, T, D = x.shape
    sh1, sc1, gt1, sh2, sc2, gt2 = [mod[:, :, D * i:D * (i + 1)] for i in range(6)]
    (q, nk, nv, ks, vs, kw, vw, wkv, mqkv, mo, op, sm, ps) = _inproj_prompt(
        x, lw["g1"], sh1, sc1, lw["w_b"], lw["w_q2"], lw["pool_bd"], lw["pool_scale"])
    kc, vc = _compress_prompt(nk, nv, lw["cw"])
    on = _nsa_prompt(q, sm, kc, vc, ks, vs, kw, vw)
    om, c_st, n_st, m_st = _mlstm_prompt(mqkv, sm, mo, lw["gate_b"])
    y, fs = _outffn_prompt(x, on, om, op, gt1, sh2, sc2, gt2, lw["g2"], lw["fg"], lw["w_out"],
                           lw["w_up"], lw["conv_w"], lw["conv_b"], lw["w_dn"], final)
    wb = wkv.shape[1]
    states = (nk.reshape(B, T, 2, HKV, HD), nv.reshape(B, T, 2, HKV, HD),
              wkv[:, :, :KV_W].reshape(B, wb, HKV, HD), wkv[:, :, KV_W:].reshape(B, wb, HKV, HD),
              c_st.reshape(B, MH, HD, HD), n_st.reshape(B, MH, HD), m_st[:, 0, 0].reshape(B, MH),
              ps[:, 2 * SUBLANES - POOL_STATE:], fs[:, SUBLANES - (FFN_CONV - 1):])
    return y, states


def _inproj_sample_kernel(x_ref, g_ref, sh_ref, sc_ref, w_ref, wq_ref, pw_ref, pscale_ref, prev_ref,
                          q_ref, nk_ref, nv_ref, wkv_ref, mqkv_ref, mo_ref, op_ref, sm_ref, pu_ref, *, pos0):
    h = _norm_mod(x_ref[...], g_ref[...], sc_ref[...], sh_ref[...]).astype(BF16)

    def seg(a, b):
        return _dot(h, w_ref[:, a:b])

    q_ref[...] = _dot(h, wq_ref[...]) * Q_SCALE
    nk_ref[...] = seg(C_NK, C_NV)
    nv_ref[...] = seg(C_NV, C_WKV)
    wkv_ref[...] = seg(C_WKV, C_MQKV)
    mqkv_ref[...] = seg(C_MQKV, C_MO)
    mo_ref[...] = seg(C_MO, C_PU)
    sm_ref[...] = seg(C_SM, C_END)
    pu = seg(C_PU, C_SM)
    pu_ref[...] = pu
    acc = pu
    sums = {}
    for i in range(1, POOL_STATE + 1):
        acc = acc + prev_ref[:, POOL_STATE - i, :]
        if i + 1 in POOL_WINDOWS:
            sums[i + 1] = acc
    cnt = {w: float(min(w, pos0 + 1)) for w in POOL_WINDOWS}
    op_ref[...] = _pool_mix(pu, sums, cnt, pw_ref, pscale_ref)


def _inproj_sample(l, x, g1, sh1, sc1, w_b, w_q, pool_bd, pool_scale, state_pool, pos0):
    DB, D = x.shape
    full = lambda shape: pl.BlockSpec(shape, lambda i: (0,) * len(shape))
    widths = [NSA_W, 256, 256, 256, 3 * MLSTM_W, MLSTM_W, POOL_W, LANES, POOL_W]
    return pl.pallas_call(
        functools.partial(_inproj_sample_kernel, pos0=pos0),
        grid=(1,),
        in_specs=[full((DB, D)), full((1, D)), full((DB, D)), full((DB, D)), full(w_b.shape),
                  full(w_q.shape), full(pool_bd.shape), full((1, POOL_W)),
                  pl.BlockSpec((None, DB, POOL_STATE, POOL_W), lambda i: (l, 0, 0, 0))],
        out_specs=[full((DB, w)) for w in widths],
        out_shape=[jax.ShapeDtypeStruct((DB, w), F32) for w in widths],
        compiler_params=_params("arbitrary"),
        name="inproj_sample",
    )(x, g1, sh1, sc1, w_b, w_q, pool_bd, pool_scale, state_pool)


PAGES_PER_STEP = 32
PAGE_SLOTS = 3


def _row_to_col(rowv):
    n = rowv.shape[1]
    eye = lax.broadcasted_iota(jnp.int32, (n, n), 0) == lax.broadcasted_iota(jnp.int32, (n, n), 1)
    return jnp.sum(jnp.where(eye, jnp.broadcast_to(rowv, (n, n)), 0.0), axis=1, keepdims=True)


def _compress_sample_t_kernel(pt_ref, ck_ref, cv_ref, wk_ref, wv_ref, sg_ref, ok_ref, ov_ref,
                              kbuf, vbuf, sem, *, l, npg, nc, nsteps):
    step = pl.program_id(0) * nc + pl.program_id(1)
    ahead = PAGE_SLOTS - 1
    slot = step % PAGE_SLOTS

    def copies(s, sl):
        b, c = s // nc, s % nc
        out = []
        for i in range(npg):
            p = pt_ref[b, c * npg + i]
            out.append(pltpu.make_async_copy(ck_ref.at[l, p, 0], kbuf.at[sl, i], sem.at[0, sl]))
            out.append(pltpu.make_async_copy(cv_ref.at[l, p, 0], vbuf.at[sl, i], sem.at[1, sl]))
        return out

    @pl.when(step == 0)
    def _():
        for s in range(min(ahead, nsteps)):
            for cp in copies(s, s):
                cp.start()

    @pl.when(step + ahead < nsteps)
    def _():
        for cp in copies(step + ahead, (step + ahead) % PAGE_SLOTS):
            cp.start()

    for cp in copies(step, slot):
        cp.wait()

    for buf, w_ref, o_ref in ((kbuf, wk_ref, ok_ref), (vbuf, wv_ref, ov_ref)):
        wa = w_ref[0:HD, :]
        wb = w_ref[HD:2 * HD, :]
        for h in range(HKV):
            prods = []
            for i in range(npg):
                x = buf[slot, i, h]
                prods.append(jnp.concatenate([x * wa, x * wb], axis=0).astype(BF16))
            o_ref[0, h] = _dot(jnp.concatenate(prods, axis=1), sg_ref[...])


def _compress_sample_t(l, cache_kt, cache_vt, page_table, cw):
    DB, n_pages = page_table.shape
    page = cache_kt.shape[-1]
    npg = min(PAGES_PER_STEP, n_pages)
    assert n_pages % npg == 0 and page % CMP_STRIDE == 0
    spp = page // CMP_STRIDE
    nseg = n_pages * spp
    assert (npg * spp) % LANES == 0 or npg * spp == nseg
    sg = np.zeros((npg * page, npg * spp), np.float32)
    sg[np.arange(npg * page), np.arange(npg * page) // CMP_STRIDE] = 1.0

    nc = n_pages // npg
    out = pl.BlockSpec((1, HKV, 2 * HD, npg * spp), lambda b, c, pt: (b, 0, 0, c))
    cst = lambda shape: pl.BlockSpec(shape, lambda b, c, pt: (0,) * len(shape))
    hbm = pl.BlockSpec(memory_space=pl.ANY)
    grid_spec = pltpu.PrefetchScalarGridSpec(
        num_scalar_prefetch=1, grid=(DB, nc),
        in_specs=[hbm, hbm, cst((2 * HD, page)), cst((2 * HD, page)), cst(sg.shape)],
        out_specs=[out] * 2,
        scratch_shapes=[pltpu.VMEM((PAGE_SLOTS, npg, HKV, HD, page), F32),
                        pltpu.VMEM((PAGE_SLOTS, npg, HKV, HD, page), F32),
                        pltpu.SemaphoreType.DMA((2, PAGE_SLOTS))])
    return pl.pallas_call(
        functools.partial(_compress_sample_t_kernel, l=l, npg=npg, nc=nc, nsteps=DB * nc),
        grid_spec=grid_spec,
        out_shape=[jax.ShapeDtypeStruct((DB, HKV, 2 * HD, nseg), F32)] * 2,
        compiler_params=_params("arbitrary", "arbitrary"),
        name="compress_sample",
    )(page_table, cache_kt, cache_vt, cw["wpk_t"], cw["wpv_t"], jnp.asarray(sg, BF16))


def _nsa_sample_a_t_kernel(q_ref, hk_ref, hv_ref, nk_ref, nv_ref, wkv_ref, wink_ref, winv_ref,
                           wpk_ref, wpv_ref, wk_ref, wv_ref, bk_ref, bv_ref, m_ref,
                           idx_ref, ocw_ref, wko_ref, wvo_ref, *, past, ns):
    ncs = hk_ref.shape[3]
    wb = wink_ref.shape[3]
    q8 = q_ref[0].astype(BF16)
    row = lax.broadcasted_iota(jnp.int32, (NSA_HEADS, 1), 0)
    lane_c = lax.broadcasted_iota(jnp.int32, (HD, ncs), 1)
    lane_w = lax.broadcasted_iota(jnp.int32, (HD, wb), 1)
    n_idx = lax.broadcasted_iota(jnp.int32, (1, ncs), 1)

    def per_head(fn):
        a, b = fn(0), fn(1)
        return jnp.where(row // GRP == 0, a, b)

    def comp_t(h_ref, new_row, wp_ref, w_ref, bias_ref, h):
        at = h_ref[0, h, 0:HD, :]
        bt = h_ref[0, h, HD:2 * HD, :]
        b_new = _row_to_col(new_row[:, HD * h:HD * (h + 1)]) * wp_ref[...]
        pooled = at + jnp.where(lane_c == ncs - 1, b_new, pltpu.roll(bt, ncs - 1, 1))
        return (_dot(w_ref[...], pooled.astype(BF16)) + bias_ref[...]).astype(BF16)

    kct = [comp_t(hk_ref, nk_ref[0], wpk_ref, wk_ref, bk_ref, h) for h in range(HKV)]
    vct = [comp_t(hv_ref, nv_ref[0], wpv_ref, wv_ref, bv_ref, h) for h in range(HKV)]
    s = per_head(lambda h: _dot(q8, kct[h]))
    p = _masked_softmax(s, (CMP_STRIDE * n_idx + CMP_LEN - 1) <= past)
    pb = p.astype(BF16)
    o_c = per_head(lambda h: _dot_nt(pb, vct[h]))

    rowi = lax.broadcasted_iota(jnp.int32, p.shape, 0)
    p0 = jnp.sum(p[0:GRP], axis=0, keepdims=True)
    p1 = jnp.sum(p[GRP:2 * GRP], axis=0, keepdims=True)
    imp = _dot_split3(jnp.where(rowi == 0, p0, jnp.where(rowi == 1, p1, 0.0)), m_ref[...])
    lane = lax.broadcasted_iota(jnp.int32, imp.shape, 1)
    cur = past // SLC_BLOCK
    valid = lane * SLC_BLOCK <= past
    forced = (lane == 0) | (lane == cur) | (lane == cur - 1)
    score = jnp.where(valid, jnp.where(forced, FORCE, imp), -1.0)
    score = jnp.where(lane < ns, score, -jnp.inf)
    lane_o = lax.broadcasted_iota(jnp.int32, (SUBLANES, LANES), 1)
    idx = jnp.full((SUBLANES, LANES), -1, jnp.int32)
    for r in range(min(N_SEL, ns)):
        mx = jnp.max(score, axis=-1, keepdims=True)
        first = jnp.min(jnp.where(score == mx, lane, 1 << 20), axis=-1, keepdims=True)
        idx = jnp.where(lane_o == r, jnp.where(mx >= 0.0, first, -1), idx)
        score = jnp.where(lane == first, -jnp.inf, score)
    idx_ref[0] = idx

    sel_head = lambda t: jnp.where(row // GRP == 0, t[:, :HD], t[:, HD:])
    k_new = wkv_ref[0][:, :KV_W]
    v_new = wkv_ref[0][:, KV_W:]
    s_w = per_head(lambda h: _dot(q8, wink_ref[0, h].astype(BF16)))
    dlt = wb - lax.broadcasted_iota(jnp.int32, (1, wb), 1)
    mask = (dlt < WINDOW) & (past - dlt >= 0)
    s_new = jnp.sum(q8.astype(F32) * sel_head(k_new).astype(BF16).astype(F32), axis=-1, keepdims=True)
    s_w = jnp.where(mask, s_w, NEG)
    mx = jnp.maximum(jnp.max(s_w, axis=-1, keepdims=True), s_new)
    e = jnp.where(mask, jnp.exp(s_w - mx), 0.0)
    e_new = jnp.exp(s_new - mx)
    den = jnp.sum(e, axis=-1, keepdims=True) + e_new
    eb = e.astype(BF16)
    o_w = per_head(lambda h: _dot_nt(eb, winv_ref[0, h].astype(BF16)))
    o_w = (o_w + e_new * sel_head(v_new).astype(BF16).astype(F32)) / den
    ocw_ref[0] = jnp.concatenate([o_c, o_w], axis=-1)
    for h in range(HKV):
        for src, new, dst in ((wink_ref, k_new, wko_ref), (winv_ref, v_new, wvo_ref)):
            col = _row_to_col(new[:, HD * h:HD * (h + 1)])
            dst[0, h] = jnp.where(lane_w == wb - 1, col, pltpu.roll(src[0, h], wb - 1, 1))


NSA_A_SEQS = 4
N_SEQ_IN = 8
N_SEQ_OUT = 4


def _nsa_sample_a_t_multi(*refs, nsq, past, ns):
    for sq in range(nsq):
        one = lambda r: r.at[pl.ds(sq, 1)]
        _nsa_sample_a_t_kernel(*[one(r) for r in refs[:N_SEQ_IN]], *refs[N_SEQ_IN:-N_SEQ_OUT],
                               *[one(r) for r in refs[-N_SEQ_OUT:]], past=past, ns=ns)


def _nsa_sample_a_t(l, q3, hk, hv, nk3, nv3, wkv3, win_kt, win_vt, cw, past):
    DB = q3.shape[0]
    ncs = hk.shape[3]
    wb = win_kt.shape[-1]
    ns = -(-(past + 1) // SLC_BLOCK)
    nsl = -(-ns // LANES) * LANES
    m = _importance_matrix(ncs, ncs, ns, 0, nsl)
    nsq = NSA_A_SEQS if DB % NSA_A_SEQS == 0 else 1
    per = lambda shape: pl.BlockSpec((nsq,) + shape, lambda b: (b,) + (0,) * len(shape))
    win = pl.BlockSpec((None, nsq, HKV, HD, wb), lambda b: (l, b, 0, 0, 0))
    shapes = [(DB, SUBLANES, LANES), (DB, NSA_HEADS, 2 * HD), (DB, HKV, HD, wb), (DB, HKV, HD, wb)]
    dts = [jnp.int32, F32, F32, F32]
    return pl.pallas_call(
        functools.partial(_nsa_sample_a_t_multi, nsq=nsq, past=past, ns=ns),
        grid=(DB // nsq,),
        in_specs=[per((NSA_HEADS, HD))] + [per((HKV, 2 * HD, ncs))] * 2 + [per((1, 256))] * 3 + [win, win]
                 + [_const_spec((HD, 1))] * 2 + [_const_spec((HD, HD))] * 2
                 + [_const_spec((HD, 1))] * 2 + [_const_spec(m.shape)],
        out_specs=[per(s[1:]) for s in shapes],
        out_shape=[jax.ShapeDtypeStruct(s, d) for s, d in zip(shapes, dts)],
        compiler_params=_params("arbitrary"),
        name="nsa_sample_a",
    )(q3, hk, hv, nk3, nv3, wkv3, win_kt, win_vt, cw["wpk_b0"], cw["wpv_b0"], cw["wk_t"], cw["wv_t"],
      cw["bk_col"], cw["bv_col"], jnp.asarray(m, BF16))


def _nsa_sample_b_t_kernel(pt_ref, ix_ref, ck_ref, cv_ref, q_ref, ocw_ref, nk_ref, nv_ref, g_ref, o_ref,
                           kbuf, vbuf, sem, *, l, past, nbp, nsel, page, nsteps):
    b = pl.program_id(0)
    bpp = page // SLC_BLOCK
    ahead = PAGE_SLOTS - 1
    slot = b % PAGE_SLOTS

    def copies(s, sl):
        out = []
        for h in range(HKV):
            for r in range(nsel):
                p = pt_ref[s, jnp.clip(ix_ref[s, h, r], 0, nbp - 1) // bpp]
                out.append(pltpu.make_async_copy(ck_ref.at[l, p, 1, h], kbuf.at[sl, h * nsel + r], sem.at[0, sl]))
                out.append(pltpu.make_async_copy(cv_ref.at[l, p, 1, h], vbuf.at[sl, h * nsel + r], sem.at[1, sl]))
        return out

    @pl.when(b == 0)
    def _():
        for s in range(min(ahead, nsteps)):
            for cp in copies(s, s):
                cp.start()

    @pl.when(b + ahead < nsteps)
    def _():
        for cp in copies(b + ahead, (b + ahead) % PAGE_SLOTS):
            cp.start()

    for cp in copies(b, slot):
        cp.wait()

    q8 = q_ref[0].astype(BF16)
    row = lax.broadcasted_iota(jnp.int32, (NSA_HEADS, 1), 0)
    sel_head = lambda t: jnp.where(row // GRP == 0, t[:, :HD], t[:, HD:])
    k_new = sel_head(nk_ref[0][:, KV_W:]).astype(BF16).astype(F32)
    v_new = sel_head(nv_ref[0][:, KV_W:]).astype(BF16).astype(F32)
    s_new = jnp.sum(q8.astype(F32) * k_new, axis=-1, keepdims=True)
    nk = nsel * page
    lane = lax.broadcasted_iota(jnp.int32, (1, nk), 1)
    o_s = jnp.zeros((NSA_HEADS, HD), F32)
    for h in range(HKV):
        kt = jnp.concatenate([kbuf[slot, h * nsel + r] for r in range(nsel)], axis=1).astype(BF16)
        vt = jnp.concatenate([vbuf[slot, h * nsel + r] for r in range(nsel)], axis=1).astype(BF16)
        jv = jnp.full((1, nk), -1, jnp.int32)
        n_new = jnp.int32(0)
        for r in range(nsel):
            j = ix_ref[b, h, r]
            jv = jnp.where(lane // page == r, j, jv)
            n_new = n_new + (j == nbp).astype(jnp.int32)
        t_in = lane % page
        tok = (jv // bpp) * page + t_in
        mask = (jv >= 0) & (jv < nbp) & (t_in // SLC_BLOCK == jv % bpp) & (tok <= past)
        has_new = n_new > 0
        s = jnp.where(mask, _dot(q8, kt), NEG)
        sn = jnp.where(has_new, s_new, NEG)
        mx = jnp.maximum(jnp.max(s, axis=-1, keepdims=True), sn)
        e = jnp.where(mask, jnp.exp(s - mx), 0.0)
        e_new = jnp.where(has_new, jnp.exp(sn - mx), 0.0)
        den = jnp.maximum(jnp.sum(e, axis=-1, keepdims=True) + e_new, jnp.finfo(jnp.float32).tiny)
        o_h = (_dot_nt(e.astype(BF16), vt) + e_new * v_new) / den
        o_s = jnp.where(row // GRP == h, o_h, o_s)
    g = jax.nn.sigmoid(g_ref[0])
    ocw = ocw_ref[0]
    o_ref[0] = g[:, 0:1] * ocw[:, :HD] + g[:, 1:2] * o_s + g[:, 2:3] * ocw[:, HD:]


def _nsa_sample_b_t(l, cache_kt, cache_vt, page_table, idx, q3, ocw, nk3, nv3, g3, past):
    DB = q3.shape[0]
    page = cache_kt.shape[-1]
    bpp = page // SLC_BLOCK
    nbp = past // SLC_BLOCK
    nsel = idx.shape[2]

    per = lambda shape: pl.BlockSpec((1,) + shape, lambda b, pt, ix: (b,) + (0,) * len(shape))
    hbm = pl.BlockSpec(memory_space=pl.ANY)
    n = HKV * nsel
    grid_spec = pltpu.PrefetchScalarGridSpec(
        num_scalar_prefetch=2, grid=(DB,),
        in_specs=[hbm, hbm, per((NSA_HEADS, HD)), per((NSA_HEADS, 2 * HD)), per((1, 256)), per((1, 256)),
                  per((NSA_HEADS, 3))],
        out_specs=per((NSA_HEADS, HD)),
        scratch_shapes=[pltpu.VMEM((PAGE_SLOTS, n, HD, page), F32), pltpu.VMEM((PAGE_SLOTS, n, HD, page), F32),
                        pltpu.SemaphoreType.DMA((2, PAGE_SLOTS))])
    return pl.pallas_call(
        functools.partial(_nsa_sample_b_t_kernel, l=l, past=past, nbp=nbp, nsel=nsel, page=page, nsteps=DB),
        grid_spec=grid_spec,
        out_shape=jax.ShapeDtypeStruct((DB, NSA_HEADS, HD), F32),
        compiler_params=_params("arbitrary"),
        name="nsa_sample_b",
    )(page_table, idx, cache_kt, cache_vt, q3, ocw, nk3, nv3, g3)


SEQ_PER_STEP = 8


def _mlstm_sample_kernel(mqkv_ref, sm_ref, mo_ref, gb_ref, c_ref, n_ref, m_ref,
                         om_ref, c_out, n_out, m_out, *, nb):
    eye = (lax.broadcasted_iota(jnp.int32, (HD, HD), 0) == lax.broadcasted_iota(jnp.int32, (HD, HD), 1))
    lane = lax.broadcasted_iota(jnp.int32, (1, LANES), 1)

    def col(rowv):
        return jnp.sum(jnp.where(eye, jnp.broadcast_to(rowv, (HD, HD)), 0.0), axis=1, keepdims=True)

    for b in range(nb):
        g = sm_ref[b] + gb_ref[...]
        lf = _log_sigmoid(g)
        heads = []
        m_tile = jnp.zeros((1, LANES), F32)
        for hd in range(MH):
            ig = g[:, SM_IG + hd:SM_IG + hd + 1]
            a = lf[:, SM_FG + hd:SM_FG + hd + 1] + m_ref[b][:, hd:hd + 1]
            mt = jnp.maximum(a, ig)
            dm = jnp.exp(ig - mt)
            inter = jnp.exp(a - mt)
            q = mqkv_ref[b][:, HD * hd:HD * (hd + 1)]
            k = mqkv_ref[b][:, MLSTM_W + HD * hd:MLSTM_W + HD * (hd + 1)] * Q_SCALE
            v = mqkv_ref[b][:, 2 * MLSTM_W + HD * hd:2 * MLSTM_W + HD * (hd + 1)]
            c_prev = c_ref[b, hd]
            n_prev = n_ref[b][hd:hd + 1, :]
            qk = jnp.sum(q * k, axis=-1, keepdims=True) * dm
            q_c = jnp.sum(col(q) * c_prev, axis=0, keepdims=True)
            num = qk * v + inter * q_c
            den = qk + inter * jnp.sum(q * n_prev, axis=-1, keepdims=True)
            hout = num / jnp.maximum(jnp.abs(den), jnp.exp(-mt))
            heads.append(jax.nn.sigmoid(mo_ref[b][:, HD * hd:HD * (hd + 1)]) * hout)
            w = jnp.exp(ig - mt)
            decay = jnp.exp(a - mt)
            c_out[b, hd] = decay * c_prev + (w * col(k)) * v
            n_out[b, hd:hd + 1, :] = decay * n_prev + w * k
            m_tile = jnp.where(lane == hd, mt, m_tile)
        om_ref[b] = jnp.concatenate(heads, axis=-1)
        m_out[b] = m_tile


def _mlstm_sample(l, mqkv3, sm3, mo3, gate_b_tile, state_c, state_n, state_m4):
    DB = mqkv3.shape[0]
    nb = min(SEQ_PER_STEP, DB)
    assert DB % nb == 0
    per = lambda w: pl.BlockSpec((nb, 1, w), lambda i: (i, 0, 0))
    return pl.pallas_call(
        functools.partial(_mlstm_sample_kernel, nb=nb),
        grid=(DB // nb,),
        in_specs=[per(3 * MLSTM_W), per(LANES), per(MLSTM_W), _const_spec((1, LANES)),
                  pl.BlockSpec((None, nb, MH, HD, HD), lambda i: (l, i, 0, 0, 0)),
                  pl.BlockSpec((None, nb, MH, HD), lambda i: (l, i, 0, 0)),
                  pl.BlockSpec((None, nb, 1, MH), lambda i: (l, i, 0, 0))],
        out_specs=[per(MLSTM_W), pl.BlockSpec((nb, MH, HD, HD), lambda i: (i, 0, 0, 0)),
                   pl.BlockSpec((nb, MH, HD), lambda i: (i, 0, 0)), per(LANES)],
        out_shape=[jax.ShapeDtypeStruct((DB, 1, MLSTM_W), F32), jax.ShapeDtypeStruct((DB, MH, HD, HD), F32),
                   jax.ShapeDtypeStruct((DB, MH, HD), F32), jax.ShapeDtypeStruct((DB, 1, LANES), F32)],
        compiler_params=_params("arbitrary"),
        name="mlstm_sample",
    )(mqkv3, sm3, mo3, gate_b_tile, state_c, state_n, state_m4)


def _outffn_sample_kernel(x_ref, on_ref, om_ref, op_ref, gt1_ref, sh2_ref, sc2_ref, gt2_ref, g2_ref, fg_ref,
                          wout_ref, wup_ref, cw_ref, cb_ref, wdn_ref, prev_ref, y_ref, up_ref, *, d_ff, final):
    mix = (_dot(on_ref[...].astype(BF16), wout_ref[0:NSA_W, :])
           + _dot(om_ref[...].astype(BF16), wout_ref[NSA_W:NSA_W + MLSTM_W, :])
           + _dot(op_ref[...].astype(BF16), wout_ref[NSA_W + MLSTM_W:, :]))
    x1 = x_ref[...] + gt1_ref[...] * mix
    h2 = _norm_mod(x1, g2_ref[...], sc2_ref[...], sh2_ref[...]).astype(BF16)
    up = _dot(h2, wup_ref[...])
    up_ref[...] = up
    y = cb_ref[...] + cw_ref[FFN_CONV - 1:FFN_CONV, :] * up
    for j in range(FFN_CONV - 1):
        y = y + cw_ref[j:j + 1, :] * prev_ref[:, j, :]
    a, b = y[:, :d_ff], y[:, d_ff:]
    f = _dot((a * jax.nn.sigmoid(a) * b).astype(BF16), wdn_ref[...])
    x2 = x1 + gt2_ref[...] * f
    y_ref[...] = _rmsnorm(x2, fg_ref[...]) if final else x2


def _outffn_sample(l, x, on, om, op, gt1, sh2, sc2, gt2, g2, fg, w_out, w_up, conv_w, conv_b, w_dn,
                   state_ffn, final):
    DB, D = x.shape
    d_ff = w_dn.shape[0]
    full = lambda shape: pl.BlockSpec(shape, lambda i: (0,) * len(shape))
    args = (x, on, om, op, gt1, sh2, sc2, gt2, g2, fg, w_out, w_up, conv_w, conv_b, w_dn)
    return pl.pallas_call(
        functools.partial(_outffn_sample_kernel, d_ff=d_ff, final=final),
        grid=(1,),
        in_specs=[full(a.shape) for a in args]
                 + [pl.BlockSpec((None, DB, FFN_CONV - 1, 2 * d_ff), lambda i: (l, 0, 0, 0))],
        out_specs=[full((DB, D)), full((DB, 2 * d_ff))],
        out_shape=[jax.ShapeDtypeStruct((DB, D), F32), jax.ShapeDtypeStruct((DB, 2 * d_ff), F32)],
        compiler_params=_params("arbitrary"),
        name="outffn_sample",
    )(*args, state_ffn)


def _sample_layer(l, x, mod, lw, caches, final):
    DB, D = x.shape
    ckt, cvt, page_table, win_kt, win_vt, st_c, st_n, st_m, st_pool, st_ffn = caches
    past = page_table.shape[1] * ckt.shape[-1]
    sh1, sc1, gt1, sh2, sc2, gt2 = [mod[:, D * i:D * (i + 1)] for i in range(6)]
    q, nk, nv, wkv, mqkv, mo, op, sm, pu = _inproj_sample(
        l, x, lw["g1"], sh1, sc1, lw["w_b"], lw["w_q"], lw["pool_bd"], lw["pool_scale"], st_pool, past)
    r3 = lambda t: t[:, None, :]
    hk, hv = _compress_sample_t(l, ckt, cvt, page_table, lw["cw"])
    q3 = q.reshape(DB, NSA_HEADS, HD)
    idx, ocw, wk_new, wv_new = _nsa_sample_a_t(l, q3, hk, hv, r3(nk), r3(nv), r3(wkv), win_kt, win_vt,
                                               lw["cw"], past)
    g3 = sm[:, :3 * NSA_HEADS].reshape(DB, NSA_HEADS, 3)
    on = _nsa_sample_b_t(l, ckt, cvt, page_table, idx[:, :HKV, :N_SEL], q3, ocw, r3(nk), r3(nv), g3, past)
    om, c_new, n_new, m_new = _mlstm_sample(l, r3(mqkv), r3(sm), r3(mo), lw["gate_b"], st_c, st_n,
                                            st_m.reshape(st_m.shape[0], DB, 1, MH))
    y, up = _outffn_sample(l, x, on.reshape(DB, NSA_W), om[:, 0], op, gt1, sh2, sc2, gt2,
                           lw["g2"], lw["fg"], lw["w_out"], lw["w_up"], lw["conv_w"],
                           lw["conv_b"], lw["w_dn"], st_ffn, final)
    states = (nk.reshape(DB, 1, 2, HKV, HD), nv.reshape(DB, 1, 2, HKV, HD),
              jnp.transpose(wk_new, (0, 3, 1, 2)), jnp.transpose(wv_new, (0, 3, 1, 2)),
              c_new, n_new, m_new[:, 0, :MH],
              jnp.concatenate([st_pool[l][:, 1:], pu[:, None, :]], axis=1),
              jnp.concatenate([st_ffn[l][:, 1:], up[:, None, :]], axis=1))
    return y, states


def kernel(x_prompt, x_sample, cache_k, cache_v, cache_win_k, cache_win_v, state_mlstm_C, state_mlstm_n,
           state_mlstm_m, state_pool, state_ffn_conv, page_table, c_prompt, c_sample, norm1_g, norm2_g, ada_w,
           ada_b, w_in, nsa_cmp_pos, nsa_cmp_w, nsa_cmp_b, mlstm_gate_b, pool_w, pool_scale, w_out, ffn_w_up,
           ffn_conv_w, ffn_conv_b, ffn_w_down, final_g):
    B = x_prompt.shape[0]
    DB, DS, D = x_sample.shape
    assert DS == 1
    depth = w_in.shape[0]
    w = dict(norm1_g=norm1_g, norm2_g=norm2_g, w_in=w_in, nsa_cmp_pos=nsa_cmp_pos, nsa_cmp_w=nsa_cmp_w,
             nsa_cmp_b=nsa_cmp_b, mlstm_gate_b=mlstm_gate_b, pool_w=pool_w, pool_scale=pool_scale, w_out=w_out,
             ffn_w_up=ffn_w_up, ffn_conv_w=ffn_conv_w, ffn_conv_b=ffn_conv_b, ffn_w_down=ffn_w_down,
             final_g=final_g)
    mod = _ada_mod(jnp.concatenate([c_prompt, c_sample], axis=0), ada_w, ada_b)
    page = cache_k.shape[2]
    ckt = jnp.transpose(cache_k, (0, 1, 3, 4, 5, 2))
    cvt = jnp.transpose(cache_v, (0, 1, 3, 4, 5, 2))
    caches = (ckt, cvt, page_table, jnp.transpose(cache_win_k, (0, 1, 3, 4, 2)),
              jnp.transpose(cache_win_v, (0, 1, 3, 4, 2)),
              state_mlstm_C, state_mlstm_n, state_mlstm_m, state_pool, state_ffn_conv)
    xp, xs = x_prompt, x_sample[:, 0, :]
    acc_p = [[] for _ in range(9)]
    acc_s = [[] for _ in range(9)]
    for l in range(depth):
        lw = _prep_layer(l, w, page)
        final = l == depth - 1
        xp, st_p = _prompt_layer(xp, mod[l, :B][:, None, :], lw, final)
        xs, st_s = _sample_layer(l, xs, mod[l, B:], lw, caches, final)
        for a, v in zip(acc_p, st_p):
            a.append(v)
        for a, v in zip(acc_s, st_s):
            a.append(v)
    sp = [jnp.stack(a, axis=0) for a in acc_p]
    ss = [jnp.stack(a, axis=0) for a in acc_s]
    return (xp, xs[:, None, :], *sp, *ss)
```

```python
import functools

import numpy as np
import jax
import jax.numpy as jnp
from jax import lax
from jax.experimental import pallas as pl
from jax.experimental.pallas import tpu as pltpu

F32 = jnp.float32
BF16 = jnp.bfloat16

HD = 64
NSA_HEADS = 8
HKV = 2
GRP = NSA_HEADS // HKV
MH = 4
NSA_W = NSA_HEADS * HD
KV_W = HKV * HD
MLSTM_W = MH * HD
POOL_W = 4 * HD
CMP_STRIDE = 16
CMP_LEN = 32
SLC_BLOCK = 64
SLC_RATIO = SLC_BLOCK // CMP_STRIDE
N_SEL = 16
WINDOW = 512
POOL_WINDOWS = (2, 4, 8, 16)
POOL_STATE = 15
FFN_CONV = 3
RMS_EPS = 1e-6
NEG = -1e30
FORCE = 1e6
Q_SCALE = HD ** -0.5

LANES = 128
SUBLANES = 8
VMEM_LIMIT = 56 * 1024 * 1024

C_NK = 0
C_NV = C_NK + 256
C_WKV = C_NV + 256
C_MQKV = C_WKV + 256
C_MO = C_MQKV + 3 * MLSTM_W
C_PU = C_MO + MLSTM_W
C_SM = C_PU + POOL_W
C_END = C_SM + LANES
SM_IG = 3 * NSA_HEADS
SM_FG = SM_IG + MH


def _dot(a, b):
    return jnp.dot(a, b, preferred_element_type=F32)


def _dot_nt(a, b):
    return lax.dot_general(a, b, (((1,), (1,)), ((), ())), preferred_element_type=F32)


def _dot_tn(a, b):
    return lax.dot_general(a, b, (((0,), (0,)), ((), ())), preferred_element_type=F32)


def _dot_split3(x, m):
    x1 = x.astype(BF16)
    r1 = x - x1.astype(F32)
    x2 = r1.astype(BF16)
    x3 = (r1 - x2.astype(F32)).astype(BF16)
    return _dot(x1, m) + _dot(x2, m) + _dot(x3, m)


def _masked_softmax(s, mask):
    s = jnp.where(mask, s, NEG)
    e = jnp.where(mask, jnp.exp(s - jnp.max(s, axis=-1, keepdims=True)), 0.0)
    return e / jnp.maximum(jnp.sum(e, axis=-1, keepdims=True), jnp.finfo(jnp.float32).tiny)


def _norm_mod(x, g, sc, sh):
    ms = jnp.mean(x * x, axis=-1, keepdims=True)
    return (x * lax.rsqrt(ms + RMS_EPS) * g) * (1.0 + sc) + sh


def _rmsnorm(x, g):
    ms = jnp.mean(x * x, axis=-1, keepdims=True)
    return x * lax.rsqrt(ms + RMS_EPS) * g


def _log_sigmoid(x):
    return jnp.minimum(x, 0.0) - jnp.log1p(jnp.exp(-jnp.abs(x)))


def _cumsum_rows(x):
    n = x.shape[0]
    row = lax.broadcasted_iota(jnp.int32, x.shape, 0)
    sh = 1
    while sh < n:
        x = x + jnp.where(row >= sh, pltpu.roll(x, sh, 0), 0.0)
        sh *= 2
    return x


def _const_spec(shape):
    nd = len(shape)
    return pl.BlockSpec(shape, lambda *_: (0,) * nd, pipeline_mode=pl.Buffered(1))


def _params(*sem):
    return pltpu.CompilerParams(dimension_semantics=sem, vmem_limit_bytes=VMEM_LIMIT)


def _ada_kernel(c_ref, w_ref, b_ref, o_ref):
    c = c_ref[...]
    s = c * jax.nn.sigmoid(c)
    o_ref[0] = _dot(s.astype(BF16), w_ref[0].astype(BF16)) + b_ref[0]


def _ada_mod(c_all, ada_w, ada_b):
    depth, d, n = ada_w.shape
    rows = c_all.shape[0]
    tn = 1536
    return pl.pallas_call(
        _ada_kernel,
        grid=(depth, n // tn),
        in_specs=[pl.BlockSpec((rows, d), lambda l, j: (0, 0)),
                  pl.BlockSpec((1, d, tn), lambda l, j: (l, 0, j)),
                  pl.BlockSpec((1, 1, tn), lambda l, j: (l, 0, j))],
        out_specs=pl.BlockSpec((1, rows, tn), lambda l, j: (l, 0, j)),
        out_shape=jax.ShapeDtypeStruct((depth, rows, n), F32),
        compiler_params=_params("arbitrary", "arbitrary"),
        name="ada_mod",
    )(c_all, ada_w, ada_b.reshape(depth, 1, n))


def _prep_w_in(w):
    d = w.shape[0]
    o = 0
    o += NSA_W
    nkv = w[:, o:o + 6 * KV_W]; o += 6 * KV_W
    ng = w[:, o:o + 3 * NSA_HEADS]; o += 3 * NSA_HEADS
    mqkv = w[:, o:o + 3 * MLSTM_W]; o += 3 * MLSTM_W
    mif = w[:, o:o + 2 * MH]; o += 2 * MH
    mo = w[:, o:o + MLSTM_W]; o += MLSTM_W
    pu = w[:, o:o + POOL_W]
    k_cmp, v_cmp, k_slc, v_slc, k_win, v_win = [nkv[:, KV_W * i:KV_W * (i + 1)] for i in range(6)]
    cols = [
        k_cmp, k_slc, v_cmp, v_slc, k_win, v_win, mqkv, mo, pu,
        ng, mif, jnp.zeros((d, LANES - 3 * NSA_HEADS - 2 * MH), w.dtype)]
    out = jnp.concatenate(cols, axis=1).astype(BF16)
    assert out.shape[1] == C_END
    return out


def _pool_mix(pu, sums, cnt, pw_ref, pscale_ref):
    lane = lax.broadcasted_iota(jnp.int32, pu.shape, 1)
    grp = lane // HD
    mean = jnp.where(grp == 0, sums[2] / cnt[2],
                     jnp.where(grp == 1, sums[4] / cnt[4],
                               jnp.where(grp == 2, sums[8] / cnt[8], sums[16] / cnt[16])))
    d = mean - pu
    return _dot(d.astype(BF16), pw_ref[...]) * pscale_ref[...]


def _split2(x):
    hi = x.astype(BF16)
    return hi, (x - hi.astype(F32)).astype(BF16)


def _dot_split(a, b):
    return _dot(a[0], b[0]) + _dot(a[1], b[0]) + _dot(a[0], b[1])


def _inproj_prompt_kernel(x_ref, g_ref, sh_ref, sc_ref, w_ref, wqh_ref, wql_ref, pw_ref, pscale_ref,
                          q_ref, nk_ref, nv_ref, ks_ref, vs_ref, kw_ref, vw_ref, wkv_ref,
                          mqkv_ref, mo_ref, op_ref, sm_ref, ps_ref, zs_ref, *, tm):
    t = pl.program_id(1)
    hf = _norm_mod(x_ref[0], g_ref[...], sc_ref[0], sh_ref[0])
    h, h_lo = _split2(hf)

    def seg(a, b):
        return _dot(h, w_ref[:, a:b])

    q_ref[0] = _dot_split((h, h_lo), (wqh_ref[...], wql_ref[...])) * Q_SCALE
    nk = seg(C_NK, C_NV)
    nv = seg(C_NV, C_WKV)
    wkv = seg(C_WKV, C_MQKV)
    nk_ref[0] = nk
    nv_ref[0] = nv
    wkv_ref[0] = wkv
    kw_ref[0] = wkv[:, :KV_W].astype(BF16)
    blk = (t * tm + lax.broadcasted_iota(jnp.int32, (tm, LANES), 0)) // SLC_BLOCK
    lane = lax.broadcasted_iota(jnp.int32, (tm, LANES), 1)
    low = lane < HD

    def tiles(x, spare0, spare1):
        return (jnp.where(low, x, spare0).astype(BF16), jnp.where(low, spare1, x).astype(BF16))

    k_slc, v_slc, v_win = nk[:, KV_W:], nv[:, KV_W:], wkv[:, KV_W:]
    one0 = jnp.where(lane == HD, 1.0, 0.0)
    one1 = jnp.where(lane == 0, 1.0, 0.0)
    for dst, pair in ((ks_ref, tiles(k_slc, jnp.where(lane - HD == blk, 1.0, 0.0), jnp.where(lane == blk, 1.0, 0.0))),
                      (vs_ref, tiles(v_slc, one0, one1)), (vw_ref, tiles(v_win, one0, one1))):
        dst[0, :, 0:LANES] = pair[0]
        dst[0, :, LANES:2 * LANES] = pair[1]
    mqkv_ref[0] = seg(C_MQKV, C_MO)
    mo_ref[0] = seg(C_MO, C_PU)
    sm_ref[0] = seg(C_SM, C_END)

    pu = seg(C_PU, C_SM)
    halo = 2 * SUBLANES

    @pl.when(t == 0)
    def _():
        zs_ref[0:halo, :] = jnp.zeros((halo, POOL_W), F32)

    @pl.when(t > 0)
    def _():
        zs_ref[0:halo, :] = zs_ref[tm:tm + halo, :]

    zs_ref[halo:halo + tm, :] = pu
    acc = pu
    sums = {}
    for i in range(1, POOL_STATE + 1):
        acc = acc + zs_ref[pl.ds(halo - i, tm), :]
        if i + 1 in POOL_WINDOWS:
            sums[i + 1] = acc
    pos1 = (t * tm + lax.broadcasted_iota(jnp.int32, (tm, 1), 0) + 1).astype(F32)
    cnt = {w: jnp.minimum(float(w), pos1) for w in POOL_WINDOWS}
    op_ref[0] = _pool_mix(pu, sums, cnt, pw_ref, pscale_ref).astype(BF16)
    ps_ref[0] = zs_ref[tm:tm + halo, :]


def _inproj_prompt(x, g1, sh1, sc1, w_b, w_q2, pool_bd, pool_scale, tm=512):
    B, T, D = x.shape
    nT = T // tm
    assert T % tm == 0 and tm >= 2 * SUBLANES and WINDOW % tm == 0
    nwin = WINDOW // tm
    row = lambda w: pl.BlockSpec((1, tm, w), lambda b, t: (b, t, 0))
    mod = pl.BlockSpec((1, 1, D), lambda b, t: (b, 0, 0))
    outs = [
        (row(NSA_W), (B, T, NSA_W), F32),
        (row(256), (B, T, 256), F32),
        (row(256), (B, T, 256), F32),
        (row(256), (B, T, 256), BF16),
        (row(256), (B, T, 256), BF16),
        (row(KV_W), (B, T, KV_W), BF16),
        (row(256), (B, T, 256), BF16),
        (pl.BlockSpec((1, tm, 256), lambda b, t: (b, jnp.maximum(t - (nT - nwin), 0), 0)),
         (B, WINDOW, 256), F32),
        (row(3 * MLSTM_W), (B, T, 3 * MLSTM_W), F32),
        (row(MLSTM_W), (B, T, MLSTM_W), F32),
        (row(POOL_W), (B, T, POOL_W), BF16),
        (row(LANES), (B, T, LANES), F32),
        (pl.BlockSpec((1, 2 * SUBLANES, POOL_W), lambda b, t: (b, 0, 0)), (B, 2 * SUBLANES, POOL_W), F32),
    ]
    return pl.pallas_call(
        functools.partial(_inproj_prompt_kernel, tm=tm),
        grid=(B, nT),
        in_specs=[pl.BlockSpec((1, tm, D), lambda b, t: (b, t, 0)),
                  _const_spec((1, D)), mod, mod,
                  _const_spec(w_b.shape), _const_spec(w_q2[0].shape), _const_spec(w_q2[1].shape),
                  _const_spec(pool_bd.shape), _const_spec((1, POOL_W))],
        out_specs=[o[0] for o in outs],
        out_shape=[jax.ShapeDtypeStruct(o[1], o[2]) for o in outs],
        scratch_shapes=[pltpu.VMEM((2 * SUBLANES + tm, POOL_W), F32)],
        compiler_params=_params("arbitrary", "arbitrary"),
        name="inproj_prompt",
    )(x, g1, sh1, sc1, w_b, w_q2[0], w_q2[1], pool_bd, pool_scale)


def _compress_prompt_kernel(k_ref, v_ref, wpk_ref, wpv_ref, wk_ref, wkl_ref, wv_ref, bk_ref, bv_ref,
                            kc_ref, vc_ref, *, nseg):
    def pooled(src_ref, wp_ref):
        a = jnp.zeros((nseg, KV_W), F32)
        b = jnp.zeros((nseg, KV_W), F32)
        for j in range(CMP_STRIDE):
            xj = src_ref[0, pl.ds(j, nseg, stride=CMP_STRIDE), :]
            a = a + xj * wp_ref[j:j + 1, :]
            b = b + xj * wp_ref[CMP_STRIDE + j:CMP_STRIDE + j + 1, :]
        return a + pltpu.roll(b, nseg - 1, 0)

    kc_ref[0] = _dot_split(_split2(pooled(k_ref, wpk_ref)), (wk_ref[...], wkl_ref[...])) + bk_ref[...]
    vc_ref[0] = (_dot(pooled(v_ref, wpv_ref).astype(BF16), wv_ref[...]) + bv_ref[...]).astype(BF16)


def _compress_prompt(nk, nv, cw):
    B, T, _ = nk.shape
    nseg = T // CMP_STRIDE
    src = pl.BlockSpec((1, T, KV_W), lambda b: (b, 0, 0))
    return pl.pallas_call(
        functools.partial(_compress_prompt_kernel, nseg=nseg),
        grid=(B,),
        in_specs=[src, src, _const_spec((CMP_LEN, KV_W)), _const_spec((CMP_LEN, KV_W)),
                  _const_spec((KV_W, KV_W)), _const_spec((KV_W, KV_W)), _const_spec((KV_W, 2 * KV_W)),
                  _const_spec((1, KV_W)), _const_spec((1, 2 * KV_W))],
        out_specs=[pl.BlockSpec((1, nseg, KV_W), lambda b: (b, 0, 0)),
                   pl.BlockSpec((1, nseg, 2 * KV_W), lambda b: (b, 0, 0))],
        out_shape=[jax.ShapeDtypeStruct((B, nseg, KV_W), F32),
                   jax.ShapeDtypeStruct((B, nseg, 2 * KV_W), BF16)],
        compiler_params=_params("arbitrary"),
        name="compress_prompt",
    )(nk, nv, cw["wpk"], cw["wpv"], cw["wk"], cw["wk_lo"], cw["wv_dup"], cw["bk"], cw["bv_dup"])


def _prep_compress(cmp_pos, cmp_w, cmp_b, page):
    tile2 = lambda t: jnp.concatenate([t, t], axis=-1)
    z = jnp.zeros((HD, HD), F32)
    wk, wv = cmp_w[0], cmp_w[1]
    wk_bd = jnp.block([[wk, z], [z, wk]])
    wv_dup = jnp.block([[wv, z, z, z], [z, z, z, wv]])

    def pos_t(p):
        reps = page // CMP_STRIDE
        return jnp.concatenate([jnp.tile(p[:CMP_STRIDE].T, (1, reps)), jnp.tile(p[CMP_STRIDE:].T, (1, reps))], axis=0)

    return dict(
        wpk=tile2(cmp_pos[0]), wpv=tile2(cmp_pos[1]),
        wk=wk_bd.astype(BF16), wk_lo=(wk_bd - wk_bd.astype(BF16).astype(F32)).astype(BF16),
        wv_dup=wv_dup.astype(BF16),
        bk=tile2(cmp_b[0])[None, :],
        bv_dup=jnp.concatenate([cmp_b[1], jnp.zeros((2 * HD,), F32), cmp_b[1]])[None, :],
        wpk_t=pos_t(cmp_pos[0]), wpv_t=pos_t(cmp_pos[1]),
        wpk_b0=cmp_pos[0][CMP_STRIDE][:, None], wpv_b0=cmp_pos[1][CMP_STRIDE][:, None],
        wk_t=wk.T.astype(BF16), wv_t=wv.T.astype(BF16),
        bk_col=cmp_b[0][:, None], bv_col=cmp_b[1][:, None])


def _importance_matrix(nc_rows, nc_valid, ns, lane_off, width):
    m = np.zeros((nc_rows, width), np.float32)
    for j in range(ns):
        for n in range(SLC_RATIO * j - 1, SLC_RATIO * j + SLC_RATIO):
            if 0 <= n < nc_valid:
                m[n, lane_off + j] = 1.0
    return m


def _select_blocks_t(st_ref, h, ns):
    nb, nq = st_ref.shape[1], st_ref.shape[2]
    ngrp = nb // SUBLANES
    groups = [st_ref[h, SUBLANES * r:SUBLANES * (r + 1), :] for r in range(ngrp)]
    ranks = [jnp.zeros((SUBLANES, nq), F32) for _ in range(ngrp)]
    jrow = lax.broadcasted_iota(jnp.int32, (SUBLANES, nq), 0)
    for jp in range(min(ns, nb)):
        row = jnp.broadcast_to(st_ref[h, pl.ds(jp, 1), :], (SUBLANES, nq))
        for r in range(ngrp):
            ge = jnp.where(row >= groups[r], 1.0, 0.0)
            gt = jnp.where(row > groups[r], 1.0, 0.0)
            if jp < SUBLANES * r:
                inc = ge
            elif jp >= SUBLANES * (r + 1):
                inc = gt
            else:
                inc = jnp.where(jrow + SUBLANES * r > jp, ge, gt)
            ranks[r] = ranks[r] + inc
    return jnp.concatenate(
        [jnp.where((ranks[r] < N_SEL) & (groups[r] >= 0.0), 0.0, NEG) for r in range(ngrp)], axis=0)


def _nsa_prompt_kernel(q_ref, sm_ref, kc_ref, vc_ref, ks_ref, vs_ref, kw_ref, vw_ref, mt_ref,
                       o_ref, acc_ref, m_ref, qa_ref, st_ref, sa_ref, sb_ref, *, T, tk, qb):
    i = pl.program_id(1)
    ns = T // SLC_BLOCK
    nc = kc_ref.shape[1]
    rows = GRP * qb
    qpos = i * qb + lax.broadcasted_iota(jnp.int32, (qb, 1), 0)
    qpos4 = jnp.concatenate([qpos] * GRP, axis=0)
    qpos_t = i * qb + lax.broadcasted_iota(jnp.int32, (HD, qb), 1)
    jb_t = lax.broadcasted_iota(jnp.int32, (HD, qb), 0)
    sig = jax.nn.sigmoid(sm_ref[0])
    lane = lax.broadcasted_iota(jnp.int32, (qb, LANES), 1)
    wk = WINDOW + qb
    n_idx = lax.broadcasted_iota(jnp.int32, (1, nc), 1)
    cmp_mask = (CMP_STRIDE * n_idx + CMP_LEN - 1) <= qpos4
    wstart = pl.multiple_of(jnp.maximum(i - WINDOW // qb, 0) * qb, qb)
    dlt = qpos - (wstart + lax.broadcasted_iota(jnp.int32, (1, wk), 1))
    wbias = jnp.where((dlt >= 0) & (dlt < WINDOW), 0.0, NEG)
    cur_t = qpos_t // SLC_BLOCK
    valid_t = (jb_t * SLC_BLOCK <= qpos_t) & (jb_t < ns)
    forced_t = (jb_t == 0) | (jb_t == cur_t) | (jb_t == cur_t - 1)
    zpad = jnp.zeros((LANES - HD, qb), F32)
    kc_hi, kc_lo = _split2(kc_ref[0])
    sum_lane = (HD, 0)

    o_cs, o_ws = [], []
    for h in range(HKV):
        parts = []
        for g in range(GRP):
            hd = GRP * h + g
            src = q_ref[0, :, LANES * (hd // 2):LANES * (hd // 2 + 1)]
            if hd % 2 != h:
                src = pltpu.roll(src, HD, 1)
            parts.append(jnp.where((lane < HD) if h == 0 else (lane >= HD), src, 0.0))
        qh, qh_lo = _split2(jnp.concatenate(parts, axis=0))

        p_c = _masked_softmax(_dot_nt(qh, kc_hi) + _dot_nt(qh_lo, kc_hi) + _dot_nt(qh, kc_lo), cmp_mask)
        o_cs.append(_dot(p_c.astype(BF16), vc_ref[0, :, LANES * h:LANES * (h + 1)]))

        psum = p_c[0:qb] + p_c[qb:2 * qb] + p_c[2 * qb:3 * qb] + p_c[3 * qb:4 * qb]
        x1 = psum.astype(BF16)
        r1 = psum - x1.astype(F32)
        x2 = r1.astype(BF16)
        x3 = (r1 - x2.astype(F32)).astype(BF16)
        mt = mt_ref[...]
        imp_t = _dot_nt(mt, x1) + _dot_nt(mt, x2) + _dot_nt(mt, x3)
        score_t = jnp.where(valid_t, jnp.where(forced_t, FORCE, imp_t), -1.0)
        st_ref[h] = jnp.where(jb_t < ns, score_t, -jnp.inf)
        sel_t = _select_blocks_t(st_ref, h, ns)
        selneg = (jnp.concatenate([zpad, sel_t], axis=0) if h == 0
                  else jnp.concatenate([sel_t, zpad], axis=0)).T
        qa_ref[h] = qh + jnp.concatenate([selneg.astype(BF16)] * GRP, axis=0)

        sw = _dot_nt(qh, kw_ref[0, pl.ds(wstart, wk), :])
        sw = jnp.concatenate([sw[qb * g:qb * (g + 1)] + wbias for g in range(GRP)], axis=0)
        e = jnp.exp(sw - jnp.max(sw, axis=-1, keepdims=True))
        o_w = _dot(e.astype(BF16), vw_ref[0, pl.ds(wstart, wk), LANES * h:LANES * (h + 1)])
        o_ws.append(o_w / o_w[:, sum_lane[h]:sum_lane[h] + 1])

    m_ref[...] = jnp.full(m_ref.shape, -jnp.inf, F32)
    acc_ref[...] = jnp.zeros(acc_ref.shape, F32)

    def scores(c, s_ref):
        start = pl.multiple_of(c * tk, tk)
        for h in range(HKV):
            s_ref[h] = _dot_nt(qa_ref[h], ks_ref[0, pl.ds(start, tk), LANES * h:LANES * (h + 1)])

    def accumulate(c, s_ref, diagonal):
        start = pl.multiple_of(c * tk, tk)
        for h in range(HKV):
            sc = s_ref[h]
            if diagonal:
                tok = start + lax.broadcasted_iota(jnp.int32, (1, tk), 1)
                sc = jnp.where(tok <= qpos4, sc, NEG)
            m_prev = m_ref[h]
            m_new = jnp.maximum(m_prev, jnp.max(sc, axis=-1, keepdims=True))
            alpha = jnp.exp(m_prev - m_new)
            p = jnp.exp(sc - m_new[:, 0:1])
            v = vs_ref[0, pl.ds(start, tk), LANES * h:LANES * (h + 1)]
            acc_ref[h] = alpha * acc_ref[h] + _dot(p.astype(BF16), v)
            m_ref[h] = m_new

    n_full = (i * qb) // tk
    scores(0, sa_ref)

    def chunk_pair(cc, carry):
        c0 = 2 * cc
        scores(c0 + 1, sb_ref)
        accumulate(c0, sa_ref, False)
        scores(c0 + 2, sa_ref)
        accumulate(c0 + 1, sb_ref, False)
        return carry

    lax.fori_loop(0, n_full // 2, chunk_pair, 0)

    @pl.when(n_full % 2 == 0)
    def _():
        accumulate(n_full, sa_ref, True)

    @pl.when(n_full % 2 == 1)
    def _():
        scores(n_full, sb_ref)
        accumulate(n_full - 1, sa_ref, False)
        accumulate(n_full, sb_ref, True)

    for h in range(HKV):
        acc = acc_ref[h]
        o_s = acc / acc[:, sum_lane[h]:sum_lane[h] + 1]
        outs = []
        for g in range(GRP):
            c0 = 3 * (GRP * h + g)
            r = slice(qb * g, qb * (g + 1))
            outs.append(sig[:, c0:c0 + 1] * o_cs[h][r] + sig[:, c0 + 1:c0 + 2] * o_s[r]
                        + sig[:, c0 + 2:c0 + 3] * o_ws[h][r])
        base = GRP * HD * h
        for pair in range(GRP // 2):
            lo, hi = outs[2 * pair], outs[2 * pair + 1]
            if h == 0:
                hi = pltpu.roll(hi, HD, 1)
            else:
                lo = pltpu.roll(lo, HD, 1)
            o_ref[0, :, base + LANES * pair:base + LANES * (pair + 1)] = jnp.where(lane < HD, lo, hi).astype(BF16)


def _nsa_prompt(q, sm, kc, vc, ks, vs, kw, vw, qb=256, tk=512):
    B, T, _ = q.shape
    ns = T // SLC_BLOCK
    nc = kc.shape[1]
    assert ns <= HD and T % tk == 0 and T >= WINDOW + qb and WINDOW % qb == 0 and tk % qb == 0
    m2 = _importance_matrix(nc, nc - 1, ns, 0, HD).T
    full = lambda w: pl.BlockSpec((1, T, w), lambda b, i: (b, 0, 0))
    return pl.pallas_call(
        functools.partial(_nsa_prompt_kernel, T=T, tk=tk, qb=qb),
        grid=(B, T // qb),
        in_specs=[pl.BlockSpec((1, qb, NSA_W), lambda b, i: (b, i, 0)),
                  pl.BlockSpec((1, qb, LANES), lambda b, i: (b, i, 0)),
                  pl.BlockSpec((1, nc, KV_W), lambda b, i: (b, 0, 0)),
                  pl.BlockSpec((1, nc, 2 * KV_W), lambda b, i: (b, 0, 0)),
                  full(256), full(256), full(KV_W), full(256),
                  _const_spec(m2.shape)],
        out_specs=pl.BlockSpec((1, qb, NSA_W), lambda b, i: (b, i, 0)),
        out_shape=jax.ShapeDtypeStruct((B, T, NSA_W), BF16),
        scratch_shapes=[pltpu.VMEM((HKV, GRP * qb, LANES), F32)] * 2
                       + [pltpu.VMEM((HKV, GRP * qb, LANES), BF16), pltpu.VMEM((HKV, HD, qb), F32)]
                       + [pltpu.VMEM((HKV, GRP * qb, tk), F32)] * 2,
        compiler_params=_params("arbitrary", "arbitrary"),
        name="nsa_prompt",
    )(q, sm, kc, vc, ks, vs, kw, vw, jnp.asarray(m2, BF16))


def _mlstm_gates(g):
    b = pltpu.roll(_cumsum_rows(_log_sigmoid(g)), LANES - MH, 1)
    return b, g - b


def _mlstm_prompt_kernel(mqkv_ref, sm_ref, mo_ref, gb_ref, om_ref, c_out, n_out, m_out,
                         c_s, n_s, m_s, *, B, L):
    t = pl.program_id(0)

    @pl.when(t == 0)
    def _():
        c_s[...] = jnp.zeros(c_s.shape, F32)
        n_s[...] = jnp.zeros(n_s.shape, F32)
        m_s[...] = jnp.zeros(m_s.shape, F32)

    li = lax.broadcasted_iota(jnp.int32, (L, L), 0)
    si = lax.broadcasted_iota(jnp.int32, (L, L), 1)
    for b in range(B):
        b_al, r = _mlstm_gates(sm_ref[b] + gb_ref[...])
        r_t = jnp.concatenate([r, jnp.zeros((LANES - L, LANES), F32)], axis=0).T if L < LANES else r.T
        heads = []
        for hd in range(MH):
            idx = b * MH + hd
            bcol = b_al[:, SM_IG + hd:SM_IG + hd + 1]
            rcol = r[:, SM_IG + hd:SM_IG + hd + 1]
            rrow = r_t[SM_IG + hd:SM_IG + hd + 1, 0:L]
            mprev = m_s[idx][:, 0:1]
            acol = bcol + mprev
            logd = jnp.where(si <= li, bcol + rrow, NEG)
            mt = jnp.maximum(acol, jnp.max(logd, axis=-1, keepdims=True))
            dm = jnp.exp(logd - mt)
            inter = jnp.exp(acol - mt)
            q = mqkv_ref[b, :, HD * hd:HD * (hd + 1)]
            k = mqkv_ref[b, :, MLSTM_W + HD * hd:MLSTM_W + HD * (hd + 1)] * Q_SCALE
            v = mqkv_ref[b, :, 2 * MLSTM_W + HD * hd:2 * MLSTM_W + HD * (hd + 1)]
            qb, kb = q.astype(BF16), k.astype(BF16)
            qk = _dot_nt(qb, kb) * dm
            c_prev = c_s[idx]
            n_prev = n_s[idx]
            num = _dot(qk.astype(BF16), v.astype(BF16)) + inter * _dot(qb, c_prev.astype(BF16))
            den = jnp.sum(qk, axis=-1, keepdims=True) + inter * jnp.sum(q * n_prev, axis=-1, keepdims=True)
            hout = num / jnp.maximum(jnp.abs(den), jnp.exp(-mt))
            heads.append(jax.nn.sigmoid(mo_ref[b, :, HD * hd:HD * (hd + 1)]) * hout)
            m_last = mt[L - 1:L, :]
            b_last = bcol[L - 1:L, :]
            wcol = jnp.exp(b_last + rcol - m_last)
            decay = jnp.exp(b_last + mprev - m_last)
            c_s[idx] = decay * c_prev + _dot_tn(kb, (wcol * v).astype(BF16))
            n_s[idx] = decay * n_prev + jnp.sum(wcol * k, axis=0, keepdims=True)
            m_s[idx] = jnp.broadcast_to(m_last, (1, LANES))
        om_ref[b] = jnp.concatenate(heads, axis=-1).astype(BF16)

    @pl.when(t == pl.num_programs(0) - 1)
    def _():
        c_out[...] = c_s[...]
        n_out[...] = n_s[...]
        m_out[...] = m_s[...]


def _mlstm_prompt(mqkv, sm, mo, gate_b_tile, L=512):
    B, T, _ = mqkv.shape
    assert T % L == 0 and (L <= LANES or L % LANES == 0)
    blk = lambda w: pl.BlockSpec((B, L, w), lambda t: (0, t, 0))
    st = lambda shape: pl.BlockSpec(shape, lambda t: (0,) * len(shape))
    shapes = [(B * MH, HD, HD), (B * MH, 1, HD), (B * MH, 1, LANES)]
    return pl.pallas_call(
        functools.partial(_mlstm_prompt_kernel, B=B, L=L),
        grid=(T // L,),
        in_specs=[blk(3 * MLSTM_W), blk(LANES), blk(MLSTM_W), _const_spec((1, LANES))],
        out_specs=[blk(MLSTM_W)] + [st(s) for s in shapes],
        out_shape=[jax.ShapeDtypeStruct((B, T, MLSTM_W), BF16)] + [jax.ShapeDtypeStruct(s, F32) for s in shapes],
        scratch_shapes=[pltpu.VMEM(s, F32) for s in shapes],
        compiler_params=_params("arbitrary"),
        name="mlstm_prompt",
    )(mqkv, sm, mo, gate_b_tile)


FF_CHUNK = 2816


def _outffn_prompt_kernel(x_ref, on_ref, om_ref, op_ref, gt1_ref, sh2_ref, sc2_ref, gt2_ref,
                          g2_ref, fg_ref, wout_ref, wup_ref, cw_ref, cb_ref, wdn_ref,
                          y_ref, fs_ref, prev_ref, es_ref, *, tm, d_ff, final):
    t = pl.program_id(1)
    mix = (_dot(on_ref[0], wout_ref[0:NSA_W, :])
           + _dot(om_ref[0], wout_ref[NSA_W:NSA_W + MLSTM_W, :])
           + _dot(op_ref[0], wout_ref[NSA_W + MLSTM_W:, :]))
    x1 = x_ref[0] + gt1_ref[0] * mix
    h2 = _norm_mod(x1, g2_ref[...], sc2_ref[0], sh2_ref[0]).astype(BF16)

    @pl.when(t == 0)
    def _():
        prev_ref[...] = jnp.zeros(prev_ref.shape, F32)

    f = jnp.zeros(x1.shape, F32)
    w = FF_CHUNK
    for c in range(d_ff // w):
        for half, off in ((0, c * w), (1, d_ff + c * w)):
            es_ref[0:SUBLANES, half * w:(half + 1) * w] = prev_ref[:, off:off + w]
            es_ref[SUBLANES:SUBLANES + tm, half * w:(half + 1) * w] = _dot(h2, wup_ref[:, off:off + w])
            prev_ref[:, off:off + w] = es_ref[tm:tm + SUBLANES, half * w:(half + 1) * w]
        ys = []
        for half, off in ((0, c * w), (1, d_ff + c * w)):
            y = cb_ref[:, off:off + w]
            for j in range(FFN_CONV):
                y = y + cw_ref[j:j + 1, off:off + w] * es_ref[pl.ds(SUBLANES - (FFN_CONV - 1) + j, tm), half * w:(half + 1) * w]
            ys.append(y)
        act = ys[0] * jax.nn.sigmoid(ys[0]) * ys[1]
        f = f + _dot(act.astype(BF16), wdn_ref[c * w:(c + 1) * w, :])
    x2 = x1 + gt2_ref[0] * f
    y_ref[0] = _rmsnorm(x2, fg_ref[...]) if final else x2
    fs_ref[0] = prev_ref[...]


def _outffn_prompt(x, on, om, op, gt1, sh2, sc2, gt2, g2, fg, w_out, w_up, conv_w, conv_b, w_dn, final, tm=512):
    B, T, D = x.shape
    d_ff = w_dn.shape[0]
    assert T % tm == 0 and d_ff % FF_CHUNK == 0
    row = lambda w: pl.BlockSpec((1, tm, w), lambda b, t: (b, t, 0))
    mod = pl.BlockSpec((1, 1, D), lambda b, t: (b, 0, 0))
    return pl.pallas_call(
        functools.partial(_outffn_prompt_kernel, tm=tm, d_ff=d_ff, final=final),
        grid=(B, T // tm),
        in_specs=[row(D), row(NSA_W), row(MLSTM_W), row(POOL_W), mod, mod, mod, mod,
                  _const_spec((1, D)), _const_spec((1, D)), _const_spec(w_out.shape), _const_spec(w_up.shape),
                  _const_spec(conv_w.shape), _const_spec(conv_b.shape), _const_spec(w_dn.shape)],
        out_specs=[row(D), pl.BlockSpec((1, SUBLANES, 2 * d_ff), lambda b, t: (b, 0, 0))],
        out_shape=[jax.ShapeDtypeStruct((B, T, D), F32), jax.ShapeDtypeStruct((B, SUBLANES, 2 * d_ff), F32)],
        scratch_shapes=[pltpu.VMEM((SUBLANES, 2 * d_ff), F32), pltpu.VMEM((SUBLANES + tm, 2 * FF_CHUNK), F32)],
        compiler_params=_params("arbitrary", "arbitrary"),
        name="outffn_prompt",
    )(x, on, om, op, gt1, sh2, sc2, gt2, g2, fg, w_out, w_up, conv_w, conv_b, w_dn)


def _prep_layer(l, w, page):
    gate_b = jnp.zeros((1, LANES), F32).at[0, SM_IG:SM_IG + 2 * MH].set(w["mlstm_gate_b"][l])
    pw = w["pool_w"][l]
    z = jnp.zeros((HD, HD), F32)
    pool_bd = jnp.block([[pw[i] if i == j else z for j in range(4)] for i in range(4)]).astype(BF16)
    wq = w["w_in"][l][:, :NSA_W]
    wq_hi = wq.astype(BF16)
    return dict(
        w_q=wq_hi, w_q2=(wq_hi, (wq - wq_hi.astype(F32)).astype(BF16)),
        g1=w["norm1_g"][l][None], g2=w["norm2_g"][l][None],
        w_b=_prep_w_in(w["w_in"][l]), pool_bd=pool_bd, pool_scale=w["pool_scale"][l][None],
        cw=_prep_compress(w["nsa_cmp_pos"][l], w["nsa_cmp_w"][l], w["nsa_cmp_b"][l], page),
        gate_b=gate_b, w_out=w["w_out"][l].astype(BF16), w_up=w["ffn_w_up"][l].astype(BF16),
        conv_w=w["ffn_conv_w"][l], conv_b=w["ffn_conv_b"][l][None], w_dn=w["ffn_w_down"][l].astype(BF16),
        fg=w["final_g"][None])


def _prompt_layer(x, mod, lw, final):
    B, T, D = x.shape
    sh1, sc1, gt1, sh2, sc2, gt2 = [mod[:, :, D * i:D * (i + 1)] for i in range(6)]
    (q, nk, nv, ks, vs, kw, vw, wkv, mqkv, mo, op, sm, ps) = _inproj_prompt(
        x, lw["g1"], sh1, sc1, lw["w_b"], lw["w_q2"], lw["pool_bd"], lw["pool_scale"])
    kc, vc = _compress_prompt(nk, nv, lw["cw"])
    on = _nsa_prompt(q, sm, kc, vc, ks, vs, kw, vw)
    om, c_st, n_st, m_st = _mlstm_prompt(mqkv, sm, mo, lw["gate_b"])
    y, fs = _outffn_prompt(x, on, om, op, gt1, sh2, sc2, gt2, lw["g2"], lw["fg"], lw["w_out"],
                           lw["w_up"], lw["conv_w"], lw["conv_b"], lw["w_dn"], final)
    wb = wkv.shape[1]
    states = (nk.reshape(B, T, 2, HKV, HD), nv.reshape(B, T, 2, HKV, HD),
              wkv[:, :, :KV_W].reshape(B, wb, HKV, HD), wkv[:, :, KV_W:].reshape(B, wb, HKV, HD),
              c_st.reshape(B, MH, HD, HD), n_st.reshape(B, MH, HD), m_st[:, 0, 0].reshape(B, MH),
              ps[:, 2 * SUBLANES - POOL_STATE:], fs[:, SUBLANES - (FFN_CONV - 1):])
    return y, states


def _inproj_sample_kernel(x_ref, g_ref, sh_ref, sc_ref, w_ref, wq_ref, pw_ref, pscale_ref, prev_ref,
                          q_ref, nk_ref, nv_ref, wkv_ref, mqkv_ref, mo_ref, op_ref, sm_ref, pu_ref, *, pos0):
    h = _norm_mod(x_ref[...], g_ref[...], sc_ref[...], sh_ref[...]).astype(BF16)

    def seg(a, b):
        return _dot(h, w_ref[:, a:b])

    q_ref[...] = _dot(h, wq_ref[...]) * Q_SCALE
    nk_ref[...] = seg(C_NK, C_NV)
    nv_ref[...] = seg(C_NV, C_WKV)
    wkv_ref[...] = seg(C_WKV, C_MQKV)
    mqkv_ref[...] = seg(C_MQKV, C_MO)
    mo_ref[...] = seg(C_MO, C_PU)
    sm_ref[...] = seg(C_SM, C_END)
    pu = seg(C_PU, C_SM)
    pu_ref[...] = pu
    acc = pu
    sums = {}
    for i in range(1, POOL_STATE + 1):
        acc = acc + prev_ref[:, POOL_STATE - i, :]
        if i + 1 in POOL_WINDOWS:
            sums[i + 1] = acc
    cnt = {w: float(min(w, pos0 + 1)) for w in POOL_WINDOWS}
    op_ref[...] = _pool_mix(pu, sums, cnt, pw_ref, pscale_ref)


def _inproj_sample(l, x, g1, sh1, sc1, w_b, w_q, pool_bd, pool_scale, state_pool, pos0):
    DB, D = x.shape
    full = lambda shape: pl.BlockSpec(shape, lambda i: (0,) * len(shape))
    widths = [NSA_W, 256, 256, 256, 3 * MLSTM_W, MLSTM_W, POOL_W, LANES, POOL_W]
    return pl.pallas_call(
        functools.partial(_inproj_sample_kernel, pos0=pos0),
        grid=(1,),
        in_specs=[full((DB, D)), full((1, D)), full((DB, D)), full((DB, D)), full(w_b.shape),
                  full(w_q.shape), full(pool_bd.shape), full((1, POOL_W)),
                  pl.BlockSpec((None, DB, POOL_STATE, POOL_W), lambda i: (l, 0, 0, 0))],
        out_specs=[full((DB, w)) for w in widths],
        out_shape=[jax.ShapeDtypeStruct((DB, w), F32) for w in widths],
        compiler_params=_params("arbitrary"),
        name="inproj_sample",
    )(x, g1, sh1, sc1, w_b, w_q, pool_bd, pool_scale, state_pool)


PAGES_PER_STEP = 32
PAGE_SLOTS = 3


def _row_to_col(rowv):
    n = rowv.shape[1]
    eye = lax.broadcasted_iota(jnp.int32, (n, n), 0) == lax.broadcasted_iota(jnp.int32, (n, n), 1)
    return jnp.sum(jnp.where(eye, jnp.broadcast_to(rowv, (n, n)), 0.0), axis=1, keepdims=True)


def _compress_sample_t_kernel(pt_ref, ck_ref, cv_ref, wk_ref, wv_ref, sg_ref, ok_ref, ov_ref,
                              kbuf, vbuf, sem, *, l, npg, nc, nsteps):
    step = pl.program_id(0) * nc + pl.program_id(1)
    ahead = PAGE_SLOTS - 1
    slot = step % PAGE_SLOTS

    def copies(s, sl):
        b, c = s // nc, s % nc
        out = []
        for i in range(npg):
            p = pt_ref[b, c * npg + i]
            out.append(pltpu.make_async_copy(ck_ref.at[l, p, 0], kbuf.at[sl, i], sem.at[0, sl]))
            out.append(pltpu.make_async_copy(cv_ref.at[l, p, 0], vbuf.at[sl, i], sem.at[1, sl]))
        return out

    @pl.when(step == 0)
    def _():
        for s in range(min(ahead, nsteps)):
            for cp in copies(s, s):
                cp.start()

    @pl.when(step + ahead < nsteps)
    def _():
        for cp in copies(step + ahead, (step + ahead) % PAGE_SLOTS):
            cp.start()

    for cp in copies(step, slot):
        cp.wait()

    for buf, w_ref, o_ref in ((kbuf, wk_ref, ok_ref), (vbuf, wv_ref, ov_ref)):
        wa = w_ref[0:HD, :]
        wb = w_ref[HD:2 * HD, :]
        for h in range(HKV):
            prods = []
            for i in range(npg):
                x = buf[slot, i, h]
                prods.append(jnp.concatenate([x * wa, x * wb], axis=0).astype(BF16))
            o_ref[0, h] = _dot(jnp.concatenate(prods, axis=1), sg_ref[...])


def _compress_sample_t(l, cache_kt, cache_vt, page_table, cw):
    DB, n_pages = page_table.shape
    page = cache_kt.shape[-1]
    npg = min(PAGES_PER_STEP, n_pages)
    assert n_pages % npg == 0 and page % CMP_STRIDE == 0
    spp = page // CMP_STRIDE
    nseg = n_pages * spp
    assert (npg * spp) % LANES == 0 or npg * spp == nseg
    sg = np.zeros((npg * page, npg * spp), np.float32)
    sg[np.arange(npg * page), np.arange(npg * page) // CMP_STRIDE] = 1.0

    nc = n_pages // npg
    out = pl.BlockSpec((1, HKV, 2 * HD, npg * spp), lambda b, c, pt: (b, 0, 0, c))
    cst = lambda shape: pl.BlockSpec(shape, lambda b, c, pt: (0,) * len(shape))
    hbm = pl.BlockSpec(memory_space=pl.ANY)
    grid_spec = pltpu.PrefetchScalarGridSpec(
        num_scalar_prefetch=1, grid=(DB, nc),
        in_specs=[hbm, hbm, cst((2 * HD, page)), cst((2 * HD, page)), cst(sg.shape)],
        out_specs=[out] * 2,
        scratch_shapes=[pltpu.VMEM((PAGE_SLOTS, npg, HKV, HD, page), F32),
                        pltpu.VMEM((PAGE_SLOTS, npg, HKV, HD, page), F32),
                        pltpu.SemaphoreType.DMA((2, PAGE_SLOTS))])
    return pl.pallas_call(
        functools.partial(_compress_sample_t_kernel, l=l, npg=npg, nc=nc, nsteps=DB * nc),
        grid_spec=grid_spec,
        out_shape=[jax.ShapeDtypeStruct((DB, HKV, 2 * HD, nseg), F32)] * 2,
        compiler_params=_params("arbitrary", "arbitrary"),
        name="compress_sample",
    )(page_table, cache_kt, cache_vt, cw["wpk_t"], cw["wpv_t"], jnp.asarray(sg, BF16))


def _nsa_sample_a_t_kernel(q_ref, hk_ref, hv_ref, nk_ref, nv_ref, wkv_ref, wink_ref, winv_ref,
                           wpk_ref, wpv_ref, wk_ref, wv_ref, bk_ref, bv_ref, m_ref,
                           idx_ref, ocw_ref, wko_ref, wvo_ref, *, past, ns):
    ncs = hk_ref.shape[3]
    wb = wink_ref.shape[3]
    q8 = q_ref[0].astype(BF16)
    row = lax.broadcasted_iota(jnp.int32, (NSA_HEADS, 1), 0)
    lane_c = lax.broadcasted_iota(jnp.int32, (HD, ncs), 1)
    lane_w = lax.broadcasted_iota(jnp.int32, (HD, wb), 1)
    n_idx = lax.broadcasted_iota(jnp.int32, (1, ncs), 1)

    def per_head(fn):
        a, b = fn(0), fn(1)
        return jnp.where(row // GRP == 0, a, b)

    def comp_t(h_ref, new_row, wp_ref, w_ref, bias_ref, h):
        at = h_ref[0, h, 0:HD, :]
        bt = h_ref[0, h, HD:2 * HD, :]
        b_new = _row_to_col(new_row[:, HD * h:HD * (h + 1)]) * wp_ref[...]
        pooled = at + jnp.where(lane_c == ncs - 1, b_new, pltpu.roll(bt, ncs - 1, 1))
        return (_dot(w_ref[...], pooled.astype(BF16)) + bias_ref[...]).astype(BF16)

    kct = [comp_t(hk_ref, nk_ref[0], wpk_ref, wk_ref, bk_ref, h) for h in range(HKV)]
    vct = [comp_t(hv_ref, nv_ref[0], wpv_ref, wv_ref, bv_ref, h) for h in range(HKV)]
    s = per_head(lambda h: _dot(q8, kct[h]))
    p = _masked_softmax(s, (CMP_STRIDE * n_idx + CMP_LEN - 1) <= past)
    pb = p.astype(BF16)
    o_c = per_head(lambda h: _dot_nt(pb, vct[h]))

    rowi = lax.broadcasted_iota(jnp.int32, p.shape, 0)
    p0 = jnp.sum(p[0:GRP], axis=0, keepdims=True)
    p1 = jnp.sum(p[GRP:2 * GRP], axis=0, keepdims=True)
    imp = _dot_split3(jnp.where(rowi == 0, p0, jnp.where(rowi == 1, p1, 0.0)), m_ref[...])
    lane = lax.broadcasted_iota(jnp.int32, imp.shape, 1)
    cur = past // SLC_BLOCK
    valid = lane * SLC_BLOCK <= past
    forced = (lane == 0) | (lane == cur) | (lane == cur - 1)
    score = jnp.where(valid, jnp.where(forced, FORCE, imp), -1.0)
    score = jnp.where(lane < ns, score, -jnp.inf)
    lane_o = lax.broadcasted_iota(jnp.int32, (SUBLANES, LANES), 1)
    idx = jnp.full((SUBLANES, LANES), -1, jnp.int32)
    for r in range(min(N_SEL, ns)):
        mx = jnp.max(score, axis=-1, keepdims=True)
        first = jnp.min(jnp.where(score == mx, lane, 1 << 20), axis=-1, keepdims=True)
        idx = jnp.where(lane_o == r, jnp.where(mx >= 0.0, first, -1), idx)
        score = jnp.where(lane == first, -jnp.inf, score)
    idx_ref[0] = idx

    sel_head = lambda t: jnp.where(row // GRP == 0, t[:, :HD], t[:, HD:])
    k_new = wkv_ref[0][:, :KV_W]
    v_new = wkv_ref[0][:, KV_W:]
    s_w = per_head(lambda h: _dot(q8, wink_ref[0, h].astype(BF16)))
    dlt = wb - lax.broadcasted_iota(jnp.int32, (1, wb), 1)
    mask = (dlt < WINDOW) & (past - dlt >= 0)
    s_new = jnp.sum(q8.astype(F32) * sel_head(k_new).astype(BF16).astype(F32), axis=-1, keepdims=True)
    s_w = jnp.where(mask, s_w, NEG)
    mx = jnp.maximum(jnp.max(s_w, axis=-1, keepdims=True), s_new)
    e = jnp.where(mask, jnp.exp(s_w - mx), 0.0)
    e_new = jnp.exp(s_new - mx)
    den = jnp.sum(e, axis=-1, keepdims=True) + e_new
    eb = e.astype(BF16)
    o_w = per_head(lambda h: _dot_nt(eb, winv_ref[0, h].astype(BF16)))
    o_w = (o_w + e_new * sel_head(v_new).astype(BF16).astype(F32)) / den
    ocw_ref[0] = jnp.concatenate([o_c, o_w], axis=-1)
    for h in range(HKV):
        for src, new, dst in ((wink_ref, k_new, wko_ref), (winv_ref, v_new, wvo_ref)):
            col = _row_to_col(new[:, HD * h:HD * (h + 1)])
            dst[0, h] = jnp.where(lane_w == wb - 1, col, pltpu.roll(src[0, h], wb - 1, 1))


NSA_A_SEQS = 4
N_SEQ_IN = 8
N_SEQ_OUT = 4


def _nsa_sample_a_t_multi(*refs, nsq, past, ns):
    for sq in range(nsq):
        one = lambda r: r.at[pl.ds(sq, 1)]
        _nsa_sample_a_t_kernel(*[one(r) for r in refs[:N_SEQ_IN]], *refs[N_SEQ_IN:-N_SEQ_OUT],
                               *[one(r) for r in refs[-N_SEQ_OUT:]], past=past, ns=ns)


def _nsa_sample_a_t(l, q3, hk, hv, nk3, nv3, wkv3, win_kt, win_vt, cw, past):
    DB = q3.shape[0]
    ncs = hk.shape[3]
    wb = win_kt.shape[-1]
    ns = -(-(past + 1) // SLC_BLOCK)
    nsl = -(-ns // LANES) * LANES
    m = _importance_matrix(ncs, ncs, ns, 0, nsl)
    nsq = NSA_A_SEQS if DB % NSA_A_SEQS == 0 else 1
    per = lambda shape: pl.BlockSpec((nsq,) + shape, lambda b: (b,) + (0,) * len(shape))
    win = pl.BlockSpec((None, nsq, HKV, HD, wb), lambda b: (l, b, 0, 0, 0))
    shapes = [(DB, SUBLANES, LANES), (DB, NSA_HEADS, 2 * HD), (DB, HKV, HD, wb), (DB, HKV, HD, wb)]
    dts = [jnp.int32, F32, F32, F32]
    return pl.pallas_call(
        functools.partial(_nsa_sample_a_t_multi, nsq=nsq, past=past, ns=ns),
        grid=(DB // nsq,),
        in_specs=[per((NSA_HEADS, HD))] + [per((HKV, 2 * HD, ncs))] * 2 + [per((1, 256))] * 3 + [win, win]
                 + [_const_spec((HD, 1))] * 2 + [_const_spec((HD, HD))] * 2
                 + [_const_spec((HD, 1))] * 2 + [_const_spec(m.shape)],
        out_specs=[per(s[1:]) for s in shapes],
        out_shape=[jax.ShapeDtypeStruct(s, d) for s, d in zip(shapes, dts)],
        compiler_params=_params("arbitrary"),
        name="nsa_sample_a",
    )(q3, hk, hv, nk3, nv3, wkv3, win_kt, win_vt, cw["wpk_b0"], cw["wpv_b0"], cw["wk_t"], cw["wv_t"],
      cw["bk_col"], cw["bv_col"], jnp.asarray(m, BF16))


def _nsa_sample_b_t_kernel(pt_ref, ix_ref, ck_ref, cv_ref, q_ref, ocw_ref, nk_ref, nv_ref, g_ref, o_ref,
                           kbuf, vbuf, sem, *, l, past, nbp, nsel, page, nsteps):
    b = pl.program_id(0)
    bpp = page // SLC_BLOCK
    ahead = PAGE_SLOTS - 1
    slot = b % PAGE_SLOTS

    def copies(s, sl):
        out = []
        for h in range(HKV):
            for r in range(nsel):
                p = pt_ref[s, jnp.clip(ix_ref[s, h, r], 0, nbp - 1) // bpp]
                out.append(pltpu.make_async_copy(ck_ref.at[l, p, 1, h], kbuf.at[sl, h * nsel + r], sem.at[0, sl]))
                out.append(pltpu.make_async_copy(cv_ref.at[l, p, 1, h], vbuf.at[sl, h * nsel + r], sem.at[1, sl]))
        return out

    @pl.when(b == 0)
    def _():
        for s in range(min(ahead, nsteps)):
            for cp in copies(s, s):
                cp.start()

    @pl.when(b + ahead < nsteps)
    def _():
        for cp in copies(b + ahead, (b + ahead) % PAGE_SLOTS):
            cp.start()

    for cp in copies(b, slot):
        cp.wait()

    q8 = q_ref[0].astype(BF16)
    row = lax.broadcasted_iota(jnp.int32, (NSA_HEADS, 1), 0)
    sel_head = lambda t: jnp.where(row // GRP == 0, t[:, :HD], t[:, HD:])
    k_new = sel_head(nk_ref[0][:, KV_W:]).astype(BF16).astype(F32)
    v_new = sel_head(nv_ref[0][:, KV_W:]).astype(BF16).astype(F32)
    s_new = jnp.sum(q8.astype(F32) * k_new, axis=-1, keepdims=True)
    nk = nsel * page
    lane = lax.broadcasted_iota(jnp.int32, (1, nk), 1)
    o_s = jnp.zeros((NSA_HEADS, HD), F32)
    for h in range(HKV):
        kt = jnp.concatenate([kbuf[slot, h * nsel + r] for r in range(nsel)], axis=1).astype(BF16)
        vt = jnp.concatenate([vbuf[slot, h * nsel + r] for r in range(nsel)], axis=1).astype(BF16)
        jv = jnp.full((1, nk), -1, jnp.int32)
        n_new = jnp.int32(0)
        for r in range(nsel):
            j = ix_ref[b, h, r]
            jv = jnp.where(lane // page == r, j, jv)
            n_new = n_new + (j == nbp).astype(jnp.int32)
        t_in = lane % page
        tok = (jv // bpp) * page + t_in
        mask = (jv >= 0) & (jv < nbp) & (t_in // SLC_BLOCK == jv % bpp) & (tok <= past)
        has_new = n_new > 0
        s = jnp.where(mask, _dot(q8, kt), NEG)
        sn = jnp.where(has_new, s_new, NEG)
        mx = jnp.maximum(jnp.max(s, axis=-1, keepdims=True), sn)
        e = jnp.where(mask, jnp.exp(s - mx), 0.0)
        e_new = jnp.where(has_new, jnp.exp(sn - mx), 0.0)
        den = jnp.maximum(jnp.sum(e, axis=-1, keepdims=True) + e_new, jnp.finfo(jnp.float32).tiny)
        o_h = (_dot_nt(e.astype(BF16), vt) + e_new * v_new) / den
        o_s = jnp.where(row // GRP == h, o_h, o_s)
    g = jax.nn.sigmoid(g_ref[0])
    ocw = ocw_ref[0]
    o_ref[0] = g[:, 0:1] * ocw[:, :HD] + g[:, 1:2] * o_s + g[:, 2:3] * ocw[:, HD:]


def _nsa_sample_b_t(l, cache_kt, cache_vt, page_table, idx, q3, ocw, nk3, nv3, g3, past):
    DB = q3.shape[0]
    page = cache_kt.shape[-1]
    bpp = page // SLC_BLOCK
    nbp = past // SLC_BLOCK
    nsel = idx.shape[2]

    per = lambda shape: pl.BlockSpec((1,) + shape, lambda b, pt, ix: (b,) + (0,) * len(shape))
    hbm = pl.BlockSpec(memory_space=pl.ANY)
    n = HKV * nsel
    grid_spec = pltpu.PrefetchScalarGridSpec(
        num_scalar_prefetch=2, grid=(DB,),
        in_specs=[hbm, hbm, per((NSA_HEADS, HD)), per((NSA_HEADS, 2 * HD)), per((1, 256)), per((1, 256)),
                  per((NSA_HEADS, 3))],
        out_specs=per((NSA_HEADS, HD)),
        scratch_shapes=[pltpu.VMEM((PAGE_SLOTS, n, HD, page), F32), pltpu.VMEM((PAGE_SLOTS, n, HD, page), F32),
                        pltpu.SemaphoreType.DMA((2, PAGE_SLOTS))])
    return pl.pallas_call(
        functools.partial(_nsa_sample_b_t_kernel, l=l, past=past, nbp=nbp, nsel=nsel, page=page, nsteps=DB),
        grid_spec=grid_spec,
        out_shape=jax.ShapeDtypeStruct((DB, NSA_HEADS, HD), F32),
        compiler_params=_params("arbitrary"),
        name="nsa_sample_b",
    )(page_table, idx, cache_kt, cache_vt, q3, ocw, nk3, nv3, g3)


SEQ_PER_STEP = 8


def _mlstm_sample_kernel(mqkv_ref, sm_ref, mo_ref, gb_ref, c_ref, n_ref, m_ref,
                         om_ref, c_out, n_out, m_out, *, nb):
    eye = (lax.broadcasted_iota(jnp.int32, (HD, HD), 0) == lax.broadcasted_iota(jnp.int32, (HD, HD), 1))
    lane = lax.broadcasted_iota(jnp.int32, (1, LANES), 1)

    def col(rowv):
        return jnp.sum(jnp.where(eye, jnp.broadcast_to(rowv, (HD, HD)), 0.0), axis=1, keepdims=True)

    for b in range(nb):
        g = sm_ref[b] + gb_ref[...]
        lf = _log_sigmoid(g)
        heads = []
        m_tile = jnp.zeros((1, LANES), F32)
        for hd in range(MH):
            ig = g[:, SM_IG + hd:SM_IG + hd + 1]
            a = lf[:, SM_FG + hd:SM_FG + hd + 1] + m_ref[b][:, hd:hd + 1]
            mt = jnp.maximum(a, ig)
            dm = jnp.exp(ig - mt)
            inter = jnp.exp(a - mt)
            q = mqkv_ref[b][:, HD * hd:HD * (hd + 1)]
            k = mqkv_ref[b][:, MLSTM_W + HD * hd:MLSTM_W + HD * (hd + 1)] * Q_SCALE
            v = mqkv_ref[b][:, 2 * MLSTM_W + HD * hd:2 * MLSTM_W + HD * (hd + 1)]
            c_prev = c_ref[b, hd]
            n_prev = n_ref[b][hd:hd + 1, :]
            qk = jnp.sum(q * k, axis=-1, keepdims=True) * dm
            q_c = jnp.sum(col(q) * c_prev, axis=0, keepdims=True)
            num = qk * v + inter * q_c
            den = qk + inter * jnp.sum(q * n_prev, axis=-1, keepdims=True)
            hout = num / jnp.maximum(jnp.abs(den), jnp.exp(-mt))
            heads.append(jax.nn.sigmoid(mo_ref[b][:, HD * hd:HD * (hd + 1)]) * hout)
            w = jnp.exp(ig - mt)
            decay = jnp.exp(a - mt)
            c_out[b, hd] = decay * c_prev + (w * col(k)) * v
            n_out[b, hd:hd + 1, :] = decay * n_prev + w * k
            m_tile = jnp.where(lane == hd, mt, m_tile)
        om_ref[b] = jnp.concatenate(heads, axis=-1)
        m_out[b] = m_tile


def _mlstm_sample(l, mqkv3, sm3, mo3, gate_b_tile, state_c, state_n, state_m4):
    DB = mqkv3.shape[0]
    nb = min(SEQ_PER_STEP, DB)
    assert DB % nb == 0
    per = lambda w: pl.BlockSpec((nb, 1, w), lambda i: (i, 0, 0))
    return pl.pallas_call(
        functools.partial(_mlstm_sample_kernel, nb=nb),
        grid=(DB // nb,),
        in_specs=[per(3 * MLSTM_W), per(LANES), per(MLSTM_W), _const_spec((1, LANES)),
                  pl.BlockSpec((None, nb, MH, HD, HD), lambda i: (l, i, 0, 0, 0)),
                  pl.BlockSpec((None, nb, MH, HD), lambda i: (l, i, 0, 0)),
                  pl.BlockSpec((None, nb, 1, MH), lambda i: (l, i, 0, 0))],
        out_specs=[per(MLSTM_W), pl.BlockSpec((nb, MH, HD, HD), lambda i: (i, 0, 0, 0)),
                   pl.BlockSpec((nb, MH, HD), lambda i: (i, 0, 0)), per(LANES)],
        out_shape=[jax.ShapeDtypeStruct((DB, 1, MLSTM_W), F32), jax.ShapeDtypeStruct((DB, MH, HD, HD), F32),
                   jax.ShapeDtypeStruct((DB, MH, HD), F32), jax.ShapeDtypeStruct((DB, 1, LANES), F32)],
        compiler_params=_params("arbitrary"),
        name="mlstm_sample",
    )(mqkv3, sm3, mo3, gate_b_tile, state_c, state_n, state_m4)


def _outffn_sample_kernel(x_ref, on_ref, om_ref, op_ref, gt1_ref, sh2_ref, sc2_ref, gt2_ref, g2_ref, fg_ref,
                          wout_ref, wup_ref, cw_ref, cb_ref, wdn_ref, prev_ref, y_ref, up_ref, *, d_ff, final):
    mix = (_dot(on_ref[...].astype(BF16), wout_ref[0:NSA_W, :])
           + _dot(om_ref[...].astype(BF16), wout_ref[NSA_W:NSA_W + MLSTM_W, :])
           + _dot(op_ref[...].astype(BF16), wout_ref[NSA_W + MLSTM_W:, :]))
    x1 = x_ref[...] + gt1_ref[...] * mix
    h2 = _norm_mod(x1, g2_ref[...], sc2_ref[...], sh2_ref[...]).astype(BF16)
    up = _dot(h2, wup_ref[...])
    up_ref[...] = up
    y = cb_ref[...] + cw_ref[FFN_CONV - 1:FFN_CONV, :] * up
    for j in range(FFN_CONV - 1):
        y = y + cw_ref[j:j + 1, :] * prev_ref[:, j, :]
    a, b = y[:, :d_ff], y[:, d_ff:]
    f = _dot((a * jax.nn.sigmoid(a) * b).astype(BF16), wdn_ref[...])
    x2 = x1 + gt2_ref[...] * f
    y_ref[...] = _rmsnorm(x2, fg_ref[...]) if final else x2


def _outffn_sample(l, x, on, om, op, gt1, sh2, sc2, gt2, g2, fg, w_out, w_up, conv_w, conv_b, w_dn,
                   state_ffn, final):
    DB, D = x.shape
    d_ff = w_dn.shape[0]
    full = lambda shape: pl.BlockSpec(shape, lambda i: (0,) * len(shape))
    args = (x, on, om, op, gt1, sh2, sc2, gt2, g2, fg, w_out, w_up, conv_w, conv_b, w_dn)
    return pl.pallas_call(
        functools.partial(_outffn_sample_kernel, d_ff=d_ff, final=final),
        grid=(1,),
        in_specs=[full(a.shape) for a in args]
                 + [pl.BlockSpec((None, DB, FFN_CONV - 1, 2 * d_ff), lambda i: (l, 0, 0, 0))],
        out_specs=[full((DB, D)), full((DB, 2 * d_ff))],
        out_shape=[jax.ShapeDtypeStruct((DB, D), F32), jax.ShapeDtypeStruct((DB, 2 * d_ff), F32)],
        compiler_params=_params("arbitrary"),
        name="outffn_sample",
    )(*args, state_ffn)


def _sample_layer(l, x, mod, lw, caches, final):
    DB, D = x.shape
    ckt, cvt, page_table, win_kt, win_vt, st_c, st_n, st_m, st_pool, st_ffn = caches
    past = page_table.shape[1] * ckt.shape[-1]
    sh1, sc1, gt1, sh2, sc2, gt2 = [mod[:, D * i:D * (i + 1)] for i in range(6)]
    q, nk, nv, wkv, mqkv, mo, op, sm, pu = _inproj_sample(
        l, x, lw["g1"], sh1, sc1, lw["w_b"], lw["w_q"], lw["pool_bd"], lw["pool_scale"], st_pool, past)
    r3 = lambda t: t[:, None, :]
    hk, hv = _compress_sample_t(l, ckt, cvt, page_table, lw["cw"])
    q3 = q.reshape(DB, NSA_HEADS, HD)
    idx, ocw, wk_new, wv_new = _nsa_sample_a_t(l, q3, hk, hv, r3(nk), r3(nv), r3(wkv), win_kt, win_vt,
                                               lw["cw"], past)
    g3 = sm[:, :3 * NSA_HEADS].reshape(DB, NSA_HEADS, 3)
    on = _nsa_sample_b_t(l, ckt, cvt, page_table, idx[:, :HKV, :N_SEL], q3, ocw, r3(nk), r3(nv), g3, past)
    om, c_new, n_new, m_new = _mlstm_sample(l, r3(mqkv), r3(sm), r3(mo), lw["gate_b"], st_c, st_n,
                                            st_m.reshape(st_m.shape[0], DB, 1, MH))
    y, up = _outffn_sample(l, x, on.reshape(DB, NSA_W), om[:, 0], op, gt1, sh2, sc2, gt2,
                           lw["g2"], lw["fg"], lw["w_out"], lw["w_up"], lw["conv_w"],
                           lw["conv_b"], lw["w_dn"], st_ffn, final)
    states = (nk.reshape(DB, 1, 2, HKV, HD), nv.reshape(DB, 1, 2, HKV, HD),
              jnp.transpose(wk_new, (0, 3, 1, 2)), jnp.transpose(wv_new, (0, 3, 1, 2)),
              c_new, n_new, m_new[:, 0, :MH],
              jnp.concatenate([st_pool[l][:, 1:], pu[:, None, :]], axis=1),
              jnp.concatenate([st_ffn[l][:, 1:], up[:, None, :]], axis=1))
    return y, states


def kernel(x_prompt, x_sample, cache_k, cache_v, cache_win_k, cache_win_v, state_mlstm_C, state_mlstm_n,
           state_mlstm_m, state_pool, state_ffn_conv, page_table, c_prompt, c_sample, norm1_g, norm2_g, ada_w,
           ada_b, w_in, nsa_cmp_pos, nsa_cmp_w, nsa_cmp_b, mlstm_gate_b, pool_w, pool_scale, w_out, ffn_w_up,
           ffn_conv_w, ffn_conv_b, ffn_w_down, final_g):
    B = x_prompt.shape[0]
    DB, DS, D = x_sample.shape
    assert DS == 1
    depth = w_in.shape[0]
    w = dict(norm1_g=norm1_g, norm2_g=norm2_g, w_in=w_in, nsa_cmp_pos=nsa_cmp_pos, nsa_cmp_w=nsa_cmp_w,
             nsa_cmp_b=nsa_cmp_b, mlstm_gate_b=mlstm_gate_b, pool_w=pool_w, pool_scale=pool_scale, w_out=w_out,
             ffn_w_up=ffn_w_up, ffn_conv_w=ffn_conv_w, ffn_conv_b=ffn_conv_b, ffn_w_down=ffn_w_down,
             final_g=final_g)
    mod = _ada_mod(jnp.concatenate([c_prompt, c_sample], axis=0), ada_w, ada_b)
    page = cache_k.shape[2]
    ckt = jnp.transpose(cache_k, (0, 1, 3, 4, 5, 2))
    cvt = jnp.transpose(cache_v, (0, 1, 3, 4, 5, 2))
    caches = (ckt, cvt, page_table, jnp.transpose(cache_win_k, (0, 1, 3, 4, 2)),
              jnp.transpose(cache_win_v, (0, 1, 3, 4, 2)),
              state_mlstm_C, state_mlstm_n, state_mlstm_m, state_pool, state_ffn_conv)
    xp, xs = x_prompt, x_sample[:, 0, :]
    acc_p = [[] for _ in range(9)]
    acc_s = [[] for _ in range(9)]
    for l in range(depth):
        lw = _prep_layer(l, w, page)
        final = l == depth - 1
        xp, st_p = _prompt_layer(xp, mod[l, :B][:, None, :], lw, final)
        xs, st_s = _sample_layer(l, xs, mod[l, B:], lw, caches, final)
        for a, v in zip(acc_p, st_p):
            a.append(v)
        for a, v in zip(acc_s, st_s):
            a.append(v)
    sp = [jnp.stack(a, axis=0) for a in acc_p]
    ss = [jnp.stack(a, axis=0) for a in acc_s]
    return (xp, xs[:, None, :], *sp, *ss)
```
